```python
import math
import jax, jax.numpy as jnp
from jax import lax
import numpy as np

D_MODEL = 1024
BATCH = 2
SEQ = 16384
DEPTH = 2

GRID_W = 64
CTX_LEN = 256
EPS = 1e-6

ATT_WIDTH = D_MODEL // 2
ATT_HEAD_DIM = 64
ATT_HEADS = ATT_WIDTH // (2 * ATT_HEAD_DIM)
Q_BLOCK = 128
ROPE_BASE = 10000.0
ROPE_PAIRS = ATT_HEAD_DIM // 4

GDN_WIDTH = D_MODEL - ATT_WIDTH
GDN_HEAD_DIM = 128
GDN_HEADS = GDN_WIDTH // GDN_HEAD_DIM
GDN_CHUNK = 64
CONV_W = 5

MIX_WIDTH = ATT_WIDTH + GDN_WIDTH
IN_COLS = 3 * ATT_WIDTH + 4 * GDN_WIDTH + 4 * GDN_HEADS

N_EXPERTS = 32
TOP_K = 4
D_EXPERT = D_MODEL
SWIGLU_ALPHA = 1.702
SWIGLU_LIMIT = 7.0
MOE_BLOCK = 256

kernel_name = 'hybrid_diffattn_gdn_moe_dit'


def rms_norm(x, g):
    xf = x.astype(jnp.float32)
    y = xf * lax.rsqrt(jnp.mean(xf * xf, axis=-1, keepdims=True) + EPS)
    return (y * g.astype(jnp.float32)).astype(x.dtype)


def l2_norm(x):
    xf = x.astype(jnp.float32)
    return (xf * lax.rsqrt(jnp.sum(xf * xf, axis=-1, keepdims=True) + EPS)).astype(x.dtype)


def modulate(x, shift, scale):
    return x * (1 + scale) + shift


def rotate_pairs(x, ang):
    cos = jnp.cos(ang)[:, None, None, :].astype(x.dtype)
    sin = jnp.sin(ang)[:, None, None, :].astype(x.dtype)
    x1, x2 = jnp.split(x, 2, axis=-1)
    return jnp.concatenate([x1 * cos - x2 * sin, x2 * cos + x1 * sin], axis=-1)


def axial_rope(x, ang_row, ang_col):
    x_row, x_col = jnp.split(x, 2, axis=-1)
    return jnp.concatenate([rotate_pairs(x_row, ang_row), rotate_pairs(x_col, ang_col)], axis=-1)


def diff_qkv(p, q_norm_g, k_norm_g, ang_row, ang_col):
    B, L = p.shape[:2]
    q = p[..., :ATT_WIDTH].reshape(B, L, ATT_HEADS, 2, ATT_HEAD_DIM)
    k = p[..., ATT_WIDTH:2 * ATT_WIDTH].reshape(B, L, ATT_HEADS, 2, ATT_HEAD_DIM)
    v = p[..., 2 * ATT_WIDTH:3 * ATT_WIDTH].reshape(B, L, ATT_HEADS, 2 * ATT_HEAD_DIM)
    q = rms_norm(q, q_norm_g)
    k = rms_norm(k, k_norm_g)
    if ang_row is not None:
        q = axial_rope(q, ang_row, ang_col)
        k = axial_rope(k, ang_row, ang_col)
    return q * (ATT_HEAD_DIM ** -0.5), k, v


def diff_attend(q, k, v, lam):
    s = jnp.einsum('bqhpd,bkhpd->bhpqk', q, k).astype(jnp.float32)
    p = jax.nn.softmax(s, axis=-1)
    p = p[:, :, 0] - lam * p[:, :, 1]
    return jnp.einsum('bhqk,bkhe->bqhe', p.astype(v.dtype), v)


def diff_out(o, subln_g, lam_init):
    B, L = o.shape[:2]
    return (rms_norm(o, subln_g) * (1.0 - lam_init)).reshape(B, L, ATT_WIDTH)


def gdn_inputs(p, conv_w, a_log, dt_bias):
    B, L = p.shape[:2]
    off = 3 * ATT_WIDTH
    qkv = p[..., off:off + 3 * GDN_WIDTH]
    taps = conv_w.reshape(CONV_W, 1, 3 * GDN_WIDTH).astype(qkv.dtype)
    qkv = jax.nn.silu(lax.conv_general_dilated(
        qkv, taps, window_strides=(1,), padding=[(CONV_W // 2, CONV_W // 2)],
        dimension_numbers=('NWC', 'WIO', 'NWC'), feature_group_count=3 * GDN_WIDTH))
    qkv = qkv.reshape(B, L, 3, GDN_HEADS, GDN_HEAD_DIM)
    q, k, v = l2_norm(qkv[:, :, 0]), l2_norm(qkv[:, :, 1]), qkv[:, :, 2]
    z = p[..., off + 3 * GDN_WIDTH:off + 4 * GDN_WIDTH].reshape(B, L, GDN_HEADS, GDN_HEAD_DIM)
    ab = p[..., off + 4 * GDN_WIDTH:].astype(jnp.float32).reshape(B, L, 2, 2, GDN_HEADS)
    g = -jnp.exp(a_log.astype(jnp.float32)) * jax.nn.softplus(ab[:, :, 0] + dt_bias.astype(jnp.float32))
    beta = jax.nn.sigmoid(ab[:, :, 1])
    return q, k, v, z, g, beta


def gated_delta_rule(q, k, v, g, beta):
    B, L, H, DK = q.shape
    DV = v.shape[-1]
    C = GDN_CHUNK
    N = L // C
    f32 = jnp.float32

    def to_chunks(t):
        t = t.astype(f32).reshape((B, N, C, H) + t.shape[3:])
        return jnp.moveaxis(t, (1, 3), (0, 2))

    qc = to_chunks(q) * (DK ** -0.5)
    kc, vc, gc, bc = to_chunks(k), to_chunks(v), to_chunks(g), to_chunks(beta)
    gcum = jnp.cumsum(gc, axis=-1)
    causal = jnp.tril(jnp.ones((C, C), bool))
    strict = jnp.tril(jnp.ones((C, C), bool), -1)
    gdiff = gcum[..., :, None] - gcum[..., None, :]
    decay = jnp.where(causal, jnp.exp(jnp.where(causal, gdiff, 0.0)), 0.0)
    kbeta = kc * bc[..., None]
    a_mat = jnp.where(strict, jnp.einsum('nbhcd,nbhsd->nbhcs', kbeta, kc) * decay, 0.0)
    eye = jnp.eye(C, dtype=f32)
    t_mat = lax.linalg.triangular_solve(eye + a_mat, jnp.broadcast_to(eye, a_mat.shape),
                                        left_side=True, lower=True, unit_diagonal=True)
    u = jnp.einsum('nbhcs,nbhse->nbhce', t_mat, vc * bc[..., None])
    w = jnp.einsum('nbhcs,nbhsd->nbhcd', t_mat, kbeta * jnp.exp(gcum)[..., None])
    qk_intra = jnp.where(causal, jnp.einsum('nbhcd,nbhsd->nbhcs', qc, kc) * decay, 0.0)

    def chunk_step(state, xs):
        q_i, k_i, u_i, w_i, g_i, a_i = xs
        v_new = u_i - jnp.einsum('bhcd,bhde->bhce', w_i, state)
        o_i = (jnp.einsum('bhcd,bhde->bhce', q_i * jnp.exp(g_i)[..., None], state)
               + jnp.einsum('bhcs,bhse->bhce', a_i, v_new))
        g_last = g_i[..., -1:]
        k_dec = k_i * jnp.exp(g_last - g_i)[..., None]
        state = state * jnp.exp(g_last)[..., None] + jnp.einsum('bhcd,bhce->bhde', k_dec, v_new)
        return state, o_i

    s0 = jnp.zeros((B, H, DK, DV), f32)
    _, o = lax.scan(chunk_step, s0, (qc, kc, u, w, gcum, qk_intra))
    o = jnp.moveaxis(o, (0, 2), (1, 3)).reshape(B, L, H, DV)
    return o.astype(v.dtype)


def bidir_gdn(ctx_in, lat_in):
    Lc = ctx_in[0].shape[1]

    def run(direction):
        def arrange(t_ctx, t_lat):
            if direction == 1:
                t_ctx, t_lat = jnp.flip(t_ctx, 1), jnp.flip(t_lat, 1)
            return jnp.concatenate([t_ctx, t_lat], axis=1)
        q = arrange(ctx_in[0], lat_in[0])
        k = arrange(ctx_in[1], lat_in[1])
        v = arrange(ctx_in[2], lat_in[2])
        g = arrange(ctx_in[3][:, :, direction], lat_in[3][:, :, direction])
        beta = arrange(ctx_in[4][:, :, direction], lat_in[4][:, :, direction])
        o = gated_delta_rule(q, k, v, g, beta)
        o_ctx, o_lat = o[:, :Lc], o[:, Lc:]
        if direction == 1:
            o_ctx, o_lat = jnp.flip(o_ctx, 1), jnp.flip(o_lat, 1)
        return o_ctx, o_lat

    f_ctx, f_lat = run(0)
    b_ctx, b_lat = run(1)
    return f_ctx + b_ctx, f_lat + b_lat


def gdn_out(o, z, gdn_norm_g):
    B, L = o.shape[:2]
    return (rms_norm(o, gdn_norm_g) * jax.nn.silu(z)).reshape(B, L, GDN_WIDTH)


def hybrid_mixer(u_lat, u_ctx, w_in, w_out, q_norm_g, k_norm_g, lam_full, lam_init, subln_g,
                 conv_w, a_log, dt_bias, gdn_norm_g, ang_row, ang_col, need_ctx):
    B, L, _ = u_lat.shape
    p_lat = u_lat @ w_in
    p_ctx = u_ctx @ w_in

    q_lat, k_lat, v_lat = diff_qkv(p_lat, q_norm_g, k_norm_g, ang_row, ang_col)
    q_ctx, k_ctx, v_ctx = diff_qkv(p_ctx, q_norm_g, k_norm_g, None, None)
    k_all = jnp.concatenate([k_ctx, k_lat], axis=1)
    v_all = jnp.concatenate([v_ctx, v_lat], axis=1)
    n_blk = L // Q_BLOCK
    q_blocks = jnp.moveaxis(q_lat.reshape(B, n_blk, Q_BLOCK, ATT_HEADS, 2, ATT_HEAD_DIM), 1, 0)
    o_att = lax.map(lambda qb: diff_attend(qb, k_all, v_all, lam_full), q_blocks)
    o_att = jnp.moveaxis(o_att, 0, 1).reshape(B, L, ATT_HEADS, 2 * ATT_HEAD_DIM)
    o_att_lat = diff_out(o_att, subln_g, lam_init)

    gq_l, gk_l, gv_l, gz_l, gg_l, gb_l = gdn_inputs(p_lat, conv_w, a_log, dt_bias)
    gq_c, gk_c, gv_c, gz_c, gg_c, gb_c = gdn_inputs(p_ctx, conv_w, a_log, dt_bias)
    o_gdn_ctx, o_gdn_lat = bidir_gdn((gq_c, gk_c, gv_c, gg_c, gb_c), (gq_l, gk_l, gv_l, gg_l, gb_l))

    out_lat = jnp.concatenate([o_att_lat, gdn_out(o_gdn_lat, gz_l, gdn_norm_g)], axis=-1) @ w_out
    if not need_ctx:
        return out_lat, None
    o_att_ctx = diff_out(diff_attend(q_ctx, k_ctx, v_ctx, lam_full), subln_g, lam_init)
    out_ctx = jnp.concatenate([o_att_ctx, gdn_out(o_gdn_ctx, gz_c, gdn_norm_g)], axis=-1) @ w_out
    return out_lat, out_ctx


def moe_ffn(h, router_w, router_b, w_gate_up, b_gate_up, w_down, b_down):
    T, D = h.shape
    n_assign = T * TOP_K
    n_blocks = -(-n_assign // MOE_BLOCK) + N_EXPERTS
    logits = (h @ router_w).astype(jnp.float32) + router_b.astype(jnp.float32)
    top_logit, top_e = lax.top_k(logits, TOP_K)
    gate = jax.nn.softmax(top_logit, axis=-1)
    flat_e = top_e.reshape(-1)
    order = jnp.argsort(flat_e)
    e_sorted = flat_e[order]
    tok_sorted = (order // TOP_K).astype(jnp.int32)
    gate_sorted = gate.reshape(-1)[order]
    counts = jnp.bincount(flat_e, length=N_EXPERTS)
    padded = (counts + MOE_BLOCK - 1) // MOE_BLOCK * MOE_BLOCK
    start = jnp.cumsum(counts) - counts
    pad_end = jnp.cumsum(padded)
    pad_start = pad_end - padded
    slot = pad_start[e_sorted] + (jnp.arange(n_assign, dtype=jnp.int32) - start[e_sorted])
    slot_tok = jnp.full((n_blocks * MOE_BLOCK,), T, jnp.int32).at[slot].set(tok_sorted)
    slot_gate = jnp.zeros((n_blocks * MOE_BLOCK,), jnp.float32).at[slot].set(gate_sorted)
    block_e = jnp.minimum(jnp.searchsorted(pad_end, jnp.arange(n_blocks, dtype=jnp.int32) * MOE_BLOCK,
                                           side='right'), N_EXPERTS - 1)
    h_pad = jnp.concatenate([h, jnp.zeros((1, D), h.dtype)], axis=0)

    def expert_block(args):
        idx, e = args
        xb = h_pad[idx]
        gu = xb @ w_gate_up[e] + b_gate_up[e]
        g_, up = gu[:, :D_EXPERT], gu[:, D_EXPERT:]
        g_ = jnp.minimum(g_, SWIGLU_LIMIT)
        up = jnp.clip(up, -SWIGLU_LIMIT, SWIGLU_LIMIT)
        glu = g_ * jax.nn.sigmoid(SWIGLU_ALPHA * g_)
        return ((up + 1) * glu) @ w_down[e] + b_down[e]

    y = lax.map(expert_block, (slot_tok.reshape(n_blocks, MOE_BLOCK), block_e))
    y = y.reshape(-1, D) * slot_gate[:, None].astype(h.dtype)
    return jnp.zeros((T + 1, D), h.dtype).at[slot_tok].add(y)[:T]


def setup_inputs(seed: int = 0) -> dict:
    key = jax.random.key(seed)
    ks = jax.random.split(key, 28)
    f32 = jnp.float32
    D = D_MODEL

    def nrm(k, shape, scale):
        return jax.random.normal(k, shape, f32) * scale

    def gain(k, shape):
        return 1.0 + 0.02 * jax.random.normal(k, shape, f32)

    dt = jnp.exp(jax.random.uniform(ks[14], (DEPTH, 2, GDN_HEADS), f32, math.log(1e-3), math.log(1e-1)))
    return {
        'x': nrm(ks[0], (BATCH, SEQ, D), 1.0),
        'c': nrm(ks[1], (BATCH, D), 1.0),
        'ctx': nrm(ks[2], (BATCH, CTX_LEN, D), 1.0),
        'c_ctx': nrm(ks[3], (D,), 1.0),
        'w_mod': nrm(ks[4], (DEPTH, D, 6 * D), 0.5 * D ** -0.5),
        'b_mod': nrm(ks[5], (DEPTH, 6 * D), 0.02),
        'norm_mix_g': gain(ks[6], (DEPTH, D)),
        'w_in': nrm(ks[7], (DEPTH, D, IN_COLS), D ** -0.5),
        'q_norm_g': gain(ks[8], (DEPTH, ATT_HEAD_DIM)),
        'k_norm_g': gain(ks[9], (DEPTH, ATT_HEAD_DIM)),
        'lam_q1': nrm(ks[10], (DEPTH, ATT_HEAD_DIM), 0.1),
        'lam_k1': nrm(ks[11], (DEPTH, ATT_HEAD_DIM), 0.1),
        'lam_q2': nrm(ks[12], (DEPTH, ATT_HEAD_DIM), 0.1),
        'lam_k2': nrm(ks[13], (DEPTH, ATT_HEAD_DIM), 0.1),
        'subln_g': gain(ks[15], (DEPTH, 2 * ATT_HEAD_DIM)),
        'conv_w': nrm(ks[16], (DEPTH, CONV_W, 3 * GDN_WIDTH), CONV_W ** -0.5),
        'a_log': jnp.log(jax.random.uniform(ks[17], (DEPTH, 2, GDN_HEADS), f32, 1.0, 16.0)),
        'dt_bias': dt + jnp.log(-jnp.expm1(-dt)),
        'gdn_norm_g': gain(ks[18], (DEPTH, GDN_HEAD_DIM)),
        'w_out': nrm(ks[19], (DEPTH, MIX_WIDTH, D), MIX_WIDTH ** -0.5),
        'norm_ffn_g': gain(ks[20], (DEPTH, D)),
        'router_w': nrm(ks[21], (DEPTH, D, N_EXPERTS), D ** -0.5),
        'router_b': nrm(ks[22], (DEPTH, N_EXPERTS), 0.01),
        'w_gate_up': nrm(ks[23], (DEPTH, N_EXPERTS, D, 2 * D_EXPERT), D ** -0.5),
        'b_gate_up': nrm(ks[24], (DEPTH, N_EXPERTS, 2 * D_EXPERT), 0.02),
        'w_down': nrm(ks[25], (DEPTH, N_EXPERTS, D_EXPERT, D), D_EXPERT ** -0.5),
        'b_down': nrm(ks[26], (DEPTH, N_EXPERTS, D), 0.02),
    }


def reference(x, c, ctx, c_ctx, w_mod, b_mod, norm_mix_g, w_in, q_norm_g, k_norm_g,
              lam_q1, lam_k1, lam_q2, lam_k2, subln_g, conv_w, a_log, dt_bias, gdn_norm_g,
              w_out, norm_ffn_g, router_w, router_b, w_gate_up, b_gate_up, w_down, b_down):
    B, L, D = x.shape
    Lc = ctx.shape[1]
    ROWS = L // GRID_W
    row = jnp.repeat(jnp.arange(ROWS, dtype=jnp.float32), GRID_W)
    col = (jnp.arange(L, dtype=jnp.int32) % GRID_W).astype(jnp.float32)
    inv_freq = ROPE_BASE ** (-jnp.arange(ROPE_PAIRS, dtype=jnp.float32) / ROPE_PAIRS)
    ang_row = row[:, None] * inv_freq
    ang_col = col[:, None] * inv_freq

    silu_c = jax.nn.silu(c)
    silu_cc = jax.nn.silu(c_ctx)
    h_lat, h_ctx = x, ctx
    for layer in range(DEPTH):
        last = layer == DEPTH - 1
        mod_lat = jnp.split((silu_c @ w_mod[layer] + b_mod[layer])[:, None, :], 6, axis=-1)
        mod_ctx = jnp.split((silu_cc @ w_mod[layer] + b_mod[layer])[None, None, :], 6, axis=-1)

        lam_init = 0.8 - 0.6 * math.exp(-0.3 * layer)
        lam_full = (jnp.exp(jnp.sum(lam_q1[layer].astype(jnp.float32) * lam_k1[layer].astype(jnp.float32)))
                    - jnp.exp(jnp.sum(lam_q2[layer].astype(jnp.float32) * lam_k2[layer].astype(jnp.float32)))
                    + lam_init)

        u_lat = modulate(rms_norm(h_lat, norm_mix_g[layer]), mod_lat[0], mod_lat[1])
        u_ctx = modulate(rms_norm(h_ctx, norm_mix_g[layer]), mod_ctx[0], mod_ctx[1])
        mix_lat, mix_ctx = hybrid_mixer(u_lat, u_ctx, w_in[layer], w_out[layer], q_norm_g[layer],
                                        k_norm_g[layer], lam_full, lam_init, subln_g[layer],
                                        conv_w[layer], a_log[layer], dt_bias[layer], gdn_norm_g[layer],
                                        ang_row, ang_col, not last)
        h_lat = h_lat + mod_lat[2] * mix_lat
        v_lat = modulate(rms_norm(h_lat, norm_ffn_g[layer]), mod_lat[3], mod_lat[4])
        moe_args = (router_w[layer], router_b[layer], w_gate_up[layer], b_gate_up[layer],
                    w_down[layer], b_down[layer])
        if last:
            y = moe_ffn(v_lat.reshape(B * L, D), *moe_args)
            h_lat = h_lat + mod_lat[5] * y.reshape(B, L, D)
        else:
            h_ctx = h_ctx + mod_ctx[2] * mix_ctx
            v_ctx = modulate(rms_norm(h_ctx, norm_ffn_g[layer]), mod_ctx[3], mod_ctx[4])
            y = moe_ffn(jnp.concatenate([v_lat.reshape(B * L, D), v_ctx.reshape(B * Lc, D)], axis=0), *moe_args)
            h_lat = h_lat + mod_lat[5] * y[:B * L].reshape(B, L, D)
            h_ctx = h_ctx + mod_ctx[5] * y[B * L:].reshape(B, Lc, D)
    return h_lat
```

```python
import functools
import math

import jax
import jax.numpy as jnp
from jax import lax
from jax.experimental import pallas as pl
from jax.experimental.pallas import tpu as pltpu

F32 = jnp.float32
BF16 = jnp.bfloat16
HIGHEST = lax.Precision.HIGHEST

D_MODEL = 1024
GRID_W = 64
EPS = 1e-6
ATT_WIDTH = 512
ATT_HEAD_DIM = 64
ATT_HEADS = 4
ROPE_BASE = 10000.0
ROPE_PAIRS = ATT_HEAD_DIM // 4
GDN_WIDTH = 512
GDN_HEAD_DIM = 128
GDN_HEADS = 4
GDN_CHUNK = 64
CONV_W = 5
IN_MAIN = 3 * ATT_WIDTH + 4 * GDN_WIDTH
N_EXPERTS = 32
TOP_K = 4
D_EXPERT = 1024
SWIGLU_ALPHA = 1.702
SWIGLU_LIMIT = 7.0
MOE_BLOCK = 256

LANES = 128
ROW_TILE = 256
VMEM_LIMIT = 48 * 1024 * 1024


def _cparams(sem):
    return pltpu.CompilerParams(dimension_semantics=sem, vmem_limit_bytes=VMEM_LIMIT)


def _dot(a, b):
    return jnp.dot(a, b, preferred_element_type=F32)


def _dot_nt(a, b):
    return lax.dot_general(a, b, (((1,), (1,)), ((), ())), preferred_element_type=F32)


def _dot_tn(a, b):
    return lax.dot_general(a, b, (((0,), (0,)), ((), ())), preferred_element_type=F32)


def _dot_hi(a, b):
    return jnp.dot(a, b, preferred_element_type=F32, precision=HIGHEST)


def _sigmoid(x):
    return 1.0 / (1.0 + jnp.exp(-x))


def _adaln_kernel(c_ref, w_ref, b_ref, o_ref):
    c = c_ref[...]
    s = c * _sigmoid(c)
    o_ref[...] = _dot_hi(s, w_ref[...]) + b_ref[...]


def _adaln(c_rows, w, b):
    n = w.shape[1]
    tn = 1024
    return pl.pallas_call(
        _adaln_kernel,
        out_shape=jax.ShapeDtypeStruct((8, n), F32),
        grid=(n // tn,),
        in_specs=[pl.BlockSpec((8, D_MODEL), lambda j: (0, 0)),
                  pl.BlockSpec((D_MODEL, tn), lambda j: (0, j)),
                  pl.BlockSpec((1, tn), lambda j: (0, j))],
        out_specs=pl.BlockSpec((8, tn), lambda j: (0, j)),
        compiler_params=_cparams(("arbitrary",)),
        name="adaln",
    )(c_rows, w, b.reshape(1, n))


def _inproj_kernel(h_ref, mod_ref, g_ref, w_ref, wab_ref, qk_ref, v_ref, gdn_ref, z_ref, ab_ref):
    x = h_ref[0]
    ms = jnp.mean(x * x, axis=-1, keepdims=True)
    y = x * lax.rsqrt(ms + EPS) * g_ref[...]
    shift = mod_ref[0, 0, 0:1, :]
    scale = mod_ref[0, 0, 1:2, :]
    u = (y * (1.0 + scale) + shift).astype(BF16)
    qk_ref[0] = _dot(u, w_ref[:, 0:2 * ATT_WIDTH])
    vv = _dot(u, w_ref[:, 2 * ATT_WIDTH:3 * ATT_WIDTH]).astype(BF16)
    for hh in range(ATT_HEADS):
        v_ref[0, hh] = vv[:, hh * LANES:(hh + 1) * LANES]
    off = 3 * ATT_WIDTH
    gdn_ref[0] = _dot(u, w_ref[:, off:off + 3 * GDN_WIDTH])
    z_ref[0] = _dot(u, w_ref[:, off + 3 * GDN_WIDTH:off + 4 * GDN_WIDTH])
    ab_ref[0] = _dot(u, wab_ref[...])


def _inproj(h, modtab, g, w_main, w_ab, n_lat_blocks):
    B, S, _ = h.shape
    tm = ROW_TILE
    row = lambda b, i: (b, i, 0)
    return pl.pallas_call(
        _inproj_kernel,
        out_shape=(jax.ShapeDtypeStruct((B, S, 2 * ATT_WIDTH), F32),
                   jax.ShapeDtypeStruct((B, ATT_HEADS, S, LANES), BF16),
                   jax.ShapeDtypeStruct((B, S, 3 * GDN_WIDTH), F32),
                   jax.ShapeDtypeStruct((B, S, GDN_WIDTH), F32),
                   jax.ShapeDtypeStruct((B, S, LANES), F32)),
        grid=(B, S // tm),
        in_specs=[pl.BlockSpec((1, tm, D_MODEL), row),
                  pl.BlockSpec((1, 1, 8, D_MODEL), lambda b, i: (b, (i >= n_lat_blocks).astype(jnp.int32), 0, 0)),
                  pl.BlockSpec((1, D_MODEL), lambda b, i: (0, 0)),
                  pl.BlockSpec((D_MODEL, IN_MAIN), lambda b, i: (0, 0)),
                  pl.BlockSpec((D_MODEL, LANES), lambda b, i: (0, 0))],
        out_specs=(pl.BlockSpec((1, tm, 2 * ATT_WIDTH), row),
                   pl.BlockSpec((1, ATT_HEADS, tm, LANES), lambda b, i: (b, 0, i, 0)),
                   pl.BlockSpec((1, tm, 3 * GDN_WIDTH), row),
                   pl.BlockSpec((1, tm, GDN_WIDTH), row),
                   pl.BlockSpec((1, tm, LANES), row)),
        compiler_params=_cparams(("parallel", "parallel")),
        name="inproj",
    )(h, modtab, g, w_main, w_ab)


def _qkprep_kernel(p_ref, cos_ref, sin_ref, gq_ref, gk_ref, qz_ref, k_ref):
    tm = p_ref.shape[1]
    lane = lax.broadcasted_iota(jnp.int32, (tm, LANES), 1)
    hi16 = (lane & 16) != 0
    first = lane < ATT_HEAD_DIM
    r = lax.broadcasted_iota(jnp.int32, (LANES, LANES), 0) >> 6
    c = lax.broadcasted_iota(jnp.int32, (LANES, LANES), 1) >> 6
    gmat = jnp.where(r == c, 1.0 / ATT_HEAD_DIM, 0.0).astype(F32)
    cosv = cos_ref[...]
    sinv = sin_ref[...]
    for j in range(2 * ATT_HEADS):
        x = p_ref[0, :, j * LANES:(j + 1) * LANES]
        ms = _dot_hi(x * x, gmat)
        g = gq_ref[...] if j < ATT_HEADS else gk_ref[...]
        y = x * lax.rsqrt(ms + EPS) * g
        sw = jnp.where(hi16, pltpu.roll(y, 16, 1), pltpu.roll(y, LANES - 16, 1))
        y = y * cosv + sw * sinv
        if j < ATT_HEADS:
            y = y * (ATT_HEAD_DIM ** -0.5)
            qz_ref[0, j, 0] = jnp.where(first, y, 0.0).astype(BF16)
            qz_ref[0, j, 1] = jnp.where(first, 0.0, y).astype(BF16)
        else:
            k_ref[0, j - ATT_HEADS] = y.astype(BF16)


def _qkprep(p_qk, cos_t, sin_t, gq, gk):
    B, S, _ = p_qk.shape
    tm = ROW_TILE
    return pl.pallas_call(
        _qkprep_kernel,
        out_shape=(jax.ShapeDtypeStruct((B, ATT_HEADS, 2, S, LANES), BF16),
                   jax.ShapeDtypeStruct((B, ATT_HEADS, S, LANES), BF16)),
        grid=(B, S // tm),
        in_specs=[pl.BlockSpec((1, tm, 2 * ATT_WIDTH), lambda b, i: (b, i, 0)),
                  pl.BlockSpec((tm, LANES), lambda b, i: (i, 0)),
                  pl.BlockSpec((tm, LANES), lambda b, i: (i, 0)),
                  pl.BlockSpec((1, LANES), lambda b, i: (0, 0)),
                  pl.BlockSpec((1, LANES), lambda b, i: (0, 0))],
        out_specs=(pl.BlockSpec((1, ATT_HEADS, 2, tm, LANES), lambda b, i: (b, 0, 0, i, 0)),
                   pl.BlockSpec((1, ATT_HEADS, tm, LANES), lambda b, i: (b, 0, i, 0))),
        compiler_params=_cparams(("parallel", "parallel")),
        name="qkprep",
    )(p_qk, cos_t, sin_t, gq, gk)


def _attn_kernel(lam_ref, qz_ref, k_ref, v_ref, g_ref, *rest, aliased):
    if aliased:
        _, o_ref, m_sc, l_sc, acc_sc = rest
    else:
        o_ref, m_sc, l_sc, acc_sc = rest
    kv = pl.program_id(3)

    @pl.when(kv == 0)
    def _():
        m_sc[...] = jnp.full(m_sc.shape, -jnp.inf, F32)
        l_sc[...] = jnp.zeros(l_sc.shape, F32)
        acc_sc[...] = jnp.zeros(acc_sc.shape, F32)

    k = k_ref[0, 0]
    v = v_ref[0, 0]
    for p in range(2):
        s = _dot_nt(qz_ref[0, 0, p], k)
        m_prev = m_sc[p]
        m_new = jnp.maximum(m_prev, jnp.max(s, axis=-1, keepdims=True))
        alpha = jnp.exp(m_prev - m_new)
        pe = jnp.exp(s - m_new)
        l_sc[p] = alpha * l_sc[p] + jnp.sum(pe, axis=-1, keepdims=True)
        acc_sc[p] = alpha * acc_sc[p] + _dot(pe.astype(BF16), v)
        m_sc[p] = m_new

    @pl.when(kv == pl.num_programs(3) - 1)
    def _():
        o = acc_sc[0] / l_sc[0] - lam_ref[0] * (acc_sc[1] / l_sc[1])
        ms = jnp.mean(o * o, axis=-1, keepdims=True)
        o_ref[0] = (o * lax.rsqrt(ms + EPS) * g_ref[...]).astype(BF16)


def _attention(lam, qz, k, v, g, *, tq, tk, q_blk0, n_q, kv_blk0, n_kv, prev_out=None):
    B, H, _, S, _ = qz.shape
    aliased = prev_out is not None
    in_specs = [pl.BlockSpec(memory_space=pltpu.SMEM),
                pl.BlockSpec((1, 1, 2, tq, LANES), lambda b, h, i, j: (b, h, 0, q_blk0 + i, 0)),
                pl.BlockSpec((1, 1, tk, LANES), lambda b, h, i, j: (b, h, kv_blk0 + j, 0)),
                pl.BlockSpec((1, 1, tk, LANES), lambda b, h, i, j: (b, h, kv_blk0 + j, 0)),
                pl.BlockSpec((1, LANES), lambda b, h, i, j: (0, 0))]
    args = [lam, qz, k, v, g]
    aliases = {}
    if aliased:
        in_specs.append(pl.BlockSpec(memory_space=pl.ANY))
        args.append(prev_out)
        aliases = {5: 0}
    return pl.pallas_call(
        functools.partial(_attn_kernel, aliased=aliased),
        out_shape=jax.ShapeDtypeStruct((B, S, ATT_WIDTH), BF16),
        grid=(B, H, n_q, n_kv),
        in_specs=in_specs,
        out_specs=pl.BlockSpec((1, tq, LANES), lambda b, h, i, j: (b, q_blk0 + i, h)),
        scratch_shapes=[pltpu.VMEM((2, tq, 1), F32), pltpu.VMEM((2, tq, 1), F32),
                        pltpu.VMEM((2, tq, LANES), F32)],
        input_output_aliases=aliases,
        compiler_params=_cparams(("parallel", "parallel", "parallel", "arbitrary")),
        name="attn_ctx" if aliased else "attn_lat",
    )(*args)


def _gdnprep_kernel(x_ref, prev_ref, next_ref, cw_ref, ab_ref, alog_ref, dtb_ref,
                    q_ref, k_ref, v_ref, gate_ref, ext_sc, *, nb_lat, nb_all):
    i = pl.program_id(1)
    tm = x_ref.shape[1]
    first = jnp.logical_or(i == 0, i == nb_lat)
    last = jnp.logical_or(i == nb_lat - 1, i == nb_all - 1)
    keep_prev = jnp.where(first, 0.0, 1.0)
    keep_next = jnp.where(last, 0.0, 1.0)
    ext_sc[0:8, :] = prev_ref[0] * keep_prev
    ext_sc[8:8 + tm, :] = x_ref[0]
    ext_sc[8 + tm:16 + tm, :] = next_ref[0] * keep_next
    acc = None
    for j in range(CONV_W):
        term = ext_sc[pl.ds(8 + j - CONV_W // 2, tm), :] * cw_ref[j:j + 1, :]
        acc = term if acc is None else acc + term
    y = acc * _sigmoid(acc)
    for part, ref in ((0, q_ref), (1, k_ref)):
        for hh in range(GDN_HEADS):
            lo = part * GDN_WIDTH + hh * GDN_HEAD_DIM
            t = y[:, lo:lo + GDN_HEAD_DIM]
            ref[0, :, hh * GDN_HEAD_DIM:(hh + 1) * GDN_HEAD_DIM] = (
                t * lax.rsqrt(jnp.sum(t * t, axis=-1, keepdims=True) + EPS))
    v_ref[0] = y[:, 2 * GDN_WIDTH:3 * GDN_WIDTH]
    ab = ab_ref[0]
    xs = ab + dtb_ref[...]
    sp = jnp.maximum(xs, 0.0) + jnp.log(1.0 + jnp.exp(-jnp.abs(xs)))
    g = -jnp.exp(alog_ref[...]) * sp
    lane = lax.broadcasted_iota(jnp.int32, ab.shape, 1)
    gate_ref[0] = jnp.where(lane < 2 * GDN_HEADS, g, _sigmoid(ab))


def _gdnprep(p_gdn, conv_w8, ab, alog_row, dtb_row, n_lat_blocks):
    B, S, W = p_gdn.shape
    tm = ROW_TILE
    nb = S // tm
    r8 = tm // 8
    row = lambda b, i: (b, i, 0)
    return pl.pallas_call(
        functools.partial(_gdnprep_kernel, nb_lat=n_lat_blocks, nb_all=nb),
        out_shape=(jax.ShapeDtypeStruct((B, S, GDN_WIDTH), F32),
                   jax.ShapeDtypeStruct((B, S, GDN_WIDTH), F32),
                   jax.ShapeDtypeStruct((B, S, GDN_WIDTH), F32),
                   jax.ShapeDtypeStruct((B, S, LANES), F32)),
        grid=(B, nb),
        in_specs=[pl.BlockSpec((1, tm, W), row),
                  pl.BlockSpec((1, 8, W), lambda b, i: (b, jnp.maximum(i * r8 - 1, 0), 0)),
                  pl.BlockSpec((1, 8, W), lambda b, i: (b, jnp.minimum((i + 1) * r8, nb * r8 - 1), 0)),
                  pl.BlockSpec((8, W), lambda b, i: (0, 0)),
                  pl.BlockSpec((1, tm, LANES), row),
                  pl.BlockSpec((1, LANES), lambda b, i: (0, 0)),
                  pl.BlockSpec((1, LANES), lambda b, i: (0, 0))],
        out_specs=(pl.BlockSpec((1, tm, GDN_WIDTH), row),
                   pl.BlockSpec((1, tm, GDN_WIDTH), row),
                   pl.BlockSpec((1, tm, GDN_WIDTH), row),
                   pl.BlockSpec((1, tm, LANES), row)),
        scratch_shapes=[pltpu.VMEM((tm + 16, W), F32)],
        compiler_params=_cparams(("parallel", "parallel")),
        name="gdnprep",
    )(p_gdn, p_gdn, p_gdn, conv_w8, ab, alog_row, dtb_row)


def _gdnchunk_kernel(q_ref, k_ref, v_ref, gate_ref, qg_ref, w_ref, kd_ref, u_ref, aq_ref, eg_ref):
    tm = q_ref.shape[1]
    nc = tm // GDN_CHUNK
    gates = gate_ref[0]
    ri = lax.broadcasted_iota(jnp.int32, (tm, tm), 0)
    ci = lax.broadcasted_iota(jnp.int32, (tm, tm), 1)
    same = (ri >> 6) == (ci >> 6)
    eye = ri == ci
    tot = _dot_hi(jnp.where(same, 1.0, 0.0).astype(F32), gates)
    for d in range(2):
        incl = jnp.logical_and(same, (ci <= ri) if d == 0 else (ci >= ri))
        strict = jnp.logical_and(incl, jnp.logical_not(eye))
        gc = _dot_hi(jnp.where(incl, 1.0, 0.0).astype(F32), gates)
        gc_t = gc.T
        for hh in range(GDN_HEADS):
            ln = d * GDN_HEADS + hh
            sl = slice(hh * GDN_HEAD_DIM, (hh + 1) * GDN_HEAD_DIM)
            gcol = gc[:, ln:ln + 1]
            grow = gc_t[ln:ln + 1, :]
            bcol = gates[:, 2 * GDN_HEADS + ln:2 * GDN_HEADS + ln + 1]
            tcol = tot[:, ln:ln + 1]
            diff = gcol - grow
            decay = jnp.where(incl, jnp.exp(jnp.where(incl, diff, 0.0)), 0.0)
            k = k_ref[0, :, sl]
            q = q_ref[0, :, sl] * (GDN_HEAD_DIM ** -0.5)
            v = v_ref[0, :, sl]
            kb = k * bcol
            k16 = k.astype(BF16)
            a = jnp.where(strict, _dot_nt(kb.astype(BF16), k16) * decay, 0.0)
            x = -a
            t = jnp.where(eye, 1.0, 0.0).astype(F32) + x
            for _ in range(5):
                x16 = x.astype(BF16)
                x = _dot(x16, x16)
                t = t + _dot(t.astype(BF16), x.astype(BF16))
            t16 = t.astype(BF16)
            egc = jnp.exp(gcol)
            u_ref[0, d, :, sl] = _dot(t16, (v * bcol).astype(BF16))
            w_ref[0, d, :, sl] = _dot(t16, (kb * egc).astype(BF16)).astype(BF16)
            aqk = jnp.where(incl, _dot_nt(q.astype(BF16), k16) * decay, 0.0)
            kd_ref[0, d, :, sl] = (k * jnp.exp(tcol - gcol)).astype(BF16)
            qg_ref[0, d, :, sl] = (q * egc).astype(BF16)
            for cc in range(nc):
                rs = slice(cc * GDN_CHUNK, (cc + 1) * GDN_CHUNK)
                aq_ref[0, d, rs, hh * GDN_CHUNK:(hh + 1) * GDN_CHUNK] = aqk[rs, rs].astype(BF16)
                eg_ref[0, d, cc, hh:hh + 1, :] = jnp.broadcast_to(
                    jnp.exp(tcol[cc * GDN_CHUNK:cc * GDN_CHUNK + 1, :]), (1, LANES))


def _gdnchunk(gq, gk, gv, gates):
    B, S, W = gq.shape
    tm = ROW_TILE
    nc = tm // GDN_CHUNK
    row = lambda b, i: (b, i, 0)
    drow = lambda b, i: (b, 0, i, 0)
    big = lambda dt: jax.ShapeDtypeStruct((B, 2, S, W), dt)
    return pl.pallas_call(
        _gdnchunk_kernel,
        out_shape=(big(BF16), big(BF16), big(BF16), big(F32),
                   jax.ShapeDtypeStruct((B, 2, S, GDN_HEADS * GDN_CHUNK), BF16),
                   jax.ShapeDtypeStruct((B, 2, S // GDN_CHUNK, GDN_HEADS, LANES), F32)),
        grid=(B, S // tm),
        in_specs=[pl.BlockSpec((1, tm, W), row), pl.BlockSpec((1, tm, W), row),
                  pl.BlockSpec((1, tm, W), row), pl.BlockSpec((1, tm, LANES), row)],
        out_specs=(pl.BlockSpec((1, 2, tm, W), drow), pl.BlockSpec((1, 2, tm, W), drow),
                   pl.BlockSpec((1, 2, tm, W), drow), pl.BlockSpec((1, 2, tm, W), drow),
                   pl.BlockSpec((1, 2, tm, GDN_HEADS * GDN_CHUNK), drow),
                   pl.BlockSpec((1, 2, nc, GDN_HEADS, LANES), lambda b, i: (b, 0, i, 0, 0))),
        compiler_params=_cparams(("parallel", "parallel")),
        name="gdnchunk",
    )(gq, gk, gv, gates)


def _gdnscan_kernel(qg0, w0, kd0, u0, aq0, eg0, qg1, w1, kd1, u1, aq1, eg1, of_ref, ob_ref, s_sc):
    @pl.when(pl.program_id(1) == 0)
    def _():
        s_sc[...] = jnp.zeros(s_sc.shape, F32)

    for d, (qg, w, kd, u, aq, eg, o_ref) in enumerate(
            ((qg0, w0, kd0, u0, aq0, eg0, of_ref), (qg1, w1, kd1, u1, aq1, eg1, ob_ref))):
        for hh in range(GDN_HEADS):
            sl = slice(hh * GDN_HEAD_DIM, (hh + 1) * GDN_HEAD_DIM)
            st = s_sc[d, hh]
            st16 = st.astype(BF16)
            vnew = u[0, 0, :, sl] - _dot(w[0, 0, :, sl], st16)
            vn16 = vnew.astype(BF16)
            o_ref[0, :, sl] = (_dot(qg[0, 0, :, sl], st16)
                               + _dot(aq[0, 0, :, hh * GDN_CHUNK:(hh + 1) * GDN_CHUNK], vn16))
            s_sc[d, hh] = st * eg[0, 0, 0, hh:hh + 1, :] + _dot_tn(kd[0, 0, :, sl], vn16)


def _gdnscan(qg, w, kd, u, aq, eg, n_lat_chunks, n_ctx_chunks):
    B, _, S, W = qg.shape
    C = GDN_CHUNK
    n = S // C

    def fwd_chunk(i):
        return jnp.where(i < n_ctx_chunks, n_lat_chunks + i, i - n_ctx_chunks)

    def bwd_chunk(i):
        return jnp.where(i < n_ctx_chunks, n_lat_chunks + n_ctx_chunks - 1 - i, n - 1 - i)

    def specs(d, chunk_of):
        big = pl.BlockSpec((1, 1, C, W), lambda b, i: (b, d, chunk_of(i), 0))
        return [big, big, big, big,
                pl.BlockSpec((1, 1, C, GDN_HEADS * C), lambda b, i: (b, d, chunk_of(i), 0)),
                pl.BlockSpec((1, 1, 1, GDN_HEADS, LANES), lambda b, i: (b, d, chunk_of(i), 0, 0))]

    return pl.pallas_call(
        _gdnscan_kernel,
        out_shape=(jax.ShapeDtypeStruct((B, S, W), F32), jax.ShapeDtypeStruct((B, S, W), F32)),
        grid=(B, n),
        in_specs=specs(0, fwd_chunk) + specs(1, bwd_chunk),
        out_specs=(pl.BlockSpec((1, C, W), lambda b, i: (b, fwd_chunk(i), 0)),
                   pl.BlockSpec((1, C, W), lambda b, i: (b, bwd_chunk(i), 0))),
        scratch_shapes=[pltpu.VMEM((2, GDN_HEADS, GDN_HEAD_DIM, GDN_HEAD_DIM), F32)],
        compiler_params=_cparams(("parallel", "arbitrary")),
        name="gdnscan",
    )(qg, w, kd, u, aq, eg, qg, w, kd, u, aq, eg)


def _mixout_kernel(oa_ref, of_ref, ob_ref, z_ref, h_ref, mod_ref, gg_ref, wo_ref, gf_ref, rw_ref, rb_ref,
                   hn_ref, v_ref, te_ref, tg_ref):
    og = of_ref[0] + ob_ref[0]
    z = z_ref[0]
    parts = [oa_ref[0]]
    for hh in range(GDN_HEADS):
        sl = slice(hh * GDN_HEAD_DIM, (hh + 1) * GDN_HEAD_DIM)
        t = og[:, sl]
        t = t * lax.rsqrt(jnp.mean(t * t, axis=-1, keepdims=True) + EPS) * gg_ref[...]
        zz = z[:, sl]
        parts.append((t * (zz * _sigmoid(zz))).astype(BF16))
    mix_in = jnp.concatenate(parts, axis=-1)
    mix = _dot(mix_in, wo_ref[...])
    hn = h_ref[0] + mod_ref[0, 0, 2:3, :] * mix
    hn_ref[0] = hn
    y = hn * lax.rsqrt(jnp.mean(hn * hn, axis=-1, keepdims=True) + EPS) * gf_ref[...]
    v = y * (1.0 + mod_ref[0, 0, 4:5, :]) + mod_ref[0, 0, 3:4, :]
    v_ref[0] = v.astype(BF16)
    logits = _dot_hi(v, rw_ref[...]) + rb_ref[...]
    lane = lax.broadcasted_iota(jnp.int32, logits.shape, 1)
    cur = logits
    vals, idxs = [], []
    for _ in range(TOP_K):
        m = jnp.max(cur, axis=-1, keepdims=True)
        idx = jnp.min(jnp.where(cur == m, lane, LANES), axis=-1, keepdims=True)
        vals.append(m)
        idxs.append(idx)
        cur = jnp.where(lane == idx, -jnp.inf, cur)
    es = [jnp.exp(vv - vals[0]) for vv in vals]
    inv = 1.0 / (es[0] + es[1] + es[2] + es[3])
    te = jnp.zeros(logits.shape, jnp.int32)
    tg = jnp.zeros(logits.shape, F32)
    for kk in range(TOP_K):
        te = jnp.where(lane == kk, idxs[kk], te)
        tg = jnp.where(lane == kk, es[kk] * inv, tg)
    te_ref[0] = te
    tg_ref[0] = tg


def _mixout(o_att, o_f, o_b, z, h, modtab, gg, w_out, gf, rw, rb, n_lat_blocks):
    B, S, _ = h.shape
    tm = ROW_TILE
    row = lambda b, i: (b, i, 0)
    const = lambda b, i: (0, 0)
    return pl.pallas_call(
        _mixout_kernel,
        out_shape=(jax.ShapeDtypeStruct((B, S, D_MODEL), F32),
                   jax.ShapeDtypeStruct((B, S, D_MODEL), BF16),
                   jax.ShapeDtypeStruct((B, S, LANES), jnp.int32),
                   jax.ShapeDtypeStruct((B, S, LANES), F32)),
        grid=(B, S // tm),
        in_specs=[pl.BlockSpec((1, tm, ATT_WIDTH), row),
                  pl.BlockSpec((1, tm, GDN_WIDTH), row),
                  pl.BlockSpec((1, tm, GDN_WIDTH), row),
                  pl.BlockSpec((1, tm, GDN_WIDTH), row),
                  pl.BlockSpec((1, tm, D_MODEL), row),
                  pl.BlockSpec((1, 1, 8, D_MODEL), lambda b, i: (b, (i >= n_lat_blocks).astype(jnp.int32), 0, 0)),
                  pl.BlockSpec((1, LANES), const),
                  pl.BlockSpec((D_MODEL, D_MODEL), const),
                  pl.BlockSpec((1, D_MODEL), const),
                  pl.BlockSpec((D_MODEL, LANES), const),
                  pl.BlockSpec((1, LANES), const)],
        out_specs=(pl.BlockSpec((1, tm, D_MODEL), row),
                   pl.BlockSpec((1, tm, D_MODEL), row),
                   pl.BlockSpec((1, tm, LANES), row),
                   pl.BlockSpec((1, tm, LANES), row)),
        compiler_params=_cparams(("parallel", "parallel")),
        name="mixout",
    )(o_att, o_f, o_b, z, h, modtab, gg, w_out, gf, rw, rb)


def _expert_kernel(be_ref, x_ref, wgu_ref, bgu_ref, wd_ref, bd_ref, sg_ref, y_ref):
    del be_ref
    gu = _dot(x_ref[...], wgu_ref[0]) + bgu_ref[0]
    g_ = jnp.minimum(gu[:, :D_EXPERT], SWIGLU_LIMIT)
    up = jnp.clip(gu[:, D_EXPERT:], -SWIGLU_LIMIT, SWIGLU_LIMIT)
    glu = g_ * _sigmoid(SWIGLU_ALPHA * g_)
    act = ((up + 1.0) * glu).astype(BF16)
    y_ref[...] = (_dot(act, wd_ref[0]) + bd_ref[0]) * sg_ref[...]


def _experts(block_e, x_sorted, wgu, bgu, wd, bd, slot_gate):
    n_slots = x_sorted.shape[0]
    nb = n_slots // MOE_BLOCK
    grid_spec = pltpu.PrefetchScalarGridSpec(
        num_scalar_prefetch=1,
        grid=(nb,),
        in_specs=[pl.BlockSpec((MOE_BLOCK, D_MODEL), lambda i, be: (i, 0)),
                  pl.BlockSpec((1, D_MODEL, 2 * D_EXPERT), lambda i, be: (be[i], 0, 0)),
                  pl.BlockSpec((1, 1, 2 * D_EXPERT), lambda i, be: (be[i], 0, 0)),
                  pl.BlockSpec((1, D_EXPERT, D_MODEL), lambda i, be: (be[i], 0, 0)),
                  pl.BlockSpec((1, 1, D_MODEL), lambda i, be: (be[i], 0, 0)),
                  pl.BlockSpec((MOE_BLOCK, 1), lambda i, be: (i, 0))],
        out_specs=pl.BlockSpec((MOE_BLOCK, D_MODEL), lambda i, be: (i, 0)),
    )
    return pl.pallas_call(
        _expert_kernel,
        out_shape=jax.ShapeDtypeStruct((n_slots, D_MODEL), F32),
        grid_spec=grid_spec,
        compiler_params=_cparams(("arbitrary",)),
        name="experts",
    )(block_e, x_sorted, wgu, bgu, wd, bd, slot_gate)


def _moe(v_flat, top_e, top_g, wgu, bgu, wd, bd):
    T = v_flat.shape[0]
    n_assign = T * TOP_K
    n_blocks = -(-n_assign // MOE_BLOCK) + N_EXPERTS
    flat_e = top_e.reshape(-1)
    order = jnp.argsort(flat_e)
    e_sorted = flat_e[order]
    tok_sorted = (order // TOP_K).astype(jnp.int32)
    gate_sorted = top_g.reshape(-1)[order]
    counts = jnp.bincount(flat_e, length=N_EXPERTS)
    padded = (counts + MOE_BLOCK - 1) // MOE_BLOCK * MOE_BLOCK
    start = jnp.cumsum(counts) - counts
    pad_end = jnp.cumsum(padded)
    pad_start = pad_end - padded
    slot = (pad_start[e_sorted] + (jnp.arange(n_assign, dtype=jnp.int32) - start[e_sorted])).astype(jnp.int32)
    n_slots = n_blocks * MOE_BLOCK
    slot_tok = jnp.full((n_slots,), T, jnp.int32).at[slot].set(tok_sorted)
    slot_gate = jnp.zeros((n_slots,), F32).at[slot].set(gate_sorted)
    block_e = jnp.minimum(jnp.searchsorted(pad_end, jnp.arange(n_blocks, dtype=jnp.int32) * MOE_BLOCK,
                                           side='right'), N_EXPERTS - 1).astype(jnp.int32)
    v_pad = jnp.concatenate([v_flat, jnp.zeros((1, D_MODEL), v_flat.dtype)], axis=0)
    x_sorted = v_pad[slot_tok]
    y_sorted = _experts(block_e, x_sorted, wgu, bgu, wd, bd, slot_gate.reshape(n_slots, 1))
    slot_of = jnp.zeros((n_assign,), jnp.int32).at[order].set(slot)
    return y_sorted[slot_of].reshape(T, TOP_K, D_MODEL).sum(axis=1)


def _pick_tile(n, cands):
    for t in cands:
        if n % t == 0:
            return t
    raise ValueError(f"no tile for {n}")


def _rope_tables(L, Lc):
    rows = L // GRID_W
    row = jnp.repeat(jnp.arange(rows, dtype=F32), GRID_W)
    col = (jnp.arange(L, dtype=jnp.int32) % GRID_W).astype(F32)
    inv_freq = ROPE_BASE ** (-jnp.arange(ROPE_PAIRS, dtype=F32) / ROPE_PAIRS)
    ar = row[:, None] * inv_freq
    ac = col[:, None] * inv_freq
    cos64 = jnp.concatenate([jnp.cos(ar), jnp.cos(ar), jnp.cos(ac), jnp.cos(ac)], axis=-1)
    sin64 = jnp.concatenate([-jnp.sin(ar), jnp.sin(ar), -jnp.sin(ac), jnp.sin(ac)], axis=-1)
    cos_t = jnp.concatenate([jnp.tile(cos64, (1, 2)), jnp.ones((Lc, LANES), F32)], axis=0)
    sin_t = jnp.concatenate([jnp.tile(sin64, (1, 2)), jnp.zeros((Lc, LANES), F32)], axis=0)
    return cos_t, sin_t


def _pad_lanes(v):
    v = v.reshape(1, -1).astype(F32)
    return jnp.pad(v, ((0, 0), (0, LANES - v.shape[1])))


def kernel(x, c, ctx, c_ctx, w_mod, b_mod, norm_mix_g, w_in, q_norm_g, k_norm_g, lam_q1, lam_k1, lam_q2, lam_k2, subln_g, conv_w, a_log, dt_bias, gdn_norm_g, w_out, norm_ffn_g, router_w, router_b, w_gate_up, b_gate_up, w_down, b_down):
    B, L, D = x.shape
    Lc = ctx.shape[1]
    S = L + Lc
    depth = w_mod.shape[0]
    tm = ROW_TILE
    n_lat_blocks = L // tm
    cos_t, sin_t = _rope_tables(L, Lc)
    tq = _pick_tile(L, (512, 256))
    tk = _pick_tile(S, (1280, 640, 256))

    c_rows = jnp.zeros((8, D), F32).at[:B].set(c).at[B].set(c_ctx)
    h = jnp.concatenate([x, ctx], axis=1)

    for layer in range(depth):
        mod = _adaln(c_rows, w_mod[layer], b_mod[layer])
        mod6 = mod.reshape(8, 6, D)
        lat_mod = mod6[:B]
        ctx_mod = jnp.broadcast_to(mod6[B][None], (B, 6, D))
        modtab = jnp.pad(jnp.stack([lat_mod, ctx_mod], axis=1), ((0, 0), (0, 0), (0, 2), (0, 0)))

        lam_init = 0.8 - 0.6 * math.exp(-0.3 * layer)
        lam_full = (jnp.exp(jnp.sum(lam_q1[layer] * lam_k1[layer]))
                    - jnp.exp(jnp.sum(lam_q2[layer] * lam_k2[layer])) + lam_init).reshape(1).astype(F32)

        w_l = w_in[layer]
        w_main = w_l[:, :IN_MAIN].astype(BF16)
        w_ab = jnp.pad(w_l[:, IN_MAIN:], ((0, 0), (0, LANES - (w_l.shape[1] - IN_MAIN)))).astype(BF16)
        p_qk, v_att, p_gdn, z, ab = _inproj(h, modtab, norm_mix_g[layer].reshape(1, D), w_main, w_ab,
                                            n_lat_blocks)

        gq = jnp.tile(q_norm_g[layer].reshape(1, ATT_HEAD_DIM), (1, 2))
        gk = jnp.tile(k_norm_g[layer].reshape(1, ATT_HEAD_DIM), (1, 2))
        qz, k_att = _qkprep(p_qk, cos_t, sin_t, gq, gk)
        g_sub = (subln_g[layer] * (1.0 - lam_init)).reshape(1, LANES).astype(F32)
        o_att = _attention(lam_full, qz, k_att, v_att, g_sub, tq=tq, tk=tk,
                           q_blk0=0, n_q=L // tq, kv_blk0=0, n_kv=S // tk)
        o_att = _attention(lam_full, qz, k_att, v_att, g_sub, tq=Lc, tk=Lc,
                           q_blk0=L // Lc, n_q=1, kv_blk0=L // Lc, n_kv=1, prev_out=o_att)

        conv_w8 = jnp.pad(conv_w[layer], ((0, 8 - CONV_W), (0, 0)))
        gq_g, gk_g, gv_g, gates = _gdnprep(p_gdn, conv_w8, ab, _pad_lanes(a_log[layer]),
                                           _pad_lanes(dt_bias[layer]), n_lat_blocks)
        qg, w_g, kd, u_g, aq, eg = _gdnchunk(gq_g, gk_g, gv_g, gates)
        o_f, o_b = _gdnscan(qg, w_g, kd, u_g, aq, eg, L // GDN_CHUNK, Lc // GDN_CHUNK)

        rw = jnp.pad(router_w[layer], ((0, 0), (0, LANES - N_EXPERTS)))
        rb = jnp.pad(router_b[layer].reshape(1, N_EXPERTS).astype(F32), ((0, 0), (0, LANES - N_EXPERTS)),
                     constant_values=-1e30)
        h_new, v_ffn, top_e, top_g = _mixout(
            o_att, o_f, o_b, z, h, modtab, gdn_norm_g[layer].reshape(1, LANES), w_out[layer].astype(BF16),
            norm_ffn_g[layer].reshape(1, D), rw, rb, n_lat_blocks)

        y = _moe(v_ffn.reshape(B * S, D), top_e.reshape(B * S, LANES)[:, :TOP_K],
                 top_g.reshape(B * S, LANES)[:, :TOP_K],
                 w_gate_up[layer].astype(BF16), b_gate_up[layer].reshape(N_EXPERTS, 1, 2 * D_EXPERT),
                 w_down[layer].astype(BF16), b_down[layer].reshape(N_EXPERTS, 1, D))
        gate_ffn = modtab[:, :, 5, :]
        gate_rows = jnp.concatenate([jnp.broadcast_to(gate_ffn[:, 0:1], (B, L, D)),
                                     jnp.broadcast_to(gate_ffn[:, 1:2], (B, Lc, D))], axis=1)
        h = h_new + gate_rows * y.reshape(B, S, D)
    return h[:, :L]
```

```python
import functools
import math

import jax
import jax.numpy as jnp
from jax import lax
from jax.experimental import pallas as pl
from jax.experimental.pallas import tpu as pltpu

F32 = jnp.float32
BF16 = jnp.bfloat16
HIGHEST = lax.Precision.HIGHEST

D_MODEL = 1024
GRID_W = 64
EPS = 1e-6
ATT_WIDTH = 512
ATT_HEAD_DIM = 64
ATT_HEADS = 4
ROPE_BASE = 10000.0
ROPE_PAIRS = ATT_HEAD_DIM // 4
GDN_WIDTH = 512
GDN_HEAD_DIM = 128
GDN_HEADS = 4
GDN_CHUNK = 64
CONV_W = 5
IN_MAIN = 3 * ATT_WIDTH + 4 * GDN_WIDTH
N_EXPERTS = 32
TOP_K = 4
D_EXPERT = 1024
SWIGLU_ALPHA = 1.702
SWIGLU_LIMIT = 7.0
MOE_BLOCK = 256

LANES = 128
ROW_TILE = 256
KV_CHUNK = 256
VMEM_LIMIT = 48 * 1024 * 1024


def _cparams(sem):
    return pltpu.CompilerParams(dimension_semantics=sem, vmem_limit_bytes=VMEM_LIMIT)


def _dot(a, b):
    return jnp.dot(a, b, preferred_element_type=F32)


def _dot_nt(a, b):
    return lax.dot_general(a, b, (((1,), (1,)), ((), ())), preferred_element_type=F32)


def _dot_tn(a, b):
    return lax.dot_general(a, b, (((0,), (0,)), ((), ())), preferred_element_type=F32)


def _dot_hi(a, b):
    return jnp.dot(a, b, preferred_element_type=F32, precision=HIGHEST)


def _sigmoid(x):
    return 1.0 / (1.0 + jnp.exp(-x))


def _adaln_kernel(c_ref, w_ref, b_ref, o_ref):
    c = c_ref[...]
    s = c * _sigmoid(c)
    o_ref[...] = _dot_hi(s, w_ref[...]) + b_ref[...]


def _adaln(c_rows, w, b):
    n = w.shape[1]
    tn = 1024
    return pl.pallas_call(
        _adaln_kernel,
        out_shape=jax.ShapeDtypeStruct((8, n), F32),
        grid=(n // tn,),
        in_specs=[pl.BlockSpec((8, D_MODEL), lambda j: (0, 0)),
                  pl.BlockSpec((D_MODEL, tn), lambda j: (0, j)),
                  pl.BlockSpec((1, tn), lambda j: (0, j))],
        out_specs=pl.BlockSpec((8, tn), lambda j: (0, j)),
        compiler_params=_cparams(("arbitrary",)),
        name="adaln",
    )(c_rows, w, b.reshape(1, n))


def _inproj_kernel(h_ref, mod_ref, g_ref, w_ref, wab_ref, qk_ref, v_ref, gdn_ref, z_ref, ab_ref):
    x = h_ref[0]
    ms = jnp.mean(x * x, axis=-1, keepdims=True)
    y = x * lax.rsqrt(ms + EPS) * g_ref[...]
    shift = mod_ref[0, 0, 0:1, :]
    scale = mod_ref[0, 0, 1:2, :]
    u = (y * (1.0 + scale) + shift).astype(BF16)
    qk_ref[0] = _dot(u, w_ref[:, 0:2 * ATT_WIDTH])
    vv = _dot(u, w_ref[:, 2 * ATT_WIDTH:3 * ATT_WIDTH])
    for hh in range(ATT_HEADS):
        v_ref[0, hh] = vv[:, hh * LANES:(hh + 1) * LANES].T.astype(BF16)
    off = 3 * ATT_WIDTH
    gdn_ref[0] = _dot(u, w_ref[:, off:off + 3 * GDN_WIDTH])
    z_ref[0] = _dot(u, w_ref[:, off + 3 * GDN_WIDTH:off + 4 * GDN_WIDTH])
    ab_ref[0] = _dot(u, wab_ref[...])


def _inproj(h, modtab, g, w_main, w_ab, n_lat_blocks):
    B, S, _ = h.shape
    tm = ROW_TILE
    row = lambda b, i: (b, i, 0)
    return pl.pallas_call(
        _inproj_kernel,
        out_shape=(jax.ShapeDtypeStruct((B, S, 2 * ATT_WIDTH), F32),
                   jax.ShapeDtypeStruct((B, ATT_HEADS, LANES, S), BF16),
                   jax.ShapeDtypeStruct((B, S, 3 * GDN_WIDTH), F32),
                   jax.ShapeDtypeStruct((B, S, GDN_WIDTH), F32),
                   jax.ShapeDtypeStruct((B, S, LANES), F32)),
        grid=(B, S // tm),
        in_specs=[pl.BlockSpec((1, tm, D_MODEL), row),
                  pl.BlockSpec((1, 1, 8, D_MODEL), lambda b, i: (b, (i >= n_lat_blocks).astype(jnp.int32), 0, 0)),
                  pl.BlockSpec((1, D_MODEL), lambda b, i: (0, 0)),
                  pl.BlockSpec((D_MODEL, IN_MAIN), lambda b, i: (0, 0)),
                  pl.BlockSpec((D_MODEL, LANES), lambda b, i: (0, 0))],
        out_specs=(pl.BlockSpec((1, tm, 2 * ATT_WIDTH), row),
                   pl.BlockSpec((1, ATT_HEADS, LANES, tm), lambda b, i: (b, 0, 0, i)),
                   pl.BlockSpec((1, tm, 3 * GDN_WIDTH), row),
                   pl.BlockSpec((1, tm, GDN_WIDTH), row),
                   pl.BlockSpec((1, tm, LANES), row)),
        compiler_params=_cparams(("parallel", "parallel")),
        name="inproj",
    )(h, modtab, g, w_main, w_ab)


def _qkprep_kernel(p_ref, cos_ref, sin_ref, gq_ref, gk_ref, qz_ref, k_ref):
    tm = p_ref.shape[1]
    lane = lax.broadcasted_iota(jnp.int32, (tm, LANES), 1)
    hi16 = (lane & 16) != 0
    first = lax.broadcasted_iota(jnp.int32, (LANES, tm), 0) < ATT_HEAD_DIM
    r = lax.broadcasted_iota(jnp.int32, (LANES, LANES), 0) >> 6
    c = lax.broadcasted_iota(jnp.int32, (LANES, LANES), 1) >> 6
    gmat = jnp.where(r == c, 1.0 / ATT_HEAD_DIM, 0.0).astype(F32)
    cosv = cos_ref[...]
    sinv = sin_ref[...]
    for j in range(2 * ATT_HEADS):
        x = p_ref[0, :, j * LANES:(j + 1) * LANES]
        ms = _dot_hi(x * x, gmat)
        g = gq_ref[...] if j < ATT_HEADS else gk_ref[...]
        y = x * lax.rsqrt(ms + EPS) * g
        sw = jnp.where(hi16, pltpu.roll(y, 16, 1), pltpu.roll(y, LANES - 16, 1))
        y = y * cosv + sw * sinv
        if j < ATT_HEADS:
            yt = (y * (ATT_HEAD_DIM ** -0.5)).T
            qz_ref[0, j, 0] = jnp.where(first, yt, 0.0).astype(BF16)
            qz_ref[0, j, 1] = jnp.where(first, 0.0, yt).astype(BF16)
        else:
            k_ref[0, j - ATT_HEADS] = y.astype(BF16)


def _qkprep(p_qk, cos_t, sin_t, gq, gk):
    B, S, _ = p_qk.shape
    tm = ROW_TILE
    return pl.pallas_call(
        _qkprep_kernel,
        out_shape=(jax.ShapeDtypeStruct((B, ATT_HEADS, 2, LANES, S), BF16),
                   jax.ShapeDtypeStruct((B, ATT_HEADS, S, LANES), BF16)),
        grid=(B, S // tm),
        in_specs=[pl.BlockSpec((1, tm, 2 * ATT_WIDTH), lambda b, i: (b, i, 0)),
                  pl.BlockSpec((tm, LANES), lambda b, i: (i, 0)),
                  pl.BlockSpec((tm, LANES), lambda b, i: (i, 0)),
                  pl.BlockSpec((1, LANES), lambda b, i: (0, 0)),
                  pl.BlockSpec((1, LANES), lambda b, i: (0, 0))],
        out_specs=(pl.BlockSpec((1, ATT_HEADS, 2, LANES, tm), lambda b, i: (b, 0, 0, 0, i)),
                   pl.BlockSpec((1, ATT_HEADS, tm, LANES), lambda b, i: (b, 0, i, 0))),
        compiler_params=_cparams(("parallel", "parallel")),
        name="qkprep",
    )(p_qk, cos_t, sin_t, gq, gk)


def _attn_kernel(lam_ref, qz_ref, k_ref, v_ref, g_ref, *rest, aliased, tk, n_kv):
    if aliased:
        rest = rest[1:]
    o_ref, s_sc, mt_sc, m_sc, l_sc, acc_sc = rest
    m_sc[...] = jnp.full(m_sc.shape, -jnp.inf, F32)
    l_sc[...] = jnp.zeros(l_sc.shape, F32)
    acc_sc[...] = jnp.zeros(acc_sc.shape, F32)

    def rows(j):
        if isinstance(j, int):
            return pl.ds(j * tk, tk)
        return pl.ds(pl.multiple_of(j * tk, tk), tk)

    def qk_tile(j, slot):
        kt = k_ref[0, 0, rows(j), :]
        for p in range(2):
            s = _dot(kt, qz_ref[0, 0, p])
            s_sc[slot, p] = s
            mt_sc[slot, p] = jnp.max(s, axis=0, keepdims=True)

    def pv_tile(j, slot):
        for p in range(2):
            m_prev = m_sc[p]
            m_new = jnp.maximum(m_prev, mt_sc[slot, p])
            alpha = jnp.exp(m_prev - m_new)
            lsum = None
            acc = None
            for c in range(tk // KV_CHUNK):
                cs = slice(c * KV_CHUNK, (c + 1) * KV_CHUNK)
                pe = jnp.exp(s_sc[slot, p, cs, :] - m_new)
                ps = jnp.sum(pe, axis=0, keepdims=True)
                pv = _dot(v_ref[0, 0, j, :, cs], pe.astype(BF16))
                lsum = ps if lsum is None else lsum + ps
                acc = pv if acc is None else acc + pv
            l_sc[p] = alpha * l_sc[p] + lsum
            acc_sc[p] = alpha * acc_sc[p] + acc
            m_sc[p] = m_new

    qk_tile(0, 0)
    n_pairs = (n_kv - 1) // 2

    def pair(jj, carry):
        j = 2 * jj
        qk_tile(j + 1, 1)
        pv_tile(j, 0)
        qk_tile(j + 2, 0)
        pv_tile(j + 1, 1)
        return carry

    if n_pairs > 0:
        lax.fori_loop(0, n_pairs, pair, 0)
    j = 2 * n_pairs
    if j == n_kv - 1:
        pv_tile(j, 0)
    else:
        qk_tile(j + 1, 1)
        pv_tile(j, 0)
        pv_tile(j + 1, 1)

    o = acc_sc[0] / l_sc[0] - lam_ref[0] * (acc_sc[1] / l_sc[1])
    ms = jnp.mean(o * o, axis=0, keepdims=True)
    o_ref[0] = ((o * lax.rsqrt(ms + EPS)).T * g_ref[...]).astype(BF16)


def _attention(lam, qz, k, vt, g, *, tq, tk, q_blk0, n_q, kv_blk0, n_kv, prev_out=None):
    B, H, _, _, S = qz.shape
    aliased = prev_out is not None
    skv = tk * n_kv
    in_specs = [pl.BlockSpec(memory_space=pltpu.SMEM),
                pl.BlockSpec((1, 1, 2, LANES, tq), lambda b, h, i: (b, h, 0, 0, q_blk0 + i)),
                pl.BlockSpec((1, 1, skv, LANES), lambda b, h, i: (b, h, kv_blk0, 0)),
                pl.BlockSpec((1, 1, n_kv, LANES, tk), lambda b, h, i: (b, h, 0, 0, 0)),
                pl.BlockSpec((1, LANES), lambda b, h, i: (0, 0))]
    args = [lam, qz, k, vt, g]
    aliases = {}
    if aliased:
        in_specs.append(pl.BlockSpec(memory_space=pl.ANY))
        args.append(prev_out)
        aliases = {5: 0}
    return pl.pallas_call(
        functools.partial(_attn_kernel, aliased=aliased, tk=tk, n_kv=n_kv),
        out_shape=jax.ShapeDtypeStruct((B, S, ATT_WIDTH), BF16),
        grid=(B, H, n_q),
        in_specs=in_specs,
        out_specs=pl.BlockSpec((1, tq, LANES), lambda b, h, i: (b, q_blk0 + i, h)),
        scratch_shapes=[pltpu.VMEM((2, 2, tk, tq), F32), pltpu.VMEM((2, 2, 1, tq), F32),
                        pltpu.VMEM((2, 1, tq), F32), pltpu.VMEM((2, 1, tq), F32),
                        pltpu.VMEM((2, LANES, tq), F32)],
        input_output_aliases=aliases,
        compiler_params=_cparams(("parallel", "parallel", "arbitrary")),
        name="attn_ctx" if aliased else "attn_lat",
    )(*args)


def _attention_lat(lam, qz, k, vt, g, L):
    B, H, _, S = vt.shape
    tq = _pick_tile(L, (512, 256))
    tk = _pick_tile(S, (1280, 640, 256))
    n_kv = S // tk
    vt_tiles = vt.reshape(B, H, LANES, n_kv, tk).transpose(0, 1, 3, 2, 4)
    return _attention(lam, qz, k, vt_tiles, g, tq=tq, tk=tk, q_blk0=0, n_q=L // tq, kv_blk0=0, n_kv=n_kv)


def _attention_ctx(lam, qz, k, vt, g, L, prev_out):
    B, H, _, S = vt.shape
    Lc = S - L
    vt_ctx = vt[:, :, :, L:].reshape(B, H, 1, LANES, Lc)
    return _attention(lam, qz, k, vt_ctx, g, tq=Lc, tk=Lc, q_blk0=L // Lc, n_q=1, kv_blk0=L // Lc, n_kv=1,
                      prev_out=prev_out)


def _gdnprep_kernel(x_ref, prev_ref, next_ref, cw_ref, ab_ref, alog_ref, dtb_ref,
                    q_ref, k_ref, v_ref, gate_ref, ext_sc, *, nb_lat, nb_all):
    i = pl.program_id(1)
    tm = x_ref.shape[1]
    first = jnp.logical_or(i == 0, i == nb_lat)
    last = jnp.logical_or(i == nb_lat - 1, i == nb_all - 1)
    keep_prev = jnp.where(first, 0.0, 1.0)
    keep_next = jnp.where(last, 0.0, 1.0)
    ext_sc[0:8, :] = prev_ref[0] * keep_prev
    ext_sc[8:8 + tm, :] = x_ref[0]
    ext_sc[8 + tm:16 + tm, :] = next_ref[0] * keep_next
    acc = None
    for j in range(CONV_W):
        term = ext_sc[pl.ds(8 + j - CONV_W // 2, tm), :] * cw_ref[j:j + 1, :]
        acc = term if acc is None else acc + term
    y = acc * _sigmoid(acc)
    for part, ref in ((0, q_ref), (1, k_ref)):
        for hh in range(GDN_HEADS):
            lo = part * GDN_WIDTH + hh * GDN_HEAD_DIM
            t = y[:, lo:lo + GDN_HEAD_DIM]
            ref[0, :, hh * GDN_HEAD_DIM:(hh + 1) * GDN_HEAD_DIM] = (
                t * lax.rsqrt(jnp.sum(t * t, axis=-1, keepdims=True) + EPS))
    v_ref[0] = y[:, 2 * GDN_WIDTH:3 * GDN_WIDTH]
    ab = ab_ref[0]
    xs = ab + dtb_ref[...]
    sp = jnp.maximum(xs, 0.0) + jnp.log(1.0 + jnp.exp(-jnp.abs(xs)))
    g = -jnp.exp(alog_ref[...]) * sp
    lane = lax.broadcasted_iota(jnp.int32, ab.shape, 1)
    gate_ref[0] = jnp.where(lane < 2 * GDN_HEADS, g, _sigmoid(ab))


def _gdnprep(p_gdn, conv_w8, ab, alog_row, dtb_row, n_lat_blocks):
    B, S, W = p_gdn.shape
    tm = ROW_TILE
    nb = S // tm
    r8 = tm // 8
    row = lambda b, i: (b, i, 0)
    return pl.pallas_call(
        functools.partial(_gdnprep_kernel, nb_lat=n_lat_blocks, nb_all=nb),
        out_shape=(jax.ShapeDtypeStruct((B, S, GDN_WIDTH), F32),
                   jax.ShapeDtypeStruct((B, S, GDN_WIDTH), F32),
                   jax.ShapeDtypeStruct((B, S, GDN_WIDTH), F32),
                   jax.ShapeDtypeStruct((B, S, LANES), F32)),
        grid=(B, nb),
        in_specs=[pl.BlockSpec((1, tm, W), row),
                  pl.BlockSpec((1, 8, W), lambda b, i: (b, jnp.maximum(i * r8 - 1, 0), 0)),
                  pl.BlockSpec((1, 8, W), lambda b, i: (b, jnp.minimum((i + 1) * r8, nb * r8 - 1), 0)),
                  pl.BlockSpec((8, W), lambda b, i: (0, 0)),
                  pl.BlockSpec((1, tm, LANES), row),
                  pl.BlockSpec((1, LANES), lambda b, i: (0, 0)),
                  pl.BlockSpec((1, LANES), lambda b, i: (0, 0))],
        out_specs=(pl.BlockSpec((1, tm, GDN_WIDTH), row),
                   pl.BlockSpec((1, tm, GDN_WIDTH), row),
                   pl.BlockSpec((1, tm, GDN_WIDTH), row),
                   pl.BlockSpec((1, tm, LANES), row)),
        scratch_shapes=[pltpu.VMEM((tm + 16, W), F32)],
        compiler_params=_cparams(("parallel", "parallel")),
        name="gdnprep",
    )(p_gdn, p_gdn, p_gdn, conv_w8, ab, alog_row, dtb_row)


def _gdnchunk_kernel(q_ref, k_ref, v_ref, gate_ref, qg_ref, w_ref, kd_ref, u_ref, aq_ref, eg_ref):
    tm = q_ref.shape[1]
    nc = tm // GDN_CHUNK
    gates = gate_ref[0]
    ri = lax.broadcasted_iota(jnp.int32, (tm, tm), 0)
    ci = lax.broadcasted_iota(jnp.int32, (tm, tm), 1)
    same = (ri >> 6) == (ci >> 6)
    eye = ri == ci
    tot = _dot_hi(jnp.where(same, 1.0, 0.0).astype(F32), gates)
    for d in range(2):
        incl = jnp.logical_and(same, (ci <= ri) if d == 0 else (ci >= ri))
        strict = jnp.logical_and(incl, jnp.logical_not(eye))
        gc = _dot_hi(jnp.where(incl, 1.0, 0.0).astype(F32), gates)
        gc_t = gc.T
        for hh in range(GDN_HEADS):
            ln = d * GDN_HEADS + hh
            sl = slice(hh * GDN_HEAD_DIM, (hh + 1) * GDN_HEAD_DIM)
            gcol = gc[:, ln:ln + 1]
            grow = gc_t[ln:ln + 1, :]
            bcol = gates[:, 2 * GDN_HEADS + ln:2 * GDN_HEADS + ln + 1]
            tcol = tot[:, ln:ln + 1]
            diff = gcol - grow
            decay = jnp.where(incl, jnp.exp(jnp.where(incl, diff, 0.0)), 0.0)
            k = k_ref[0, :, sl]
            q = q_ref[0, :, sl] * (GDN_HEAD_DIM ** -0.5)
            v = v_ref[0, :, sl]
            kb = k * bcol
            k16 = k.astype(BF16)
            a = jnp.where(strict, _dot_nt(kb.astype(BF16), k16) * decay, 0.0)
            x = -a
            t = jnp.where(eye, 1.0, 0.0).astype(F32) + x
            for _ in range(5):
                x16 = x.astype(BF16)
                x = _dot(x16, x16)
                t = t + _dot(t.astype(BF16), x.astype(BF16))
            t16 = t.astype(BF16)
            egc = jnp.exp(gcol)
            u_ref[0, d, :, sl] = _dot(t16, (v * bcol).astype(BF16))
            w_ref[0, d, :, sl] = _dot(t16, (kb * egc).astype(BF16)).astype(BF16)
            aqk = jnp.where(incl, _dot_nt(q.astype(BF16), k16) * decay, 0.0)
            kd_ref[0, d, :, sl] = (k * jnp.exp(tcol - gcol)).astype(BF16)
            qg_ref[0, d, :, sl] = (q * egc).astype(BF16)
            for cc in range(nc):
                rs = slice(cc * GDN_CHUNK, (cc + 1) * GDN_CHUNK)
                aq_ref[0, d, rs, hh * GDN_CHUNK:(hh + 1) * GDN_CHUNK] = aqk[rs, rs].astype(BF16)
                eg_ref[0, d, cc, hh:hh + 1, :] = jnp.broadcast_to(
                    jnp.exp(tcol[cc * GDN_CHUNK:cc * GDN_CHUNK + 1, :]), (1, LANES))


def _gdnchunk(gq, gk, gv, gates):
    B, S, W = gq.shape
    tm = ROW_TILE
    nc = tm // GDN_CHUNK
    row = lambda b, i: (b, i, 0)
    drow = lambda b, i: (b, 0, i, 0)
    big = lambda dt: jax.ShapeDtypeStruct((B, 2, S, W), dt)
    return pl.pallas_call(
        _gdnchunk_kernel,
        out_shape=(big(BF16), big(BF16), big(BF16), big(F32),
                   jax.ShapeDtypeStruct((B, 2, S, GDN_HEADS * GDN_CHUNK), BF16),
                   jax.ShapeDtypeStruct((B, 2, S // GDN_CHUNK, GDN_HEADS, LANES), F32)),
        grid=(B, S // tm),
        in_specs=[pl.BlockSpec((1, tm, W), row), pl.BlockSpec((1, tm, W), row),
                  pl.BlockSpec((1, tm, W), row), pl.BlockSpec((1, tm, LANES), row)],
        out_specs=(pl.BlockSpec((1, 2, tm, W), drow), pl.BlockSpec((1, 2, tm, W), drow),
                   pl.BlockSpec((1, 2, tm, W), drow), pl.BlockSpec((1, 2, tm, W), drow),
                   pl.BlockSpec((1, 2, tm, GDN_HEADS * GDN_CHUNK), drow),
                   pl.BlockSpec((1, 2, nc, GDN_HEADS, LANES), lambda b, i: (b, 0, i, 0, 0))),
        compiler_params=_cparams(("parallel", "parallel")),
        name="gdnchunk",
    )(gq, gk, gv, gates)


def _gdnscan_kernel(qg0, w0, kd0, u0, aq0, eg0, qg1, w1, kd1, u1, aq1, eg1, of_ref, ob_ref, s_sc):
    @pl.when(pl.program_id(1) == 0)
    def _():
        s_sc[...] = jnp.zeros(s_sc.shape, F32)

    for d, (qg, w, kd, u, aq, eg, o_ref) in enumerate(
            ((qg0, w0, kd0, u0, aq0, eg0, of_ref), (qg1, w1, kd1, u1, aq1, eg1, ob_ref))):
        for hh in range(GDN_HEADS):
            sl = slice(hh * GDN_HEAD_DIM, (hh + 1) * GDN_HEAD_DIM)
            st = s_sc[d, hh]
            st16 = st.astype(BF16)
            vnew = u[0, 0, :, sl] - _dot(w[0, 0, :, sl], st16)
            vn16 = vnew.astype(BF16)
            o_ref[0, :, sl] = (_dot(qg[0, 0, :, sl], st16)
                               + _dot(aq[0, 0, :, hh * GDN_CHUNK:(hh + 1) * GDN_CHUNK], vn16))
            s_sc[d, hh] = st * eg[0, 0, 0, hh:hh + 1, :] + _dot_tn(kd[0, 0, :, sl], vn16)


def _gdnscan(qg, w, kd, u, aq, eg, n_lat_chunks, n_ctx_chunks):
    B, _, S, W = qg.shape
    C = GDN_CHUNK
    n = S // C

    def fwd_chunk(i):
        return jnp.where(i < n_ctx_chunks, n_lat_chunks + i, i - n_ctx_chunks)

    def bwd_chunk(i):
        return jnp.where(i < n_ctx_chunks, n_lat_chunks + n_ctx_chunks - 1 - i, n - 1 - i)

    def specs(d, chunk_of):
        big = pl.BlockSpec((1, 1, C, W), lambda b, i: (b, d, chunk_of(i), 0))
        return [big, big, big, big,
                pl.BlockSpec((1, 1, C, GDN_HEADS * C), lambda b, i: (b, d, chunk_of(i), 0)),
                pl.BlockSpec((1, 1, 1, GDN_HEADS, LANES), lambda b, i: (b, d, chunk_of(i), 0, 0))]

    return pl.pallas_call(
        _gdnscan_kernel,
        out_shape=(jax.ShapeDtypeStruct((B, S, W), F32), jax.ShapeDtypeStruct((B, S, W), F32)),
        grid=(B, n),
        in_specs=specs(0, fwd_chunk) + specs(1, bwd_chunk),
        out_specs=(pl.BlockSpec((1, C, W), lambda b, i: (b, fwd_chunk(i), 0)),
                   pl.BlockSpec((1, C, W), lambda b, i: (b, bwd_chunk(i), 0))),
        scratch_shapes=[pltpu.VMEM((2, GDN_HEADS, GDN_HEAD_DIM, GDN_HEAD_DIM), F32)],
        compiler_params=_cparams(("parallel", "arbitrary")),
        name="gdnscan",
    )(qg, w, kd, u, aq, eg, qg, w, kd, u, aq, eg)


def _mixout_kernel(oa_ref, of_ref, ob_ref, z_ref, h_ref, mod_ref, gg_ref, wo_ref, gf_ref, rw_ref, rb_ref,
                   hn_ref, v_ref, te_ref, tg_ref):
    og = of_ref[0] + ob_ref[0]
    z = z_ref[0]
    parts = [oa_ref[0]]
    for hh in range(GDN_HEADS):
        sl = slice(hh * GDN_HEAD_DIM, (hh + 1) * GDN_HEAD_DIM)
        t = og[:, sl]
        t = t * lax.rsqrt(jnp.mean(t * t, axis=-1, keepdims=True) + EPS) * gg_ref[...]
        zz = z[:, sl]
        parts.append((t * (zz * _sigmoid(zz))).astype(BF16))
    mix_in = jnp.concatenate(parts, axis=-1)
    mix = _dot(mix_in, wo_ref[...])
    hn = h_ref[0] + mod_ref[0, 0, 2:3, :] * mix
    hn_ref[0] = hn
    y = hn * lax.rsqrt(jnp.mean(hn * hn, axis=-1, keepdims=True) + EPS) * gf_ref[...]
    v = y * (1.0 + mod_ref[0, 0, 4:5, :]) + mod_ref[0, 0, 3:4, :]
    v_ref[0] = v.astype(BF16)
    logits = _dot_hi(v, rw_ref[...]) + rb_ref[...]
    lane = lax.broadcasted_iota(jnp.int32, logits.shape, 1)
    cur = logits
    vals, idxs = [], []
    for _ in range(TOP_K):
        m = jnp.max(cur, axis=-1, keepdims=True)
        idx = jnp.min(jnp.where(cur == m, lane, LANES), axis=-1, keepdims=True)
        vals.append(m)
        idxs.append(idx)
        cur = jnp.where(lane == idx, -jnp.inf, cur)
    es = [jnp.exp(vv - vals[0]) for vv in vals]
    inv = 1.0 / (es[0] + es[1] + es[2] + es[3])
    te = jnp.zeros(logits.shape, jnp.int32)
    tg = jnp.zeros(logits.shape, F32)
    for kk in range(TOP_K):
        te = jnp.where(lane == kk, idxs[kk], te)
        tg = jnp.where(lane == kk, es[kk] * inv, tg)
    te_ref[0] = te
    tg_ref[0] = tg


def _mixout(o_att, o_f, o_b, z, h, modtab, gg, w_out, gf, rw, rb, n_lat_blocks):
    B, S, _ = h.shape
    tm = ROW_TILE
    row = lambda b, i: (b, i, 0)
    const = lambda b, i: (0, 0)
    return pl.pallas_call(
        _mixout_kernel,
        out_shape=(jax.ShapeDtypeStruct((B, S, D_MODEL), F32),
                   jax.ShapeDtypeStruct((B, S, D_MODEL), BF16),
                   jax.ShapeDtypeStruct((B, S, LANES), jnp.int32),
                   jax.ShapeDtypeStruct((B, S, LANES), F32)),
        grid=(B, S // tm),
        in_specs=[pl.BlockSpec((1, tm, ATT_WIDTH), row),
                  pl.BlockSpec((1, tm, GDN_WIDTH), row),
                  pl.BlockSpec((1, tm, GDN_WIDTH), row),
                  pl.BlockSpec((1, tm, GDN_WIDTH), row),
                  pl.BlockSpec((1, tm, D_MODEL), row),
                  pl.BlockSpec((1, 1, 8, D_MODEL), lambda b, i: (b, (i >= n_lat_blocks).astype(jnp.int32), 0, 0)),
                  pl.BlockSpec((1, LANES), const),
                  pl.BlockSpec((D_MODEL, D_MODEL), const),
                  pl.BlockSpec((1, D_MODEL), const),
                  pl.BlockSpec((D_MODEL, LANES), const),
                  pl.BlockSpec((1, LANES), const)],
        out_specs=(pl.BlockSpec((1, tm, D_MODEL), row),
                   pl.BlockSpec((1, tm, D_MODEL), row),
                   pl.BlockSpec((1, tm, LANES), row),
                   pl.BlockSpec((1, tm, LANES), row)),
        compiler_params=_cparams(("parallel", "parallel")),
        name="mixout",
    )(o_att, o_f, o_b, z, h, modtab, gg, w_out, gf, rw, rb)


def _expert_kernel(be_ref, x_ref, wgu_ref, bgu_ref, wd_ref, bd_ref, sg_ref, y_ref):
    del be_ref
    gu = _dot(x_ref[...], wgu_ref[0]) + bgu_ref[0]
    g_ = jnp.minimum(gu[:, :D_EXPERT], SWIGLU_LIMIT)
    up = jnp.clip(gu[:, D_EXPERT:], -SWIGLU_LIMIT, SWIGLU_LIMIT)
    glu = g_ * _sigmoid(SWIGLU_ALPHA * g_)
    act = ((up + 1.0) * glu).astype(BF16)
    y_ref[...] = (_dot(act, wd_ref[0]) + bd_ref[0]) * sg_ref[...]


def _experts(block_e, x_sorted, wgu, bgu, wd, bd, slot_gate):
    n_slots = x_sorted.shape[0]
    nb = n_slots // MOE_BLOCK
    grid_spec = pltpu.PrefetchScalarGridSpec(
        num_scalar_prefetch=1,
        grid=(nb,),
        in_specs=[pl.BlockSpec((MOE_BLOCK, D_MODEL), lambda i, be: (i, 0)),
                  pl.BlockSpec((1, D_MODEL, 2 * D_EXPERT), lambda i, be: (be[i], 0, 0)),
                  pl.BlockSpec((1, 1, 2 * D_EXPERT), lambda i, be: (be[i], 0, 0)),
                  pl.BlockSpec((1, D_EXPERT, D_MODEL), lambda i, be: (be[i], 0, 0)),
                  pl.BlockSpec((1, 1, D_MODEL), lambda i, be: (be[i], 0, 0)),
                  pl.BlockSpec((MOE_BLOCK, 1), lambda i, be: (i, 0))],
        out_specs=pl.BlockSpec((MOE_BLOCK, D_MODEL), lambda i, be: (i, 0)),
    )
    return pl.pallas_call(
        _expert_kernel,
        out_shape=jax.ShapeDtypeStruct((n_slots, D_MODEL), F32),
        grid_spec=grid_spec,
        compiler_params=_cparams(("arbitrary",)),
        name="experts",
    )(block_e, x_sorted, wgu, bgu, wd, bd, slot_gate)


def _moe(v_flat, top_e, top_g, wgu, bgu, wd, bd):
    T = v_flat.shape[0]
    n_assign = T * TOP_K
    n_blocks = -(-n_assign // MOE_BLOCK) + N_EXPERTS
    flat_e = top_e.reshape(-1)
    order = jnp.argsort(flat_e)
    e_sorted = flat_e[order]
    tok_sorted = (order // TOP_K).astype(jnp.int32)
    gate_sorted = top_g.reshape(-1)[order]
    counts = jnp.bincount(flat_e, length=N_EXPERTS)
    padded = (counts + MOE_BLOCK - 1) // MOE_BLOCK * MOE_BLOCK
    start = jnp.cumsum(counts) - counts
    pad_end = jnp.cumsum(padded)
    pad_start = pad_end - padded
    slot = (pad_start[e_sorted] + (jnp.arange(n_assign, dtype=jnp.int32) - start[e_sorted])).astype(jnp.int32)
    n_slots = n_blocks * MOE_BLOCK
    slot_tok = jnp.full((n_slots,), T, jnp.int32).at[slot].set(tok_sorted)
    slot_gate = jnp.zeros((n_slots,), F32).at[slot].set(gate_sorted)
    block_e = jnp.minimum(jnp.searchsorted(pad_end, jnp.arange(n_blocks, dtype=jnp.int32) * MOE_BLOCK,
                                           side='right'), N_EXPERTS - 1).astype(jnp.int32)
    v_pad = jnp.concatenate([v_flat, jnp.zeros((1, D_MODEL), v_flat.dtype)], axis=0)
    x_sorted = v_pad[slot_tok]
    y_sorted = _experts(block_e, x_sorted, wgu, bgu, wd, bd, slot_gate.reshape(n_slots, 1))
    slot_of = jnp.zeros((n_assign,), jnp.int32).at[order].set(slot)
    return y_sorted[slot_of].reshape(T, TOP_K, D_MODEL).sum(axis=1)


def _pick_tile(n, cands):
    for t in cands:
        if n % t == 0:
            return t
    raise ValueError(f"no tile for {n}")


def _rope_tables(L, Lc):
    rows = L // GRID_W
    row = jnp.repeat(jnp.arange(rows, dtype=F32), GRID_W)
    col = (jnp.arange(L, dtype=jnp.int32) % GRID_W).astype(F32)
    inv_freq = ROPE_BASE ** (-jnp.arange(ROPE_PAIRS, dtype=F32) / ROPE_PAIRS)
    ar = row[:, None] * inv_freq
    ac = col[:, None] * inv_freq
    cos64 = jnp.concatenate([jnp.cos(ar), jnp.cos(ar), jnp.cos(ac), jnp.cos(ac)], axis=-1)
    sin64 = jnp.concatenate([-jnp.sin(ar), jnp.sin(ar), -jnp.sin(ac), jnp.sin(ac)], axis=-1)
    cos_t = jnp.concatenate([jnp.tile(cos64, (1, 2)), jnp.ones((Lc, LANES), F32)], axis=0)
    sin_t = jnp.concatenate([jnp.tile(sin64, (1, 2)), jnp.zeros((Lc, LANES), F32)], axis=0)
    return cos_t, sin_t


def _pad_lanes(v):
    v = v.reshape(1, -1).astype(F32)
    return jnp.pad(v, ((0, 0), (0, LANES - v.shape[1])))


def kernel(x, c, ctx, c_ctx, w_mod, b_mod, norm_mix_g, w_in, q_norm_g, k_norm_g, lam_q1, lam_k1, lam_q2, lam_k2, subln_g, conv_w, a_log, dt_bias, gdn_norm_g, w_out, norm_ffn_g, router_w, router_b, w_gate_up, b_gate_up, w_down, b_down):
    B, L, D = x.shape
    Lc = ctx.shape[1]
    S = L + Lc
    depth = w_mod.shape[0]
    tm = ROW_TILE
    n_lat_blocks = L // tm
    cos_t, sin_t = _rope_tables(L, Lc)

    c_rows = jnp.zeros((8, D), F32).at[:B].set(c).at[B].set(c_ctx)
    h = jnp.concatenate([x, ctx], axis=1)

    for layer in range(depth):
        mod = _adaln(c_rows, w_mod[layer], b_mod[layer])
        mod6 = mod.reshape(8, 6, D)
        lat_mod = mod6[:B]
        ctx_mod = jnp.broadcast_to(mod6[B][None], (B, 6, D))
        modtab = jnp.pad(jnp.stack([lat_mod, ctx_mod], axis=1), ((0, 0), (0, 0), (0, 2), (0, 0)))

        lam_init = 0.8 - 0.6 * math.exp(-0.3 * layer)
        lam_full = (jnp.exp(jnp.sum(lam_q1[layer] * lam_k1[layer]))
                    - jnp.exp(jnp.sum(lam_q2[layer] * lam_k2[layer])) + lam_init).reshape(1).astype(F32)

        w_l = w_in[layer]
        w_main = w_l[:, :IN_MAIN].astype(BF16)
        w_ab = jnp.pad(w_l[:, IN_MAIN:], ((0, 0), (0, LANES - (w_l.shape[1] - IN_MAIN)))).astype(BF16)
        p_qk, v_att, p_gdn, z, ab = _inproj(h, modtab, norm_mix_g[layer].reshape(1, D), w_main, w_ab,
                                            n_lat_blocks)

        gq = jnp.tile(q_norm_g[layer].reshape(1, ATT_HEAD_DIM), (1, 2))
        gk = jnp.tile(k_norm_g[layer].reshape(1, ATT_HEAD_DIM), (1, 2))
        qz, k_att = _qkprep(p_qk, cos_t, sin_t, gq, gk)
        g_sub = (subln_g[layer] * (1.0 - lam_init)).reshape(1, LANES).astype(F32)
        o_att = _attention_lat(lam_full, qz, k_att, v_att, g_sub, L)
        o_att = _attention_ctx(lam_full, qz, k_att, v_att, g_sub, L, o_att)

        conv_w8 = jnp.pad(conv_w[layer], ((0, 8 - CONV_W), (0, 0)))
        gq_g, gk_g, gv_g, gates = _gdnprep(p_gdn, conv_w8, ab, _pad_lanes(a_log[layer]),
                                           _pad_lanes(dt_bias[layer]), n_lat_blocks)
        qg, w_g, kd, u_g, aq, eg = _gdnchunk(gq_g, gk_g, gv_g, gates)
        o_f, o_b = _gdnscan(qg, w_g, kd, u_g, aq, eg, L // GDN_CHUNK, Lc // GDN_CHUNK)

        rw = jnp.pad(router_w[layer], ((0, 0), (0, LANES - N_EXPERTS)))
        rb = jnp.pad(router_b[layer].reshape(1, N_EXPERTS).astype(F32), ((0, 0), (0, LANES - N_EXPERTS)),
                     constant_values=-1e30)
        h_new, v_ffn, top_e, top_g = _mixout(
            o_att, o_f, o_b, z, h, modtab, gdn_norm_g[layer].reshape(1, LANES), w_out[layer].astype(BF16),
            norm_ffn_g[layer].reshape(1, D), rw, rb, n_lat_blocks)

        y = _moe(v_ffn.reshape(B * S, D), top_e.reshape(B * S, LANES)[:, :TOP_K],
                 top_g.reshape(B * S, LANES)[:, :TOP_K],
                 w_gate_up[layer].astype(BF16), b_gate_up[layer].reshape(N_EXPERTS, 1, 2 * D_EXPERT),
                 w_down[layer].astype(BF16), b_down[layer].reshape(N_EXPERTS, 1, D))
        gate_ffn = modtab[:, :, 5, :]
        gate_rows = jnp.concatenate([jnp.broadcast_to(gate_ffn[:, 0:1], (B, L, D)),
                                     jnp.broadcast_to(gate_ffn[:, 1:2], (B, Lc, D))], axis=1)
        h = h_new + gate_rows * y.reshape(B, S, D)
    return h[:, :L]
```

```python
import functools
import math

import jax
import jax.numpy as jnp
from jax import lax
from jax.experimental import pallas as pl
from jax.experimental.pallas import tpu as pltpu

F32 = jnp.float32
BF16 = jnp.bfloat16
HIGHEST = lax.Precision.HIGHEST

D_MODEL = 1024
GRID_W = 64
EPS = 1e-6
ATT_WIDTH = 512
ATT_HEAD_DIM = 64
ATT_HEADS = 4
ROPE_BASE = 10000.0
ROPE_PAIRS = ATT_HEAD_DIM // 4
GDN_WIDTH = 512
GDN_HEAD_DIM = 128
GDN_HEADS = 4
GDN_CHUNK = 64
CONV_W = 5
IN_MAIN = 3 * ATT_WIDTH + 4 * GDN_WIDTH
N_EXPERTS = 32
TOP_K = 4
D_EXPERT = 1024
SWIGLU_ALPHA = 1.702
SWIGLU_LIMIT = 7.0
MOE_BLOCK = 256

LANES = 128
ROW_TILE = 256
KV_CHUNK = 256
Q_SCALE = ATT_HEAD_DIM ** -0.5 * math.log2(math.e)
VMEM_LIMIT = 48 * 1024 * 1024
EXPERT_VMEM_LIMIT = 56 * 1024 * 1024


def _cparams(sem):
    return pltpu.CompilerParams(dimension_semantics=sem, vmem_limit_bytes=VMEM_LIMIT)


def _dot(a, b):
    return jnp.dot(a, b, preferred_element_type=F32)


def _dot_nt(a, b):
    return lax.dot_general(a, b, (((1,), (1,)), ((), ())), preferred_element_type=F32)


def _dot_tn(a, b):
    return lax.dot_general(a, b, (((0,), (0,)), ((), ())), preferred_element_type=F32)


def _dot_hi(a, b):
    return jnp.dot(a, b, preferred_element_type=F32, precision=HIGHEST)


def _sigmoid(x):
    return 1.0 / (1.0 + jnp.exp(-x))


def _adaln_kernel(c_ref, w_ref, b_ref, o_ref):
    c = c_ref[...]
    s = c * _sigmoid(c)
    o_ref[...] = _dot_hi(s, w_ref[...]) + b_ref[...]


def _adaln(c_rows, w, b):
    n = w.shape[1]
    tn = 1024
    return pl.pallas_call(
        _adaln_kernel,
        out_shape=jax.ShapeDtypeStruct((8, n), F32),
        grid=(n // tn,),
        in_specs=[pl.BlockSpec((8, D_MODEL), lambda j: (0, 0)),
                  pl.BlockSpec((D_MODEL, tn), lambda j: (0, j)),
                  pl.BlockSpec((1, tn), lambda j: (0, j))],
        out_specs=pl.BlockSpec((8, tn), lambda j: (0, j)),
        compiler_params=_cparams(("arbitrary",)),
        name="adaln",
    )(c_rows, w, b.reshape(1, n))


def _inproj_kernel(h_ref, mod_ref, g_ref, w_ref, wab_ref, qk_ref, v_ref, gdn_ref, z_ref, ab_ref):
    x = h_ref[0]
    ms = jnp.mean(x * x, axis=-1, keepdims=True)
    y = x * lax.rsqrt(ms + EPS) * g_ref[...]
    shift = mod_ref[0, 0, 0:1, :]
    scale = mod_ref[0, 0, 1:2, :]
    u = (y * (1.0 + scale) + shift).astype(BF16)
    qk_ref[0] = _dot(u, w_ref[:, 0:2 * ATT_WIDTH])
    vv = _dot(u, w_ref[:, 2 * ATT_WIDTH:3 * ATT_WIDTH])
    for hh in range(ATT_HEADS):
        v_ref[0, hh] = vv[:, hh * LANES:(hh + 1) * LANES].T.astype(BF16)
    off = 3 * ATT_WIDTH
    gdn_ref[0] = _dot(u, w_ref[:, off:off + 3 * GDN_WIDTH])
    z_ref[0] = _dot(u, w_ref[:, off + 3 * GDN_WIDTH:off + 4 * GDN_WIDTH])
    ab_ref[0] = _dot(u, wab_ref[...])


def _inproj(h, modtab, g, w_main, w_ab, n_lat_blocks):
    B, S, _ = h.shape
    tm = ROW_TILE
    row = lambda b, i: (b, i, 0)
    return pl.pallas_call(
        _inproj_kernel,
        out_shape=(jax.ShapeDtypeStruct((B, S, 2 * ATT_WIDTH), F32),
                   jax.ShapeDtypeStruct((B, ATT_HEADS, LANES, S), BF16),
                   jax.ShapeDtypeStruct((B, S, 3 * GDN_WIDTH), F32),
                   jax.ShapeDtypeStruct((B, S, GDN_WIDTH), F32),
                   jax.ShapeDtypeStruct((B, S, LANES), F32)),
        grid=(B, S // tm),
        in_specs=[pl.BlockSpec((1, tm, D_MODEL), row),
                  pl.BlockSpec((1, 1, 8, D_MODEL), lambda b, i: (b, (i >= n_lat_blocks).astype(jnp.int32), 0, 0)),
                  pl.BlockSpec((1, D_MODEL), lambda b, i: (0, 0)),
                  pl.BlockSpec((D_MODEL, IN_MAIN), lambda b, i: (0, 0)),
                  pl.BlockSpec((D_MODEL, LANES), lambda b, i: (0, 0))],
        out_specs=(pl.BlockSpec((1, tm, 2 * ATT_WIDTH), row),
                   pl.BlockSpec((1, ATT_HEADS, LANES, tm), lambda b, i: (b, 0, 0, i)),
                   pl.BlockSpec((1, tm, 3 * GDN_WIDTH), row),
                   pl.BlockSpec((1, tm, GDN_WIDTH), row),
                   pl.BlockSpec((1, tm, LANES), row)),
        compiler_params=_cparams(("parallel", "parallel")),
        name="inproj",
    )(h, modtab, g, w_main, w_ab)


def _qkprep_kernel(p_ref, cos_ref, sin_ref, gq_ref, gk_ref, qz_ref, k_ref):
    tm = p_ref.shape[1]
    lane = lax.broadcasted_iota(jnp.int32, (tm, LANES), 1)
    hi16 = (lane & 16) != 0
    first = lax.broadcasted_iota(jnp.int32, (LANES, tm), 0) < ATT_HEAD_DIM
    r = lax.broadcasted_iota(jnp.int32, (LANES, LANES), 0) >> 6
    c = lax.broadcasted_iota(jnp.int32, (LANES, LANES), 1) >> 6
    gmat = jnp.where(r == c, 1.0 / ATT_HEAD_DIM, 0.0).astype(F32)
    cosv = cos_ref[...]
    sinv = sin_ref[...]
    for j in range(2 * ATT_HEADS):
        x = p_ref[0, :, j * LANES:(j + 1) * LANES]
        ms = _dot_hi(x * x, gmat)
        g = gq_ref[...] if j < ATT_HEADS else gk_ref[...]
        y = x * lax.rsqrt(ms + EPS) * g
        sw = jnp.where(hi16, pltpu.roll(y, 16, 1), pltpu.roll(y, LANES - 16, 1))
        y = y * cosv + sw * sinv
        if j < ATT_HEADS:
            yt = (y * Q_SCALE).T
            qz_ref[0, j, 0] = jnp.where(first, yt, 0.0).astype(BF16)
            qz_ref[0, j, 1] = jnp.where(first, 0.0, yt).astype(BF16)
        else:
            k_ref[0, j - ATT_HEADS] = y.astype(BF16)


def _qkprep(p_qk, cos_t, sin_t, gq, gk):
    B, S, _ = p_qk.shape
    tm = ROW_TILE
    return pl.pallas_call(
        _qkprep_kernel,
        out_shape=(jax.ShapeDtypeStruct((B, ATT_HEADS, 2, LANES, S), BF16),
                   jax.ShapeDtypeStruct((B, ATT_HEADS, S, LANES), BF16)),
        grid=(B, S // tm),
        in_specs=[pl.BlockSpec((1, tm, 2 * ATT_WIDTH), lambda b, i: (b, i, 0)),
                  pl.BlockSpec((tm, LANES), lambda b, i: (i, 0)),
                  pl.BlockSpec((tm, LANES), lambda b, i: (i, 0)),
                  pl.BlockSpec((1, LANES), lambda b, i: (0, 0)),
                  pl.BlockSpec((1, LANES), lambda b, i: (0, 0))],
        out_specs=(pl.BlockSpec((1, ATT_HEADS, 2, LANES, tm), lambda b, i: (b, 0, 0, 0, i)),
                   pl.BlockSpec((1, ATT_HEADS, tm, LANES), lambda b, i: (b, 0, i, 0))),
        compiler_params=_cparams(("parallel", "parallel")),
        name="qkprep",
    )(p_qk, cos_t, sin_t, gq, gk)


def _attn_kernel(lam_ref, qz_ref, k_ref, v_ref, g_ref, *rest, aliased, tk, n_kv):
    if aliased:
        rest = rest[1:]
    o_ref, s_sc, mt_sc, m_sc, l_sc, acc_sc = rest
    m_sc[...] = jnp.full(m_sc.shape, -jnp.inf, F32)
    l_sc[...] = jnp.zeros(l_sc.shape, F32)
    acc_sc[...] = jnp.zeros(acc_sc.shape, F32)

    def rows(j):
        if isinstance(j, int):
            return pl.ds(j * tk, tk)
        return pl.ds(pl.multiple_of(j * tk, tk), tk)

    def qk_tile(j, slot):
        kt = k_ref[0, 0, rows(j), :]
        for p in range(2):
            s = _dot(kt, qz_ref[0, 0, p])
            s_sc[slot, p] = s
            mt_sc[slot, p] = jnp.max(s, axis=0, keepdims=True)

    def pv_tile(j, slot):
        for p in range(2):
            m_prev = m_sc[p]
            m_new = jnp.maximum(m_prev, mt_sc[slot, p])
            alpha = jnp.exp2(m_prev - m_new)
            lsum = None
            acc = None
            for c in range(tk // KV_CHUNK):
                cs = slice(c * KV_CHUNK, (c + 1) * KV_CHUNK)
                pe = jnp.exp2(s_sc[slot, p, cs, :] - m_new)
                ps = jnp.sum(pe, axis=0, keepdims=True)
                pv = _dot(v_ref[0, 0, j, :, cs], pe.astype(BF16))
                lsum = ps if lsum is None else lsum + ps
                acc = pv if acc is None else acc + pv
            l_sc[p] = alpha * l_sc[p] + lsum
            acc_sc[p] = alpha * acc_sc[p] + acc
            m_sc[p] = m_new

    qk_tile(0, 0)
    n_pairs = (n_kv - 1) // 2

    def pair(jj, carry):
        j = 2 * jj
        qk_tile(j + 1, 1)
        pv_tile(j, 0)
        qk_tile(j + 2, 0)
        pv_tile(j + 1, 1)
        return carry

    if n_pairs > 0:
        lax.fori_loop(0, n_pairs, pair, 0)
    j = 2 * n_pairs
    if j == n_kv - 1:
        pv_tile(j, 0)
    else:
        qk_tile(j + 1, 1)
        pv_tile(j, 0)
        pv_tile(j + 1, 1)

    o = acc_sc[0] / l_sc[0] - lam_ref[0] * (acc_sc[1] / l_sc[1])
    ms = jnp.mean(o * o, axis=0, keepdims=True)
    o_ref[0] = ((o * lax.rsqrt(ms + EPS)).T * g_ref[...]).astype(BF16)


def _attention(lam, qz, k, vt, g, *, tq, tk, q_blk0, n_q, kv_blk0, n_kv, prev_out=None):
    B, H, _, _, S = qz.shape
    aliased = prev_out is not None
    skv = tk * n_kv
    in_specs = [pl.BlockSpec(memory_space=pltpu.SMEM),
                pl.BlockSpec((1, 1, 2, LANES, tq), lambda b, h, i: (b, h, 0, 0, q_blk0 + i)),
                pl.BlockSpec((1, 1, skv, LANES), lambda b, h, i: (b, h, kv_blk0, 0)),
                pl.BlockSpec((1, 1, n_kv, LANES, tk), lambda b, h, i: (b, h, 0, 0, 0)),
                pl.BlockSpec((1, LANES), lambda b, h, i: (0, 0))]
    args = [lam, qz, k, vt, g]
    aliases = {}
    if aliased:
        in_specs.append(pl.BlockSpec(memory_space=pl.ANY))
        args.append(prev_out)
        aliases = {5: 0}
    return pl.pallas_call(
        functools.partial(_attn_kernel, aliased=aliased, tk=tk, n_kv=n_kv),
        out_shape=jax.ShapeDtypeStruct((B, S, ATT_WIDTH), BF16),
        grid=(B, H, n_q),
        in_specs=in_specs,
        out_specs=pl.BlockSpec((1, tq, LANES), lambda b, h, i: (b, q_blk0 + i, h)),
        scratch_shapes=[pltpu.VMEM((2, 2, tk, tq), F32), pltpu.VMEM((2, 2, 1, tq), F32),
                        pltpu.VMEM((2, 1, tq), F32), pltpu.VMEM((2, 1, tq), F32),
                        pltpu.VMEM((2, LANES, tq), F32)],
        input_output_aliases=aliases,
        compiler_params=_cparams(("parallel", "parallel", "arbitrary")),
        name="attn_ctx" if aliased else "attn_lat",
    )(*args)


def _attention_lat(lam, qz, k, vt, g, L):
    B, H, _, S = vt.shape
    tq = _pick_tile(L, (512, 256))
    tk = _pick_tile(S, (1280, 640, 256))
    n_kv = S // tk
    vt_tiles = vt.reshape(B, H, LANES, n_kv, tk).transpose(0, 1, 3, 2, 4)
    return _attention(lam, qz, k, vt_tiles, g, tq=tq, tk=tk, q_blk0=0, n_q=L // tq, kv_blk0=0, n_kv=n_kv)


def _attention_ctx(lam, qz, k, vt, g, L, prev_out):
    B, H, _, S = vt.shape
    Lc = S - L
    vt_ctx = vt[:, :, :, L:].reshape(B, H, 1, LANES, Lc)
    return _attention(lam, qz, k, vt_ctx, g, tq=Lc, tk=Lc, q_blk0=L // Lc, n_q=1, kv_blk0=L // Lc, n_kv=1,
                      prev_out=prev_out)


def _gdnprep_kernel(x_ref, prev_ref, next_ref, cw_ref, ab_ref, alog_ref, dtb_ref,
                    q_ref, k_ref, v_ref, gate_ref, ext_sc, *, nb_lat, nb_all):
    i = pl.program_id(1)
    tm = x_ref.shape[1]
    first = jnp.logical_or(i == 0, i == nb_lat)
    last = jnp.logical_or(i == nb_lat - 1, i == nb_all - 1)
    keep_prev = jnp.where(first, 0.0, 1.0)
    keep_next = jnp.where(last, 0.0, 1.0)
    ext_sc[0:8, :] = prev_ref[0] * keep_prev
    ext_sc[8:8 + tm, :] = x_ref[0]
    ext_sc[8 + tm:16 + tm, :] = next_ref[0] * keep_next
    acc = None
    for j in range(CONV_W):
        term = ext_sc[pl.ds(8 + j - CONV_W // 2, tm), :] * cw_ref[j:j + 1, :]
        acc = term if acc is None else acc + term
    y = acc * _sigmoid(acc)
    for part, ref in ((0, q_ref), (1, k_ref)):
        for hh in range(GDN_HEADS):
            lo = part * GDN_WIDTH + hh * GDN_HEAD_DIM
            t = y[:, lo:lo + GDN_HEAD_DIM]
            ref[0, :, hh * GDN_HEAD_DIM:(hh + 1) * GDN_HEAD_DIM] = (
                t * lax.rsqrt(jnp.sum(t * t, axis=-1, keepdims=True) + EPS))
    v_ref[0] = y[:, 2 * GDN_WIDTH:3 * GDN_WIDTH]
    ab = ab_ref[0]
    xs = ab + dtb_ref[...]
    sp = jnp.maximum(xs, 0.0) + jnp.log(1.0 + jnp.exp(-jnp.abs(xs)))
    g = -jnp.exp(alog_ref[...]) * sp
    lane = lax.broadcasted_iota(jnp.int32, ab.shape, 1)
    gate_ref[0] = jnp.where(lane < 2 * GDN_HEADS, g, _sigmoid(ab))


def _gdnprep(p_gdn, conv_w8, ab, alog_row, dtb_row, n_lat_blocks):
    B, S, W = p_gdn.shape
    tm = ROW_TILE
    nb = S // tm
    r8 = tm // 8
    row = lambda b, i: (b, i, 0)
    return pl.pallas_call(
        functools.partial(_gdnprep_kernel, nb_lat=n_lat_blocks, nb_all=nb),
        out_shape=(jax.ShapeDtypeStruct((B, S, GDN_WIDTH), F32),
                   jax.ShapeDtypeStruct((B, S, GDN_WIDTH), F32),
                   jax.ShapeDtypeStruct((B, S, GDN_WIDTH), F32),
                   jax.ShapeDtypeStruct((B, S, LANES), F32)),
        grid=(B, nb),
        in_specs=[pl.BlockSpec((1, tm, W), row),
                  pl.BlockSpec((1, 8, W), lambda b, i: (b, jnp.maximum(i * r8 - 1, 0), 0)),
                  pl.BlockSpec((1, 8, W), lambda b, i: (b, jnp.minimum((i + 1) * r8, nb * r8 - 1), 0)),
                  pl.BlockSpec((8, W), lambda b, i: (0, 0)),
                  pl.BlockSpec((1, tm, LANES), row),
                  pl.BlockSpec((1, LANES), lambda b, i: (0, 0)),
                  pl.BlockSpec((1, LANES), lambda b, i: (0, 0))],
        out_specs=(pl.BlockSpec((1, tm, GDN_WIDTH), row),
                   pl.BlockSpec((1, tm, GDN_WIDTH), row),
                   pl.BlockSpec((1, tm, GDN_WIDTH), row),
                   pl.BlockSpec((1, tm, LANES), row)),
        scratch_shapes=[pltpu.VMEM((tm + 16, W), F32)],
        compiler_params=_cparams(("parallel", "parallel")),
        name="gdnprep",
    )(p_gdn, p_gdn, p_gdn, conv_w8, ab, alog_row, dtb_row)


def _gdnchunk_kernel(q_ref, k_ref, v_ref, gate_ref, qg_ref, w_ref, kd_ref, u_ref, aq_ref, eg_ref):
    tm = q_ref.shape[1]
    nc = tm // GDN_CHUNK
    gates = gate_ref[0]
    ri = lax.broadcasted_iota(jnp.int32, (tm, tm), 0)
    ci = lax.broadcasted_iota(jnp.int32, (tm, tm), 1)
    same = (ri >> 6) == (ci >> 6)
    eye = ri == ci
    tot = _dot_hi(jnp.where(same, 1.0, 0.0).astype(F32), gates)
    for d in range(2):
        incl = jnp.logical_and(same, (ci <= ri) if d == 0 else (ci >= ri))
        strict = jnp.logical_and(incl, jnp.logical_not(eye))
        gc = _dot_hi(jnp.where(incl, 1.0, 0.0).astype(F32), gates)
        gc_t = gc.T
        for hh in range(GDN_HEADS):
            ln = d * GDN_HEADS + hh
            sl = slice(hh * GDN_HEAD_DIM, (hh + 1) * GDN_HEAD_DIM)
            gcol = gc[:, ln:ln + 1]
            grow = gc_t[ln:ln + 1, :]
            bcol = gates[:, 2 * GDN_HEADS + ln:2 * GDN_HEADS + ln + 1]
            tcol = tot[:, ln:ln + 1]
            diff = gcol - grow
            decay = jnp.where(incl, jnp.exp(jnp.where(incl, diff, 0.0)), 0.0)
            k = k_ref[0, :, sl]
            q = q_ref[0, :, sl] * (GDN_HEAD_DIM ** -0.5)
            v = v_ref[0, :, sl]
            kb = k * bcol
            k16 = k.astype(BF16)
            a = jnp.where(strict, _dot_nt(kb.astype(BF16), k16) * decay, 0.0)
            x = -a
            t = jnp.where(eye, 1.0, 0.0).astype(F32) + x
            for _ in range(5):
                x16 = x.astype(BF16)
                x = _dot(x16, x16)
                t = t + _dot(t.astype(BF16), x.astype(BF16))
            t16 = t.astype(BF16)
            egc = jnp.exp(gcol)
            u_ref[0, d, :, sl] = _dot(t16, (v * bcol).astype(BF16))
            w_ref[0, d, :, sl] = _dot(t16, (kb * egc).astype(BF16)).astype(BF16)
            aqk = jnp.where(incl, _dot_nt(q.astype(BF16), k16) * decay, 0.0)
            kd_ref[0, d, :, sl] = (k * jnp.exp(tcol - gcol)).astype(BF16)
            qg_ref[0, d, :, sl] = (q * egc).astype(BF16)
            for cc in range(nc):
                rs = slice(cc * GDN_CHUNK, (cc + 1) * GDN_CHUNK)
                aq_ref[0, d, rs, hh * GDN_CHUNK:(hh + 1) * GDN_CHUNK] = aqk[rs, rs].astype(BF16)
                eg_ref[0, d, cc, hh:hh + 1, :] = jnp.broadcast_to(
                    jnp.exp(tcol[cc * GDN_CHUNK:cc * GDN_CHUNK + 1, :]), (1, LANES))


def _gdnchunk(gq, gk, gv, gates):
    B, S, W = gq.shape
    tm = ROW_TILE
    nc = tm // GDN_CHUNK
    row = lambda b, i: (b, i, 0)
    drow = lambda b, i: (b, 0, i, 0)
    big = lambda dt: jax.ShapeDtypeStruct((B, 2, S, W), dt)
    return pl.pallas_call(
        _gdnchunk_kernel,
        out_shape=(big(BF16), big(BF16), big(BF16), big(F32),
                   jax.ShapeDtypeStruct((B, 2, S, GDN_HEADS * GDN_CHUNK), BF16),
                   jax.ShapeDtypeStruct((B, 2, S // GDN_CHUNK, GDN_HEADS, LANES), F32)),
        grid=(B, S // tm),
        in_specs=[pl.BlockSpec((1, tm, W), row), pl.BlockSpec((1, tm, W), row),
                  pl.BlockSpec((1, tm, W), row), pl.BlockSpec((1, tm, LANES), row)],
        out_specs=(pl.BlockSpec((1, 2, tm, W), drow), pl.BlockSpec((1, 2, tm, W), drow),
                   pl.BlockSpec((1, 2, tm, W), drow), pl.BlockSpec((1, 2, tm, W), drow),
                   pl.BlockSpec((1, 2, tm, GDN_HEADS * GDN_CHUNK), drow),
                   pl.BlockSpec((1, 2, nc, GDN_HEADS, LANES), lambda b, i: (b, 0, i, 0, 0))),
        compiler_params=_cparams(("parallel", "parallel")),
        name="gdnchunk",
    )(gq, gk, gv, gates)


def _gdnscan_kernel(qg0, w0, kd0, u0, aq0, eg0, qg1, w1, kd1, u1, aq1, eg1, of_ref, ob_ref, s_sc):
    @pl.when(pl.program_id(1) == 0)
    def _():
        s_sc[...] = jnp.zeros(s_sc.shape, F32)

    for d, (qg, w, kd, u, aq, eg, o_ref) in enumerate(
            ((qg0, w0, kd0, u0, aq0, eg0, of_ref), (qg1, w1, kd1, u1, aq1, eg1, ob_ref))):
        for hh in range(GDN_HEADS):
            sl = slice(hh * GDN_HEAD_DIM, (hh + 1) * GDN_HEAD_DIM)
            st = s_sc[d, hh]
            st16 = st.astype(BF16)
            vnew = u[0, 0, :, sl] - _dot(w[0, 0, :, sl], st16)
            vn16 = vnew.astype(BF16)
            o_ref[0, :, sl] = (_dot(qg[0, 0, :, sl], st16)
                               + _dot(aq[0, 0, :, hh * GDN_CHUNK:(hh + 1) * GDN_CHUNK], vn16))
            s_sc[d, hh] = st * eg[0, 0, 0, hh:hh + 1, :] + _dot_tn(kd[0, 0, :, sl], vn16)


def _gdnscan(qg, w, kd, u, aq, eg, n_lat_chunks, n_ctx_chunks):
    B, _, S, W = qg.shape
    C = GDN_CHUNK
    n = S // C

    def fwd_chunk(i):
        return jnp.where(i < n_ctx_chunks, n_lat_chunks + i, i - n_ctx_chunks)

    def bwd_chunk(i):
        return jnp.where(i < n_ctx_chunks, n_lat_chunks + n_ctx_chunks - 1 - i, n - 1 - i)

    def specs(d, chunk_of):
        big = pl.BlockSpec((1, 1, C, W), lambda b, i: (b, d, chunk_of(i), 0))
        return [big, big, big, big,
                pl.BlockSpec((1, 1, C, GDN_HEADS * C), lambda b, i: (b, d, chunk_of(i), 0)),
                pl.BlockSpec((1, 1, 1, GDN_HEADS, LANES), lambda b, i: (b, d, chunk_of(i), 0, 0))]

    return pl.pallas_call(
        _gdnscan_kernel,
        out_shape=(jax.ShapeDtypeStruct((B, S, W), F32), jax.ShapeDtypeStruct((B, S, W), F32)),
        grid=(B, n),
        in_specs=specs(0, fwd_chunk) + specs(1, bwd_chunk),
        out_specs=(pl.BlockSpec((1, C, W), lambda b, i: (b, fwd_chunk(i), 0)),
                   pl.BlockSpec((1, C, W), lambda b, i: (b, bwd_chunk(i), 0))),
        scratch_shapes=[pltpu.VMEM((2, GDN_HEADS, GDN_HEAD_DIM, GDN_HEAD_DIM), F32)],
        compiler_params=_cparams(("parallel", "arbitrary")),
        name="gdnscan",
    )(qg, w, kd, u, aq, eg, qg, w, kd, u, aq, eg)


def _mixout_kernel(oa_ref, of_ref, ob_ref, z_ref, h_ref, mod_ref, gg_ref, wo_ref, gf_ref, rw_ref, rb_ref,
                   hn_ref, v_ref, te_ref, tg_ref):
    og = of_ref[0] + ob_ref[0]
    z = z_ref[0]
    parts = [oa_ref[0]]
    for hh in range(GDN_HEADS):
        sl = slice(hh * GDN_HEAD_DIM, (hh + 1) * GDN_HEAD_DIM)
        t = og[:, sl]
        t = t * lax.rsqrt(jnp.mean(t * t, axis=-1, keepdims=True) + EPS) * gg_ref[...]
        zz = z[:, sl]
        parts.append((t * (zz * _sigmoid(zz))).astype(BF16))
    mix_in = jnp.concatenate(parts, axis=-1)
    mix = _dot(mix_in, wo_ref[...])
    hn = h_ref[0] + mod_ref[0, 0, 2:3, :] * mix
    hn_ref[0] = hn
    y = hn * lax.rsqrt(jnp.mean(hn * hn, axis=-1, keepdims=True) + EPS) * gf_ref[...]
    v = y * (1.0 + mod_ref[0, 0, 4:5, :]) + mod_ref[0, 0, 3:4, :]
    v_ref[0] = v.astype(BF16)
    logits = _dot_hi(v, rw_ref[...]) + rb_ref[...]
    lane = lax.broadcasted_iota(jnp.int32, logits.shape, 1)
    cur = logits
    vals, idxs = [], []
    for _ in range(TOP_K):
        m = jnp.max(cur, axis=-1, keepdims=True)
        idx = jnp.min(jnp.where(cur == m, lane, LANES), axis=-1, keepdims=True)
        vals.append(m)
        idxs.append(idx)
        cur = jnp.where(lane == idx, -jnp.inf, cur)
    es = [jnp.exp(vv - vals[0]) for vv in vals]
    inv = 1.0 / (es[0] + es[1] + es[2] + es[3])
    te = jnp.zeros(logits.shape, jnp.int32)
    tg = jnp.zeros(logits.shape, F32)
    for kk in range(TOP_K):
        te = jnp.where(lane == kk, idxs[kk], te)
        tg = jnp.where(lane == kk, es[kk] * inv, tg)
    te_ref[0] = te
    tg_ref[0] = tg


def _mixout(o_att, o_f, o_b, z, h, modtab, gg, w_out, gf, rw, rb, n_lat_blocks):
    B, S, _ = h.shape
    tm = ROW_TILE
    row = lambda b, i: (b, i, 0)
    const = lambda b, i: (0, 0)
    return pl.pallas_call(
        _mixout_kernel,
        out_shape=(jax.ShapeDtypeStruct((B, S, D_MODEL), F32),
                   jax.ShapeDtypeStruct((B, S, D_MODEL), BF16),
                   jax.ShapeDtypeStruct((B, S, LANES), jnp.int32),
                   jax.ShapeDtypeStruct((B, S, LANES), F32)),
        grid=(B, S // tm),
        in_specs=[pl.BlockSpec((1, tm, ATT_WIDTH), row),
                  pl.BlockSpec((1, tm, GDN_WIDTH), row),
                  pl.BlockSpec((1, tm, GDN_WIDTH), row),
                  pl.BlockSpec((1, tm, GDN_WIDTH), row),
                  pl.BlockSpec((1, tm, D_MODEL), row),
                  pl.BlockSpec((1, 1, 8, D_MODEL), lambda b, i: (b, (i >= n_lat_blocks).astype(jnp.int32), 0, 0)),
                  pl.BlockSpec((1, LANES), const),
                  pl.BlockSpec((D_MODEL, D_MODEL), const),
                  pl.BlockSpec((1, D_MODEL), const),
                  pl.BlockSpec((D_MODEL, LANES), const),
                  pl.BlockSpec((1, LANES), const)],
        out_specs=(pl.BlockSpec((1, tm, D_MODEL), row),
                   pl.BlockSpec((1, tm, D_MODEL), row),
                   pl.BlockSpec((1, tm, LANES), row),
                   pl.BlockSpec((1, tm, LANES), row)),
        compiler_params=_cparams(("parallel", "parallel")),
        name="mixout",
    )(o_att, o_f, o_b, z, h, modtab, gg, w_out, gf, rw, rb)


def _expert_kernel(be_ref, nv_ref, x_ref, wgu_ref, bgu_ref, wd_ref, bd_ref, sg_ref, y_ref, wgu_sc, wd_sc):
    i = pl.program_id(0)
    new_expert = jnp.logical_or(i == 0, be_ref[i] != be_ref[jnp.maximum(i - 1, 0)])

    @pl.when(new_expert)
    def _():
        wgu_sc[...] = wgu_ref[0].astype(BF16)
        wd_sc[...] = wd_ref[0].astype(BF16)

    @pl.when(nv_ref[i] > 0)
    def _():
        gu = _dot(x_ref[...], wgu_sc[...]) + bgu_ref[0]
        g_ = jnp.minimum(gu[:, :D_EXPERT], SWIGLU_LIMIT)
        up = jnp.clip(gu[:, D_EXPERT:], -SWIGLU_LIMIT, SWIGLU_LIMIT)
        glu = g_ * _sigmoid(SWIGLU_ALPHA * g_)
        act = ((up + 1.0) * glu).astype(BF16)
        y_ref[...] = (_dot(act, wd_sc[...]) + bd_ref[0]) * sg_ref[...]

    @pl.when(nv_ref[i] == 0)
    def _():
        y_ref[...] = jnp.zeros(y_ref.shape, F32)


def _experts(block_e, n_valid, x_sorted, wgu, bgu, wd, bd, slot_gate):
    n_slots = x_sorted.shape[0]
    nb = n_slots // MOE_BLOCK
    grid_spec = pltpu.PrefetchScalarGridSpec(
        num_scalar_prefetch=2,
        grid=(nb,),
        in_specs=[pl.BlockSpec((MOE_BLOCK, D_MODEL), lambda i, be, nv: (i, 0)),
                  pl.BlockSpec((1, D_MODEL, 2 * D_EXPERT), lambda i, be, nv: (be[i], 0, 0)),
                  pl.BlockSpec((1, 1, 2 * D_EXPERT), lambda i, be, nv: (be[i], 0, 0)),
                  pl.BlockSpec((1, D_EXPERT, D_MODEL), lambda i, be, nv: (be[i], 0, 0)),
                  pl.BlockSpec((1, 1, D_MODEL), lambda i, be, nv: (be[i], 0, 0)),
                  pl.BlockSpec((MOE_BLOCK, 1), lambda i, be, nv: (i, 0))],
        out_specs=pl.BlockSpec((MOE_BLOCK, D_MODEL), lambda i, be, nv: (i, 0)),
        scratch_shapes=[pltpu.VMEM((D_MODEL, 2 * D_EXPERT), BF16), pltpu.VMEM((D_EXPERT, D_MODEL), BF16)],
    )
    return pl.pallas_call(
        _expert_kernel,
        out_shape=jax.ShapeDtypeStruct((n_slots, D_MODEL), F32),
        grid_spec=grid_spec,
        compiler_params=pltpu.CompilerParams(dimension_semantics=("arbitrary",),
                                             vmem_limit_bytes=EXPERT_VMEM_LIMIT),
        name="experts",
    )(block_e, n_valid, x_sorted, wgu, bgu, wd, bd, slot_gate)


def _moe_plan(top_e, top_g):
    T = top_e.shape[0]
    n_assign = T * TOP_K
    n_blocks = -(-n_assign // MOE_BLOCK) + N_EXPERTS
    n_slots = n_blocks * MOE_BLOCK
    i32 = jnp.int32
    flat_e = top_e.reshape(-1)
    gate_flat = top_g.reshape(-1)
    order = jnp.argsort(flat_e).astype(i32)
    rank = jnp.argsort(order).astype(i32)
    e_sorted = flat_e[order]
    e_ids = jnp.arange(N_EXPERTS, dtype=i32)
    start = jnp.searchsorted(e_sorted, e_ids, side='left').astype(i32)
    counts = jnp.searchsorted(e_sorted, e_ids, side='right').astype(i32) - start
    padded = (counts + MOE_BLOCK - 1) // MOE_BLOCK * MOE_BLOCK
    pad_end = jnp.cumsum(padded)
    pad_start = pad_end - padded
    block_e = jnp.minimum(jnp.searchsorted(pad_end, jnp.arange(n_blocks, dtype=i32) * MOE_BLOCK, side='right'),
                          N_EXPERTS - 1).astype(i32)
    e_slot = jnp.repeat(block_e, MOE_BLOCK)
    off = jnp.arange(n_slots, dtype=i32) - pad_start[e_slot]
    valid = off < counts[e_slot]
    a_slot = order[jnp.clip(off + start[e_slot], 0, n_assign - 1)]
    slot_tok = jnp.where(valid, a_slot // TOP_K, 0)
    slot_gate = jnp.where(valid, gate_flat[a_slot], 0.0)
    n_valid = jnp.sum(valid.reshape(n_blocks, MOE_BLOCK), axis=1).astype(i32)
    slot_of = (pad_start[flat_e] + rank - start[flat_e]).astype(i32)
    return slot_tok, slot_gate.reshape(n_slots, 1), block_e, n_valid, slot_of


def _combine_kernel(y_ref, h_ref, mod_ref, o_ref):
    y = y_ref[0, 0] + y_ref[1, 0] + y_ref[2, 0] + y_ref[3, 0]
    o_ref[0] = h_ref[0] + mod_ref[0, 0, 5:6, :] * y


def _combine(y4, h, modtab, n_lat_blocks):
    B, S, _ = h.shape
    tm = ROW_TILE
    return pl.pallas_call(
        _combine_kernel,
        out_shape=jax.ShapeDtypeStruct((B, S, D_MODEL), F32),
        grid=(B, S // tm),
        in_specs=[pl.BlockSpec((TOP_K, 1, tm, D_MODEL), lambda b, i: (0, b, i, 0)),
                  pl.BlockSpec((1, tm, D_MODEL), lambda b, i: (b, i, 0)),
                  pl.BlockSpec((1, 1, 8, D_MODEL), lambda b, i: (b, (i >= n_lat_blocks).astype(jnp.int32), 0, 0))],
        out_specs=pl.BlockSpec((1, tm, D_MODEL), lambda b, i: (b, i, 0)),
        compiler_params=_cparams(("parallel", "parallel")),
        name="combine",
    )(y4, h, modtab)


def _moe(v_ffn, top_e, top_g, h_new, modtab, wgu, bgu, wd, bd, n_lat_blocks):
    B, S, _ = v_ffn.shape
    T = B * S
    slot_tok, slot_gate, block_e, n_valid, slot_of = _moe_plan(
        top_e.reshape(T, LANES)[:, :TOP_K], top_g.reshape(T, LANES)[:, :TOP_K])
    x_sorted = v_ffn.reshape(T, D_MODEL)[slot_tok]
    y_sorted = _experts(block_e, n_valid, x_sorted, wgu, bgu, wd, bd, slot_gate)
    y4 = y_sorted[slot_of.reshape(T, TOP_K).T.reshape(-1)].reshape(TOP_K, B, S, D_MODEL)
    return _combine(y4, h_new, modtab, n_lat_blocks)


def _pick_tile(n, cands):
    for t in cands:
        if n % t == 0:
            return t
    raise ValueError(f"no tile for {n}")


def _rope_tables(L, Lc):
    rows = L // GRID_W
    row = jnp.repeat(jnp.arange(rows, dtype=F32), GRID_W)
    col = (jnp.arange(L, dtype=jnp.int32) % GRID_W).astype(F32)
    inv_freq = ROPE_BASE ** (-jnp.arange(ROPE_PAIRS, dtype=F32) / ROPE_PAIRS)
    ar = row[:, None] * inv_freq
    ac = col[:, None] * inv_freq
    cos64 = jnp.concatenate([jnp.cos(ar), jnp.cos(ar), jnp.cos(ac), jnp.cos(ac)], axis=-1)
    sin64 = jnp.concatenate([-jnp.sin(ar), jnp.sin(ar), -jnp.sin(ac), jnp.sin(ac)], axis=-1)
    cos_t = jnp.concatenate([jnp.tile(cos64, (1, 2)), jnp.ones((Lc, LANES), F32)], axis=0)
    sin_t = jnp.concatenate([jnp.tile(sin64, (1, 2)), jnp.zeros((Lc, LANES), F32)], axis=0)
    return cos_t, sin_t


def _pad_lanes(v):
    v = v.reshape(1, -1).astype(F32)
    return jnp.pad(v, ((0, 0), (0, LANES - v.shape[1])))


def kernel(x, c, ctx, c_ctx, w_mod, b_mod, norm_mix_g, w_in, q_norm_g, k_norm_g, lam_q1, lam_k1, lam_q2, lam_k2, subln_g, conv_w, a_log, dt_bias, gdn_norm_g, w_out, norm_ffn_g, router_w, router_b, w_gate_up, b_gate_up, w_down, b_down):
    B, L, D = x.shape
    Lc = ctx.shape[1]
    S = L + Lc
    depth = w_mod.shape[0]
    tm = ROW_TILE
    n_lat_blocks = L // tm
    cos_t, sin_t = _rope_tables(L, Lc)

    c_rows = jnp.zeros((8, D), F32).at[:B].set(c).at[B].set(c_ctx)
    h = jnp.concatenate([x, ctx], axis=1)

    for layer in range(depth):
        mod = _adaln(c_rows, w_mod[layer], b_mod[layer])
        mod6 = mod.reshape(8, 6, D)
        lat_mod = mod6[:B]
        ctx_mod = jnp.broadcast_to(mod6[B][None], (B, 6, D))
        modtab = jnp.pad(jnp.stack([lat_mod, ctx_mod], axis=1), ((0, 0), (0, 0), (0, 2), (0, 0)))

        lam_init = 0.8 - 0.6 * math.exp(-0.3 * layer)
        lam_full = (jnp.exp(jnp.sum(lam_q1[layer] * lam_k1[layer]))
                    - jnp.exp(jnp.sum(lam_q2[layer] * lam_k2[layer])) + lam_init).reshape(1).astype(F32)

        w_l = w_in[layer]
        w_main = w_l[:, :IN_MAIN].astype(BF16)
        w_ab = jnp.pad(w_l[:, IN_MAIN:], ((0, 0), (0, LANES - (w_l.shape[1] - IN_MAIN)))).astype(BF16)
        p_qk, v_att, p_gdn, z, ab = _inproj(h, modtab, norm_mix_g[layer].reshape(1, D), w_main, w_ab,
                                            n_lat_blocks)

        gq = jnp.tile(q_norm_g[layer].reshape(1, ATT_HEAD_DIM), (1, 2))
        gk = jnp.tile(k_norm_g[layer].reshape(1, ATT_HEAD_DIM), (1, 2))
        qz, k_att = _qkprep(p_qk, cos_t, sin_t, gq, gk)
        g_sub = (subln_g[layer] * (1.0 - lam_init)).reshape(1, LANES).astype(F32)
        o_att = _attention_lat(lam_full, qz, k_att, v_att, g_sub, L)
        o_att = _attention_ctx(lam_full, qz, k_att, v_att, g_sub, L, o_att)

        conv_w8 = jnp.pad(conv_w[layer], ((0, 8 - CONV_W), (0, 0)))
        gq_g, gk_g, gv_g, gates = _gdnprep(p_gdn, conv_w8, ab, _pad_lanes(a_log[layer]),
                                           _pad_lanes(dt_bias[layer]), n_lat_blocks)
        qg, w_g, kd, u_g, aq, eg = _gdnchunk(gq_g, gk_g, gv_g, gates)
        o_f, o_b = _gdnscan(qg, w_g, kd, u_g, aq, eg, L // GDN_CHUNK, Lc // GDN_CHUNK)

        rw = jnp.pad(router_w[layer], ((0, 0), (0, LANES - N_EXPERTS)))
        rb = jnp.pad(router_b[layer].reshape(1, N_EXPERTS).astype(F32), ((0, 0), (0, LANES - N_EXPERTS)),
                     constant_values=-1e30)
        h_new, v_ffn, top_e, top_g = _mixout(
            o_att, o_f, o_b, z, h, modtab, gdn_norm_g[layer].reshape(1, LANES), w_out[layer].astype(BF16),
            norm_ffn_g[layer].reshape(1, D), rw, rb, n_lat_blocks)

        h = _moe(v_ffn, top_e, top_g, h_new, modtab,
                 w_gate_up[layer], b_gate_up[layer].reshape(N_EXPERTS, 1, 2 * D_EXPERT),
                 w_down[layer], b_down[layer].reshape(N_EXPERTS, 1, D), n_lat_blocks)
    return h[:, :L]
```

```python
import functools
import math

import jax
import jax.numpy as jnp
from jax import lax
from jax.experimental import pallas as pl
from jax.experimental.pallas import tpu as pltpu

F32 = jnp.float32
BF16 = jnp.bfloat16
HIGHEST = lax.Precision.HIGHEST

D_MODEL = 1024
GRID_W = 64
EPS = 1e-6
ATT_WIDTH = 512
ATT_HEAD_DIM = 64
ATT_HEADS = 4
ROPE_BASE = 10000.0
ROPE_PAIRS = ATT_HEAD_DIM // 4
GDN_WIDTH = 512
GDN_HEAD_DIM = 128
GDN_HEADS = 4
GDN_CHUNK = 64
CONV_W = 5
IN_MAIN = 3 * ATT_WIDTH + 4 * GDN_WIDTH
N_EXPERTS = 32
TOP_K = 4
D_EXPERT = 1024
SWIGLU_ALPHA = 1.702
SWIGLU_LIMIT = 7.0
MOE_BLOCK = 512

LANES = 128
ROW_TILE = 256
KV_CHUNK = 256
Q_SCALE = ATT_HEAD_DIM ** -0.5 * math.log2(math.e)
VMEM_LIMIT = 48 * 1024 * 1024
EXPERT_VMEM_LIMIT = 56 * 1024 * 1024


def _cparams(sem):
    return pltpu.CompilerParams(dimension_semantics=sem, vmem_limit_bytes=VMEM_LIMIT)


def _dot(a, b):
    return jnp.dot(a, b, preferred_element_type=F32)


def _dot_nt(a, b):
    return lax.dot_general(a, b, (((1,), (1,)), ((), ())), preferred_element_type=F32)


def _dot_tn(a, b):
    return lax.dot_general(a, b, (((0,), (0,)), ((), ())), preferred_element_type=F32)


def _dot_hi(a, b):
    return jnp.dot(a, b, preferred_element_type=F32, precision=HIGHEST)


def _sigmoid(x):
    return 1.0 / (1.0 + jnp.exp(-x))


def _adaln_kernel(c_ref, w_ref, b_ref, o_ref):
    c = c_ref[...]
    s = c * _sigmoid(c)
    o_ref[...] = _dot_hi(s, w_ref[...]) + b_ref[...]


def _adaln(c_rows, w, b):
    n = w.shape[1]
    tn = 1024
    return pl.pallas_call(
        _adaln_kernel,
        out_shape=jax.ShapeDtypeStruct((8, n), F32),
        grid=(n // tn,),
        in_specs=[pl.BlockSpec((8, D_MODEL), lambda j: (0, 0)),
                  pl.BlockSpec((D_MODEL, tn), lambda j: (0, j)),
                  pl.BlockSpec((1, tn), lambda j: (0, j))],
        out_specs=pl.BlockSpec((8, tn), lambda j: (0, j)),
        compiler_params=_cparams(("arbitrary",)),
        name="adaln",
    )(c_rows, w, b.reshape(1, n))


def _inproj_kernel(h_ref, mod_ref, g_ref, w_ref, wab_ref, qk_ref, v_ref, gdn_ref, z_ref, ab_ref):
    x = h_ref[0]
    ms = jnp.mean(x * x, axis=-1, keepdims=True)
    y = x * lax.rsqrt(ms + EPS) * g_ref[...]
    shift = mod_ref[0, 0, 0:1, :]
    scale = mod_ref[0, 0, 1:2, :]
    u = (y * (1.0 + scale) + shift).astype(BF16)
    qk_ref[0] = _dot(u, w_ref[:, 0:2 * ATT_WIDTH])
    vv = _dot(u, w_ref[:, 2 * ATT_WIDTH:3 * ATT_WIDTH])
    for hh in range(ATT_HEADS):
        v_ref[0, hh] = vv[:, hh * LANES:(hh + 1) * LANES].T.astype(BF16)
    off = 3 * ATT_WIDTH
    gdn_ref[0] = _dot(u, w_ref[:, off:off + 3 * GDN_WIDTH])
    z_ref[0] = _dot(u, w_ref[:, off + 3 * GDN_WIDTH:off + 4 * GDN_WIDTH])
    ab_ref[0] = _dot(u, wab_ref[...])


def _inproj(h, modtab, g, w_main, w_ab, n_lat_blocks):
    B, S, _ = h.shape
    tm = ROW_TILE
    row = lambda b, i: (b, i, 0)
    return pl.pallas_call(
        _inproj_kernel,
        out_shape=(jax.ShapeDtypeStruct((B, S, 2 * ATT_WIDTH), F32),
                   jax.ShapeDtypeStruct((B, ATT_HEADS, LANES, S), BF16),
                   jax.ShapeDtypeStruct((B, S, 3 * GDN_WIDTH), F32),
                   jax.ShapeDtypeStruct((B, S, GDN_WIDTH), F32),
                   jax.ShapeDtypeStruct((B, S, LANES), F32)),
        grid=(B, S // tm),
        in_specs=[pl.BlockSpec((1, tm, D_MODEL), row),
                  pl.BlockSpec((1, 1, 8, D_MODEL), lambda b, i: (b, (i >= n_lat_blocks).astype(jnp.int32), 0, 0)),
                  pl.BlockSpec((1, D_MODEL), lambda b, i: (0, 0)),
                  pl.BlockSpec((D_MODEL, IN_MAIN), lambda b, i: (0, 0)),
                  pl.BlockSpec((D_MODEL, LANES), lambda b, i: (0, 0))],
        out_specs=(pl.BlockSpec((1, tm, 2 * ATT_WIDTH), row),
                   pl.BlockSpec((1, ATT_HEADS, LANES, tm), lambda b, i: (b, 0, 0, i)),
                   pl.BlockSpec((1, tm, 3 * GDN_WIDTH), row),
                   pl.BlockSpec((1, tm, GDN_WIDTH), row),
                   pl.BlockSpec((1, tm, LANES), row)),
        compiler_params=_cparams(("parallel", "parallel")),
        name="inproj",
    )(h, modtab, g, w_main, w_ab)


def _qkprep_kernel(p_ref, cos_ref, sin_ref, gq_ref, gk_ref, qz_ref, k_ref):
    tm = p_ref.shape[1]
    lane = lax.broadcasted_iota(jnp.int32, (tm, LANES), 1)
    hi16 = (lane & 16) != 0
    first = lax.broadcasted_iota(jnp.int32, (LANES, tm), 0) < ATT_HEAD_DIM
    r = lax.broadcasted_iota(jnp.int32, (LANES, LANES), 0) >> 6
    c = lax.broadcasted_iota(jnp.int32, (LANES, LANES), 1) >> 6
    gmat = jnp.where(r == c, 1.0 / ATT_HEAD_DIM, 0.0).astype(F32)
    cosv = cos_ref[...]
    sinv = sin_ref[...]
    for j in range(2 * ATT_HEADS):
        x = p_ref[0, :, j * LANES:(j + 1) * LANES]
        ms = _dot_hi(x * x, gmat)
        g = gq_ref[...] if j < ATT_HEADS else gk_ref[...]
        y = x * lax.rsqrt(ms + EPS) * g
        sw = jnp.where(hi16, pltpu.roll(y, 16, 1), pltpu.roll(y, LANES - 16, 1))
        y = y * cosv + sw * sinv
        if j < ATT_HEADS:
            yt = (y * Q_SCALE).T
            qz_ref[0, j, 0] = jnp.where(first, yt, 0.0).astype(BF16)
            qz_ref[0, j, 1] = jnp.where(first, 0.0, yt).astype(BF16)
        else:
            k_ref[0, j - ATT_HEADS] = y.astype(BF16)


def _qkprep(p_qk, cos_t, sin_t, gq, gk):
    B, S, _ = p_qk.shape
    tm = ROW_TILE
    return pl.pallas_call(
        _qkprep_kernel,
        out_shape=(jax.ShapeDtypeStruct((B, ATT_HEADS, 2, LANES, S), BF16),
                   jax.ShapeDtypeStruct((B, ATT_HEADS, S, LANES), BF16)),
        grid=(B, S // tm),
        in_specs=[pl.BlockSpec((1, tm, 2 * ATT_WIDTH), lambda b, i: (b, i, 0)),
                  pl.BlockSpec((tm, LANES), lambda b, i: (i, 0)),
                  pl.BlockSpec((tm, LANES), lambda b, i: (i, 0)),
                  pl.BlockSpec((1, LANES), lambda b, i: (0, 0)),
                  pl.BlockSpec((1, LANES), lambda b, i: (0, 0))],
        out_specs=(pl.BlockSpec((1, ATT_HEADS, 2, LANES, tm), lambda b, i: (b, 0, 0, 0, i)),
                   pl.BlockSpec((1, ATT_HEADS, tm, LANES), lambda b, i: (b, 0, i, 0))),
        compiler_params=_cparams(("parallel", "parallel")),
        name="qkprep",
    )(p_qk, cos_t, sin_t, gq, gk)


def _attn_kernel(lam_ref, qz_ref, k_ref, v_ref, g_ref, *rest, aliased, tk, n_kv):
    if aliased:
        rest = rest[1:]
    o_ref, s_sc, mt_sc, m_sc, l_sc, acc_sc = rest
    m_sc[...] = jnp.full(m_sc.shape, -jnp.inf, F32)
    l_sc[...] = jnp.zeros(l_sc.shape, F32)
    acc_sc[...] = jnp.zeros(acc_sc.shape, F32)

    def rows(j):
        if isinstance(j, int):
            return pl.ds(j * tk, tk)
        return pl.ds(pl.multiple_of(j * tk, tk), tk)

    def qk_tile(j, slot):
        kt = k_ref[0, 0, rows(j), :]
        for p in range(2):
            s = _dot(kt, qz_ref[0, 0, p])
            s_sc[slot, p] = s
            mt_sc[slot, p] = jnp.max(s, axis=0, keepdims=True)

    def pv_tile(j, slot):
        for p in range(2):
            m_prev = m_sc[p]
            m_new = jnp.maximum(m_prev, mt_sc[slot, p])
            alpha = jnp.exp2(m_prev - m_new)
            lsum = None
            acc = None
            for c in range(tk // KV_CHUNK):
                cs = slice(c * KV_CHUNK, (c + 1) * KV_CHUNK)
                pe = jnp.exp2(s_sc[slot, p, cs, :] - m_new)
                ps = jnp.sum(pe, axis=0, keepdims=True)
                pv = _dot(v_ref[0, 0, j, :, cs], pe.astype(BF16))
                lsum = ps if lsum is None else lsum + ps
                acc = pv if acc is None else acc + pv
            l_sc[p] = alpha * l_sc[p] + lsum
            acc_sc[p] = alpha * acc_sc[p] + acc
            m_sc[p] = m_new

    qk_tile(0, 0)
    n_pairs = (n_kv - 1) // 2

    def pair(jj, carry):
        j = 2 * jj
        qk_tile(j + 1, 1)
        pv_tile(j, 0)
        qk_tile(j + 2, 0)
        pv_tile(j + 1, 1)
        return carry

    if n_pairs > 0:
        lax.fori_loop(0, n_pairs, pair, 0)
    j = 2 * n_pairs
    if j == n_kv - 1:
        pv_tile(j, 0)
    else:
        qk_tile(j + 1, 1)
        pv_tile(j, 0)
        pv_tile(j + 1, 1)

    o = acc_sc[0] / l_sc[0] - lam_ref[0] * (acc_sc[1] / l_sc[1])
    ms = jnp.mean(o * o, axis=0, keepdims=True)
    o_ref[0] = ((o * lax.rsqrt(ms + EPS)).T * g_ref[...]).astype(BF16)


def _attention(lam, qz, k, vt, g, *, tq, tk, q_blk0, n_q, kv_blk0, n_kv, prev_out=None):
    B, H, _, _, S = qz.shape
    aliased = prev_out is not None
    skv = tk * n_kv
    in_specs = [pl.BlockSpec(memory_space=pltpu.SMEM),
                pl.BlockSpec((1, 1, 2, LANES, tq), lambda b, h, i: (b, h, 0, 0, q_blk0 + i)),
                pl.BlockSpec((1, 1, skv, LANES), lambda b, h, i: (b, h, kv_blk0, 0)),
                pl.BlockSpec((1, 1, n_kv, LANES, tk), lambda b, h, i: (b, h, 0, 0, 0)),
                pl.BlockSpec((1, LANES), lambda b, h, i: (0, 0))]
    args = [lam, qz, k, vt, g]
    aliases = {}
    if aliased:
        in_specs.append(pl.BlockSpec(memory_space=pl.ANY))
        args.append(prev_out)
        aliases = {5: 0}
    return pl.pallas_call(
        functools.partial(_attn_kernel, aliased=aliased, tk=tk, n_kv=n_kv),
        out_shape=jax.ShapeDtypeStruct((B, S, ATT_WIDTH), BF16),
        grid=(B, H, n_q),
        in_specs=in_specs,
        out_specs=pl.BlockSpec((1, tq, LANES), lambda b, h, i: (b, q_blk0 + i, h)),
        scratch_shapes=[pltpu.VMEM((2, 2, tk, tq), F32), pltpu.VMEM((2, 2, 1, tq), F32),
                        pltpu.VMEM((2, 1, tq), F32), pltpu.VMEM((2, 1, tq), F32),
                        pltpu.VMEM((2, LANES, tq), F32)],
        input_output_aliases=aliases,
        compiler_params=_cparams(("parallel", "parallel", "arbitrary")),
        name="attn_ctx" if aliased else "attn_lat",
    )(*args)


def _attention_lat(lam, qz, k, vt, g, L):
    B, H, _, S = vt.shape
    tq = _pick_tile(L, (512, 256))
    tk = _pick_tile(S, (1280, 640, 256))
    n_kv = S // tk
    vt_tiles = vt.reshape(B, H, LANES, n_kv, tk).transpose(0, 1, 3, 2, 4)
    return _attention(lam, qz, k, vt_tiles, g, tq=tq, tk=tk, q_blk0=0, n_q=L // tq, kv_blk0=0, n_kv=n_kv)


def _attention_ctx(lam, qz, k, vt, g, L, prev_out):
    B, H, _, S = vt.shape
    Lc = S - L
    vt_ctx = vt[:, :, :, L:].reshape(B, H, 1, LANES, Lc)
    return _attention(lam, qz, k, vt_ctx, g, tq=Lc, tk=Lc, q_blk0=L // Lc, n_q=1, kv_blk0=L // Lc, n_kv=1,
                      prev_out=prev_out)


def _gdnprep_kernel(x_ref, prev_ref, next_ref, cw_ref, ab_ref, alog_ref, dtb_ref,
                    q_ref, k_ref, v_ref, gate_ref, ext_sc, *, nb_lat, nb_all):
    i = pl.program_id(1)
    tm = x_ref.shape[1]
    first = jnp.logical_or(i == 0, i == nb_lat)
    last = jnp.logical_or(i == nb_lat - 1, i == nb_all - 1)
    keep_prev = jnp.where(first, 0.0, 1.0)
    keep_next = jnp.where(last, 0.0, 1.0)
    ext_sc[0:8, :] = prev_ref[0] * keep_prev
    ext_sc[8:8 + tm, :] = x_ref[0]
    ext_sc[8 + tm:16 + tm, :] = next_ref[0] * keep_next
    acc = None
    for j in range(CONV_W):
        term = ext_sc[pl.ds(8 + j - CONV_W // 2, tm), :] * cw_ref[j:j + 1, :]
        acc = term if acc is None else acc + term
    y = acc * _sigmoid(acc)
    for part, ref in ((0, q_ref), (1, k_ref)):
        for hh in range(GDN_HEADS):
            lo = part * GDN_WIDTH + hh * GDN_HEAD_DIM
            t = y[:, lo:lo + GDN_HEAD_DIM]
            ref[0, :, hh * GDN_HEAD_DIM:(hh + 1) * GDN_HEAD_DIM] = (
                t * lax.rsqrt(jnp.sum(t * t, axis=-1, keepdims=True) + EPS))
    v_ref[0] = y[:, 2 * GDN_WIDTH:3 * GDN_WIDTH]
    ab = ab_ref[0]
    xs = ab + dtb_ref[...]
    sp = jnp.maximum(xs, 0.0) + jnp.log(1.0 + jnp.exp(-jnp.abs(xs)))
    g = -jnp.exp(alog_ref[...]) * sp
    lane = lax.broadcasted_iota(jnp.int32, ab.shape, 1)
    gate_ref[0] = jnp.where(lane < 2 * GDN_HEADS, g, _sigmoid(ab))


def _gdnprep(p_gdn, conv_w8, ab, alog_row, dtb_row, n_lat_blocks):
    B, S, W = p_gdn.shape
    tm = ROW_TILE
    nb = S // tm
    r8 = tm // 8
    row = lambda b, i: (b, i, 0)
    return pl.pallas_call(
        functools.partial(_gdnprep_kernel, nb_lat=n_lat_blocks, nb_all=nb),
        out_shape=(jax.ShapeDtypeStruct((B, S, GDN_WIDTH), F32),
                   jax.ShapeDtypeStruct((B, S, GDN_WIDTH), F32),
                   jax.ShapeDtypeStruct((B, S, GDN_WIDTH), F32),
                   jax.ShapeDtypeStruct((B, S, LANES), F32)),
        grid=(B, nb),
        in_specs=[pl.BlockSpec((1, tm, W), row),
                  pl.BlockSpec((1, 8, W), lambda b, i: (b, jnp.maximum(i * r8 - 1, 0), 0)),
                  pl.BlockSpec((1, 8, W), lambda b, i: (b, jnp.minimum((i + 1) * r8, nb * r8 - 1), 0)),
                  pl.BlockSpec((8, W), lambda b, i: (0, 0)),
                  pl.BlockSpec((1, tm, LANES), row),
                  pl.BlockSpec((1, LANES), lambda b, i: (0, 0)),
                  pl.BlockSpec((1, LANES), lambda b, i: (0, 0))],
        out_specs=(pl.BlockSpec((1, tm, GDN_WIDTH), row),
                   pl.BlockSpec((1, tm, GDN_WIDTH), row),
                   pl.BlockSpec((1, tm, GDN_WIDTH), row),
                   pl.BlockSpec((1, tm, LANES), row)),
        scratch_shapes=[pltpu.VMEM((tm + 16, W), F32)],
        compiler_params=_cparams(("parallel", "parallel")),
        name="gdnprep",
    )(p_gdn, p_gdn, p_gdn, conv_w8, ab, alog_row, dtb_row)


def _gdnchunk_kernel(q_ref, k_ref, v_ref, gate_ref, qg_ref, w_ref, kd_ref, u_ref, aq_ref, eg_ref):
    tm = q_ref.shape[1]
    nc = tm // GDN_CHUNK
    gates = gate_ref[0]
    ri = lax.broadcasted_iota(jnp.int32, (tm, tm), 0)
    ci = lax.broadcasted_iota(jnp.int32, (tm, tm), 1)
    same = (ri >> 6) == (ci >> 6)
    eye = ri == ci
    tot = _dot_hi(jnp.where(same, 1.0, 0.0).astype(F32), gates)
    eye_f = jnp.where(eye, 1.0, 0.0).astype(F32)

    heads = []
    for hh in range(GDN_HEADS):
        sl = slice(hh * GDN_HEAD_DIM, (hh + 1) * GDN_HEAD_DIM)
        k = k_ref[0, :, sl]
        k16 = k.astype(BF16)
        q = q_ref[0, :, sl] * (GDN_HEAD_DIM ** -0.5)
        heads.append((sl, k, q, _dot_nt(k16, k16), _dot_nt(q.astype(BF16), k16)))

    chains = []
    for d in range(2):
        incl = jnp.logical_and(same, (ci <= ri) if d == 0 else (ci >= ri))
        strict = jnp.logical_and(incl, jnp.logical_not(eye))
        gc = _dot_hi(jnp.where(incl, 1.0, 0.0).astype(F32), gates)
        gc_t = gc.T
        for hh in range(GDN_HEADS):
            ln = d * GDN_HEADS + hh
            gcol = gc[:, ln:ln + 1]
            bcol = gates[:, 2 * GDN_HEADS + ln:2 * GDN_HEADS + ln + 1]
            tcol = tot[:, ln:ln + 1]
            diff = gcol - gc_t[ln:ln + 1, :]
            decay = jnp.where(incl, jnp.exp(jnp.where(incl, diff, 0.0)), 0.0)
            a = jnp.where(strict, heads[hh][3] * bcol * decay, 0.0)
            aqk = heads[hh][4] * decay
            chains.append(dict(d=d, hh=hh, x=-a, t=eye_f - a, aqk=aqk, gcol=gcol, bcol=bcol, tcol=tcol))

    for _ in range(5):
        for ch in chains:
            x16 = ch["x"].astype(BF16)
            ch["x"] = _dot(x16, x16)
        for ch in chains:
            ch["t"] = ch["t"] + _dot(ch["t"].astype(BF16), ch["x"].astype(BF16))

    for ch in chains:
        d, hh, gcol, bcol, tcol = ch["d"], ch["hh"], ch["gcol"], ch["bcol"], ch["tcol"]
        sl, k, q = heads[hh][0], heads[hh][1], heads[hh][2]
        t16 = ch["t"].astype(BF16)
        egc = jnp.exp(gcol)
        kb = k * bcol
        u_ref[0, d, :, sl] = _dot(t16, (v_ref[0, :, sl] * bcol).astype(BF16))
        w_ref[0, d, :, sl] = _dot(t16, (kb * egc).astype(BF16)).astype(BF16)
        kd_ref[0, d, :, sl] = (k * jnp.exp(tcol - gcol)).astype(BF16)
        qg_ref[0, d, :, sl] = (q * egc).astype(BF16)
        for cc in range(nc):
            rs = slice(cc * GDN_CHUNK, (cc + 1) * GDN_CHUNK)
            aq_ref[0, d, rs, hh * GDN_CHUNK:(hh + 1) * GDN_CHUNK] = ch["aqk"][rs, rs].astype(BF16)
            eg_ref[0, d, cc, hh:hh + 1, :] = jnp.broadcast_to(
                jnp.exp(tcol[cc * GDN_CHUNK:cc * GDN_CHUNK + 1, :]), (1, LANES))


def _gdnchunk(gq, gk, gv, gates):
    B, S, W = gq.shape
    tm = ROW_TILE
    nc = tm // GDN_CHUNK
    row = lambda b, i: (b, i, 0)
    drow = lambda b, i: (b, 0, i, 0)
    big = lambda dt: jax.ShapeDtypeStruct((B, 2, S, W), dt)
    return pl.pallas_call(
        _gdnchunk_kernel,
        out_shape=(big(BF16), big(BF16), big(BF16), big(F32),
                   jax.ShapeDtypeStruct((B, 2, S, GDN_HEADS * GDN_CHUNK), BF16),
                   jax.ShapeDtypeStruct((B, 2, S // GDN_CHUNK, GDN_HEADS, LANES), F32)),
        grid=(B, S // tm),
        in_specs=[pl.BlockSpec((1, tm, W), row), pl.BlockSpec((1, tm, W), row),
                  pl.BlockSpec((1, tm, W), row), pl.BlockSpec((1, tm, LANES), row)],
        out_specs=(pl.BlockSpec((1, 2, tm, W), drow), pl.BlockSpec((1, 2, tm, W), drow),
                   pl.BlockSpec((1, 2, tm, W), drow), pl.BlockSpec((1, 2, tm, W), drow),
                   pl.BlockSpec((1, 2, tm, GDN_HEADS * GDN_CHUNK), drow),
                   pl.BlockSpec((1, 2, nc, GDN_HEADS, LANES), lambda b, i: (b, 0, i, 0, 0))),
        compiler_params=_cparams(("parallel", "parallel")),
        name="gdnchunk",
    )(gq, gk, gv, gates)


def _gdnscan_kernel(qg0, w0, kd0, u0, aq0, eg0, qg1, w1, kd1, u1, aq1, eg1, of_ref, ob_ref, s_sc):
    @pl.when(pl.program_id(1) == 0)
    def _():
        s_sc[...] = jnp.zeros(s_sc.shape, F32)

    dirs = ((qg0, w0, kd0, u0, aq0, eg0, of_ref), (qg1, w1, kd1, u1, aq1, eg1, ob_ref))
    chains = [(d, hh) for d in range(2) for hh in range(GDN_HEADS)]
    st, st16, vn16, qs = {}, {}, {}, {}
    for d, hh in chains:
        st[d, hh] = s_sc[d, hh]
        st16[d, hh] = st[d, hh].astype(BF16)
    for d, hh in chains:
        qg, w, kd, u, aq, eg, o_ref = dirs[d]
        sl = slice(hh * GDN_HEAD_DIM, (hh + 1) * GDN_HEAD_DIM)
        vn16[d, hh] = (u[0, 0, :, sl] - _dot(w[0, 0, :, sl], st16[d, hh])).astype(BF16)
        qs[d, hh] = _dot(qg[0, 0, :, sl], st16[d, hh])
    for d, hh in chains:
        qg, w, kd, u, aq, eg, o_ref = dirs[d]
        sl = slice(hh * GDN_HEAD_DIM, (hh + 1) * GDN_HEAD_DIM)
        o_ref[0, :, sl] = qs[d, hh] + _dot(aq[0, 0, :, hh * GDN_CHUNK:(hh + 1) * GDN_CHUNK], vn16[d, hh])
        s_sc[d, hh] = st[d, hh] * eg[0, 0, 0, hh:hh + 1, :] + _dot_tn(kd[0, 0, :, sl], vn16[d, hh])


def _gdnscan(qg, w, kd, u, aq, eg, n_lat_chunks, n_ctx_chunks):
    B, _, S, W = qg.shape
    C = GDN_CHUNK
    n = S // C

    def fwd_chunk(i):
        return jnp.where(i < n_ctx_chunks, n_lat_chunks + i, i - n_ctx_chunks)

    def bwd_chunk(i):
        return jnp.where(i < n_ctx_chunks, n_lat_chunks + n_ctx_chunks - 1 - i, n - 1 - i)

    def specs(d, chunk_of):
        big = pl.BlockSpec((1, 1, C, W), lambda b, i: (b, d, chunk_of(i), 0))
        return [big, big, big, big,
                pl.BlockSpec((1, 1, C, GDN_HEADS * C), lambda b, i: (b, d, chunk_of(i), 0)),
                pl.BlockSpec((1, 1, 1, GDN_HEADS, LANES), lambda b, i: (b, d, chunk_of(i), 0, 0))]

    return pl.pallas_call(
        _gdnscan_kernel,
        out_shape=(jax.ShapeDtypeStruct((B, S, W), F32), jax.ShapeDtypeStruct((B, S, W), F32)),
        grid=(B, n),
        in_specs=specs(0, fwd_chunk) + specs(1, bwd_chunk),
        out_specs=(pl.BlockSpec((1, C, W), lambda b, i: (b, fwd_chunk(i), 0)),
                   pl.BlockSpec((1, C, W), lambda b, i: (b, bwd_chunk(i), 0))),
        scratch_shapes=[pltpu.VMEM((2, GDN_HEADS, GDN_HEAD_DIM, GDN_HEAD_DIM), F32)],
        compiler_params=_cparams(("parallel", "arbitrary")),
        name="gdnscan",
    )(qg, w, kd, u, aq, eg, qg, w, kd, u, aq, eg)


def _mixout_kernel(oa_ref, of_ref, ob_ref, z_ref, h_ref, mod_ref, gg_ref, wo_ref, gf_ref, rw_ref, rb_ref,
                   hn_ref, v_ref, te_ref, tg_ref):
    og = of_ref[0] + ob_ref[0]
    z = z_ref[0]
    parts = [oa_ref[0]]
    for hh in range(GDN_HEADS):
        sl = slice(hh * GDN_HEAD_DIM, (hh + 1) * GDN_HEAD_DIM)
        t = og[:, sl]
        t = t * lax.rsqrt(jnp.mean(t * t, axis=-1, keepdims=True) + EPS) * gg_ref[...]
        zz = z[:, sl]
        parts.append((t * (zz * _sigmoid(zz))).astype(BF16))
    mix_in = jnp.concatenate(parts, axis=-1)
    mix = _dot(mix_in, wo_ref[...])
    hn = h_ref[0] + mod_ref[0, 0, 2:3, :] * mix
    hn_ref[0] = hn
    y = hn * lax.rsqrt(jnp.mean(hn * hn, axis=-1, keepdims=True) + EPS) * gf_ref[...]
    v = y * (1.0 + mod_ref[0, 0, 4:5, :]) + mod_ref[0, 0, 3:4, :]
    v_ref[0] = v.astype(BF16)
    logits = _dot_hi(v, rw_ref[...]) + rb_ref[...]
    lane = lax.broadcasted_iota(jnp.int32, logits.shape, 1)
    cur = logits
    vals, idxs = [], []
    for _ in range(TOP_K):
        m = jnp.max(cur, axis=-1, keepdims=True)
        idx = jnp.min(jnp.where(cur == m, lane, LANES), axis=-1, keepdims=True)
        vals.append(m)
        idxs.append(idx)
        cur = jnp.where(lane == idx, -jnp.inf, cur)
    es = [jnp.exp(vv - vals[0]) for vv in vals]
    inv = 1.0 / (es[0] + es[1] + es[2] + es[3])
    te = jnp.zeros(logits.shape, jnp.int32)
    tg = jnp.zeros(logits.shape, F32)
    for kk in range(TOP_K):
        te = jnp.where(lane == kk, idxs[kk], te)
        tg = jnp.where(lane == kk, es[kk] * inv, tg)
    te_ref[0] = te
    tg_ref[0] = tg


def _mixout(o_att, o_f, o_b, z, h, modtab, gg, w_out, gf, rw, rb, n_lat_blocks):
    B, S, _ = h.shape
    tm = ROW_TILE
    row = lambda b, i: (b, i, 0)
    const = lambda b, i: (0, 0)
    return pl.pallas_call(
        _mixout_kernel,
        out_shape=(jax.ShapeDtypeStruct((B, S, D_MODEL), F32),
                   jax.ShapeDtypeStruct((B, S, D_MODEL), BF16),
                   jax.ShapeDtypeStruct((B, S, LANES), jnp.int32),
                   jax.ShapeDtypeStruct((B, S, LANES), F32)),
        grid=(B, S // tm),
        in_specs=[pl.BlockSpec((1, tm, ATT_WIDTH), row),
                  pl.BlockSpec((1, tm, GDN_WIDTH), row),
                  pl.BlockSpec((1, tm, GDN_WIDTH), row),
                  pl.BlockSpec((1, tm, GDN_WIDTH), row),
                  pl.BlockSpec((1, tm, D_MODEL), row),
                  pl.BlockSpec((1, 1, 8, D_MODEL), lambda b, i: (b, (i >= n_lat_blocks).astype(jnp.int32), 0, 0)),
                  pl.BlockSpec((1, LANES), const),
                  pl.BlockSpec((D_MODEL, D_MODEL), const),
                  pl.BlockSpec((1, D_MODEL), const),
                  pl.BlockSpec((D_MODEL, LANES), const),
                  pl.BlockSpec((1, LANES), const)],
        out_specs=(pl.BlockSpec((1, tm, D_MODEL), row),
                   pl.BlockSpec((1, tm, D_MODEL), row),
                   pl.BlockSpec((1, tm, LANES), row),
                   pl.BlockSpec((1, tm, LANES), row)),
        compiler_params=_cparams(("parallel", "parallel")),
        name="mixout",
    )(o_att, o_f, o_b, z, h, modtab, gg, w_out, gf, rw, rb)


def _expert_kernel(be_ref, nv_ref, x_ref, wgu_ref, bgu_ref, wd_ref, bd_ref, sg_ref, y_ref, wgu_sc, wd_sc):
    i = pl.program_id(0)
    new_expert = jnp.logical_or(i == 0, be_ref[i] != be_ref[jnp.maximum(i - 1, 0)])

    @pl.when(new_expert)
    def _():
        wgu_sc[...] = wgu_ref[0].astype(BF16)
        wd_sc[...] = wd_ref[0].astype(BF16)

    @pl.when(nv_ref[i] > 0)
    def _():
        gu = _dot(x_ref[...], wgu_sc[...]) + bgu_ref[0]
        g_ = jnp.minimum(gu[:, :D_EXPERT], SWIGLU_LIMIT)
        up = jnp.clip(gu[:, D_EXPERT:], -SWIGLU_LIMIT, SWIGLU_LIMIT)
        glu = g_ * _sigmoid(SWIGLU_ALPHA * g_)
        act = ((up + 1.0) * glu).astype(BF16)
        y_ref[...] = (_dot(act, wd_sc[...]) + bd_ref[0]) * sg_ref[...]

    @pl.when(nv_ref[i] == 0)
    def _():
        y_ref[...] = jnp.zeros(y_ref.shape, F32)


def _experts(block_e, n_valid, x_sorted, wgu, bgu, wd, bd, slot_gate):
    n_slots = x_sorted.shape[0]
    nb = n_slots // MOE_BLOCK
    grid_spec = pltpu.PrefetchScalarGridSpec(
        num_scalar_prefetch=2,
        grid=(nb,),
        in_specs=[pl.BlockSpec((MOE_BLOCK, D_MODEL), lambda i, be, nv: (i, 0)),
                  pl.BlockSpec((1, D_MODEL, 2 * D_EXPERT), lambda i, be, nv: (be[i], 0, 0)),
                  pl.BlockSpec((1, 1, 2 * D_EXPERT), lambda i, be, nv: (be[i], 0, 0)),
                  pl.BlockSpec((1, D_EXPERT, D_MODEL), lambda i, be, nv: (be[i], 0, 0)),
                  pl.BlockSpec((1, 1, D_MODEL), lambda i, be, nv: (be[i], 0, 0)),
                  pl.BlockSpec((MOE_BLOCK, 1), lambda i, be, nv: (i, 0))],
        out_specs=pl.BlockSpec((MOE_BLOCK, D_MODEL), lambda i, be, nv: (i, 0)),
        scratch_shapes=[pltpu.VMEM((D_MODEL, 2 * D_EXPERT), BF16), pltpu.VMEM((D_EXPERT, D_MODEL), BF16)],
    )
    return pl.pallas_call(
        _expert_kernel,
        out_shape=jax.ShapeDtypeStruct((n_slots, D_MODEL), F32),
        grid_spec=grid_spec,
        compiler_params=pltpu.CompilerParams(dimension_semantics=("arbitrary",),
                                             vmem_limit_bytes=EXPERT_VMEM_LIMIT),
        name="experts",
    )(block_e, n_valid, x_sorted, wgu, bgu, wd, bd, slot_gate)


def _moe_plan(top_e, top_g):
    T = top_e.shape[0]
    n_assign = T * TOP_K
    n_blocks = -(-n_assign // MOE_BLOCK) + N_EXPERTS
    n_slots = n_blocks * MOE_BLOCK
    i32 = jnp.int32
    flat_e = top_e.reshape(-1)
    gate_flat = top_g.reshape(-1)
    order = jnp.argsort(flat_e).astype(i32)
    rank = jnp.argsort(order).astype(i32)
    e_sorted = flat_e[order]
    e_ids = jnp.arange(N_EXPERTS, dtype=i32)
    start = jnp.searchsorted(e_sorted, e_ids, side='left').astype(i32)
    counts = jnp.searchsorted(e_sorted, e_ids, side='right').astype(i32) - start
    padded = (counts + MOE_BLOCK - 1) // MOE_BLOCK * MOE_BLOCK
    pad_end = jnp.cumsum(padded)
    pad_start = pad_end - padded
    block_e = jnp.minimum(jnp.searchsorted(pad_end, jnp.arange(n_blocks, dtype=i32) * MOE_BLOCK, side='right'),
                          N_EXPERTS - 1).astype(i32)
    e_slot = jnp.repeat(block_e, MOE_BLOCK)
    off = jnp.arange(n_slots, dtype=i32) - pad_start[e_slot]
    valid = off < counts[e_slot]
    a_slot = order[jnp.clip(off + start[e_slot], 0, n_assign - 1)]
    slot_tok = jnp.where(valid, a_slot // TOP_K, 0)
    slot_gate = jnp.where(valid, gate_flat[a_slot], 0.0)
    n_valid = jnp.sum(valid.reshape(n_blocks, MOE_BLOCK), axis=1).astype(i32)
    slot_of = (pad_start[flat_e] + rank - start[flat_e]).astype(i32)
    return slot_tok, slot_gate.reshape(n_slots, 1), block_e, n_valid, slot_of


def _combine_kernel(y_ref, h_ref, mod_ref, o_ref):
    y = y_ref[0, 0] + y_ref[1, 0] + y_ref[2, 0] + y_ref[3, 0]
    o_ref[0] = h_ref[0] + mod_ref[0, 0, 5:6, :] * y


def _combine(y4, h, modtab, n_lat_blocks):
    B, S, _ = h.shape
    tm = ROW_TILE
    return pl.pallas_call(
        _combine_kernel,
        out_shape=jax.ShapeDtypeStruct((B, S, D_MODEL), F32),
        grid=(B, S // tm),
        in_specs=[pl.BlockSpec((TOP_K, 1, tm, D_MODEL), lambda b, i: (0, b, i, 0)),
                  pl.BlockSpec((1, tm, D_MODEL), lambda b, i: (b, i, 0)),
                  pl.BlockSpec((1, 1, 8, D_MODEL), lambda b, i: (b, (i >= n_lat_blocks).astype(jnp.int32), 0, 0))],
        out_specs=pl.BlockSpec((1, tm, D_MODEL), lambda b, i: (b, i, 0)),
        compiler_params=_cparams(("parallel", "parallel")),
        name="combine",
    )(y4, h, modtab)


def _moe(v_ffn, top_e, top_g, h_new, modtab, wgu, bgu, wd, bd, n_lat_blocks):
    B, S, _ = v_ffn.shape
    T = B * S
    slot_tok, slot_gate, block_e, n_valid, slot_of = _moe_plan(
        top_e.reshape(T, LANES)[:, :TOP_K], top_g.reshape(T, LANES)[:, :TOP_K])
    x_sorted = v_ffn.reshape(T, D_MODEL)[slot_tok]
    y_sorted = _experts(block_e, n_valid, x_sorted, wgu, bgu, wd, bd, slot_gate)
    y4 = y_sorted[slot_of.reshape(T, TOP_K).T.reshape(-1)].reshape(TOP_K, B, S, D_MODEL)
    return _combine(y4, h_new, modtab, n_lat_blocks)


def _pick_tile(n, cands):
    for t in cands:
        if n % t == 0:
            return t
    raise ValueError(f"no tile for {n}")


def _rope_tables(L, Lc):
    rows = L // GRID_W
    row = jnp.repeat(jnp.arange(rows, dtype=F32), GRID_W)
    col = (jnp.arange(L, dtype=jnp.int32) % GRID_W).astype(F32)
    inv_freq = ROPE_BASE ** (-jnp.arange(ROPE_PAIRS, dtype=F32) / ROPE_PAIRS)
    ar = row[:, None] * inv_freq
    ac = col[:, None] * inv_freq
    cos64 = jnp.concatenate([jnp.cos(ar), jnp.cos(ar), jnp.cos(ac), jnp.cos(ac)], axis=-1)
    sin64 = jnp.concatenate([-jnp.sin(ar), jnp.sin(ar), -jnp.sin(ac), jnp.sin(ac)], axis=-1)
    cos_t = jnp.concatenate([jnp.tile(cos64, (1, 2)), jnp.ones((Lc, LANES), F32)], axis=0)
    sin_t = jnp.concatenate([jnp.tile(sin64, (1, 2)), jnp.zeros((Lc, LANES), F32)], axis=0)
    return cos_t, sin_t


def _pad_lanes(v):
    v = v.reshape(1, -1).astype(F32)
    return jnp.pad(v, ((0, 0), (0, LANES - v.shape[1])))


def kernel(x, c, ctx, c_ctx, w_mod, b_mod, norm_mix_g, w_in, q_norm_g, k_norm_g, lam_q1, lam_k1, lam_q2, lam_k2, subln_g, conv_w, a_log, dt_bias, gdn_norm_g, w_out, norm_ffn_g, router_w, router_b, w_gate_up, b_gate_up, w_down, b_down):
    B, L, D = x.shape
    Lc = ctx.shape[1]
    S = L + Lc
    depth = w_mod.shape[0]
    tm = ROW_TILE
    n_lat_blocks = L // tm
    cos_t, sin_t = _rope_tables(L, Lc)

    c_rows = jnp.zeros((8, D), F32).at[:B].set(c).at[B].set(c_ctx)
    h = jnp.concatenate([x, ctx], axis=1)

    for layer in range(depth):
        mod = _adaln(c_rows, w_mod[layer], b_mod[layer])
        mod6 = mod.reshape(8, 6, D)
        lat_mod = mod6[:B]
        ctx_mod = jnp.broadcast_to(mod6[B][None], (B, 6, D))
        modtab = jnp.pad(jnp.stack([lat_mod, ctx_mod], axis=1), ((0, 0), (0, 0), (0, 2), (0, 0)))

        lam_init = 0.8 - 0.6 * math.exp(-0.3 * layer)
        lam_full = (jnp.exp(jnp.sum(lam_q1[layer] * lam_k1[layer]))
                    - jnp.exp(jnp.sum(lam_q2[layer] * lam_k2[layer])) + lam_init).reshape(1).astype(F32)

        w_l = w_in[layer]
        w_main = w_l[:, :IN_MAIN].astype(BF16)
        w_ab = jnp.pad(w_l[:, IN_MAIN:], ((0, 0), (0, LANES - (w_l.shape[1] - IN_MAIN)))).astype(BF16)
        p_qk, v_att, p_gdn, z, ab = _inproj(h, modtab, norm_mix_g[layer].reshape(1, D), w_main, w_ab,
                                            n_lat_blocks)

        gq = jnp.tile(q_norm_g[layer].reshape(1, ATT_HEAD_DIM), (1, 2))
        gk = jnp.tile(k_norm_g[layer].reshape(1, ATT_HEAD_DIM), (1, 2))
        qz, k_att = _qkprep(p_qk, cos_t, sin_t, gq, gk)
        g_sub = (subln_g[layer] * (1.0 - lam_init)).reshape(1, LANES).astype(F32)
        o_att = _attention_lat(lam_full, qz, k_att, v_att, g_sub, L)
        o_att = _attention_ctx(lam_full, qz, k_att, v_att, g_sub, L, o_att)

        conv_w8 = jnp.pad(conv_w[layer], ((0, 8 - CONV_W), (0, 0)))
        gq_g, gk_g, gv_g, gates = _gdnprep(p_gdn, conv_w8, ab, _pad_lanes(a_log[layer]),
                                           _pad_lanes(dt_bias[layer]), n_lat_blocks)
        qg, w_g, kd, u_g, aq, eg = _gdnchunk(gq_g, gk_g, gv_g, gates)
        o_f, o_b = _gdnscan(qg, w_g, kd, u_g, aq, eg, L // GDN_CHUNK, Lc // GDN_CHUNK)

        rw = jnp.pad(router_w[layer], ((0, 0), (0, LANES - N_EXPERTS)))
        rb = jnp.pad(router_b[layer].reshape(1, N_EXPERTS).astype(F32), ((0, 0), (0, LANES - N_EXPERTS)),
                     constant_values=-1e30)
        h_new, v_ffn, top_e, top_g = _mixout(
            o_att, o_f, o_b, z, h, modtab, gdn_norm_g[layer].reshape(1, LANES), w_out[layer].astype(BF16),
            norm_ffn_g[layer].reshape(1, D), rw, rb, n_lat_blocks)

        h = _moe(v_ffn, top_e, top_g, h_new, modtab,
                 w_gate_up[layer], b_gate_up[layer].reshape(N_EXPERTS, 1, 2 * D_EXPERT),
                 w_down[layer], b_down[layer].reshape(N_EXPERTS, 1, D), n_lat_blocks)
    return h[:, :L]
```

```python
import functools
import math

import jax
import jax.numpy as jnp
from jax import lax
from jax.experimental import pallas as pl
from jax.experimental.pallas import tpu as pltpu

F32 = jnp.float32
BF16 = jnp.bfloat16
HIGHEST = lax.Precision.HIGHEST

D_MODEL = 1024
GRID_W = 64
EPS = 1e-6
ATT_WIDTH = 512
ATT_HEAD_DIM = 64
ATT_HEADS = 4
ROPE_BASE = 10000.0
ROPE_PAIRS = ATT_HEAD_DIM // 4
GDN_WIDTH = 512
GDN_HEAD_DIM = 128
GDN_HEADS = 4
GDN_CHUNK = 64
CONV_W = 5
IN_MAIN = 3 * ATT_WIDTH + 4 * GDN_WIDTH
N_EXPERTS = 32
TOP_K = 4
D_EXPERT = 1024
SWIGLU_ALPHA = 1.702
SWIGLU_LIMIT = 7.0
MOE_BLOCK = 512

LANES = 128
ROW_TILE = 256
KV_CHUNK = 256
Q_SCALE = ATT_HEAD_DIM ** -0.5 * math.log2(math.e)
VMEM_LIMIT = 48 * 1024 * 1024
EXPERT_VMEM_LIMIT = 56 * 1024 * 1024


def _cparams(sem):
    return pltpu.CompilerParams(dimension_semantics=sem, vmem_limit_bytes=VMEM_LIMIT)


def _dot(a, b):
    return jnp.dot(a, b, preferred_element_type=F32)


def _dot_nt(a, b):
    return lax.dot_general(a, b, (((1,), (1,)), ((), ())), preferred_element_type=F32)


def _dot_tn(a, b):
    return lax.dot_general(a, b, (((0,), (0,)), ((), ())), preferred_element_type=F32)


def _dot_hi(a, b):
    return jnp.dot(a, b, preferred_element_type=F32, precision=HIGHEST)


def _sigmoid(x):
    return 1.0 / (1.0 + jnp.exp(-x))


def _adaln_kernel(c_ref, w_ref, b_ref, o_ref):
    c = c_ref[...]
    s = c * _sigmoid(c)
    o_ref[...] = _dot_hi(s, w_ref[0]) + b_ref[0]


def _adaln(c_rows, w, b, layer):
    depth, _, n = w.shape
    tn = 1024
    return pl.pallas_call(
        _adaln_kernel,
        out_shape=jax.ShapeDtypeStruct((8, n), F32),
        grid=(n // tn,),
        in_specs=[pl.BlockSpec((8, D_MODEL), lambda j: (0, 0)),
                  pl.BlockSpec((1, D_MODEL, tn), lambda j: (layer, 0, j)),
                  pl.BlockSpec((1, 1, tn), lambda j: (layer, 0, j))],
        out_specs=pl.BlockSpec((8, tn), lambda j: (0, j)),
        compiler_params=_cparams(("arbitrary",)),
        name="adaln",
    )(c_rows, w, b.reshape(depth, 1, n))


def _inproj_kernel(h_ref, mod_ref, g_ref, w_ref, wab_ref, qk_ref, v_ref, gdn_ref, z_ref, ab_ref):
    x = h_ref[0]
    ms = jnp.mean(x * x, axis=-1, keepdims=True)
    y = x * lax.rsqrt(ms + EPS) * g_ref[...]
    shift = mod_ref[0, 0, 0:1, :]
    scale = mod_ref[0, 0, 1:2, :]
    u = (y * (1.0 + scale) + shift).astype(BF16)
    qk_ref[0] = _dot(u, w_ref[:, 0:2 * ATT_WIDTH])
    vv = _dot(u, w_ref[:, 2 * ATT_WIDTH:3 * ATT_WIDTH])
    for hh in range(ATT_HEADS):
        v_ref[0, hh] = vv[:, hh * LANES:(hh + 1) * LANES].T.astype(BF16)
    off = 3 * ATT_WIDTH
    gdn_ref[0] = _dot(u, w_ref[:, off:off + 3 * GDN_WIDTH])
    z_ref[0] = _dot(u, w_ref[:, off + 3 * GDN_WIDTH:off + 4 * GDN_WIDTH])
    ab_ref[0] = _dot(u, wab_ref[...])


def _inproj(h, modtab, g, w_main, w_ab, n_lat_blocks):
    B, S, _ = h.shape
    tm = ROW_TILE
    row = lambda b, i: (b, i, 0)
    return pl.pallas_call(
        _inproj_kernel,
        out_shape=(jax.ShapeDtypeStruct((B, S, 2 * ATT_WIDTH), F32),
                   jax.ShapeDtypeStruct((B, ATT_HEADS, LANES, S), BF16),
                   jax.ShapeDtypeStruct((B, S, 3 * GDN_WIDTH), F32),
                   jax.ShapeDtypeStruct((B, S, GDN_WIDTH), F32),
                   jax.ShapeDtypeStruct((B, S, LANES), F32)),
        grid=(B, S // tm),
        in_specs=[pl.BlockSpec((1, tm, D_MODEL), row),
                  pl.BlockSpec((1, 1, 8, D_MODEL), lambda b, i: (b, (i >= n_lat_blocks).astype(jnp.int32), 0, 0)),
                  pl.BlockSpec((1, D_MODEL), lambda b, i: (0, 0)),
                  pl.BlockSpec((D_MODEL, IN_MAIN), lambda b, i: (0, 0)),
                  pl.BlockSpec((D_MODEL, LANES), lambda b, i: (0, 0))],
        out_specs=(pl.BlockSpec((1, tm, 2 * ATT_WIDTH), row),
                   pl.BlockSpec((1, ATT_HEADS, LANES, tm), lambda b, i: (b, 0, 0, i)),
                   pl.BlockSpec((1, tm, 3 * GDN_WIDTH), row),
                   pl.BlockSpec((1, tm, GDN_WIDTH), row),
                   pl.BlockSpec((1, tm, LANES), row)),
        compiler_params=_cparams(("parallel", "parallel")),
        name="inproj",
    )(h, modtab, g, w_main, w_ab)


def _qkprep_kernel(p_ref, cos_ref, sin_ref, gq_ref, gk_ref, qz_ref, k_ref):
    tm = p_ref.shape[1]
    lane = lax.broadcasted_iota(jnp.int32, (tm, LANES), 1)
    hi16 = (lane & 16) != 0
    first = lax.broadcasted_iota(jnp.int32, (LANES, tm), 0) < ATT_HEAD_DIM
    r = lax.broadcasted_iota(jnp.int32, (LANES, LANES), 0) >> 6
    c = lax.broadcasted_iota(jnp.int32, (LANES, LANES), 1) >> 6
    gmat = jnp.where(r == c, 1.0 / ATT_HEAD_DIM, 0.0).astype(F32)
    cosv = cos_ref[...]
    sinv = sin_ref[...]
    for j in range(2 * ATT_HEADS):
        x = p_ref[0, :, j * LANES:(j + 1) * LANES]
        ms = _dot_hi(x * x, gmat)
        g = gq_ref[...] if j < ATT_HEADS else gk_ref[...]
        y = x * lax.rsqrt(ms + EPS) * g
        sw = jnp.where(hi16, pltpu.roll(y, 16, 1), pltpu.roll(y, LANES - 16, 1))
        y = y * cosv + sw * sinv
        if j < ATT_HEADS:
            yt = (y * Q_SCALE).T
            qz_ref[0, j, 0] = jnp.where(first, yt, 0.0).astype(BF16)
            qz_ref[0, j, 1] = jnp.where(first, 0.0, yt).astype(BF16)
        else:
            k_ref[0, j - ATT_HEADS] = y.astype(BF16)


def _qkprep(p_qk, cos_t, sin_t, gq, gk):
    B, S, _ = p_qk.shape
    tm = ROW_TILE
    return pl.pallas_call(
        _qkprep_kernel,
        out_shape=(jax.ShapeDtypeStruct((B, ATT_HEADS, 2, LANES, S), BF16),
                   jax.ShapeDtypeStruct((B, ATT_HEADS, S, LANES), BF16)),
        grid=(B, S // tm),
        in_specs=[pl.BlockSpec((1, tm, 2 * ATT_WIDTH), lambda b, i: (b, i, 0)),
                  pl.BlockSpec((tm, LANES), lambda b, i: (i, 0)),
                  pl.BlockSpec((tm, LANES), lambda b, i: (i, 0)),
                  pl.BlockSpec((1, LANES), lambda b, i: (0, 0)),
                  pl.BlockSpec((1, LANES), lambda b, i: (0, 0))],
        out_specs=(pl.BlockSpec((1, ATT_HEADS, 2, LANES, tm), lambda b, i: (b, 0, 0, 0, i)),
                   pl.BlockSpec((1, ATT_HEADS, tm, LANES), lambda b, i: (b, 0, i, 0))),
        compiler_params=_cparams(("parallel", "parallel")),
        name="qkprep",
    )(p_qk, cos_t, sin_t, gq, gk)


def _attn_kernel(lam_ref, qz_ref, k_ref, v_ref, g_ref, *rest, aliased, tk, n_kv):
    if aliased:
        rest = rest[1:]
    o_ref, s_sc, mt_sc, m_sc, l_sc, acc_sc = rest
    m_sc[...] = jnp.full(m_sc.shape, -jnp.inf, F32)
    l_sc[...] = jnp.zeros(l_sc.shape, F32)
    acc_sc[...] = jnp.zeros(acc_sc.shape, F32)

    def rows(j):
        if isinstance(j, int):
            return pl.ds(j * tk, tk)
        return pl.ds(pl.multiple_of(j * tk, tk), tk)

    def qk_tile(j, slot):
        kt = k_ref[0, 0, rows(j), :]
        for p in range(2):
            s = _dot(kt, qz_ref[0, 0, p])
            s_sc[slot, p] = s
            mt_sc[slot, p] = jnp.max(s, axis=0, keepdims=True)

    def pv_tile(j, slot):
        for p in range(2):
            m_prev = m_sc[p]
            m_new = jnp.maximum(m_prev, mt_sc[slot, p])
            alpha = jnp.exp2(m_prev - m_new)
            lsum = None
            acc = None
            for c in range(tk // KV_CHUNK):
                cs = slice(c * KV_CHUNK, (c + 1) * KV_CHUNK)
                pe = jnp.exp2(s_sc[slot, p, cs, :] - m_new)
                ps = jnp.sum(pe, axis=0, keepdims=True)
                pv = _dot(v_ref[0, 0, j, :, cs], pe.astype(BF16))
                lsum = ps if lsum is None else lsum + ps
                acc = pv if acc is None else acc + pv
            l_sc[p] = alpha * l_sc[p] + lsum
            acc_sc[p] = alpha * acc_sc[p] + acc
            m_sc[p] = m_new

    qk_tile(0, 0)
    n_pairs = (n_kv - 1) // 2

    def pair(jj, carry):
        j = 2 * jj
        qk_tile(j + 1, 1)
        pv_tile(j, 0)
        qk_tile(j + 2, 0)
        pv_tile(j + 1, 1)
        return carry

    if n_pairs > 0:
        lax.fori_loop(0, n_pairs, pair, 0)
    j = 2 * n_pairs
    if j == n_kv - 1:
        pv_tile(j, 0)
    else:
        qk_tile(j + 1, 1)
        pv_tile(j, 0)
        pv_tile(j + 1, 1)

    o = acc_sc[0] / l_sc[0] - lam_ref[0] * (acc_sc[1] / l_sc[1])
    ms = jnp.mean(o * o, axis=0, keepdims=True)
    o_ref[0] = ((o * lax.rsqrt(ms + EPS)).T * g_ref[...]).astype(BF16)


def _attention(lam, qz, k, vt, g, *, tq, tk, q_blk0, n_q, kv_blk0, n_kv, prev_out=None):
    B, H, _, _, S = qz.shape
    aliased = prev_out is not None
    skv = tk * n_kv
    in_specs = [pl.BlockSpec(memory_space=pltpu.SMEM),
                pl.BlockSpec((1, 1, 2, LANES, tq), lambda b, h, i: (b, h, 0, 0, q_blk0 + i)),
                pl.BlockSpec((1, 1, skv, LANES), lambda b, h, i: (b, h, kv_blk0, 0)),
                pl.BlockSpec((1, 1, n_kv, LANES, tk), lambda b, h, i: (b, h, 0, 0, 0)),
                pl.BlockSpec((1, LANES), lambda b, h, i: (0, 0))]
    args = [lam, qz, k, vt, g]
    aliases = {}
    if aliased:
        in_specs.append(pl.BlockSpec(memory_space=pl.ANY))
        args.append(prev_out)
        aliases = {5: 0}
    return pl.pallas_call(
        functools.partial(_attn_kernel, aliased=aliased, tk=tk, n_kv=n_kv),
        out_shape=jax.ShapeDtypeStruct((B, S, ATT_WIDTH), BF16),
        grid=(B, H, n_q),
        in_specs=in_specs,
        out_specs=pl.BlockSpec((1, tq, LANES), lambda b, h, i: (b, q_blk0 + i, h)),
        scratch_shapes=[pltpu.VMEM((2, 2, tk, tq), F32), pltpu.VMEM((2, 2, 1, tq), F32),
                        pltpu.VMEM((2, 1, tq), F32), pltpu.VMEM((2, 1, tq), F32),
                        pltpu.VMEM((2, LANES, tq), F32)],
        input_output_aliases=aliases,
        compiler_params=_cparams(("parallel", "parallel", "arbitrary")),
        name="attn_ctx" if aliased else "attn_lat",
    )(*args)


def _attention_lat(lam, qz, k, vt, g, L):
    B, H, _, S = vt.shape
    tq = _pick_tile(L, (512, 256))
    tk = _pick_tile(S, (1280, 640, 256))
    n_kv = S // tk
    vt_tiles = vt.reshape(B, H, LANES, n_kv, tk).transpose(0, 1, 3, 2, 4)
    return _attention(lam, qz, k, vt_tiles, g, tq=tq, tk=tk, q_blk0=0, n_q=L // tq, kv_blk0=0, n_kv=n_kv)


def _attention_ctx(lam, qz, k, vt, g, L, prev_out):
    B, H, _, S = vt.shape
    Lc = S - L
    vt_ctx = vt[:, :, :, L:].reshape(B, H, 1, LANES, Lc)
    return _attention(lam, qz, k, vt_ctx, g, tq=Lc, tk=Lc, q_blk0=L // Lc, n_q=1, kv_blk0=L // Lc, n_kv=1,
                      prev_out=prev_out)


def _gdnprep_kernel(x_ref, prev_ref, next_ref, cw_ref, ab_ref, alog_ref, dtb_ref,
                    q_ref, k_ref, v_ref, gate_ref, ext_sc, *, nb_lat, nb_all):
    i = pl.program_id(1)
    tm = x_ref.shape[1]
    first = jnp.logical_or(i == 0, i == nb_lat)
    last = jnp.logical_or(i == nb_lat - 1, i == nb_all - 1)
    keep_prev = jnp.where(first, 0.0, 1.0)
    keep_next = jnp.where(last, 0.0, 1.0)
    ext_sc[0:8, :] = prev_ref[0] * keep_prev
    ext_sc[8:8 + tm, :] = x_ref[0]
    ext_sc[8 + tm:16 + tm, :] = next_ref[0] * keep_next
    acc = None
    for j in range(CONV_W):
        term = ext_sc[pl.ds(8 + j - CONV_W // 2, tm), :] * cw_ref[j:j + 1, :]
        acc = term if acc is None else acc + term
    y = acc * _sigmoid(acc)
    for part, ref in ((0, q_ref), (1, k_ref)):
        for hh in range(GDN_HEADS):
            lo = part * GDN_WIDTH + hh * GDN_HEAD_DIM
            t = y[:, lo:lo + GDN_HEAD_DIM]
            ref[0, :, hh * GDN_HEAD_DIM:(hh + 1) * GDN_HEAD_DIM] = (
                t * lax.rsqrt(jnp.sum(t * t, axis=-1, keepdims=True) + EPS))
    v_ref[0] = y[:, 2 * GDN_WIDTH:3 * GDN_WIDTH]
    ab = ab_ref[0]
    xs = ab + dtb_ref[...]
    sp = jnp.maximum(xs, 0.0) + jnp.log(1.0 + jnp.exp(-jnp.abs(xs)))
    g = -jnp.exp(alog_ref[...]) * sp
    lane = lax.broadcasted_iota(jnp.int32, ab.shape, 1)
    gate_ref[0] = jnp.where(lane < 2 * GDN_HEADS, g, _sigmoid(ab))


def _gdnprep(p_gdn, conv_w8, ab, alog_row, dtb_row, n_lat_blocks):
    B, S, W = p_gdn.shape
    tm = ROW_TILE
    nb = S // tm
    r8 = tm // 8
    row = lambda b, i: (b, i, 0)
    return pl.pallas_call(
        functools.partial(_gdnprep_kernel, nb_lat=n_lat_blocks, nb_all=nb),
        out_shape=(jax.ShapeDtypeStruct((B, S, GDN_WIDTH), F32),
                   jax.ShapeDtypeStruct((B, S, GDN_WIDTH), F32),
                   jax.ShapeDtypeStruct((B, S, GDN_WIDTH), F32),
                   jax.ShapeDtypeStruct((B, S, LANES), F32)),
        grid=(B, nb),
        in_specs=[pl.BlockSpec((1, tm, W), row),
                  pl.BlockSpec((1, 8, W), lambda b, i: (b, jnp.maximum(i * r8 - 1, 0), 0)),
                  pl.BlockSpec((1, 8, W), lambda b, i: (b, jnp.minimum((i + 1) * r8, nb * r8 - 1), 0)),
                  pl.BlockSpec((8, W), lambda b, i: (0, 0)),
                  pl.BlockSpec((1, tm, LANES), row),
                  pl.BlockSpec((1, LANES), lambda b, i: (0, 0)),
                  pl.BlockSpec((1, LANES), lambda b, i: (0, 0))],
        out_specs=(pl.BlockSpec((1, tm, GDN_WIDTH), row),
                   pl.BlockSpec((1, tm, GDN_WIDTH), row),
                   pl.BlockSpec((1, tm, GDN_WIDTH), row),
                   pl.BlockSpec((1, tm, LANES), row)),
        scratch_shapes=[pltpu.VMEM((tm + 16, W), F32)],
        compiler_params=_cparams(("parallel", "parallel")),
        name="gdnprep",
    )(p_gdn, p_gdn, p_gdn, conv_w8, ab, alog_row, dtb_row)


def _gdnchunk_kernel(q_ref, k_ref, v_ref, gate_ref, qg_ref, w_ref, kd_ref, u_ref, aq_ref, eg_ref):
    tm = q_ref.shape[1]
    nc = tm // GDN_CHUNK
    gates = gate_ref[0]
    ri = lax.broadcasted_iota(jnp.int32, (tm, tm), 0)
    ci = lax.broadcasted_iota(jnp.int32, (tm, tm), 1)
    same = (ri >> 6) == (ci >> 6)
    eye = ri == ci
    tot = _dot_hi(jnp.where(same, 1.0, 0.0).astype(F32), gates)
    eye_f = jnp.where(eye, 1.0, 0.0).astype(F32)

    heads = []
    for hh in range(GDN_HEADS):
        sl = slice(hh * GDN_HEAD_DIM, (hh + 1) * GDN_HEAD_DIM)
        k = k_ref[0, :, sl]
        k16 = k.astype(BF16)
        q = q_ref[0, :, sl] * (GDN_HEAD_DIM ** -0.5)
        heads.append((sl, k, q, _dot_nt(k16, k16), _dot_nt(q.astype(BF16), k16)))

    chains = []
    for d in range(2):
        incl = jnp.logical_and(same, (ci <= ri) if d == 0 else (ci >= ri))
        strict = jnp.logical_and(incl, jnp.logical_not(eye))
        gc = _dot_hi(jnp.where(incl, 1.0, 0.0).astype(F32), gates)
        gc_t = gc.T
        for hh in range(GDN_HEADS):
            ln = d * GDN_HEADS + hh
            gcol = gc[:, ln:ln + 1]
            bcol = gates[:, 2 * GDN_HEADS + ln:2 * GDN_HEADS + ln + 1]
            tcol = tot[:, ln:ln + 1]
            diff = gcol - gc_t[ln:ln + 1, :]
            decay = jnp.where(incl, jnp.exp(jnp.where(incl, diff, 0.0)), 0.0)
            a = jnp.where(strict, heads[hh][3] * bcol * decay, 0.0)
            aqk = heads[hh][4] * decay
            chains.append(dict(d=d, hh=hh, x=-a, t=eye_f - a, aqk=aqk, gcol=gcol, bcol=bcol, tcol=tcol))

    for _ in range(5):
        for ch in chains:
            x16 = ch["x"].astype(BF16)
            ch["x"] = _dot(x16, x16)
        for ch in chains:
            ch["t"] = ch["t"] + _dot(ch["t"].astype(BF16), ch["x"].astype(BF16))

    for ch in chains:
        d, hh, gcol, bcol, tcol = ch["d"], ch["hh"], ch["gcol"], ch["bcol"], ch["tcol"]
        sl, k, q = heads[hh][0], heads[hh][1], heads[hh][2]
        t16 = ch["t"].astype(BF16)
        egc = jnp.exp(gcol)
        kb = k * bcol
        u_ref[0, d, :, sl] = _dot(t16, (v_ref[0, :, sl] * bcol).astype(BF16))
        w_ref[0, d, :, sl] = _dot(t16, (kb * egc).astype(BF16)).astype(BF16)
        kd_ref[0, d, :, sl] = (k * jnp.exp(tcol - gcol)).astype(BF16)
        qg_ref[0, d, :, sl] = (q * egc).astype(BF16)
        for cc in range(nc):
            rs = slice(cc * GDN_CHUNK, (cc + 1) * GDN_CHUNK)
            aq_ref[0, d, rs, hh * GDN_CHUNK:(hh + 1) * GDN_CHUNK] = ch["aqk"][rs, rs].astype(BF16)
            eg_ref[0, d, cc, hh:hh + 1, :] = jnp.broadcast_to(
                jnp.exp(tcol[cc * GDN_CHUNK:cc * GDN_CHUNK + 1, :]), (1, LANES))


def _gdnchunk(gq, gk, gv, gates):
    B, S, W = gq.shape
    tm = ROW_TILE
    nc = tm // GDN_CHUNK
    row = lambda b, i: (b, i, 0)
    drow = lambda b, i: (b, 0, i, 0)
    big = lambda dt: jax.ShapeDtypeStruct((B, 2, S, W), dt)
    return pl.pallas_call(
        _gdnchunk_kernel,
        out_shape=(big(BF16), big(BF16), big(BF16), big(F32),
                   jax.ShapeDtypeStruct((B, 2, S, GDN_HEADS * GDN_CHUNK), BF16),
                   jax.ShapeDtypeStruct((B, 2, S // GDN_CHUNK, GDN_HEADS, LANES), F32)),
        grid=(B, S // tm),
        in_specs=[pl.BlockSpec((1, tm, W), row), pl.BlockSpec((1, tm, W), row),
                  pl.BlockSpec((1, tm, W), row), pl.BlockSpec((1, tm, LANES), row)],
        out_specs=(pl.BlockSpec((1, 2, tm, W), drow), pl.BlockSpec((1, 2, tm, W), drow),
                   pl.BlockSpec((1, 2, tm, W), drow), pl.BlockSpec((1, 2, tm, W), drow),
                   pl.BlockSpec((1, 2, tm, GDN_HEADS * GDN_CHUNK), drow),
                   pl.BlockSpec((1, 2, nc, GDN_HEADS, LANES), lambda b, i: (b, 0, i, 0, 0))),
        compiler_params=_cparams(("parallel", "parallel")),
        name="gdnchunk",
    )(gq, gk, gv, gates)


def _gdnscan_kernel(qg0, w0, kd0, u0, aq0, eg0, qg1, w1, kd1, u1, aq1, eg1, of_ref, ob_ref, s_sc):
    @pl.when(pl.program_id(1) == 0)
    def _():
        s_sc[...] = jnp.zeros(s_sc.shape, F32)

    dirs = ((qg0, w0, kd0, u0, aq0, eg0, of_ref), (qg1, w1, kd1, u1, aq1, eg1, ob_ref))
    chains = [(d, hh) for d in range(2) for hh in range(GDN_HEADS)]
    st, st16, vn16, qs = {}, {}, {}, {}
    for d, hh in chains:
        st[d, hh] = s_sc[d, hh]
        st16[d, hh] = st[d, hh].astype(BF16)
    for d, hh in chains:
        qg, w, kd, u, aq, eg, o_ref = dirs[d]
        sl = slice(hh * GDN_HEAD_DIM, (hh + 1) * GDN_HEAD_DIM)
        vn16[d, hh] = (u[0, 0, :, sl] - _dot(w[0, 0, :, sl], st16[d, hh])).astype(BF16)
        qs[d, hh] = _dot(qg[0, 0, :, sl], st16[d, hh])
    for d, hh in chains:
        qg, w, kd, u, aq, eg, o_ref = dirs[d]
        sl = slice(hh * GDN_HEAD_DIM, (hh + 1) * GDN_HEAD_DIM)
        o_ref[0, :, sl] = qs[d, hh] + _dot(aq[0, 0, :, hh * GDN_CHUNK:(hh + 1) * GDN_CHUNK], vn16[d, hh])
        s_sc[d, hh] = st[d, hh] * eg[0, 0, 0, hh:hh + 1, :] + _dot_tn(kd[0, 0, :, sl], vn16[d, hh])


def _gdnscan(qg, w, kd, u, aq, eg, n_lat_chunks, n_ctx_chunks):
    B, _, S, W = qg.shape
    C = GDN_CHUNK
    n = S // C

    def fwd_chunk(i):
        return jnp.where(i < n_ctx_chunks, n_lat_chunks + i, i - n_ctx_chunks)

    def bwd_chunk(i):
        return jnp.where(i < n_ctx_chunks, n_lat_chunks + n_ctx_chunks - 1 - i, n - 1 - i)

    def specs(d, chunk_of):
        big = pl.BlockSpec((1, 1, C, W), lambda b, i: (b, d, chunk_of(i), 0))
        return [big, big, big, big,
                pl.BlockSpec((1, 1, C, GDN_HEADS * C), lambda b, i: (b, d, chunk_of(i), 0)),
                pl.BlockSpec((1, 1, 1, GDN_HEADS, LANES), lambda b, i: (b, d, chunk_of(i), 0, 0))]

    return pl.pallas_call(
        _gdnscan_kernel,
        out_shape=(jax.ShapeDtypeStruct((B, S, W), F32), jax.ShapeDtypeStruct((B, S, W), F32)),
        grid=(B, n),
        in_specs=specs(0, fwd_chunk) + specs(1, bwd_chunk),
        out_specs=(pl.BlockSpec((1, C, W), lambda b, i: (b, fwd_chunk(i), 0)),
                   pl.BlockSpec((1, C, W), lambda b, i: (b, bwd_chunk(i), 0))),
        scratch_shapes=[pltpu.VMEM((2, GDN_HEADS, GDN_HEAD_DIM, GDN_HEAD_DIM), F32)],
        compiler_params=_cparams(("parallel", "arbitrary")),
        name="gdnscan",
    )(qg, w, kd, u, aq, eg, qg, w, kd, u, aq, eg)


def _mixout_kernel(oa_ref, of_ref, ob_ref, z_ref, h_ref, mod_ref, gg_ref, wo_ref, gf_ref, rw_ref, rb_ref,
                   hn_ref, v_ref, te_ref, tg_ref):
    og = of_ref[0] + ob_ref[0]
    z = z_ref[0]
    parts = [oa_ref[0]]
    for hh in range(GDN_HEADS):
        sl = slice(hh * GDN_HEAD_DIM, (hh + 1) * GDN_HEAD_DIM)
        t = og[:, sl]
        t = t * lax.rsqrt(jnp.mean(t * t, axis=-1, keepdims=True) + EPS) * gg_ref[...]
        zz = z[:, sl]
        parts.append((t * (zz * _sigmoid(zz))).astype(BF16))
    mix_in = jnp.concatenate(parts, axis=-1)
    mix = _dot(mix_in, wo_ref[...])
    hn = h_ref[0] + mod_ref[0, 0, 2:3, :] * mix
    hn_ref[0] = hn
    y = hn * lax.rsqrt(jnp.mean(hn * hn, axis=-1, keepdims=True) + EPS) * gf_ref[...]
    v = y * (1.0 + mod_ref[0, 0, 4:5, :]) + mod_ref[0, 0, 3:4, :]
    v_ref[0] = v.astype(BF16)
    logits = _dot_hi(v, rw_ref[...]) + rb_ref[...]
    lane = lax.broadcasted_iota(jnp.int32, logits.shape, 1)
    cur = logits
    vals, idxs = [], []
    for _ in range(TOP_K):
        m = jnp.max(cur, axis=-1, keepdims=True)
        idx = jnp.min(jnp.where(cur == m, lane, LANES), axis=-1, keepdims=True)
        vals.append(m)
        idxs.append(idx)
        cur = jnp.where(lane == idx, -jnp.inf, cur)
    es = [jnp.exp(vv - vals[0]) for vv in vals]
    inv = 1.0 / (es[0] + es[1] + es[2] + es[3])
    te = jnp.zeros(logits.shape, jnp.int32)
    tg = jnp.zeros(logits.shape, F32)
    for kk in range(TOP_K):
        te = jnp.where(lane == kk, idxs[kk], te)
        tg = jnp.where(lane == kk, es[kk] * inv, tg)
    te_ref[0] = te
    tg_ref[0] = tg


def _mixout(o_att, o_f, o_b, z, h, modtab, gg, w_out, gf, rw, rb, n_lat_blocks):
    B, S, _ = h.shape
    tm = ROW_TILE
    row = lambda b, i: (b, i, 0)
    const = lambda b, i: (0, 0)
    return pl.pallas_call(
        _mixout_kernel,
        out_shape=(jax.ShapeDtypeStruct((B, S, D_MODEL), F32),
                   jax.ShapeDtypeStruct((B, S, D_MODEL), BF16),
                   jax.ShapeDtypeStruct((B, S, LANES), jnp.int32),
                   jax.ShapeDtypeStruct((B, S, LANES), F32)),
        grid=(B, S // tm),
        in_specs=[pl.BlockSpec((1, tm, ATT_WIDTH), row),
                  pl.BlockSpec((1, tm, GDN_WIDTH), row),
                  pl.BlockSpec((1, tm, GDN_WIDTH), row),
                  pl.BlockSpec((1, tm, GDN_WIDTH), row),
                  pl.BlockSpec((1, tm, D_MODEL), row),
                  pl.BlockSpec((1, 1, 8, D_MODEL), lambda b, i: (b, (i >= n_lat_blocks).astype(jnp.int32), 0, 0)),
                  pl.BlockSpec((1, LANES), const),
                  pl.BlockSpec((D_MODEL, D_MODEL), const),
                  pl.BlockSpec((1, D_MODEL), const),
                  pl.BlockSpec((D_MODEL, LANES), const),
                  pl.BlockSpec((1, LANES), const)],
        out_specs=(pl.BlockSpec((1, tm, D_MODEL), row),
                   pl.BlockSpec((1, tm, D_MODEL), row),
                   pl.BlockSpec((1, tm, LANES), row),
                   pl.BlockSpec((1, tm, LANES), row)),
        compiler_params=_cparams(("parallel", "parallel")),
        name="mixout",
    )(o_att, o_f, o_b, z, h, modtab, gg, w_out, gf, rw, rb)


def _expert_kernel(be_ref, nv_ref, x_ref, wgu_ref, bgu_ref, wd_ref, bd_ref, sg_ref, y_ref, wgu_sc, wd_sc):
    i = pl.program_id(0)
    new_expert = jnp.logical_or(i == 0, be_ref[i] != be_ref[jnp.maximum(i - 1, 0)])

    @pl.when(new_expert)
    def _():
        wgu_sc[...] = wgu_ref[0, 0].astype(BF16)
        wd_sc[...] = wd_ref[0, 0].astype(BF16)

    @pl.when(nv_ref[i] > 0)
    def _():
        gu = _dot(x_ref[...], wgu_sc[...]) + bgu_ref[0, 0]
        g_ = jnp.minimum(gu[:, :D_EXPERT], SWIGLU_LIMIT)
        up = jnp.clip(gu[:, D_EXPERT:], -SWIGLU_LIMIT, SWIGLU_LIMIT)
        glu = g_ * _sigmoid(SWIGLU_ALPHA * g_)
        act = ((up + 1.0) * glu).astype(BF16)
        y_ref[...] = ((_dot(act, wd_sc[...]) + bd_ref[0, 0]) * sg_ref[...]).astype(y_ref.dtype)

    @pl.when(nv_ref[i] == 0)
    def _():
        y_ref[...] = jnp.zeros(y_ref.shape, y_ref.dtype)


def _experts(block_e, n_valid, x_sorted, wgu, bgu, wd, bd, slot_gate, layer):
    n_slots = x_sorted.shape[0]
    nb = n_slots // MOE_BLOCK
    grid_spec = pltpu.PrefetchScalarGridSpec(
        num_scalar_prefetch=2,
        grid=(nb,),
        in_specs=[pl.BlockSpec((MOE_BLOCK, D_MODEL), lambda i, be, nv: (i, 0)),
                  pl.BlockSpec((1, 1, D_MODEL, 2 * D_EXPERT), lambda i, be, nv: (layer, be[i], 0, 0)),
                  pl.BlockSpec((1, 1, 1, 2 * D_EXPERT), lambda i, be, nv: (layer, be[i], 0, 0)),
                  pl.BlockSpec((1, 1, D_EXPERT, D_MODEL), lambda i, be, nv: (layer, be[i], 0, 0)),
                  pl.BlockSpec((1, 1, 1, D_MODEL), lambda i, be, nv: (layer, be[i], 0, 0)),
                  pl.BlockSpec((MOE_BLOCK, 1), lambda i, be, nv: (i, 0))],
        out_specs=pl.BlockSpec((MOE_BLOCK, D_MODEL), lambda i, be, nv: (i, 0)),
        scratch_shapes=[pltpu.VMEM((D_MODEL, 2 * D_EXPERT), BF16), pltpu.VMEM((D_EXPERT, D_MODEL), BF16)],
    )
    return pl.pallas_call(
        _expert_kernel,
        out_shape=jax.ShapeDtypeStruct((n_slots, D_MODEL), BF16),
        grid_spec=grid_spec,
        compiler_params=pltpu.CompilerParams(dimension_semantics=("arbitrary",),
                                             vmem_limit_bytes=EXPERT_VMEM_LIMIT),
        name="experts",
    )(block_e, n_valid, x_sorted, wgu, bgu, wd, bd, slot_gate)


def _moe_plan(top_e, top_g):
    T = top_e.shape[0]
    n_assign = T * TOP_K
    n_blocks = -(-n_assign // MOE_BLOCK) + N_EXPERTS
    n_slots = n_blocks * MOE_BLOCK
    i32 = jnp.int32
    flat_e = top_e.reshape(-1)
    gate_flat = top_g.reshape(-1)
    order = jnp.argsort(flat_e).astype(i32)
    rank = jnp.argsort(order).astype(i32)
    e_ids = jnp.arange(N_EXPERTS, dtype=i32)
    is_e = flat_e[:, None] == e_ids[None, :]
    counts = jnp.sum(is_e, axis=0, dtype=i32)
    start = jnp.cumsum(counts) - counts
    padded = (counts + MOE_BLOCK - 1) // MOE_BLOCK * MOE_BLOCK
    pad_end = jnp.cumsum(padded)
    pad_start = pad_end - padded
    blk0 = jnp.arange(n_blocks, dtype=i32) * MOE_BLOCK
    block_e = jnp.minimum(jnp.sum(pad_end[None, :] <= blk0[:, None], axis=1, dtype=i32), N_EXPERTS - 1)
    off = (blk0 - pad_start[block_e])[:, None] + jnp.arange(MOE_BLOCK, dtype=i32)[None, :]
    valid = off < counts[block_e][:, None]
    a_slot = order[jnp.clip(off + start[block_e][:, None], 0, n_assign - 1).reshape(-1)]
    valid_flat = valid.reshape(-1)
    slot_tok = jnp.where(valid_flat, a_slot // TOP_K, 0)
    slot_gate = jnp.where(valid_flat, gate_flat[a_slot], 0.0)
    n_valid = jnp.sum(valid, axis=1, dtype=i32)
    shift = jnp.sum(jnp.where(is_e, (pad_start - start)[None, :], 0), axis=1, dtype=i32)
    slot_of = rank + shift
    return slot_tok, slot_gate.reshape(n_slots, 1), block_e, n_valid, slot_of


def _combine_kernel(y_ref, h_ref, mod_ref, o_ref):
    y = (y_ref[0, 0].astype(F32) + y_ref[1, 0].astype(F32)) + (y_ref[2, 0].astype(F32) + y_ref[3, 0].astype(F32))
    o_ref[0] = h_ref[0] + mod_ref[0, 0, 5:6, :] * y


def _combine(y4, h, modtab, n_lat_blocks):
    B, S, _ = h.shape
    tm = ROW_TILE
    return pl.pallas_call(
        _combine_kernel,
        out_shape=jax.ShapeDtypeStruct((B, S, D_MODEL), F32),
        grid=(B, S // tm),
        in_specs=[pl.BlockSpec((TOP_K, 1, tm, D_MODEL), lambda b, i: (0, b, i, 0)),
                  pl.BlockSpec((1, tm, D_MODEL), lambda b, i: (b, i, 0)),
                  pl.BlockSpec((1, 1, 8, D_MODEL), lambda b, i: (b, (i >= n_lat_blocks).astype(jnp.int32), 0, 0))],
        out_specs=pl.BlockSpec((1, tm, D_MODEL), lambda b, i: (b, i, 0)),
        compiler_params=_cparams(("parallel", "parallel")),
        name="combine",
    )(y4, h, modtab)


def _moe(v_ffn, top_e, top_g, h_new, modtab, wgu, bgu, wd, bd, layer, n_lat_blocks):
    B, S, _ = v_ffn.shape
    T = B * S
    slot_tok, slot_gate, block_e, n_valid, slot_of = _moe_plan(
        top_e.reshape(T, LANES)[:, :TOP_K], top_g.reshape(T, LANES)[:, :TOP_K])
    x_sorted = v_ffn.reshape(T, D_MODEL)[slot_tok]
    y_sorted = _experts(block_e, n_valid, x_sorted, wgu, bgu, wd, bd, slot_gate, layer)
    y4 = y_sorted[slot_of.reshape(T, TOP_K).T.reshape(-1)].reshape(TOP_K, B, S, D_MODEL)
    return _combine(y4, h_new, modtab, n_lat_blocks)


def _pick_tile(n, cands):
    for t in cands:
        if n % t == 0:
            return t
    raise ValueError(f"no tile for {n}")


def _rope_tables(L, Lc):
    rows = L // GRID_W
    row = jnp.repeat(jnp.arange(rows, dtype=F32), GRID_W)
    col = (jnp.arange(L, dtype=jnp.int32) % GRID_W).astype(F32)
    inv_freq = ROPE_BASE ** (-jnp.arange(ROPE_PAIRS, dtype=F32) / ROPE_PAIRS)
    ar = row[:, None] * inv_freq
    ac = col[:, None] * inv_freq
    cos64 = jnp.concatenate([jnp.cos(ar), jnp.cos(ar), jnp.cos(ac), jnp.cos(ac)], axis=-1)
    sin64 = jnp.concatenate([-jnp.sin(ar), jnp.sin(ar), -jnp.sin(ac), jnp.sin(ac)], axis=-1)
    cos_t = jnp.concatenate([jnp.tile(cos64, (1, 2)), jnp.ones((Lc, LANES), F32)], axis=0)
    sin_t = jnp.concatenate([jnp.tile(sin64, (1, 2)), jnp.zeros((Lc, LANES), F32)], axis=0)
    return cos_t, sin_t


def _pad_lanes(v):
    v = v.reshape(1, -1).astype(F32)
    return jnp.pad(v, ((0, 0), (0, LANES - v.shape[1])))


def kernel(x, c, ctx, c_ctx, w_mod, b_mod, norm_mix_g, w_in, q_norm_g, k_norm_g, lam_q1, lam_k1, lam_q2, lam_k2, subln_g, conv_w, a_log, dt_bias, gdn_norm_g, w_out, norm_ffn_g, router_w, router_b, w_gate_up, b_gate_up, w_down, b_down):
    B, L, D = x.shape
    Lc = ctx.shape[1]
    S = L + Lc
    depth = w_mod.shape[0]
    tm = ROW_TILE
    n_lat_blocks = L // tm
    cos_t, sin_t = _rope_tables(L, Lc)

    c_rows = jnp.zeros((8, D), F32).at[:B].set(c).at[B].set(c_ctx)
    h = jnp.concatenate([x, ctx], axis=1)

    for layer in range(depth):
        mod = _adaln(c_rows, w_mod, b_mod, layer)
        mod6 = mod.reshape(8, 6, D)
        lat_mod = mod6[:B]
        ctx_mod = jnp.broadcast_to(mod6[B][None], (B, 6, D))
        modtab = jnp.pad(jnp.stack([lat_mod, ctx_mod], axis=1), ((0, 0), (0, 0), (0, 2), (0, 0)))

        lam_init = 0.8 - 0.6 * math.exp(-0.3 * layer)
        lam_full = (jnp.exp(jnp.sum(lam_q1[layer] * lam_k1[layer]))
                    - jnp.exp(jnp.sum(lam_q2[layer] * lam_k2[layer])) + lam_init).reshape(1).astype(F32)

        w_l = w_in[layer]
        w_main = w_l[:, :IN_MAIN].astype(BF16)
        w_ab = jnp.pad(w_l[:, IN_MAIN:], ((0, 0), (0, LANES - (w_l.shape[1] - IN_MAIN)))).astype(BF16)
        p_qk, v_att, p_gdn, z, ab = _inproj(h, modtab, norm_mix_g[layer].reshape(1, D), w_main, w_ab,
                                            n_lat_blocks)

        gq = jnp.tile(q_norm_g[layer].reshape(1, ATT_HEAD_DIM), (1, 2))
        gk = jnp.tile(k_norm_g[layer].reshape(1, ATT_HEAD_DIM), (1, 2))
        qz, k_att = _qkprep(p_qk, cos_t, sin_t, gq, gk)
        g_sub = (subln_g[layer] * (1.0 - lam_init)).reshape(1, LANES).astype(F32)
        o_att = _attention_lat(lam_full, qz, k_att, v_att, g_sub, L)
        o_att = _attention_ctx(lam_full, qz, k_att, v_att, g_sub, L, o_att)

        conv_w8 = jnp.pad(conv_w[layer], ((0, 8 - CONV_W), (0, 0)))
        gq_g, gk_g, gv_g, gates = _gdnprep(p_gdn, conv_w8, ab, _pad_lanes(a_log[layer]),
                                           _pad_lanes(dt_bias[layer]), n_lat_blocks)
        qg, w_g, kd, u_g, aq, eg = _gdnchunk(gq_g, gk_g, gv_g, gates)
        o_f, o_b = _gdnscan(qg, w_g, kd, u_g, aq, eg, L // GDN_CHUNK, Lc // GDN_CHUNK)

        rw = jnp.pad(router_w[layer], ((0, 0), (0, LANES - N_EXPERTS)))
        rb = jnp.pad(router_b[layer].reshape(1, N_EXPERTS).astype(F32), ((0, 0), (0, LANES - N_EXPERTS)),
                     constant_values=-1e30)
        h_new, v_ffn, top_e, top_g = _mixout(
            o_att, o_f, o_b, z, h, modtab, gdn_norm_g[layer].reshape(1, LANES), w_out[layer].astype(BF16),
            norm_ffn_g[layer].reshape(1, D), rw, rb, n_lat_blocks)

        h = _moe(v_ffn, top_e, top_g, h_new, modtab,
                 w_gate_up, b_gate_up.reshape(depth, N_EXPERTS, 1, 2 * D_EXPERT),
                 w_down, b_down.reshape(depth, N_EXPERTS, 1, D), layer, n_lat_blocks)
    return h[:, :L]
```

```python
import functools
import math

import jax
import jax.numpy as jnp
from jax import lax
from jax.experimental import pallas as pl
from jax.experimental.pallas import tpu as pltpu

F32 = jnp.float32
BF16 = jnp.bfloat16
HIGHEST = lax.Precision.HIGHEST

D_MODEL = 1024
GRID_W = 64
EPS = 1e-6
ATT_WIDTH = 512
ATT_HEAD_DIM = 64
ATT_HEADS = 4
ROPE_BASE = 10000.0
ROPE_PAIRS = ATT_HEAD_DIM // 4
GDN_WIDTH = 512
GDN_HEAD_DIM = 128
GDN_HEADS = 4
GDN_CHUNK = 64
CONV_W = 5
IN_MAIN = 3 * ATT_WIDTH + 4 * GDN_WIDTH
N_EXPERTS = 32
TOP_K = 4
D_EXPERT = 1024
SWIGLU_ALPHA = 1.702
SWIGLU_LIMIT = 7.0
MOE_BLOCK = 512

LANES = 128
ROW_TILE = 256
KV_CHUNK = 256
Q_SCALE = ATT_HEAD_DIM ** -0.5 * math.log2(math.e)
VMEM_LIMIT = 48 * 1024 * 1024
EXPERT_VMEM_LIMIT = 56 * 1024 * 1024


def _cparams(sem):
    return pltpu.CompilerParams(dimension_semantics=sem, vmem_limit_bytes=VMEM_LIMIT)


def _dot(a, b):
    return jnp.dot(a, b, preferred_element_type=F32)


def _dot_nt(a, b):
    return lax.dot_general(a, b, (((1,), (1,)), ((), ())), preferred_element_type=F32)


def _dot_tn(a, b):
    return lax.dot_general(a, b, (((0,), (0,)), ((), ())), preferred_element_type=F32)


def _dot_hi(a, b):
    return jnp.dot(a, b, preferred_element_type=F32, precision=HIGHEST)


def _sigmoid(x):
    return 1.0 / (1.0 + jnp.exp(-x))


def _adaln_kernel(c_ref, w_ref, b_ref, o_ref):
    c = c_ref[...]
    s = c * _sigmoid(c)
    o_ref[...] = _dot_hi(s, w_ref[0]) + b_ref[0]


def _adaln(c_rows, w, b, layer):
    depth, _, n = w.shape
    tn = 1024
    return pl.pallas_call(
        _adaln_kernel,
        out_shape=jax.ShapeDtypeStruct((8, n), F32),
        grid=(n // tn,),
        in_specs=[pl.BlockSpec((8, D_MODEL), lambda j: (0, 0)),
                  pl.BlockSpec((1, D_MODEL, tn), lambda j: (layer, 0, j)),
                  pl.BlockSpec((1, 1, tn), lambda j: (layer, 0, j))],
        out_specs=pl.BlockSpec((8, tn), lambda j: (0, j)),
        compiler_params=_cparams(("arbitrary",)),
        name="adaln",
    )(c_rows, w, b.reshape(depth, 1, n))


def _inproj_kernel(h_ref, mod_ref, g_ref, w_ref, wab_ref, qk_ref, v_ref, gdn_ref, z_ref, ab_ref):
    x = h_ref[0]
    ms = jnp.mean(x * x, axis=-1, keepdims=True)
    y = x * lax.rsqrt(ms + EPS) * g_ref[...]
    shift = mod_ref[0, 0, 0:1, :]
    scale = mod_ref[0, 0, 1:2, :]
    u = (y * (1.0 + scale) + shift).astype(BF16)
    qk_ref[0] = _dot(u, w_ref[:, 0:2 * ATT_WIDTH])
    vv = _dot(u, w_ref[:, 2 * ATT_WIDTH:3 * ATT_WIDTH])
    for hh in range(ATT_HEADS):
        v_ref[0, hh] = vv[:, hh * LANES:(hh + 1) * LANES].T.astype(BF16)
    off = 3 * ATT_WIDTH
    gdn_ref[0] = _dot(u, w_ref[:, off:off + 3 * GDN_WIDTH])
    z_ref[0] = _dot(u, w_ref[:, off + 3 * GDN_WIDTH:off + 4 * GDN_WIDTH])
    ab_ref[0] = _dot(u, wab_ref[...])


def _inproj(h, modtab, g, w_main, w_ab, n_lat_blocks):
    B, S, _ = h.shape
    tm = ROW_TILE
    row = lambda b, i: (b, i, 0)
    return pl.pallas_call(
        _inproj_kernel,
        out_shape=(jax.ShapeDtypeStruct((B, S, 2 * ATT_WIDTH), F32),
                   jax.ShapeDtypeStruct((B, ATT_HEADS, LANES, S), BF16),
                   jax.ShapeDtypeStruct((B, S, 3 * GDN_WIDTH), F32),
                   jax.ShapeDtypeStruct((B, S, GDN_WIDTH), F32),
                   jax.ShapeDtypeStruct((B, S, LANES), F32)),
        grid=(B, S // tm),
        in_specs=[pl.BlockSpec((1, tm, D_MODEL), row),
                  pl.BlockSpec((1, 1, 8, D_MODEL), lambda b, i: (b, (i >= n_lat_blocks).astype(jnp.int32), 0, 0)),
                  pl.BlockSpec((1, D_MODEL), lambda b, i: (0, 0)),
                  pl.BlockSpec((D_MODEL, IN_MAIN), lambda b, i: (0, 0)),
                  pl.BlockSpec((D_MODEL, LANES), lambda b, i: (0, 0))],
        out_specs=(pl.BlockSpec((1, tm, 2 * ATT_WIDTH), row),
                   pl.BlockSpec((1, ATT_HEADS, LANES, tm), lambda b, i: (b, 0, 0, i)),
                   pl.BlockSpec((1, tm, 3 * GDN_WIDTH), row),
                   pl.BlockSpec((1, tm, GDN_WIDTH), row),
                   pl.BlockSpec((1, tm, LANES), row)),
        compiler_params=_cparams(("parallel", "parallel")),
        name="inproj",
    )(h, modtab, g, w_main, w_ab)


def _qkprep_kernel(p_ref, cos_ref, sin_ref, gq_ref, gk_ref, qz_ref, k_ref):
    tm = p_ref.shape[1]
    lane = lax.broadcasted_iota(jnp.int32, (tm, LANES), 1)
    hi16 = (lane & 16) != 0
    first = lax.broadcasted_iota(jnp.int32, (LANES, tm), 0) < ATT_HEAD_DIM
    r = lax.broadcasted_iota(jnp.int32, (LANES, LANES), 0) >> 6
    c = lax.broadcasted_iota(jnp.int32, (LANES, LANES), 1) >> 6
    gmat = jnp.where(r == c, 1.0 / ATT_HEAD_DIM, 0.0).astype(F32)
    cosv = cos_ref[...]
    sinv = sin_ref[...]
    for j in range(2 * ATT_HEADS):
        x = p_ref[0, :, j * LANES:(j + 1) * LANES]
        ms = _dot_hi(x * x, gmat)
        g = gq_ref[...] if j < ATT_HEADS else gk_ref[...]
        y = x * lax.rsqrt(ms + EPS) * g
        sw = jnp.where(hi16, pltpu.roll(y, 16, 1), pltpu.roll(y, LANES - 16, 1))
        y = y * cosv + sw * sinv
        if j < ATT_HEADS:
            yt = (y * Q_SCALE).T
            qz_ref[0, j, 0] = jnp.where(first, yt, 0.0).astype(BF16)
            qz_ref[0, j, 1] = jnp.where(first, 0.0, yt).astype(BF16)
        else:
            k_ref[0, j - ATT_HEADS] = y.astype(BF16)


def _qkprep(p_qk, cos_t, sin_t, gq, gk):
    B, S, _ = p_qk.shape
    tm = ROW_TILE
    return pl.pallas_call(
        _qkprep_kernel,
        out_shape=(jax.ShapeDtypeStruct((B, ATT_HEADS, 2, LANES, S), BF16),
                   jax.ShapeDtypeStruct((B, ATT_HEADS, S, LANES), BF16)),
        grid=(B, S // tm),
        in_specs=[pl.BlockSpec((1, tm, 2 * ATT_WIDTH), lambda b, i: (b, i, 0)),
                  pl.BlockSpec((tm, LANES), lambda b, i: (i, 0)),
                  pl.BlockSpec((tm, LANES), lambda b, i: (i, 0)),
                  pl.BlockSpec((1, LANES), lambda b, i: (0, 0)),
                  pl.BlockSpec((1, LANES), lambda b, i: (0, 0))],
        out_specs=(pl.BlockSpec((1, ATT_HEADS, 2, LANES, tm), lambda b, i: (b, 0, 0, 0, i)),
                   pl.BlockSpec((1, ATT_HEADS, tm, LANES), lambda b, i: (b, 0, i, 0))),
        compiler_params=_cparams(("parallel", "parallel")),
        name="qkprep",
    )(p_qk, cos_t, sin_t, gq, gk)


def _attn_kernel(lam_ref, qz_ref, k_ref, v_ref, g_ref, *rest, aliased, tk, n_kv):
    if aliased:
        rest = rest[1:]
    o_ref, s_sc, mt_sc, m_sc, l_sc, acc_sc = rest
    m_sc[...] = jnp.full(m_sc.shape, -jnp.inf, F32)
    l_sc[...] = jnp.zeros(l_sc.shape, F32)
    acc_sc[...] = jnp.zeros(acc_sc.shape, F32)

    def rows(j):
        if isinstance(j, int):
            return pl.ds(j * tk, tk)
        return pl.ds(pl.multiple_of(j * tk, tk), tk)

    def qk_tile(j, slot, ps=(0, 1)):
        kt = k_ref[0, 0, rows(j), :]
        for p in ps:
            s = _dot(kt, qz_ref[0, 0, p])
            s_sc[slot, p] = s
            mt_sc[slot, p] = jnp.max(s, axis=0, keepdims=True)

    def pv_tile(j, slot, ps=(0, 1)):
        for p in ps:
            m_prev = m_sc[p]
            m_new = jnp.maximum(m_prev, mt_sc[slot, p])
            alpha = jnp.exp2(m_prev - m_new)
            lsum = None
            acc = None
            for c in range(tk // KV_CHUNK):
                cs = slice(c * KV_CHUNK, (c + 1) * KV_CHUNK)
                pe = jnp.exp2(s_sc[slot, p, cs, :] - m_new)
                ps = jnp.sum(pe, axis=0, keepdims=True)
                pv = _dot(v_ref[0, 0, j, :, cs], pe.astype(BF16))
                lsum = ps if lsum is None else lsum + ps
                acc = pv if acc is None else acc + pv
            l_sc[p] = alpha * l_sc[p] + lsum
            acc_sc[p] = alpha * acc_sc[p] + acc
            m_sc[p] = m_new

    def fused_tile(jq, slot_q, jp, slot_p):
        m_new = [jnp.maximum(m_sc[p], mt_sc[slot_p, p]) for p in range(2)]
        alpha = [jnp.exp2(m_sc[p] - m_new[p]) for p in range(2)]
        mx, lsum, acc = [None, None], [None, None], [None, None]
        for c in range(tk // KV_CHUNK):
            cs = slice(c * KV_CHUNK, (c + 1) * KV_CHUNK)
            if isinstance(jq, int):
                kr = pl.ds(jq * tk + c * KV_CHUNK, KV_CHUNK)
            else:
                kr = pl.ds(pl.multiple_of(jq * tk + c * KV_CHUNK, KV_CHUNK), KV_CHUNK)
            kc = k_ref[0, 0, kr, :]
            vc = v_ref[0, 0, jp, :, cs]
            for p in range(2):
                s = _dot(kc, qz_ref[0, 0, p])
                s_sc[slot_q, p, cs, :] = s
                cm = jnp.max(s, axis=0, keepdims=True)
                mx[p] = cm if mx[p] is None else jnp.maximum(mx[p], cm)
                pe = jnp.exp2(s_sc[slot_p, p, cs, :] - m_new[p])
                ps = jnp.sum(pe, axis=0, keepdims=True)
                pv = _dot(vc, pe.astype(BF16))
                lsum[p] = ps if lsum[p] is None else lsum[p] + ps
                acc[p] = pv if acc[p] is None else acc[p] + pv
        for p in range(2):
            mt_sc[slot_q, p] = mx[p]
            l_sc[p] = alpha[p] * l_sc[p] + lsum[p]
            acc_sc[p] = alpha[p] * acc_sc[p] + acc[p]
            m_sc[p] = m_new[p]

    qk_tile(0, 0)
    n_pairs = (n_kv - 1) // 2

    def pair(jj, carry):
        j = 2 * jj
        fused_tile(j + 1, 1, j, 0)
        fused_tile(j + 2, 0, j + 1, 1)
        return carry

    if n_pairs > 0:
        lax.fori_loop(0, n_pairs, pair, 0)
    j = 2 * n_pairs
    if j == n_kv - 1:
        pv_tile(j, 0)
    else:
        qk_tile(j + 1, 1)
        pv_tile(j, 0)
        pv_tile(j + 1, 1)

    o = acc_sc[0] / l_sc[0] - lam_ref[0] * (acc_sc[1] / l_sc[1])
    ms = jnp.mean(o * o, axis=0, keepdims=True)
    o_ref[0] = ((o * lax.rsqrt(ms + EPS)).T * g_ref[...]).astype(BF16)


def _attention(lam, qz, k, vt, g, *, tq, tk, q_blk0, n_q, kv_blk0, n_kv, prev_out=None):
    B, H, _, _, S = qz.shape
    aliased = prev_out is not None
    skv = tk * n_kv
    in_specs = [pl.BlockSpec(memory_space=pltpu.SMEM),
                pl.BlockSpec((1, 1, 2, LANES, tq), lambda b, h, i: (b, h, 0, 0, q_blk0 + i)),
                pl.BlockSpec((1, 1, skv, LANES), lambda b, h, i: (b, h, kv_blk0, 0)),
                pl.BlockSpec((1, 1, n_kv, LANES, tk), lambda b, h, i: (b, h, 0, 0, 0)),
                pl.BlockSpec((1, LANES), lambda b, h, i: (0, 0))]
    args = [lam, qz, k, vt, g]
    aliases = {}
    if aliased:
        in_specs.append(pl.BlockSpec(memory_space=pl.ANY))
        args.append(prev_out)
        aliases = {5: 0}
    return pl.pallas_call(
        functools.partial(_attn_kernel, aliased=aliased, tk=tk, n_kv=n_kv),
        out_shape=jax.ShapeDtypeStruct((B, S, ATT_WIDTH), BF16),
        grid=(B, H, n_q),
        in_specs=in_specs,
        out_specs=pl.BlockSpec((1, tq, LANES), lambda b, h, i: (b, q_blk0 + i, h)),
        scratch_shapes=[pltpu.VMEM((2, 2, tk, tq), F32), pltpu.VMEM((2, 2, 1, tq), F32),
                        pltpu.VMEM((2, 1, tq), F32), pltpu.VMEM((2, 1, tq), F32),
                        pltpu.VMEM((2, LANES, tq), F32)],
        input_output_aliases=aliases,
        compiler_params=_cparams(("parallel", "parallel", "arbitrary")),
        name="attn_ctx" if aliased else "attn_lat",
    )(*args)


def _attention_lat(lam, qz, k, vt, g, L):
    B, H, _, S = vt.shape
    tq = _pick_tile(L, (512, 256))
    tk = _pick_tile(S, (1280, 640, 256))
    n_kv = S // tk
    vt_tiles = vt.reshape(B, H, LANES, n_kv, tk).transpose(0, 1, 3, 2, 4)
    return _attention(lam, qz, k, vt_tiles, g, tq=tq, tk=tk, q_blk0=0, n_q=L // tq, kv_blk0=0, n_kv=n_kv)


def _attention_ctx(lam, qz, k, vt, g, L, prev_out):
    B, H, _, S = vt.shape
    Lc = S - L
    vt_ctx = vt[:, :, :, L:].reshape(B, H, 1, LANES, Lc)
    return _attention(lam, qz, k, vt_ctx, g, tq=Lc, tk=Lc, q_blk0=L // Lc, n_q=1, kv_blk0=L // Lc, n_kv=1,
                      prev_out=prev_out)


def _gdnprep_kernel(x_ref, prev_ref, next_ref, cw_ref, ab_ref, alog_ref, dtb_ref,
                    q_ref, k_ref, v_ref, gate_ref, ext_sc, *, nb_lat, nb_all):
    i = pl.program_id(1)
    tm = x_ref.shape[1]
    first = jnp.logical_or(i == 0, i == nb_lat)
    last = jnp.logical_or(i == nb_lat - 1, i == nb_all - 1)
    keep_prev = jnp.where(first, 0.0, 1.0)
    keep_next = jnp.where(last, 0.0, 1.0)
    ext_sc[0:8, :] = prev_ref[0] * keep_prev
    ext_sc[8:8 + tm, :] = x_ref[0]
    ext_sc[8 + tm:16 + tm, :] = next_ref[0] * keep_next
    acc = None
    for j in range(CONV_W):
        term = ext_sc[pl.ds(8 + j - CONV_W // 2, tm), :] * cw_ref[j:j + 1, :]
        acc = term if acc is None else acc + term
    y = acc * _sigmoid(acc)
    for part, ref in ((0, q_ref), (1, k_ref)):
        for hh in range(GDN_HEADS):
            lo = part * GDN_WIDTH + hh * GDN_HEAD_DIM
            t = y[:, lo:lo + GDN_HEAD_DIM]
            ref[0, :, hh * GDN_HEAD_DIM:(hh + 1) * GDN_HEAD_DIM] = (
                t * lax.rsqrt(jnp.sum(t * t, axis=-1, keepdims=True) + EPS))
    v_ref[0] = y[:, 2 * GDN_WIDTH:3 * GDN_WIDTH]
    ab = ab_ref[0]
    xs = ab + dtb_ref[...]
    sp = jnp.maximum(xs, 0.0) + jnp.log(1.0 + jnp.exp(-jnp.abs(xs)))
    g = -jnp.exp(alog_ref[...]) * sp
    lane = lax.broadcasted_iota(jnp.int32, ab.shape, 1)
    gate_ref[0] = jnp.where(lane < 2 * GDN_HEADS, g, _sigmoid(ab))


def _gdnprep(p_gdn, conv_w8, ab, alog_row, dtb_row, n_lat_blocks):
    B, S, W = p_gdn.shape
    tm = ROW_TILE
    nb = S // tm
    r8 = tm // 8
    row = lambda b, i: (b, i, 0)
    return pl.pallas_call(
        functools.partial(_gdnprep_kernel, nb_lat=n_lat_blocks, nb_all=nb),
        out_shape=(jax.ShapeDtypeStruct((B, S, GDN_WIDTH), F32),
                   jax.ShapeDtypeStruct((B, S, GDN_WIDTH), F32),
                   jax.ShapeDtypeStruct((B, S, GDN_WIDTH), F32),
                   jax.ShapeDtypeStruct((B, S, LANES), F32)),
        grid=(B, nb),
        in_specs=[pl.BlockSpec((1, tm, W), row),
                  pl.BlockSpec((1, 8, W), lambda b, i: (b, jnp.maximum(i * r8 - 1, 0), 0)),
                  pl.BlockSpec((1, 8, W), lambda b, i: (b, jnp.minimum((i + 1) * r8, nb * r8 - 1), 0)),
                  pl.BlockSpec((8, W), lambda b, i: (0, 0)),
                  pl.BlockSpec((1, tm, LANES), row),
                  pl.BlockSpec((1, LANES), lambda b, i: (0, 0)),
                  pl.BlockSpec((1, LANES), lambda b, i: (0, 0))],
        out_specs=(pl.BlockSpec((1, tm, GDN_WIDTH), row),
                   pl.BlockSpec((1, tm, GDN_WIDTH), row),
                   pl.BlockSpec((1, tm, GDN_WIDTH), row),
                   pl.BlockSpec((1, tm, LANES), row)),
        scratch_shapes=[pltpu.VMEM((tm + 16, W), F32)],
        compiler_params=_cparams(("parallel", "parallel")),
        name="gdnprep",
    )(p_gdn, p_gdn, p_gdn, conv_w8, ab, alog_row, dtb_row)


def _gdnchunk_kernel(q_ref, k_ref, v_ref, gate_ref, qg_ref, w_ref, kd_ref, u_ref, aq_ref, eg_ref):
    tm = q_ref.shape[1]
    nc = tm // GDN_CHUNK
    gates = gate_ref[0]
    ri = lax.broadcasted_iota(jnp.int32, (tm, tm), 0)
    ci = lax.broadcasted_iota(jnp.int32, (tm, tm), 1)
    same = (ri >> 6) == (ci >> 6)
    eye = ri == ci
    tot = _dot_hi(jnp.where(same, 1.0, 0.0).astype(F32), gates)
    eye_f = jnp.where(eye, 1.0, 0.0).astype(F32)

    heads = []
    for hh in range(GDN_HEADS):
        sl = slice(hh * GDN_HEAD_DIM, (hh + 1) * GDN_HEAD_DIM)
        k = k_ref[0, :, sl]
        k16 = k.astype(BF16)
        q = q_ref[0, :, sl] * (GDN_HEAD_DIM ** -0.5)
        heads.append((sl, k, q, _dot_nt(k16, k16), _dot_nt(q.astype(BF16), k16)))

    chains = []
    for d in range(2):
        incl = jnp.logical_and(same, (ci <= ri) if d == 0 else (ci >= ri))
        strict = jnp.logical_and(incl, jnp.logical_not(eye))
        gc = _dot_hi(jnp.where(incl, 1.0, 0.0).astype(F32), gates)
        gc_t = gc.T
        for hh in range(GDN_HEADS):
            ln = d * GDN_HEADS + hh
            gcol = gc[:, ln:ln + 1]
            bcol = gates[:, 2 * GDN_HEADS + ln:2 * GDN_HEADS + ln + 1]
            tcol = tot[:, ln:ln + 1]
            diff = gcol - gc_t[ln:ln + 1, :]
            decay = jnp.where(incl, jnp.exp(jnp.where(incl, diff, 0.0)), 0.0)
            a = jnp.where(strict, heads[hh][3] * bcol * decay, 0.0)
            aqk = heads[hh][4] * decay
            chains.append(dict(d=d, hh=hh, x=-a, t=eye_f - a, aqk=aqk, gcol=gcol, bcol=bcol, tcol=tcol))

    for _ in range(5):
        for ch in chains:
            x16 = ch["x"].astype(BF16)
            ch["x"] = _dot(x16, x16)
        for ch in chains:
            ch["t"] = ch["t"] + _dot(ch["t"].astype(BF16), ch["x"].astype(BF16))

    for ch in chains:
        d, hh, gcol, bcol, tcol = ch["d"], ch["hh"], ch["gcol"], ch["bcol"], ch["tcol"]
        sl, k, q = heads[hh][0], heads[hh][1], heads[hh][2]
        t16 = ch["t"].astype(BF16)
        egc = jnp.exp(gcol)
        kb = k * bcol
        u_ref[0, d, :, sl] = _dot(t16, (v_ref[0, :, sl] * bcol).astype(BF16))
        w_ref[0, d, :, sl] = _dot(t16, (kb * egc).astype(BF16)).astype(BF16)
        kd_ref[0, d, :, sl] = (k * jnp.exp(tcol - gcol)).astype(BF16)
        qg_ref[0, d, :, sl] = (q * egc).astype(BF16)
        for cc in range(nc):
            rs = slice(cc * GDN_CHUNK, (cc + 1) * GDN_CHUNK)
            aq_ref[0, d, rs, hh * GDN_CHUNK:(hh + 1) * GDN_CHUNK] = ch["aqk"][rs, rs].astype(BF16)
            eg_ref[0, d, cc, hh:hh + 1, :] = jnp.broadcast_to(
                jnp.exp(tcol[cc * GDN_CHUNK:cc * GDN_CHUNK + 1, :]), (1, LANES))


def _gdnchunk(gq, gk, gv, gates):
    B, S, W = gq.shape
    tm = ROW_TILE
    nc = tm // GDN_CHUNK
    row = lambda b, i: (b, i, 0)
    drow = lambda b, i: (b, 0, i, 0)
    big = lambda dt: jax.ShapeDtypeStruct((B, 2, S, W), dt)
    return pl.pallas_call(
        _gdnchunk_kernel,
        out_shape=(big(BF16), big(BF16), big(BF16), big(F32),
                   jax.ShapeDtypeStruct((B, 2, S, GDN_HEADS * GDN_CHUNK), BF16),
                   jax.ShapeDtypeStruct((B, 2, S // GDN_CHUNK, GDN_HEADS, LANES), F32)),
        grid=(B, S // tm),
        in_specs=[pl.BlockSpec((1, tm, W), row), pl.BlockSpec((1, tm, W), row),
                  pl.BlockSpec((1, tm, W), row), pl.BlockSpec((1, tm, LANES), row)],
        out_specs=(pl.BlockSpec((1, 2, tm, W), drow), pl.BlockSpec((1, 2, tm, W), drow),
                   pl.BlockSpec((1, 2, tm, W), drow), pl.BlockSpec((1, 2, tm, W), drow),
                   pl.BlockSpec((1, 2, tm, GDN_HEADS * GDN_CHUNK), drow),
                   pl.BlockSpec((1, 2, nc, GDN_HEADS, LANES), lambda b, i: (b, 0, i, 0, 0))),
        compiler_params=_cparams(("parallel", "parallel")),
        name="gdnchunk",
    )(gq, gk, gv, gates)


def _gdnscan_kernel(qg0, w0, kd0, u0, aq0, eg0, qg1, w1, kd1, u1, aq1, eg1, of_ref, ob_ref, s_sc):
    @pl.when(pl.program_id(1) == 0)
    def _():
        s_sc[...] = jnp.zeros(s_sc.shape, F32)

    dirs = ((qg0, w0, kd0, u0, aq0, eg0, of_ref), (qg1, w1, kd1, u1, aq1, eg1, ob_ref))
    chains = [(d, hh) for d in range(2) for hh in range(GDN_HEADS)]
    st, st16, vn16, qs = {}, {}, {}, {}
    for d, hh in chains:
        st[d, hh] = s_sc[d, hh]
        st16[d, hh] = st[d, hh].astype(BF16)
    for d, hh in chains:
        qg, w, kd, u, aq, eg, o_ref = dirs[d]
        sl = slice(hh * GDN_HEAD_DIM, (hh + 1) * GDN_HEAD_DIM)
        vn16[d, hh] = (u[0, 0, :, sl] - _dot(w[0, 0, :, sl], st16[d, hh])).astype(BF16)
        qs[d, hh] = _dot(qg[0, 0, :, sl], st16[d, hh])
    for d, hh in chains:
        qg, w, kd, u, aq, eg, o_ref = dirs[d]
        sl = slice(hh * GDN_HEAD_DIM, (hh + 1) * GDN_HEAD_DIM)
        o_ref[0, :, sl] = qs[d, hh] + _dot(aq[0, 0, :, hh * GDN_CHUNK:(hh + 1) * GDN_CHUNK], vn16[d, hh])
        s_sc[d, hh] = st[d, hh] * eg[0, 0, 0, hh:hh + 1, :] + _dot_tn(kd[0, 0, :, sl], vn16[d, hh])


def _gdnscan(qg, w, kd, u, aq, eg, n_lat_chunks, n_ctx_chunks):
    B, _, S, W = qg.shape
    C = GDN_CHUNK
    n = S // C

    def fwd_chunk(i):
        return jnp.where(i < n_ctx_chunks, n_lat_chunks + i, i - n_ctx_chunks)

    def bwd_chunk(i):
        return jnp.where(i < n_ctx_chunks, n_lat_chunks + n_ctx_chunks - 1 - i, n - 1 - i)

    def specs(d, chunk_of):
        big = pl.BlockSpec((1, 1, C, W), lambda b, i: (b, d, chunk_of(i), 0))
        return [big, big, big, big,
                pl.BlockSpec((1, 1, C, GDN_HEADS * C), lambda b, i: (b, d, chunk_of(i), 0)),
                pl.BlockSpec((1, 1, 1, GDN_HEADS, LANES), lambda b, i: (b, d, chunk_of(i), 0, 0))]

    return pl.pallas_call(
        _gdnscan_kernel,
        out_shape=(jax.ShapeDtypeStruct((B, S, W), F32), jax.ShapeDtypeStruct((B, S, W), F32)),
        grid=(B, n),
        in_specs=specs(0, fwd_chunk) + specs(1, bwd_chunk),
        out_specs=(pl.BlockSpec((1, C, W), lambda b, i: (b, fwd_chunk(i), 0)),
                   pl.BlockSpec((1, C, W), lambda b, i: (b, bwd_chunk(i), 0))),
        scratch_shapes=[pltpu.VMEM((2, GDN_HEADS, GDN_HEAD_DIM, GDN_HEAD_DIM), F32)],
        compiler_params=_cparams(("parallel", "arbitrary")),
        name="gdnscan",
    )(qg, w, kd, u, aq, eg, qg, w, kd, u, aq, eg)


def _mixout_kernel(oa_ref, of_ref, ob_ref, z_ref, h_ref, mod_ref, gg_ref, wo_ref, gf_ref, rw_ref, rb_ref,
                   hn_ref, v_ref, te_ref, tg_ref):
    og = of_ref[0] + ob_ref[0]
    z = z_ref[0]
    parts = [oa_ref[0]]
    for hh in range(GDN_HEADS):
        sl = slice(hh * GDN_HEAD_DIM, (hh + 1) * GDN_HEAD_DIM)
        t = og[:, sl]
        t = t * lax.rsqrt(jnp.mean(t * t, axis=-1, keepdims=True) + EPS) * gg_ref[...]
        zz = z[:, sl]
        parts.append((t * (zz * _sigmoid(zz))).astype(BF16))
    mix_in = jnp.concatenate(parts, axis=-1)
    mix = _dot(mix_in, wo_ref[...])
    hn = h_ref[0] + mod_ref[0, 0, 2:3, :] * mix
    hn_ref[0] = hn
    y = hn * lax.rsqrt(jnp.mean(hn * hn, axis=-1, keepdims=True) + EPS) * gf_ref[...]
    v = y * (1.0 + mod_ref[0, 0, 4:5, :]) + mod_ref[0, 0, 3:4, :]
    v_ref[0] = v.astype(BF16)
    logits = _dot_hi(v, rw_ref[...]) + rb_ref[...]
    lane = lax.broadcasted_iota(jnp.int32, logits.shape, 1)
    cur = logits
    vals, idxs = [], []
    for _ in range(TOP_K):
        m = jnp.max(cur, axis=-1, keepdims=True)
        idx = jnp.min(jnp.where(cur == m, lane, LANES), axis=-1, keepdims=True)
        vals.append(m)
        idxs.append(idx)
        cur = jnp.where(lane == idx, -jnp.inf, cur)
    es = [jnp.exp(vv - vals[0]) for vv in vals]
    inv = 1.0 / (es[0] + es[1] + es[2] + es[3])
    te = jnp.zeros(logits.shape, jnp.int32)
    tg = jnp.zeros(logits.shape, F32)
    for kk in range(TOP_K):
        te = jnp.where(lane == kk, idxs[kk], te)
        tg = jnp.where(lane == kk, es[kk] * inv, tg)
    te_ref[0] = te
    tg_ref[0] = tg


def _mixout(o_att, o_f, o_b, z, h, modtab, gg, w_out, gf, rw, rb, n_lat_blocks):
    B, S, _ = h.shape
    tm = ROW_TILE
    row = lambda b, i: (b, i, 0)
    const = lambda b, i: (0, 0)
    return pl.pallas_call(
        _mixout_kernel,
        out_shape=(jax.ShapeDtypeStruct((B, S, D_MODEL), F32),
                   jax.ShapeDtypeStruct((B, S, D_MODEL), BF16),
                   jax.ShapeDtypeStruct((B, S, LANES), jnp.int32),
                   jax.ShapeDtypeStruct((B, S, LANES), F32)),
        grid=(B, S // tm),
        in_specs=[pl.BlockSpec((1, tm, ATT_WIDTH), row),
                  pl.BlockSpec((1, tm, GDN_WIDTH), row),
                  pl.BlockSpec((1, tm, GDN_WIDTH), row),
                  pl.BlockSpec((1, tm, GDN_WIDTH), row),
                  pl.BlockSpec((1, tm, D_MODEL), row),
                  pl.BlockSpec((1, 1, 8, D_MODEL), lambda b, i: (b, (i >= n_lat_blocks).astype(jnp.int32), 0, 0)),
                  pl.BlockSpec((1, LANES), const),
                  pl.BlockSpec((D_MODEL, D_MODEL), const),
                  pl.BlockSpec((1, D_MODEL), const),
                  pl.BlockSpec((D_MODEL, LANES), const),
                  pl.BlockSpec((1, LANES), const)],
        out_specs=(pl.BlockSpec((1, tm, D_MODEL), row),
                   pl.BlockSpec((1, tm, D_MODEL), row),
                   pl.BlockSpec((1, tm, LANES), row),
                   pl.BlockSpec((1, tm, LANES), row)),
        compiler_params=_cparams(("parallel", "parallel")),
        name="mixout",
    )(o_att, o_f, o_b, z, h, modtab, gg, w_out, gf, rw, rb)


def _expert_kernel(be_ref, nv_ref, x_ref, wgu_ref, bgu_ref, wd_ref, bd_ref, sg_ref, y_ref, wgu_sc, wd_sc):
    i = pl.program_id(0)
    new_expert = jnp.logical_or(i == 0, be_ref[i] != be_ref[jnp.maximum(i - 1, 0)])

    @pl.when(new_expert)
    def _():
        wgu_sc[...] = wgu_ref[0, 0].astype(BF16)
        wd_sc[...] = wd_ref[0, 0].astype(BF16)

    @pl.when(nv_ref[i] > 0)
    def _():
        gu = _dot(x_ref[...], wgu_sc[...]) + bgu_ref[0, 0]
        g_ = jnp.minimum(gu[:, :D_EXPERT], SWIGLU_LIMIT)
        up = jnp.clip(gu[:, D_EXPERT:], -SWIGLU_LIMIT, SWIGLU_LIMIT)
        glu = g_ * _sigmoid(SWIGLU_ALPHA * g_)
        act = ((up + 1.0) * glu).astype(BF16)
        y_ref[...] = ((_dot(act, wd_sc[...]) + bd_ref[0, 0]) * sg_ref[...]).astype(y_ref.dtype)

    @pl.when(nv_ref[i] == 0)
    def _():
        y_ref[...] = jnp.zeros(y_ref.shape, y_ref.dtype)


def _experts(block_e, n_valid, x_sorted, wgu, bgu, wd, bd, slot_gate, layer):
    n_slots = x_sorted.shape[0]
    nb = n_slots // MOE_BLOCK
    grid_spec = pltpu.PrefetchScalarGridSpec(
        num_scalar_prefetch=2,
        grid=(nb,),
        in_specs=[pl.BlockSpec((MOE_BLOCK, D_MODEL), lambda i, be, nv: (i, 0)),
                  pl.BlockSpec((1, 1, D_MODEL, 2 * D_EXPERT), lambda i, be, nv: (layer, be[i], 0, 0)),
                  pl.BlockSpec((1, 1, 1, 2 * D_EXPERT), lambda i, be, nv: (layer, be[i], 0, 0)),
                  pl.BlockSpec((1, 1, D_EXPERT, D_MODEL), lambda i, be, nv: (layer, be[i], 0, 0)),
                  pl.BlockSpec((1, 1, 1, D_MODEL), lambda i, be, nv: (layer, be[i], 0, 0)),
                  pl.BlockSpec((MOE_BLOCK, 1), lambda i, be, nv: (i, 0))],
        out_specs=pl.BlockSpec((MOE_BLOCK, D_MODEL), lambda i, be, nv: (i, 0)),
        scratch_shapes=[pltpu.VMEM((D_MODEL, 2 * D_EXPERT), BF16), pltpu.VMEM((D_EXPERT, D_MODEL), BF16)],
    )
    return pl.pallas_call(
        _expert_kernel,
        out_shape=jax.ShapeDtypeStruct((n_slots, D_MODEL), BF16),
        grid_spec=grid_spec,
        compiler_params=pltpu.CompilerParams(dimension_semantics=("arbitrary",),
                                             vmem_limit_bytes=EXPERT_VMEM_LIMIT),
        name="experts",
    )(block_e, n_valid, x_sorted, wgu, bgu, wd, bd, slot_gate)


def _moe_plan(top_e, top_g):
    T = top_e.shape[0]
    n_assign = T * TOP_K
    n_blocks = -(-n_assign // MOE_BLOCK) + N_EXPERTS
    n_slots = n_blocks * MOE_BLOCK
    i32 = jnp.int32
    flat_e = top_e.reshape(-1)
    gate_flat = top_g.reshape(-1)
    order = jnp.argsort(flat_e).astype(i32)
    rank = jnp.argsort(order).astype(i32)
    e_ids = jnp.arange(N_EXPERTS, dtype=i32)
    is_e = flat_e[:, None] == e_ids[None, :]
    counts = jnp.sum(is_e, axis=0, dtype=i32)
    start = jnp.cumsum(counts) - counts
    padded = (counts + MOE_BLOCK - 1) // MOE_BLOCK * MOE_BLOCK
    pad_end = jnp.cumsum(padded)
    pad_start = pad_end - padded
    blk0 = jnp.arange(n_blocks, dtype=i32) * MOE_BLOCK
    block_e = jnp.minimum(jnp.sum(pad_end[None, :] <= blk0[:, None], axis=1, dtype=i32), N_EXPERTS - 1)
    off = (blk0 - pad_start[block_e])[:, None] + jnp.arange(MOE_BLOCK, dtype=i32)[None, :]
    valid = off < counts[block_e][:, None]
    a_slot = order[jnp.clip(off + start[block_e][:, None], 0, n_assign - 1).reshape(-1)]
    valid_flat = valid.reshape(-1)
    slot_tok = jnp.where(valid_flat, a_slot // TOP_K, 0)
    slot_gate = jnp.where(valid_flat, gate_flat[a_slot], 0.0)
    n_valid = jnp.sum(valid, axis=1, dtype=i32)
    shift = jnp.sum(jnp.where(is_e, (pad_start - start)[None, :], 0), axis=1, dtype=i32)
    slot_of = rank + shift
    return slot_tok, slot_gate.reshape(n_slots, 1), block_e, n_valid, slot_of


def _combine_kernel(y_ref, h_ref, mod_ref, o_ref):
    y = (y_ref[0, 0].astype(F32) + y_ref[1, 0].astype(F32)) + (y_ref[2, 0].astype(F32) + y_ref[3, 0].astype(F32))
    o_ref[0] = h_ref[0] + mod_ref[0, 0, 5:6, :] * y


def _combine(y4, h, modtab, n_lat_blocks):
    B, S, _ = h.shape
    tm = ROW_TILE
    return pl.pallas_call(
        _combine_kernel,
        out_shape=jax.ShapeDtypeStruct((B, S, D_MODEL), F32),
        grid=(B, S // tm),
        in_specs=[pl.BlockSpec((TOP_K, 1, tm, D_MODEL), lambda b, i: (0, b, i, 0)),
                  pl.BlockSpec((1, tm, D_MODEL), lambda b, i: (b, i, 0)),
                  pl.BlockSpec((1, 1, 8, D_MODEL), lambda b, i: (b, (i >= n_lat_blocks).astype(jnp.int32), 0, 0))],
        out_specs=pl.BlockSpec((1, tm, D_MODEL), lambda b, i: (b, i, 0)),
        compiler_params=_cparams(("parallel", "parallel")),
        name="combine",
    )(y4, h, modtab)


def _moe(v_ffn, top_e, top_g, h_new, modtab, wgu, bgu, wd, bd, layer, n_lat_blocks):
    B, S, _ = v_ffn.shape
    T = B * S
    slot_tok, slot_gate, block_e, n_valid, slot_of = _moe_plan(
        top_e.reshape(T, LANES)[:, :TOP_K], top_g.reshape(T, LANES)[:, :TOP_K])
    x_sorted = v_ffn.reshape(T, D_MODEL)[slot_tok]
    y_sorted = _experts(block_e, n_valid, x_sorted, wgu, bgu, wd, bd, slot_gate, layer)
    y4 = y_sorted[slot_of.reshape(T, TOP_K).T.reshape(-1)].reshape(TOP_K, B, S, D_MODEL)
    return _combine(y4, h_new, modtab, n_lat_blocks)


def _pick_tile(n, cands):
    for t in cands:
        if n % t == 0:
            return t
    raise ValueError(f"no tile for {n}")


def _rope_tables(L, Lc):
    rows = L // GRID_W
    row = jnp.repeat(jnp.arange(rows, dtype=F32), GRID_W)
    col = (jnp.arange(L, dtype=jnp.int32) % GRID_W).astype(F32)
    inv_freq = ROPE_BASE ** (-jnp.arange(ROPE_PAIRS, dtype=F32) / ROPE_PAIRS)
    ar = row[:, None] * inv_freq
    ac = col[:, None] * inv_freq
    cos64 = jnp.concatenate([jnp.cos(ar), jnp.cos(ar), jnp.cos(ac), jnp.cos(ac)], axis=-1)
    sin64 = jnp.concatenate([-jnp.sin(ar), jnp.sin(ar), -jnp.sin(ac), jnp.sin(ac)], axis=-1)
    cos_t = jnp.concatenate([jnp.tile(cos64, (1, 2)), jnp.ones((Lc, LANES), F32)], axis=0)
    sin_t = jnp.concatenate([jnp.tile(sin64, (1, 2)), jnp.zeros((Lc, LANES), F32)], axis=0)
    return cos_t, sin_t


def _pad_lanes(v):
    v = v.reshape(1, -1).astype(F32)
    return jnp.pad(v, ((0, 0), (0, LANES - v.shape[1])))


def kernel(x, c, ctx, c_ctx, w_mod, b_mod, norm_mix_g, w_in, q_norm_g, k_norm_g, lam_q1, lam_k1, lam_q2, lam_k2, subln_g, conv_w, a_log, dt_bias, gdn_norm_g, w_out, norm_ffn_g, router_w, router_b, w_gate_up, b_gate_up, w_down, b_down):
    B, L, D = x.shape
    Lc = ctx.shape[1]
    S = L + Lc
    depth = w_mod.shape[0]
    tm = ROW_TILE
    n_lat_blocks = L // tm
    cos_t, sin_t = _rope_tables(L, Lc)

    c_rows = jnp.zeros((8, D), F32).at[:B].set(c).at[B].set(c_ctx)
    h = jnp.concatenate([x, ctx], axis=1)

    for layer in range(depth):
        mod = _adaln(c_rows, w_mod, b_mod, layer)
        mod6 = mod.reshape(8, 6, D)
        lat_mod = mod6[:B]
        ctx_mod = jnp.broadcast_to(mod6[B][None], (B, 6, D))
        modtab = jnp.pad(jnp.stack([lat_mod, ctx_mod], axis=1), ((0, 0), (0, 0), (0, 2), (0, 0)))

        lam_init = 0.8 - 0.6 * math.exp(-0.3 * layer)
        lam_full = (jnp.exp(jnp.sum(lam_q1[layer] * lam_k1[layer]))
                    - jnp.exp(jnp.sum(lam_q2[layer] * lam_k2[layer])) + lam_init).reshape(1).astype(F32)

        w_l = w_in[layer]
        w_main = w_l[:, :IN_MAIN].astype(BF16)
        w_ab = jnp.pad(w_l[:, IN_MAIN:], ((0, 0), (0, LANES - (w_l.shape[1] - IN_MAIN)))).astype(BF16)
        p_qk, v_att, p_gdn, z, ab = _inproj(h, modtab, norm_mix_g[layer].reshape(1, D), w_main, w_ab,
                                            n_lat_blocks)

        gq = jnp.tile(q_norm_g[layer].reshape(1, ATT_HEAD_DIM), (1, 2))
        gk = jnp.tile(k_norm_g[layer].reshape(1, ATT_HEAD_DIM), (1, 2))
        qz, k_att = _qkprep(p_qk, cos_t, sin_t, gq, gk)
        g_sub = (subln_g[layer] * (1.0 - lam_init)).reshape(1, LANES).astype(F32)
        o_att = _attention_lat(lam_full, qz, k_att, v_att, g_sub, L)
        o_att = _attention_ctx(lam_full, qz, k_att, v_att, g_sub, L, o_att)

        conv_w8 = jnp.pad(conv_w[layer], ((0, 8 - CONV_W), (0, 0)))
        gq_g, gk_g, gv_g, gates = _gdnprep(p_gdn, conv_w8, ab, _pad_lanes(a_log[layer]),
                                           _pad_lanes(dt_bias[layer]), n_lat_blocks)
        qg, w_g, kd, u_g, aq, eg = _gdnchunk(gq_g, gk_g, gv_g, gates)
        o_f, o_b = _gdnscan(qg, w_g, kd, u_g, aq, eg, L // GDN_CHUNK, Lc // GDN_CHUNK)

        rw = jnp.pad(router_w[layer], ((0, 0), (0, LANES - N_EXPERTS)))
        rb = jnp.pad(router_b[layer].reshape(1, N_EXPERTS).astype(F32), ((0, 0), (0, LANES - N_EXPERTS)),
                     constant_values=-1e30)
        h_new, v_ffn, top_e, top_g = _mixout(
            o_att, o_f, o_b, z, h, modtab, gdn_norm_g[layer].reshape(1, LANES), w_out[layer].astype(BF16),
            norm_ffn_g[layer].reshape(1, D), rw, rb, n_lat_blocks)

        h = _moe(v_ffn, top_e, top_g, h_new, modtab,
                 w_gate_up, b_gate_up.reshape(depth, N_EXPERTS, 1, 2 * D_EXPERT),
                 w_down, b_down.reshape(depth, N_EXPERTS, 1, D), layer, n_lat_blocks)
    return h[:, :L]
```

```python
import functools
import math

import jax
import jax.numpy as jnp
from jax import lax
from jax.experimental import pallas as pl
from jax.experimental.pallas import tpu as pltpu

F32 = jnp.float32
BF16 = jnp.bfloat16
HIGHEST = lax.Precision.HIGHEST

D_MODEL = 1024
GRID_W = 64
EPS = 1e-6
ATT_WIDTH = 512
ATT_HEAD_DIM = 64
ATT_HEADS = 4
ROPE_BASE = 10000.0
ROPE_PAIRS = ATT_HEAD_DIM // 4
GDN_WIDTH = 512
GDN_HEAD_DIM = 128
GDN_HEADS = 4
GDN_CHUNK = 64
CONV_W = 5
IN_MAIN = 3 * ATT_WIDTH + 4 * GDN_WIDTH
N_EXPERTS = 32
TOP_K = 4
D_EXPERT = 1024
SWIGLU_ALPHA = 1.702
SWIGLU_LIMIT = 7.0
MOE_BLOCK = 512

LANES = 128
ROW_SUB = D_MODEL // LANES
ROW_TILE = 256
KV_CHUNK = 256
Q_SCALE = ATT_HEAD_DIM ** -0.5 * math.log2(math.e)
VMEM_LIMIT = 48 * 1024 * 1024
EXPERT_VMEM_LIMIT = 56 * 1024 * 1024


def _cparams(sem):
    return pltpu.CompilerParams(dimension_semantics=sem, vmem_limit_bytes=VMEM_LIMIT)


def _dot(a, b):
    return jnp.dot(a, b, preferred_element_type=F32)


def _dot_nt(a, b):
    return lax.dot_general(a, b, (((1,), (1,)), ((), ())), preferred_element_type=F32)


def _dot_tn(a, b):
    return lax.dot_general(a, b, (((0,), (0,)), ((), ())), preferred_element_type=F32)


def _dot_hi(a, b):
    return jnp.dot(a, b, preferred_element_type=F32, precision=HIGHEST)


def _sigmoid(x):
    return 1.0 / (1.0 + jnp.exp(-x))


def _adaln_kernel(c_ref, w_ref, b_ref, o_ref):
    c = c_ref[...]
    s = c * _sigmoid(c)
    o_ref[...] = _dot_hi(s, w_ref[0]) + b_ref[0]


def _adaln(c_rows, w, b, layer):
    depth, _, n = w.shape
    tn = 1024
    return pl.pallas_call(
        _adaln_kernel,
        out_shape=jax.ShapeDtypeStruct((8, n), F32),
        grid=(n // tn,),
        in_specs=[pl.BlockSpec((8, D_MODEL), lambda j: (0, 0)),
                  pl.BlockSpec((1, D_MODEL, tn), lambda j: (layer, 0, j)),
                  pl.BlockSpec((1, 1, tn), lambda j: (layer, 0, j))],
        out_specs=pl.BlockSpec((8, tn), lambda j: (0, j)),
        compiler_params=_cparams(("arbitrary",)),
        name="adaln",
    )(c_rows, w, b.reshape(depth, 1, n))


def _inproj_kernel(h_ref, mod_ref, g_ref, w_ref, wab_ref, qk_ref, v_ref, gdn_ref, z_ref, ab_ref):
    x = h_ref[0]
    ms = jnp.mean(x * x, axis=-1, keepdims=True)
    y = x * lax.rsqrt(ms + EPS) * g_ref[...]
    shift = mod_ref[0, 0, 0:1, :]
    scale = mod_ref[0, 0, 1:2, :]
    u = (y * (1.0 + scale) + shift).astype(BF16)
    qk_ref[0] = _dot(u, w_ref[:, 0:2 * ATT_WIDTH])
    vv = _dot(u, w_ref[:, 2 * ATT_WIDTH:3 * ATT_WIDTH])
    for hh in range(ATT_HEADS):
        v_ref[0, hh] = vv[:, hh * LANES:(hh + 1) * LANES].T.astype(BF16)
    off = 3 * ATT_WIDTH
    gdn_ref[0] = _dot(u, w_ref[:, off:off + 3 * GDN_WIDTH])
    z_ref[0] = _dot(u, w_ref[:, off + 3 * GDN_WIDTH:off + 4 * GDN_WIDTH])
    ab_ref[0] = _dot(u, wab_ref[...])


def _inproj(h, modtab, g, w_main, w_ab, n_lat_blocks):
    B, S, _ = h.shape
    tm = ROW_TILE
    row = lambda b, i: (b, i, 0)
    return pl.pallas_call(
        _inproj_kernel,
        out_shape=(jax.ShapeDtypeStruct((B, S, 2 * ATT_WIDTH), F32),
                   jax.ShapeDtypeStruct((B, ATT_HEADS, LANES, S), BF16),
                   jax.ShapeDtypeStruct((B, S, 3 * GDN_WIDTH), F32),
                   jax.ShapeDtypeStruct((B, S, GDN_WIDTH), F32),
                   jax.ShapeDtypeStruct((B, S, LANES), F32)),
        grid=(B, S // tm),
        in_specs=[pl.BlockSpec((1, tm, D_MODEL), row),
                  pl.BlockSpec((1, 1, 8, D_MODEL), lambda b, i: (b, (i >= n_lat_blocks).astype(jnp.int32), 0, 0)),
                  pl.BlockSpec((1, D_MODEL), lambda b, i: (0, 0)),
                  pl.BlockSpec((D_MODEL, IN_MAIN), lambda b, i: (0, 0)),
                  pl.BlockSpec((D_MODEL, LANES), lambda b, i: (0, 0))],
        out_specs=(pl.BlockSpec((1, tm, 2 * ATT_WIDTH), row),
                   pl.BlockSpec((1, ATT_HEADS, LANES, tm), lambda b, i: (b, 0, 0, i)),
                   pl.BlockSpec((1, tm, 3 * GDN_WIDTH), row),
                   pl.BlockSpec((1, tm, GDN_WIDTH), row),
                   pl.BlockSpec((1, tm, LANES), row)),
        compiler_params=_cparams(("parallel", "parallel")),
        name="inproj",
    )(h, modtab, g, w_main, w_ab)


def _qkprep_kernel(p_ref, cos_ref, sin_ref, gq_ref, gk_ref, qz_ref, k_ref):
    tm = p_ref.shape[1]
    lane = lax.broadcasted_iota(jnp.int32, (tm, LANES), 1)
    hi16 = (lane & 16) != 0
    first = lax.broadcasted_iota(jnp.int32, (LANES, tm), 0) < ATT_HEAD_DIM
    r = lax.broadcasted_iota(jnp.int32, (LANES, LANES), 0) >> 6
    c = lax.broadcasted_iota(jnp.int32, (LANES, LANES), 1) >> 6
    gmat = jnp.where(r == c, 1.0 / ATT_HEAD_DIM, 0.0).astype(F32)
    cosv = cos_ref[...]
    sinv = sin_ref[...]
    for j in range(2 * ATT_HEADS):
        x = p_ref[0, :, j * LANES:(j + 1) * LANES]
        ms = _dot_hi(x * x, gmat)
        g = gq_ref[...] if j < ATT_HEADS else gk_ref[...]
        y = x * lax.rsqrt(ms + EPS) * g
        sw = jnp.where(hi16, pltpu.roll(y, 16, 1), pltpu.roll(y, LANES - 16, 1))
        y = y * cosv + sw * sinv
        if j < ATT_HEADS:
            yt = (y * Q_SCALE).T
            qz_ref[0, j, 0] = jnp.where(first, yt, 0.0).astype(BF16)
            qz_ref[0, j, 1] = jnp.where(first, 0.0, yt).astype(BF16)
        else:
            k_ref[0, j - ATT_HEADS] = y.astype(BF16)


def _qkprep(p_qk, cos_t, sin_t, gq, gk):
    B, S, _ = p_qk.shape
    tm = ROW_TILE
    return pl.pallas_call(
        _qkprep_kernel,
        out_shape=(jax.ShapeDtypeStruct((B, ATT_HEADS, 2, LANES, S), BF16),
                   jax.ShapeDtypeStruct((B, ATT_HEADS, S, LANES), BF16)),
        grid=(B, S // tm),
        in_specs=[pl.BlockSpec((1, tm, 2 * ATT_WIDTH), lambda b, i: (b, i, 0)),
                  pl.BlockSpec((tm, LANES), lambda b, i: (i, 0)),
                  pl.BlockSpec((tm, LANES), lambda b, i: (i, 0)),
                  pl.BlockSpec((1, LANES), lambda b, i: (0, 0)),
                  pl.BlockSpec((1, LANES), lambda b, i: (0, 0))],
        out_specs=(pl.BlockSpec((1, ATT_HEADS, 2, LANES, tm), lambda b, i: (b, 0, 0, 0, i)),
                   pl.BlockSpec((1, ATT_HEADS, tm, LANES), lambda b, i: (b, 0, i, 0))),
        compiler_params=_cparams(("parallel", "parallel")),
        name="qkprep",
    )(p_qk, cos_t, sin_t, gq, gk)


def _attn_kernel(lam_ref, qz_ref, k_ref, v_ref, g_ref, *rest, aliased, tk, n_kv):
    if aliased:
        rest = rest[1:]
    o_ref, s_sc, mt_sc, m_sc, l_sc, acc_sc = rest
    m_sc[...] = jnp.full(m_sc.shape, -jnp.inf, F32)
    l_sc[...] = jnp.zeros(l_sc.shape, F32)
    acc_sc[...] = jnp.zeros(acc_sc.shape, F32)

    def rows(j):
        if isinstance(j, int):
            return pl.ds(j * tk, tk)
        return pl.ds(pl.multiple_of(j * tk, tk), tk)

    def qk_tile(j, slot, ps=(0, 1)):
        kt = k_ref[0, 0, rows(j), :]
        for p in ps:
            s = _dot(kt, qz_ref[0, 0, p])
            s_sc[slot, p] = s
            mt_sc[slot, p] = jnp.max(s, axis=0, keepdims=True)

    def pv_tile(j, slot, ps=(0, 1)):
        for p in ps:
            m_prev = m_sc[p]
            m_new = jnp.maximum(m_prev, mt_sc[slot, p])
            alpha = jnp.exp2(m_prev - m_new)
            lsum = None
            acc = None
            for c in range(tk // KV_CHUNK):
                cs = slice(c * KV_CHUNK, (c + 1) * KV_CHUNK)
                pe = jnp.exp2(s_sc[slot, p, cs, :] - m_new)
                ps = jnp.sum(pe, axis=0, keepdims=True)
                pv = _dot(v_ref[0, 0, j, :, cs], pe.astype(BF16))
                lsum = ps if lsum is None else lsum + ps
                acc = pv if acc is None else acc + pv
            l_sc[p] = alpha * l_sc[p] + lsum
            acc_sc[p] = alpha * acc_sc[p] + acc
            m_sc[p] = m_new

    def fused_tile(jq, slot_q, jp, slot_p):
        m_new = [jnp.maximum(m_sc[p], mt_sc[slot_p, p]) for p in range(2)]
        alpha = [jnp.exp2(m_sc[p] - m_new[p]) for p in range(2)]
        mx, lsum, acc = [None, None], [None, None], [None, None]
        for c in range(tk // KV_CHUNK):
            cs = slice(c * KV_CHUNK, (c + 1) * KV_CHUNK)
            if isinstance(jq, int):
                kr = pl.ds(jq * tk + c * KV_CHUNK, KV_CHUNK)
            else:
                kr = pl.ds(pl.multiple_of(jq * tk + c * KV_CHUNK, KV_CHUNK), KV_CHUNK)
            kc = k_ref[0, 0, kr, :]
            vc = v_ref[0, 0, jp, :, cs]
            for p in range(2):
                s = _dot(kc, qz_ref[0, 0, p])
                s_sc[slot_q, p, cs, :] = s
                cm = jnp.max(s, axis=0, keepdims=True)
                mx[p] = cm if mx[p] is None else jnp.maximum(mx[p], cm)
                pe = jnp.exp2(s_sc[slot_p, p, cs, :] - m_new[p])
                ps = jnp.sum(pe, axis=0, keepdims=True)
                pv = _dot(vc, pe.astype(BF16))
                lsum[p] = ps if lsum[p] is None else lsum[p] + ps
                acc[p] = pv if acc[p] is None else acc[p] + pv
        for p in range(2):
            mt_sc[slot_q, p] = mx[p]
            l_sc[p] = alpha[p] * l_sc[p] + lsum[p]
            acc_sc[p] = alpha[p] * acc_sc[p] + acc[p]
            m_sc[p] = m_new[p]

    qk_tile(0, 0)
    n_pairs = (n_kv - 1) // 2

    def pair(jj, carry):
        j = 2 * jj
        fused_tile(j + 1, 1, j, 0)
        fused_tile(j + 2, 0, j + 1, 1)
        return carry

    if n_pairs > 0:
        lax.fori_loop(0, n_pairs, pair, 0)
    j = 2 * n_pairs
    if j == n_kv - 1:
        pv_tile(j, 0)
    else:
        qk_tile(j + 1, 1)
        pv_tile(j, 0)
        pv_tile(j + 1, 1)

    o = acc_sc[0] / l_sc[0] - lam_ref[0] * (acc_sc[1] / l_sc[1])
    ms = jnp.mean(o * o, axis=0, keepdims=True)
    o_ref[0] = ((o * lax.rsqrt(ms + EPS)).T * g_ref[...]).astype(BF16)


def _attention(lam, qz, k, vt, g, *, tq, tk, q_blk0, n_q, kv_blk0, n_kv, prev_out=None):
    B, H, _, _, S = qz.shape
    aliased = prev_out is not None
    skv = tk * n_kv
    in_specs = [pl.BlockSpec(memory_space=pltpu.SMEM),
                pl.BlockSpec((1, 1, 2, LANES, tq), lambda b, h, i: (b, h, 0, 0, q_blk0 + i)),
                pl.BlockSpec((1, 1, skv, LANES), lambda b, h, i: (b, h, kv_blk0, 0)),
                pl.BlockSpec((1, 1, n_kv, LANES, tk), lambda b, h, i: (b, h, 0, 0, 0)),
                pl.BlockSpec((1, LANES), lambda b, h, i: (0, 0))]
    args = [lam, qz, k, vt, g]
    aliases = {}
    if aliased:
        in_specs.append(pl.BlockSpec(memory_space=pl.ANY))
        args.append(prev_out)
        aliases = {5: 0}
    return pl.pallas_call(
        functools.partial(_attn_kernel, aliased=aliased, tk=tk, n_kv=n_kv),
        out_shape=jax.ShapeDtypeStruct((B, S, ATT_WIDTH), BF16),
        grid=(B, H, n_q),
        in_specs=in_specs,
        out_specs=pl.BlockSpec((1, tq, LANES), lambda b, h, i: (b, q_blk0 + i, h)),
        scratch_shapes=[pltpu.VMEM((2, 2, tk, tq), F32), pltpu.VMEM((2, 2, 1, tq), F32),
                        pltpu.VMEM((2, 1, tq), F32), pltpu.VMEM((2, 1, tq), F32),
                        pltpu.VMEM((2, LANES, tq), F32)],
        input_output_aliases=aliases,
        compiler_params=_cparams(("parallel", "parallel", "arbitrary")),
        name="attn_ctx" if aliased else "attn_lat",
    )(*args)


def _attention_lat(lam, qz, k, vt, g, L):
    B, H, _, S = vt.shape
    tq = _pick_tile(L, (512, 256))
    tk = _pick_tile(S, (1280, 640, 256))
    n_kv = S // tk
    vt_tiles = vt.reshape(B, H, LANES, n_kv, tk).transpose(0, 1, 3, 2, 4)
    return _attention(lam, qz, k, vt_tiles, g, tq=tq, tk=tk, q_blk0=0, n_q=L // tq, kv_blk0=0, n_kv=n_kv)


def _attention_ctx(lam, qz, k, vt, g, L, prev_out):
    B, H, _, S = vt.shape
    Lc = S - L
    vt_ctx = vt[:, :, :, L:].reshape(B, H, 1, LANES, Lc)
    return _attention(lam, qz, k, vt_ctx, g, tq=Lc, tk=Lc, q_blk0=L // Lc, n_q=1, kv_blk0=L // Lc, n_kv=1,
                      prev_out=prev_out)


def _gdnprep_kernel(x_ref, prev_ref, next_ref, cw_ref, ab_ref, alog_ref, dtb_ref,
                    q_ref, k_ref, v_ref, gate_ref, ext_sc, *, nb_lat, nb_all):
    i = pl.program_id(1)
    tm = x_ref.shape[1]
    first = jnp.logical_or(i == 0, i == nb_lat)
    last = jnp.logical_or(i == nb_lat - 1, i == nb_all - 1)
    keep_prev = jnp.where(first, 0.0, 1.0)
    keep_next = jnp.where(last, 0.0, 1.0)
    ext_sc[0:8, :] = prev_ref[0] * keep_prev
    ext_sc[8:8 + tm, :] = x_ref[0]
    ext_sc[8 + tm:16 + tm, :] = next_ref[0] * keep_next
    acc = None
    for j in range(CONV_W):
        term = ext_sc[pl.ds(8 + j - CONV_W // 2, tm), :] * cw_ref[j:j + 1, :]
        acc = term if acc is None else acc + term
    y = acc * _sigmoid(acc)
    for part, ref in ((0, q_ref), (1, k_ref)):
        for hh in range(GDN_HEADS):
            lo = part * GDN_WIDTH + hh * GDN_HEAD_DIM
            t = y[:, lo:lo + GDN_HEAD_DIM]
            ref[0, :, hh * GDN_HEAD_DIM:(hh + 1) * GDN_HEAD_DIM] = (
                t * lax.rsqrt(jnp.sum(t * t, axis=-1, keepdims=True) + EPS))
    v_ref[0] = y[:, 2 * GDN_WIDTH:3 * GDN_WIDTH]
    ab = ab_ref[0]
    xs = ab + dtb_ref[...]
    sp = jnp.maximum(xs, 0.0) + jnp.log(1.0 + jnp.exp(-jnp.abs(xs)))
    g = -jnp.exp(alog_ref[...]) * sp
    lane = lax.broadcasted_iota(jnp.int32, ab.shape, 1)
    gate_ref[0] = jnp.where(lane < 2 * GDN_HEADS, g, _sigmoid(ab))


def _gdnprep(p_gdn, conv_w8, ab, alog_row, dtb_row, n_lat_blocks):
    B, S, W = p_gdn.shape
    tm = ROW_TILE
    nb = S // tm
    r8 = tm // 8
    row = lambda b, i: (b, i, 0)
    return pl.pallas_call(
        functools.partial(_gdnprep_kernel, nb_lat=n_lat_blocks, nb_all=nb),
        out_shape=(jax.ShapeDtypeStruct((B, S, GDN_WIDTH), F32),
                   jax.ShapeDtypeStruct((B, S, GDN_WIDTH), F32),
                   jax.ShapeDtypeStruct((B, S, GDN_WIDTH), F32),
                   jax.ShapeDtypeStruct((B, S, LANES), F32)),
        grid=(B, nb),
        in_specs=[pl.BlockSpec((1, tm, W), row),
                  pl.BlockSpec((1, 8, W), lambda b, i: (b, jnp.maximum(i * r8 - 1, 0), 0)),
                  pl.BlockSpec((1, 8, W), lambda b, i: (b, jnp.minimum((i + 1) * r8, nb * r8 - 1), 0)),
                  pl.BlockSpec((8, W), lambda b, i: (0, 0)),
                  pl.BlockSpec((1, tm, LANES), row),
                  pl.BlockSpec((1, LANES), lambda b, i: (0, 0)),
                  pl.BlockSpec((1, LANES), lambda b, i: (0, 0))],
        out_specs=(pl.BlockSpec((1, tm, GDN_WIDTH), row),
                   pl.BlockSpec((1, tm, GDN_WIDTH), row),
                   pl.BlockSpec((1, tm, GDN_WIDTH), row),
                   pl.BlockSpec((1, tm, LANES), row)),
        scratch_shapes=[pltpu.VMEM((tm + 16, W), F32)],
        compiler_params=_cparams(("parallel", "parallel")),
        name="gdnprep",
    )(p_gdn, p_gdn, p_gdn, conv_w8, ab, alog_row, dtb_row)


def _gdnchunk_kernel(q_ref, k_ref, v_ref, gate_ref, qg_ref, w_ref, kd_ref, u_ref, aq_ref, eg_ref):
    tm = q_ref.shape[1]
    nc = tm // GDN_CHUNK
    gates = gate_ref[0]
    ri = lax.broadcasted_iota(jnp.int32, (tm, tm), 0)
    ci = lax.broadcasted_iota(jnp.int32, (tm, tm), 1)
    same = (ri >> 6) == (ci >> 6)
    eye = ri == ci
    tot = _dot_hi(jnp.where(same, 1.0, 0.0).astype(F32), gates)
    eye_f = jnp.where(eye, 1.0, 0.0).astype(F32)

    heads = []
    for hh in range(GDN_HEADS):
        sl = slice(hh * GDN_HEAD_DIM, (hh + 1) * GDN_HEAD_DIM)
        k = k_ref[0, :, sl]
        k16 = k.astype(BF16)
        q = q_ref[0, :, sl] * (GDN_HEAD_DIM ** -0.5)
        heads.append((sl, k, q, _dot_nt(k16, k16), _dot_nt(q.astype(BF16), k16)))

    chains = []
    for d in range(2):
        incl = jnp.logical_and(same, (ci <= ri) if d == 0 else (ci >= ri))
        strict = jnp.logical_and(incl, jnp.logical_not(eye))
        gc = _dot_hi(jnp.where(incl, 1.0, 0.0).astype(F32), gates)
        gc_t = gc.T
        for hh in range(GDN_HEADS):
            ln = d * GDN_HEADS + hh
            gcol = gc[:, ln:ln + 1]
            bcol = gates[:, 2 * GDN_HEADS + ln:2 * GDN_HEADS + ln + 1]
            tcol = tot[:, ln:ln + 1]
            diff = gcol - gc_t[ln:ln + 1, :]
            decay = jnp.where(incl, jnp.exp(jnp.where(incl, diff, 0.0)), 0.0)
            a = jnp.where(strict, heads[hh][3] * bcol * decay, 0.0)
            aqk = heads[hh][4] * decay
            chains.append(dict(d=d, hh=hh, x=-a, t=eye_f - a, aqk=aqk, gcol=gcol, bcol=bcol, tcol=tcol))

    for _ in range(5):
        for ch in chains:
            x16 = ch["x"].astype(BF16)
            ch["x"] = _dot(x16, x16)
        for ch in chains:
            ch["t"] = ch["t"] + _dot(ch["t"].astype(BF16), ch["x"].astype(BF16))

    for ch in chains:
        d, hh, gcol, bcol, tcol = ch["d"], ch["hh"], ch["gcol"], ch["bcol"], ch["tcol"]
        sl, k, q = heads[hh][0], heads[hh][1], heads[hh][2]
        t16 = ch["t"].astype(BF16)
        egc = jnp.exp(gcol)
        kb = k * bcol
        u_ref[0, d, :, sl] = _dot(t16, (v_ref[0, :, sl] * bcol).astype(BF16))
        w_ref[0, d, :, sl] = _dot(t16, (kb * egc).astype(BF16)).astype(BF16)
        kd_ref[0, d, :, sl] = (k * jnp.exp(tcol - gcol)).astype(BF16)
        qg_ref[0, d, :, sl] = (q * egc).astype(BF16)
        for cc in range(nc):
            rs = slice(cc * GDN_CHUNK, (cc + 1) * GDN_CHUNK)
            aq_ref[0, d, rs, hh * GDN_CHUNK:(hh + 1) * GDN_CHUNK] = ch["aqk"][rs, rs].astype(BF16)
            eg_ref[0, d, cc, hh:hh + 1, :] = jnp.broadcast_to(
                jnp.exp(tcol[cc * GDN_CHUNK:cc * GDN_CHUNK + 1, :]), (1, LANES))


def _gdnchunk(gq, gk, gv, gates):
    B, S, W = gq.shape
    tm = ROW_TILE
    nc = tm // GDN_CHUNK
    row = lambda b, i: (b, i, 0)
    drow = lambda b, i: (b, 0, i, 0)
    big = lambda dt: jax.ShapeDtypeStruct((B, 2, S, W), dt)
    return pl.pallas_call(
        _gdnchunk_kernel,
        out_shape=(big(BF16), big(BF16), big(BF16), big(F32),
                   jax.ShapeDtypeStruct((B, 2, S, GDN_HEADS * GDN_CHUNK), BF16),
                   jax.ShapeDtypeStruct((B, 2, S // GDN_CHUNK, GDN_HEADS, LANES), F32)),
        grid=(B, S // tm),
        in_specs=[pl.BlockSpec((1, tm, W), row), pl.BlockSpec((1, tm, W), row),
                  pl.BlockSpec((1, tm, W), row), pl.BlockSpec((1, tm, LANES), row)],
        out_specs=(pl.BlockSpec((1, 2, tm, W), drow), pl.BlockSpec((1, 2, tm, W), drow),
                   pl.BlockSpec((1, 2, tm, W), drow), pl.BlockSpec((1, 2, tm, W), drow),
                   pl.BlockSpec((1, 2, tm, GDN_HEADS * GDN_CHUNK), drow),
                   pl.BlockSpec((1, 2, nc, GDN_HEADS, LANES), lambda b, i: (b, 0, i, 0, 0))),
        compiler_params=_cparams(("parallel", "parallel")),
        name="gdnchunk",
    )(gq, gk, gv, gates)


def _gdnscan_kernel(qg0, w0, kd0, u0, aq0, eg0, qg1, w1, kd1, u1, aq1, eg1, of_ref, ob_ref, s_sc):
    @pl.when(pl.program_id(1) == 0)
    def _():
        s_sc[...] = jnp.zeros(s_sc.shape, F32)

    dirs = ((qg0, w0, kd0, u0, aq0, eg0, of_ref), (qg1, w1, kd1, u1, aq1, eg1, ob_ref))
    chains = [(d, hh) for d in range(2) for hh in range(GDN_HEADS)]
    st, st16, vn16, qs = {}, {}, {}, {}
    for d, hh in chains:
        st[d, hh] = s_sc[d, hh]
        st16[d, hh] = st[d, hh].astype(BF16)
    for d, hh in chains:
        qg, w, kd, u, aq, eg, o_ref = dirs[d]
        sl = slice(hh * GDN_HEAD_DIM, (hh + 1) * GDN_HEAD_DIM)
        vn16[d, hh] = (u[0, 0, :, sl] - _dot(w[0, 0, :, sl], st16[d, hh])).astype(BF16)
        qs[d, hh] = _dot(qg[0, 0, :, sl], st16[d, hh])
    for d, hh in chains:
        qg, w, kd, u, aq, eg, o_ref = dirs[d]
        sl = slice(hh * GDN_HEAD_DIM, (hh + 1) * GDN_HEAD_DIM)
        o_ref[0, :, sl] = qs[d, hh] + _dot(aq[0, 0, :, hh * GDN_CHUNK:(hh + 1) * GDN_CHUNK], vn16[d, hh])
        s_sc[d, hh] = st[d, hh] * eg[0, 0, 0, hh:hh + 1, :] + _dot_tn(kd[0, 0, :, sl], vn16[d, hh])


def _gdnscan(qg, w, kd, u, aq, eg, n_lat_chunks, n_ctx_chunks):
    B, _, S, W = qg.shape
    C = GDN_CHUNK
    n = S // C

    def fwd_chunk(i):
        return jnp.where(i < n_ctx_chunks, n_lat_chunks + i, i - n_ctx_chunks)

    def bwd_chunk(i):
        return jnp.where(i < n_ctx_chunks, n_lat_chunks + n_ctx_chunks - 1 - i, n - 1 - i)

    def specs(d, chunk_of):
        big = pl.BlockSpec((1, 1, C, W), lambda b, i: (b, d, chunk_of(i), 0))
        return [big, big, big, big,
                pl.BlockSpec((1, 1, C, GDN_HEADS * C), lambda b, i: (b, d, chunk_of(i), 0)),
                pl.BlockSpec((1, 1, 1, GDN_HEADS, LANES), lambda b, i: (b, d, chunk_of(i), 0, 0))]

    return pl.pallas_call(
        _gdnscan_kernel,
        out_shape=(jax.ShapeDtypeStruct((B, S, W), F32), jax.ShapeDtypeStruct((B, S, W), F32)),
        grid=(B, n),
        in_specs=specs(0, fwd_chunk) + specs(1, bwd_chunk),
        out_specs=(pl.BlockSpec((1, C, W), lambda b, i: (b, fwd_chunk(i), 0)),
                   pl.BlockSpec((1, C, W), lambda b, i: (b, bwd_chunk(i), 0))),
        scratch_shapes=[pltpu.VMEM((2, GDN_HEADS, GDN_HEAD_DIM, GDN_HEAD_DIM), F32)],
        compiler_params=_cparams(("parallel", "arbitrary")),
        name="gdnscan",
    )(qg, w, kd, u, aq, eg, qg, w, kd, u, aq, eg)


def _mixout_kernel(oa_ref, of_ref, ob_ref, z_ref, h_ref, mod_ref, gg_ref, wo_ref, gf_ref, rw_ref, rb_ref,
                   hn_ref, v_ref, te_ref, tg_ref):
    og = of_ref[0] + ob_ref[0]
    z = z_ref[0]
    parts = [oa_ref[0]]
    for hh in range(GDN_HEADS):
        sl = slice(hh * GDN_HEAD_DIM, (hh + 1) * GDN_HEAD_DIM)
        t = og[:, sl]
        t = t * lax.rsqrt(jnp.mean(t * t, axis=-1, keepdims=True) + EPS) * gg_ref[...]
        zz = z[:, sl]
        parts.append((t * (zz * _sigmoid(zz))).astype(BF16))
    mix_in = jnp.concatenate(parts, axis=-1)
    mix = _dot(mix_in, wo_ref[...])
    hn = h_ref[0] + mod_ref[0, 0, 2:3, :] * mix
    hn_ref[0] = hn
    y = hn * lax.rsqrt(jnp.mean(hn * hn, axis=-1, keepdims=True) + EPS) * gf_ref[...]
    v = y * (1.0 + mod_ref[0, 0, 4:5, :]) + mod_ref[0, 0, 3:4, :]
    v_ref[0] = v.reshape(v.shape[0], ROW_SUB, LANES)
    logits = _dot_hi(v, rw_ref[...]) + rb_ref[...]
    lane = lax.broadcasted_iota(jnp.int32, logits.shape, 1)
    cur = logits
    vals, idxs = [], []
    for _ in range(TOP_K):
        m = jnp.max(cur, axis=-1, keepdims=True)
        idx = jnp.min(jnp.where(cur == m, lane, LANES), axis=-1, keepdims=True)
        vals.append(m)
        idxs.append(idx)
        cur = jnp.where(lane == idx, -jnp.inf, cur)
    es = [jnp.exp(vv - vals[0]) for vv in vals]
    inv = 1.0 / (es[0] + es[1] + es[2] + es[3])
    te = jnp.zeros(logits.shape, jnp.int32)
    tg = jnp.zeros(logits.shape, F32)
    for kk in range(TOP_K):
        te = jnp.where(lane == kk, idxs[kk], te)
        tg = jnp.where(lane == kk, es[kk] * inv, tg)
    te_ref[0] = te
    tg_ref[0] = tg


def _mixout(o_att, o_f, o_b, z, h, modtab, gg, w_out, gf, rw, rb, n_lat_blocks):
    B, S, _ = h.shape
    tm = ROW_TILE
    row = lambda b, i: (b, i, 0)
    const = lambda b, i: (0, 0)
    return pl.pallas_call(
        _mixout_kernel,
        out_shape=(jax.ShapeDtypeStruct((B, S, D_MODEL), F32),
                   jax.ShapeDtypeStruct((B, S, ROW_SUB, LANES), F32),
                   jax.ShapeDtypeStruct((B, S, LANES), jnp.int32),
                   jax.ShapeDtypeStruct((B, S, LANES), F32)),
        grid=(B, S // tm),
        in_specs=[pl.BlockSpec((1, tm, ATT_WIDTH), row),
                  pl.BlockSpec((1, tm, GDN_WIDTH), row),
                  pl.BlockSpec((1, tm, GDN_WIDTH), row),
                  pl.BlockSpec((1, tm, GDN_WIDTH), row),
                  pl.BlockSpec((1, tm, D_MODEL), row),
                  pl.BlockSpec((1, 1, 8, D_MODEL), lambda b, i: (b, (i >= n_lat_blocks).astype(jnp.int32), 0, 0)),
                  pl.BlockSpec((1, LANES), const),
                  pl.BlockSpec((D_MODEL, D_MODEL), const),
                  pl.BlockSpec((1, D_MODEL), const),
                  pl.BlockSpec((D_MODEL, LANES), const),
                  pl.BlockSpec((1, LANES), const)],
        out_specs=(pl.BlockSpec((1, tm, D_MODEL), row),
                   pl.BlockSpec((1, tm, ROW_SUB, LANES), lambda b, i: (b, i, 0, 0)),
                   pl.BlockSpec((1, tm, LANES), row),
                   pl.BlockSpec((1, tm, LANES), row)),
        compiler_params=_cparams(("parallel", "parallel")),
        name="mixout",
    )(o_att, o_f, o_b, z, h, modtab, gg, w_out, gf, rw, rb)


def _expert_kernel(be_ref, nv_ref, x_ref, wgu_ref, bgu_ref, wd_ref, bd_ref, sg_ref, y_ref, wgu_sc, wd_sc):
    i = pl.program_id(0)
    new_expert = jnp.logical_or(i == 0, be_ref[i] != be_ref[jnp.maximum(i - 1, 0)])

    @pl.when(new_expert)
    def _():
        wgu_sc[...] = wgu_ref[0, 0].astype(BF16)
        wd_sc[...] = wd_ref[0, 0].astype(BF16)

    @pl.when(nv_ref[i] > 0)
    def _():
        x = x_ref[...].reshape(MOE_BLOCK, D_MODEL).astype(BF16)
        gu = _dot(x, wgu_sc[...]) + bgu_ref[0, 0]
        g_ = jnp.minimum(gu[:, :D_EXPERT], SWIGLU_LIMIT)
        up = jnp.clip(gu[:, D_EXPERT:], -SWIGLU_LIMIT, SWIGLU_LIMIT)
        glu = g_ * _sigmoid(SWIGLU_ALPHA * g_)
        act = ((up + 1.0) * glu).astype(BF16)
        y = (_dot(act, wd_sc[...]) + bd_ref[0, 0]) * sg_ref[...]
        y_ref[...] = y.reshape(MOE_BLOCK, ROW_SUB, LANES)

    @pl.when(nv_ref[i] == 0)
    def _():
        y_ref[...] = jnp.zeros(y_ref.shape, y_ref.dtype)


def _experts(block_e, n_valid, x_sorted, wgu, bgu, wd, bd, slot_gate, layer):
    n_slots = x_sorted.shape[0]
    nb = n_slots // MOE_BLOCK
    grid_spec = pltpu.PrefetchScalarGridSpec(
        num_scalar_prefetch=2,
        grid=(nb,),
        in_specs=[pl.BlockSpec((MOE_BLOCK, ROW_SUB, LANES), lambda i, be, nv: (i, 0, 0)),
                  pl.BlockSpec((1, 1, D_MODEL, 2 * D_EXPERT), lambda i, be, nv: (layer, be[i], 0, 0)),
                  pl.BlockSpec((1, 1, 1, 2 * D_EXPERT), lambda i, be, nv: (layer, be[i], 0, 0)),
                  pl.BlockSpec((1, 1, D_EXPERT, D_MODEL), lambda i, be, nv: (layer, be[i], 0, 0)),
                  pl.BlockSpec((1, 1, 1, D_MODEL), lambda i, be, nv: (layer, be[i], 0, 0)),
                  pl.BlockSpec((MOE_BLOCK, 1), lambda i, be, nv: (i, 0))],
        out_specs=pl.BlockSpec((MOE_BLOCK, ROW_SUB, LANES), lambda i, be, nv: (i, 0, 0)),
        scratch_shapes=[pltpu.VMEM((D_MODEL, 2 * D_EXPERT), BF16), pltpu.VMEM((D_EXPERT, D_MODEL), BF16)],
    )
    return pl.pallas_call(
        _expert_kernel,
        out_shape=jax.ShapeDtypeStruct((n_slots, ROW_SUB, LANES), F32),
        grid_spec=grid_spec,
        compiler_params=pltpu.CompilerParams(dimension_semantics=("arbitrary",),
                                             vmem_limit_bytes=EXPERT_VMEM_LIMIT),
        name="experts",
    )(block_e, n_valid, x_sorted, wgu, bgu, wd, bd, slot_gate)


def _moe_plan(top_e, top_g):
    T = top_e.shape[0]
    n_assign = T * TOP_K
    n_blocks = -(-n_assign // MOE_BLOCK) + N_EXPERTS
    n_slots = n_blocks * MOE_BLOCK
    i32 = jnp.int32
    flat_e = top_e.reshape(-1)
    gate_flat = top_g.reshape(-1)
    order = jnp.argsort(flat_e).astype(i32)
    rank = jnp.argsort(order).astype(i32)
    e_ids = jnp.arange(N_EXPERTS, dtype=i32)
    is_e = flat_e[:, None] == e_ids[None, :]
    counts = jnp.sum(is_e, axis=0, dtype=i32)
    start = jnp.cumsum(counts) - counts
    padded = (counts + MOE_BLOCK - 1) // MOE_BLOCK * MOE_BLOCK
    pad_end = jnp.cumsum(padded)
    pad_start = pad_end - padded
    blk0 = jnp.arange(n_blocks, dtype=i32) * MOE_BLOCK
    block_e = jnp.minimum(jnp.sum(pad_end[None, :] <= blk0[:, None], axis=1, dtype=i32), N_EXPERTS - 1)
    off = (blk0 - pad_start[block_e])[:, None] + jnp.arange(MOE_BLOCK, dtype=i32)[None, :]
    valid = off < counts[block_e][:, None]
    a_slot = order[jnp.clip(off + start[block_e][:, None], 0, n_assign - 1).reshape(-1)]
    valid_flat = valid.reshape(-1)
    slot_tok = jnp.where(valid_flat, a_slot // TOP_K, 0)
    slot_gate = jnp.where(valid_flat, gate_flat[a_slot], 0.0)
    n_valid = jnp.sum(valid, axis=1, dtype=i32)
    shift = jnp.sum(jnp.where(is_e, (pad_start - start)[None, :], 0), axis=1, dtype=i32)
    slot_of = rank + shift
    return slot_tok, slot_gate.reshape(n_slots, 1), block_e, n_valid, slot_of


def _combine_kernel(y_ref, h_ref, mod_ref, o_ref):
    y = (y_ref[0, 0] + y_ref[1, 0]) + (y_ref[2, 0] + y_ref[3, 0])
    o_ref[0] = h_ref[0] + mod_ref[0, 0, 5:6, :] * y.reshape(y.shape[0], D_MODEL)


def _combine(y4, h, modtab, n_lat_blocks):
    B, S, _ = h.shape
    tm = ROW_TILE
    return pl.pallas_call(
        _combine_kernel,
        out_shape=jax.ShapeDtypeStruct((B, S, D_MODEL), F32),
        grid=(B, S // tm),
        in_specs=[pl.BlockSpec((TOP_K, 1, tm, ROW_SUB, LANES), lambda b, i: (0, b, i, 0, 0)),
                  pl.BlockSpec((1, tm, D_MODEL), lambda b, i: (b, i, 0)),
                  pl.BlockSpec((1, 1, 8, D_MODEL), lambda b, i: (b, (i >= n_lat_blocks).astype(jnp.int32), 0, 0))],
        out_specs=pl.BlockSpec((1, tm, D_MODEL), lambda b, i: (b, i, 0)),
        compiler_params=_cparams(("parallel", "parallel")),
        name="combine",
    )(y4, h, modtab)


def _moe(v_ffn, top_e, top_g, h_new, modtab, wgu, bgu, wd, bd, layer, n_lat_blocks):
    B, S = v_ffn.shape[:2]
    T = B * S
    slot_tok, slot_gate, block_e, n_valid, slot_of = _moe_plan(
        top_e.reshape(T, LANES)[:, :TOP_K], top_g.reshape(T, LANES)[:, :TOP_K])
    x_sorted = v_ffn.reshape(T, ROW_SUB, LANES)[slot_tok]
    y_sorted = _experts(block_e, n_valid, x_sorted, wgu, bgu, wd, bd, slot_gate, layer)
    y4 = y_sorted[slot_of.reshape(T, TOP_K).T.reshape(-1)].reshape(TOP_K, B, S, ROW_SUB, LANES)
    return _combine(y4, h_new, modtab, n_lat_blocks)


def _pick_tile(n, cands):
    for t in cands:
        if n % t == 0:
            return t
    raise ValueError(f"no tile for {n}")


def _rope_tables(L, Lc):
    rows = L // GRID_W
    row = jnp.repeat(jnp.arange(rows, dtype=F32), GRID_W)
    col = (jnp.arange(L, dtype=jnp.int32) % GRID_W).astype(F32)
    inv_freq = ROPE_BASE ** (-jnp.arange(ROPE_PAIRS, dtype=F32) / ROPE_PAIRS)
    ar = row[:, None] * inv_freq
    ac = col[:, None] * inv_freq
    cos64 = jnp.concatenate([jnp.cos(ar), jnp.cos(ar), jnp.cos(ac), jnp.cos(ac)], axis=-1)
    sin64 = jnp.concatenate([-jnp.sin(ar), jnp.sin(ar), -jnp.sin(ac), jnp.sin(ac)], axis=-1)
    cos_t = jnp.concatenate([jnp.tile(cos64, (1, 2)), jnp.ones((Lc, LANES), F32)], axis=0)
    sin_t = jnp.concatenate([jnp.tile(sin64, (1, 2)), jnp.zeros((Lc, LANES), F32)], axis=0)
    return cos_t, sin_t


def _pad_lanes(v):
    v = v.reshape(1, -1).astype(F32)
    return jnp.pad(v, ((0, 0), (0, LANES - v.shape[1])))


def kernel(x, c, ctx, c_ctx, w_mod, b_mod, norm_mix_g, w_in, q_norm_g, k_norm_g, lam_q1, lam_k1, lam_q2, lam_k2, subln_g, conv_w, a_log, dt_bias, gdn_norm_g, w_out, norm_ffn_g, router_w, router_b, w_gate_up, b_gate_up, w_down, b_down):
    B, L, D = x.shape
    Lc = ctx.shape[1]
    S = L + Lc
    depth = w_mod.shape[0]
    tm = ROW_TILE
    n_lat_blocks = L // tm
    cos_t, sin_t = _rope_tables(L, Lc)

    c_rows = jnp.zeros((8, D), F32).at[:B].set(c).at[B].set(c_ctx)
    h = jnp.concatenate([x, ctx], axis=1)

    for layer in range(depth):
        mod = _adaln(c_rows, w_mod, b_mod, layer)
        mod6 = mod.reshape(8, 6, D)
        lat_mod = mod6[:B]
        ctx_mod = jnp.broadcast_to(mod6[B][None], (B, 6, D))
        modtab = jnp.pad(jnp.stack([lat_mod, ctx_mod], axis=1), ((0, 0), (0, 0), (0, 2), (0, 0)))

        lam_init = 0.8 - 0.6 * math.exp(-0.3 * layer)
        lam_full = (jnp.exp(jnp.sum(lam_q1[layer] * lam_k1[layer]))
                    - jnp.exp(jnp.sum(lam_q2[layer] * lam_k2[layer])) + lam_init).reshape(1).astype(F32)

        w_l = w_in[layer]
        w_main = w_l[:, :IN_MAIN].astype(BF16)
        w_ab = jnp.pad(w_l[:, IN_MAIN:], ((0, 0), (0, LANES - (w_l.shape[1] - IN_MAIN)))).astype(BF16)
        p_qk, v_att, p_gdn, z, ab = _inproj(h, modtab, norm_mix_g[layer].reshape(1, D), w_main, w_ab,
                                            n_lat_blocks)

        gq = jnp.tile(q_norm_g[layer].reshape(1, ATT_HEAD_DIM), (1, 2))
        gk = jnp.tile(k_norm_g[layer].reshape(1, ATT_HEAD_DIM), (1, 2))
        qz, k_att = _qkprep(p_qk, cos_t, sin_t, gq, gk)
        g_sub = (subln_g[layer] * (1.0 - lam_init)).reshape(1, LANES).astype(F32)
        o_att = _attention_lat(lam_full, qz, k_att, v_att, g_sub, L)
        o_att = _attention_ctx(lam_full, qz, k_att, v_att, g_sub, L, o_att)

        conv_w8 = jnp.pad(conv_w[layer], ((0, 8 - CONV_W), (0, 0)))
        gq_g, gk_g, gv_g, gates = _gdnprep(p_gdn, conv_w8, ab, _pad_lanes(a_log[layer]),
                                           _pad_lanes(dt_bias[layer]), n_lat_blocks)
        qg, w_g, kd, u_g, aq, eg = _gdnchunk(gq_g, gk_g, gv_g, gates)
        o_f, o_b = _gdnscan(qg, w_g, kd, u_g, aq, eg, L // GDN_CHUNK, Lc // GDN_CHUNK)

        rw = jnp.pad(router_w[layer], ((0, 0), (0, LANES - N_EXPERTS)))
        rb = jnp.pad(router_b[layer].reshape(1, N_EXPERTS).astype(F32), ((0, 0), (0, LANES - N_EXPERTS)),
                     constant_values=-1e30)
        h_new, v_ffn, top_e, top_g = _mixout(
            o_att, o_f, o_b, z, h, modtab, gdn_norm_g[layer].reshape(1, LANES), w_out[layer].astype(BF16),
            norm_ffn_g[layer].reshape(1, D), rw, rb, n_lat_blocks)

        h = _moe(v_ffn, top_e, top_g, h_new, modtab,
                 w_gate_up, b_gate_up.reshape(depth, N_EXPERTS, 1, 2 * D_EXPERT),
                 w_down, b_down.reshape(depth, N_EXPERTS, 1, D), layer, n_lat_blocks)
    return h[:, :L]
```

```python
import functools
import math

import jax
import jax.numpy as jnp
from jax import lax
from jax.experimental import pallas as pl
from jax.experimental.pallas import tpu as pltpu

F32 = jnp.float32
BF16 = jnp.bfloat16
HIGHEST = lax.Precision.HIGHEST

D_MODEL = 1024
GRID_W = 64
EPS = 1e-6
ATT_WIDTH = 512
ATT_HEAD_DIM = 64
ATT_HEADS = 4
ROPE_BASE = 10000.0
ROPE_PAIRS = ATT_HEAD_DIM // 4
GDN_WIDTH = 512
GDN_HEAD_DIM = 128
GDN_HEADS = 4
GDN_CHUNK = 64
CONV_W = 5
IN_MAIN = 3 * ATT_WIDTH + 4 * GDN_WIDTH
N_EXPERTS = 32
TOP_K = 4
D_EXPERT = 1024
SWIGLU_ALPHA = 1.702
SWIGLU_LIMIT = 7.0
MOE_BLOCK = 512

LANES = 128
ROW_TILE = 256
KV_CHUNK = 256
Q_SCALE = ATT_HEAD_DIM ** -0.5 * math.log2(math.e)
VMEM_LIMIT = 48 * 1024 * 1024
EXPERT_VMEM_LIMIT = 56 * 1024 * 1024


def _cparams(sem):
    return pltpu.CompilerParams(dimension_semantics=sem, vmem_limit_bytes=VMEM_LIMIT)


def _dot(a, b):
    return jnp.dot(a, b, preferred_element_type=F32)


def _dot_nt(a, b):
    return lax.dot_general(a, b, (((1,), (1,)), ((), ())), preferred_element_type=F32)


def _dot_tn(a, b):
    return lax.dot_general(a, b, (((0,), (0,)), ((), ())), preferred_element_type=F32)


def _dot_hi(a, b):
    return jnp.dot(a, b, preferred_element_type=F32, precision=HIGHEST)


def _sigmoid(x):
    return 1.0 / (1.0 + jnp.exp(-x))


def _adaln_kernel(c_ref, w_ref, b_ref, o_ref):
    c = c_ref[...]
    s = c * _sigmoid(c)
    o_ref[...] = _dot_hi(s, w_ref[0]) + b_ref[0]


def _adaln(c_rows, w, b, layer):
    depth, _, n = w.shape
    tn = 1024
    return pl.pallas_call(
        _adaln_kernel,
        out_shape=jax.ShapeDtypeStruct((8, n), F32),
        grid=(n // tn,),
        in_specs=[pl.BlockSpec((8, D_MODEL), lambda j: (0, 0)),
                  pl.BlockSpec((1, D_MODEL, tn), lambda j: (layer, 0, j)),
                  pl.BlockSpec((1, 1, tn), lambda j: (layer, 0, j))],
        out_specs=pl.BlockSpec((8, tn), lambda j: (0, j)),
        compiler_params=_cparams(("arbitrary",)),
        name="adaln",
    )(c_rows, w, b.reshape(depth, 1, n))


def _inproj_kernel(h_ref, mod_ref, g_ref, w_ref, wab_ref, qk_ref, v_ref, gdn_ref, z_ref, ab_ref):
    x = h_ref[0]
    ms = jnp.mean(x * x, axis=-1, keepdims=True)
    y = x * lax.rsqrt(ms + EPS) * g_ref[...]
    shift = mod_ref[0, 0, 0:1, :]
    scale = mod_ref[0, 0, 1:2, :]
    u = (y * (1.0 + scale) + shift).astype(BF16)
    qk_ref[0] = _dot(u, w_ref[:, 0:2 * ATT_WIDTH])
    vv = _dot(u, w_ref[:, 2 * ATT_WIDTH:3 * ATT_WIDTH])
    for hh in range(ATT_HEADS):
        v_ref[0, hh] = vv[:, hh * LANES:(hh + 1) * LANES].T.astype(BF16)
    off = 3 * ATT_WIDTH
    gdn_ref[0] = _dot(u, w_ref[:, off:off + 3 * GDN_WIDTH])
    z_ref[0] = _dot(u, w_ref[:, off + 3 * GDN_WIDTH:off + 4 * GDN_WIDTH])
    ab_ref[0] = _dot(u, wab_ref[...])


def _inproj(h, modtab, g, w_main, w_ab, n_lat_blocks):
    B, S, _ = h.shape
    tm = ROW_TILE
    row = lambda b, i: (b, i, 0)
    return pl.pallas_call(
        _inproj_kernel,
        out_shape=(jax.ShapeDtypeStruct((B, S, 2 * ATT_WIDTH), F32),
                   jax.ShapeDtypeStruct((B, ATT_HEADS, LANES, S), BF16),
                   jax.ShapeDtypeStruct((B, S, 3 * GDN_WIDTH), F32),
                   jax.ShapeDtypeStruct((B, S, GDN_WIDTH), F32),
                   jax.ShapeDtypeStruct((B, S, LANES), F32)),
        grid=(B, S // tm),
        in_specs=[pl.BlockSpec((1, tm, D_MODEL), row),
                  pl.BlockSpec((1, 1, 8, D_MODEL), lambda b, i: (b, (i >= n_lat_blocks).astype(jnp.int32), 0, 0)),
                  pl.BlockSpec((1, D_MODEL), lambda b, i: (0, 0)),
                  pl.BlockSpec((D_MODEL, IN_MAIN), lambda b, i: (0, 0)),
                  pl.BlockSpec((D_MODEL, LANES), lambda b, i: (0, 0))],
        out_specs=(pl.BlockSpec((1, tm, 2 * ATT_WIDTH), row),
                   pl.BlockSpec((1, ATT_HEADS, LANES, tm), lambda b, i: (b, 0, 0, i)),
                   pl.BlockSpec((1, tm, 3 * GDN_WIDTH), row),
                   pl.BlockSpec((1, tm, GDN_WIDTH), row),
                   pl.BlockSpec((1, tm, LANES), row)),
        compiler_params=_cparams(("parallel", "parallel")),
        name="inproj",
    )(h, modtab, g, w_main, w_ab)


def _qkprep_kernel(p_ref, cos_ref, sin_ref, gq_ref, gk_ref, qz_ref, k_ref):
    tm = p_ref.shape[1]
    lane = lax.broadcasted_iota(jnp.int32, (tm, LANES), 1)
    hi16 = (lane & 16) != 0
    first = lax.broadcasted_iota(jnp.int32, (LANES, tm), 0) < ATT_HEAD_DIM
    r = lax.broadcasted_iota(jnp.int32, (LANES, LANES), 0) >> 6
    c = lax.broadcasted_iota(jnp.int32, (LANES, LANES), 1) >> 6
    gmat = jnp.where(r == c, 1.0 / ATT_HEAD_DIM, 0.0).astype(F32)
    cosv = cos_ref[...]
    sinv = sin_ref[...]
    for j in range(2 * ATT_HEADS):
        x = p_ref[0, :, j * LANES:(j + 1) * LANES]
        ms = _dot_hi(x * x, gmat)
        g = gq_ref[...] if j < ATT_HEADS else gk_ref[...]
        y = x * lax.rsqrt(ms + EPS) * g
        sw = jnp.where(hi16, pltpu.roll(y, 16, 1), pltpu.roll(y, LANES - 16, 1))
        y = y * cosv + sw * sinv
        if j < ATT_HEADS:
            yt = (y * Q_SCALE).T
            qz_ref[0, j, 0] = jnp.where(first, yt, 0.0).astype(BF16)
            qz_ref[0, j, 1] = jnp.where(first, 0.0, yt).astype(BF16)
        else:
            k_ref[0, j - ATT_HEADS] = y.astype(BF16)


def _qkprep(p_qk, cos_t, sin_t, gq, gk):
    B, S, _ = p_qk.shape
    tm = ROW_TILE
    return pl.pallas_call(
        _qkprep_kernel,
        out_shape=(jax.ShapeDtypeStruct((B, ATT_HEADS, 2, LANES, S), BF16),
                   jax.ShapeDtypeStruct((B, ATT_HEADS, S, LANES), BF16)),
        grid=(B, S // tm),
        in_specs=[pl.BlockSpec((1, tm, 2 * ATT_WIDTH), lambda b, i: (b, i, 0)),
                  pl.BlockSpec((tm, LANES), lambda b, i: (i, 0)),
                  pl.BlockSpec((tm, LANES), lambda b, i: (i, 0)),
                  pl.BlockSpec((1, LANES), lambda b, i: (0, 0)),
                  pl.BlockSpec((1, LANES), lambda b, i: (0, 0))],
        out_specs=(pl.BlockSpec((1, ATT_HEADS, 2, LANES, tm), lambda b, i: (b, 0, 0, 0, i)),
                   pl.BlockSpec((1, ATT_HEADS, tm, LANES), lambda b, i: (b, 0, i, 0))),
        compiler_params=_cparams(("parallel", "parallel")),
        name="qkprep",
    )(p_qk, cos_t, sin_t, gq, gk)


def _attn_kernel(lam_ref, qz_ref, k_ref, v_ref, g_ref, *rest, aliased, tk, n_kv):
    if aliased:
        rest = rest[1:]
    o_ref, s_sc, mt_sc, m_sc, l_sc, acc_sc = rest
    m_sc[...] = jnp.full(m_sc.shape, -jnp.inf, F32)
    l_sc[...] = jnp.zeros(l_sc.shape, F32)
    acc_sc[...] = jnp.zeros(acc_sc.shape, F32)

    def rows(j):
        if isinstance(j, int):
            return pl.ds(j * tk, tk)
        return pl.ds(pl.multiple_of(j * tk, tk), tk)

    def qk_tile(j, slot, ps=(0, 1)):
        kt = k_ref[0, 0, rows(j), :]
        for p in ps:
            s = _dot(kt, qz_ref[0, 0, p])
            s_sc[slot, p] = s
            mt_sc[slot, p] = jnp.max(s, axis=0, keepdims=True)

    def pv_tile(j, slot, ps=(0, 1)):
        for p in ps:
            m_prev = m_sc[p]
            m_new = jnp.maximum(m_prev, mt_sc[slot, p])
            alpha = jnp.exp2(m_prev - m_new)
            lsum = None
            acc = None
            for c in range(tk // KV_CHUNK):
                cs = slice(c * KV_CHUNK, (c + 1) * KV_CHUNK)
                pe = jnp.exp2(s_sc[slot, p, cs, :] - m_new)
                ps = jnp.sum(pe, axis=0, keepdims=True)
                pv = _dot(v_ref[0, 0, j, :, cs], pe.astype(BF16))
                lsum = ps if lsum is None else lsum + ps
                acc = pv if acc is None else acc + pv
            l_sc[p] = alpha * l_sc[p] + lsum
            acc_sc[p] = alpha * acc_sc[p] + acc
            m_sc[p] = m_new

    def fused_tile(jq, slot_q, jp, slot_p):
        m_new = [jnp.maximum(m_sc[p], mt_sc[slot_p, p]) for p in range(2)]
        alpha = [jnp.exp2(m_sc[p] - m_new[p]) for p in range(2)]
        mx, lsum, acc = [None, None], [None, None], [None, None]
        for c in range(tk // KV_CHUNK):
            cs = slice(c * KV_CHUNK, (c + 1) * KV_CHUNK)
            if isinstance(jq, int):
                kr = pl.ds(jq * tk + c * KV_CHUNK, KV_CHUNK)
            else:
                kr = pl.ds(pl.multiple_of(jq * tk + c * KV_CHUNK, KV_CHUNK), KV_CHUNK)
            kc = k_ref[0, 0, kr, :]
            vc = v_ref[0, 0, jp, :, cs]
            for p in range(2):
                s = _dot(kc, qz_ref[0, 0, p])
                s_sc[slot_q, p, cs, :] = s
                cm = jnp.max(s, axis=0, keepdims=True)
                mx[p] = cm if mx[p] is None else jnp.maximum(mx[p], cm)
                pe = jnp.exp2(s_sc[slot_p, p, cs, :] - m_new[p])
                ps = jnp.sum(pe, axis=0, keepdims=True)
                pv = _dot(vc, pe.astype(BF16))
                lsum[p] = ps if lsum[p] is None else lsum[p] + ps
                acc[p] = pv if acc[p] is None else acc[p] + pv
        for p in range(2):
            mt_sc[slot_q, p] = mx[p]
            l_sc[p] = alpha[p] * l_sc[p] + lsum[p]
            acc_sc[p] = alpha[p] * acc_sc[p] + acc[p]
            m_sc[p] = m_new[p]

    qk_tile(0, 0)
    n_pairs = (n_kv - 1) // 2

    def pair(jj, carry):
        j = 2 * jj
        fused_tile(j + 1, 1, j, 0)
        fused_tile(j + 2, 0, j + 1, 1)
        return carry

    if n_pairs > 0:
        lax.fori_loop(0, n_pairs, pair, 0)
    j = 2 * n_pairs
    if j == n_kv - 1:
        pv_tile(j, 0)
    else:
        qk_tile(j + 1, 1)
        pv_tile(j, 0)
        pv_tile(j + 1, 1)

    o = acc_sc[0] / l_sc[0] - lam_ref[0] * (acc_sc[1] / l_sc[1])
    ms = jnp.mean(o * o, axis=0, keepdims=True)
    o_ref[0] = ((o * lax.rsqrt(ms + EPS)).T * g_ref[...]).astype(BF16)


def _attention(lam, qz, k, vt, g, *, tq, tk, q_blk0, n_q, kv_blk0, n_kv, prev_out=None):
    B, H, _, _, S = qz.shape
    aliased = prev_out is not None
    skv = tk * n_kv
    in_specs = [pl.BlockSpec(memory_space=pltpu.SMEM),
                pl.BlockSpec((1, 1, 2, LANES, tq), lambda b, h, i: (b, h, 0, 0, q_blk0 + i)),
                pl.BlockSpec((1, 1, skv, LANES), lambda b, h, i: (b, h, kv_blk0, 0)),
                pl.BlockSpec((1, 1, n_kv, LANES, tk), lambda b, h, i: (b, h, 0, 0, 0)),
                pl.BlockSpec((1, LANES), lambda b, h, i: (0, 0))]
    args = [lam, qz, k, vt, g]
    aliases = {}
    if aliased:
        in_specs.append(pl.BlockSpec(memory_space=pl.ANY))
        args.append(prev_out)
        aliases = {5: 0}
    return pl.pallas_call(
        functools.partial(_attn_kernel, aliased=aliased, tk=tk, n_kv=n_kv),
        out_shape=jax.ShapeDtypeStruct((B, S, ATT_WIDTH), BF16),
        grid=(B, H, n_q),
        in_specs=in_specs,
        out_specs=pl.BlockSpec((1, tq, LANES), lambda b, h, i: (b, q_blk0 + i, h)),
        scratch_shapes=[pltpu.VMEM((2, 2, tk, tq), F32), pltpu.VMEM((2, 2, 1, tq), F32),
                        pltpu.VMEM((2, 1, tq), F32), pltpu.VMEM((2, 1, tq), F32),
                        pltpu.VMEM((2, LANES, tq), F32)],
        input_output_aliases=aliases,
        compiler_params=_cparams(("parallel", "parallel", "arbitrary")),
        name="attn_ctx" if aliased else "attn_lat",
    )(*args)


def _attention_lat(lam, qz, k, vt, g, L):
    B, H, _, S = vt.shape
    tq = _pick_tile(L, (512, 256))
    tk = _pick_tile(S, (1280, 640, 256))
    n_kv = S // tk
    vt_tiles = vt.reshape(B, H, LANES, n_kv, tk).transpose(0, 1, 3, 2, 4)
    return _attention(lam, qz, k, vt_tiles, g, tq=tq, tk=tk, q_blk0=0, n_q=L // tq, kv_blk0=0, n_kv=n_kv)


def _attention_ctx(lam, qz, k, vt, g, L, prev_out):
    B, H, _, S = vt.shape
    Lc = S - L
    vt_ctx = vt[:, :, :, L:].reshape(B, H, 1, LANES, Lc)
    return _attention(lam, qz, k, vt_ctx, g, tq=Lc, tk=Lc, q_blk0=L // Lc, n_q=1, kv_blk0=L // Lc, n_kv=1,
                      prev_out=prev_out)


def _gdnprep_kernel(x_ref, prev_ref, next_ref, cw_ref, ab_ref, alog_ref, dtb_ref,
                    q_ref, k_ref, v_ref, gate_ref, ext_sc, *, nb_lat, nb_all):
    i = pl.program_id(1)
    tm = x_ref.shape[1]
    first = jnp.logical_or(i == 0, i == nb_lat)
    last = jnp.logical_or(i == nb_lat - 1, i == nb_all - 1)
    keep_prev = jnp.where(first, 0.0, 1.0)
    keep_next = jnp.where(last, 0.0, 1.0)
    ext_sc[0:8, :] = prev_ref[0] * keep_prev
    ext_sc[8:8 + tm, :] = x_ref[0]
    ext_sc[8 + tm:16 + tm, :] = next_ref[0] * keep_next
    acc = None
    for j in range(CONV_W):
        term = ext_sc[pl.ds(8 + j - CONV_W // 2, tm), :] * cw_ref[j:j + 1, :]
        acc = term if acc is None else acc + term
    y = acc * _sigmoid(acc)
    for part, ref in ((0, q_ref), (1, k_ref)):
        for hh in range(GDN_HEADS):
            lo = part * GDN_WIDTH + hh * GDN_HEAD_DIM
            t = y[:, lo:lo + GDN_HEAD_DIM]
            ref[0, :, hh * GDN_HEAD_DIM:(hh + 1) * GDN_HEAD_DIM] = (
                t * lax.rsqrt(jnp.sum(t * t, axis=-1, keepdims=True) + EPS))
    v_ref[0] = y[:, 2 * GDN_WIDTH:3 * GDN_WIDTH]
    ab = ab_ref[0]
    xs = ab + dtb_ref[...]
    sp = jnp.maximum(xs, 0.0) + jnp.log(1.0 + jnp.exp(-jnp.abs(xs)))
    g = -jnp.exp(alog_ref[...]) * sp
    lane = lax.broadcasted_iota(jnp.int32, ab.shape, 1)
    gate_ref[0] = jnp.where(lane < 2 * GDN_HEADS, g, _sigmoid(ab))


def _gdnprep(p_gdn, conv_w8, ab, alog_row, dtb_row, n_lat_blocks):
    B, S, W = p_gdn.shape
    tm = ROW_TILE
    nb = S // tm
    r8 = tm // 8
    row = lambda b, i: (b, i, 0)
    return pl.pallas_call(
        functools.partial(_gdnprep_kernel, nb_lat=n_lat_blocks, nb_all=nb),
        out_shape=(jax.ShapeDtypeStruct((B, S, GDN_WIDTH), F32),
                   jax.ShapeDtypeStruct((B, S, GDN_WIDTH), F32),
                   jax.ShapeDtypeStruct((B, S, GDN_WIDTH), F32),
                   jax.ShapeDtypeStruct((B, S, LANES), F32)),
        grid=(B, nb),
        in_specs=[pl.BlockSpec((1, tm, W), row),
                  pl.BlockSpec((1, 8, W), lambda b, i: (b, jnp.maximum(i * r8 - 1, 0), 0)),
                  pl.BlockSpec((1, 8, W), lambda b, i: (b, jnp.minimum((i + 1) * r8, nb * r8 - 1), 0)),
                  pl.BlockSpec((8, W), lambda b, i: (0, 0)),
                  pl.BlockSpec((1, tm, LANES), row),
                  pl.BlockSpec((1, LANES), lambda b, i: (0, 0)),
                  pl.BlockSpec((1, LANES), lambda b, i: (0, 0))],
        out_specs=(pl.BlockSpec((1, tm, GDN_WIDTH), row),
                   pl.BlockSpec((1, tm, GDN_WIDTH), row),
                   pl.BlockSpec((1, tm, GDN_WIDTH), row),
                   pl.BlockSpec((1, tm, LANES), row)),
        scratch_shapes=[pltpu.VMEM((tm + 16, W), F32)],
        compiler_params=_cparams(("parallel", "parallel")),
        name="gdnprep",
    )(p_gdn, p_gdn, p_gdn, conv_w8, ab, alog_row, dtb_row)


def _gdnchunk_kernel(q_ref, k_ref, v_ref, gate_ref, qg_ref, w_ref, kd_ref, u_ref, aq_ref, eg_ref):
    tm = q_ref.shape[1]
    nc = tm // GDN_CHUNK
    gates = gate_ref[0]
    ri = lax.broadcasted_iota(jnp.int32, (tm, tm), 0)
    ci = lax.broadcasted_iota(jnp.int32, (tm, tm), 1)
    same = (ri >> 6) == (ci >> 6)
    eye = ri == ci
    tot = _dot_hi(jnp.where(same, 1.0, 0.0).astype(F32), gates)
    eye_f = jnp.where(eye, 1.0, 0.0).astype(F32)

    heads = []
    for hh in range(GDN_HEADS):
        sl = slice(hh * GDN_HEAD_DIM, (hh + 1) * GDN_HEAD_DIM)
        k = k_ref[0, :, sl]
        k16 = k.astype(BF16)
        q = q_ref[0, :, sl] * (GDN_HEAD_DIM ** -0.5)
        heads.append((sl, k, q, _dot_nt(k16, k16), _dot_nt(q.astype(BF16), k16)))

    chains = []
    for d in range(2):
        incl = jnp.logical_and(same, (ci <= ri) if d == 0 else (ci >= ri))
        strict = jnp.logical_and(incl, jnp.logical_not(eye))
        gc = _dot_hi(jnp.where(incl, 1.0, 0.0).astype(F32), gates)
        gc_t = gc.T
        for hh in range(GDN_HEADS):
            ln = d * GDN_HEADS + hh
            gcol = gc[:, ln:ln + 1]
            bcol = gates[:, 2 * GDN_HEADS + ln:2 * GDN_HEADS + ln + 1]
            tcol = tot[:, ln:ln + 1]
            diff = gcol - gc_t[ln:ln + 1, :]
            decay = jnp.where(incl, jnp.exp(jnp.where(incl, diff, 0.0)), 0.0)
            a = jnp.where(strict, heads[hh][3] * bcol * decay, 0.0)
            aqk = heads[hh][4] * decay
            chains.append(dict(d=d, hh=hh, x=-a, t=eye_f - a, aqk=aqk, gcol=gcol, bcol=bcol, tcol=tcol))

    for _ in range(5):
        for ch in chains:
            x16 = ch["x"].astype(BF16)
            ch["x"] = _dot(x16, x16)
        for ch in chains:
            ch["t"] = ch["t"] + _dot(ch["t"].astype(BF16), ch["x"].astype(BF16))

    for ch in chains:
        d, hh, gcol, bcol, tcol = ch["d"], ch["hh"], ch["gcol"], ch["bcol"], ch["tcol"]
        sl, k, q = heads[hh][0], heads[hh][1], heads[hh][2]
        t16 = ch["t"].astype(BF16)
        egc = jnp.exp(gcol)
        kb = k * bcol
        u_ref[0, d, :, sl] = _dot(t16, (v_ref[0, :, sl] * bcol).astype(BF16))
        w_ref[0, d, :, sl] = _dot(t16, (kb * egc).astype(BF16)).astype(BF16)
        kd_ref[0, d, :, sl] = (k * jnp.exp(tcol - gcol)).astype(BF16)
        qg_ref[0, d, :, sl] = (q * egc).astype(BF16)
        for cc in range(nc):
            rs = slice(cc * GDN_CHUNK, (cc + 1) * GDN_CHUNK)
            aq_ref[0, d, rs, hh * GDN_CHUNK:(hh + 1) * GDN_CHUNK] = ch["aqk"][rs, rs].astype(BF16)
            eg_ref[0, d, cc, hh:hh + 1, :] = jnp.broadcast_to(
                jnp.exp(tcol[cc * GDN_CHUNK:cc * GDN_CHUNK + 1, :]), (1, LANES))


def _gdnchunk(gq, gk, gv, gates):
    B, S, W = gq.shape
    tm = ROW_TILE
    nc = tm // GDN_CHUNK
    row = lambda b, i: (b, i, 0)
    drow = lambda b, i: (b, 0, i, 0)
    big = lambda dt: jax.ShapeDtypeStruct((B, 2, S, W), dt)
    return pl.pallas_call(
        _gdnchunk_kernel,
        out_shape=(big(BF16), big(BF16), big(BF16), big(F32),
                   jax.ShapeDtypeStruct((B, 2, S, GDN_HEADS * GDN_CHUNK), BF16),
                   jax.ShapeDtypeStruct((B, 2, S // GDN_CHUNK, GDN_HEADS, LANES), F32)),
        grid=(B, S // tm),
        in_specs=[pl.BlockSpec((1, tm, W), row), pl.BlockSpec((1, tm, W), row),
                  pl.BlockSpec((1, tm, W), row), pl.BlockSpec((1, tm, LANES), row)],
        out_specs=(pl.BlockSpec((1, 2, tm, W), drow), pl.BlockSpec((1, 2, tm, W), drow),
                   pl.BlockSpec((1, 2, tm, W), drow), pl.BlockSpec((1, 2, tm, W), drow),
                   pl.BlockSpec((1, 2, tm, GDN_HEADS * GDN_CHUNK), drow),
                   pl.BlockSpec((1, 2, nc, GDN_HEADS, LANES), lambda b, i: (b, 0, i, 0, 0))),
        compiler_params=_cparams(("parallel", "parallel")),
        name="gdnchunk",
    )(gq, gk, gv, gates)


def _gdnscan_kernel(qg0, w0, kd0, u0, aq0, eg0, qg1, w1, kd1, u1, aq1, eg1, of_ref, ob_ref, s_sc):
    @pl.when(pl.program_id(1) == 0)
    def _():
        s_sc[...] = jnp.zeros(s_sc.shape, F32)

    dirs = ((qg0, w0, kd0, u0, aq0, eg0, of_ref), (qg1, w1, kd1, u1, aq1, eg1, ob_ref))
    chains = [(d, hh) for d in range(2) for hh in range(GDN_HEADS)]
    st, st16, vn16, qs = {}, {}, {}, {}
    for d, hh in chains:
        st[d, hh] = s_sc[d, hh]
        st16[d, hh] = st[d, hh].astype(BF16)
    for d, hh in chains:
        qg, w, kd, u, aq, eg, o_ref = dirs[d]
        sl = slice(hh * GDN_HEAD_DIM, (hh + 1) * GDN_HEAD_DIM)
        vn16[d, hh] = (u[0, 0, :, sl] - _dot(w[0, 0, :, sl], st16[d, hh])).astype(BF16)
        qs[d, hh] = _dot(qg[0, 0, :, sl], st16[d, hh])
    for d, hh in chains:
        qg, w, kd, u, aq, eg, o_ref = dirs[d]
        sl = slice(hh * GDN_HEAD_DIM, (hh + 1) * GDN_HEAD_DIM)
        o_ref[0, :, sl] = qs[d, hh] + _dot(aq[0, 0, :, hh * GDN_CHUNK:(hh + 1) * GDN_CHUNK], vn16[d, hh])
        s_sc[d, hh] = st[d, hh] * eg[0, 0, 0, hh:hh + 1, :] + _dot_tn(kd[0, 0, :, sl], vn16[d, hh])


def _gdnscan(qg, w, kd, u, aq, eg, n_lat_chunks, n_ctx_chunks):
    B, _, S, W = qg.shape
    C = GDN_CHUNK
    n = S // C

    def fwd_chunk(i):
        return jnp.where(i < n_ctx_chunks, n_lat_chunks + i, i - n_ctx_chunks)

    def bwd_chunk(i):
        return jnp.where(i < n_ctx_chunks, n_lat_chunks + n_ctx_chunks - 1 - i, n - 1 - i)

    def specs(d, chunk_of):
        big = pl.BlockSpec((1, 1, C, W), lambda b, i: (b, d, chunk_of(i), 0))
        return [big, big, big, big,
                pl.BlockSpec((1, 1, C, GDN_HEADS * C), lambda b, i: (b, d, chunk_of(i), 0)),
                pl.BlockSpec((1, 1, 1, GDN_HEADS, LANES), lambda b, i: (b, d, chunk_of(i), 0, 0))]

    return pl.pallas_call(
        _gdnscan_kernel,
        out_shape=(jax.ShapeDtypeStruct((B, S, W), F32), jax.ShapeDtypeStruct((B, S, W), F32)),
        grid=(B, n),
        in_specs=specs(0, fwd_chunk) + specs(1, bwd_chunk),
        out_specs=(pl.BlockSpec((1, C, W), lambda b, i: (b, fwd_chunk(i), 0)),
                   pl.BlockSpec((1, C, W), lambda b, i: (b, bwd_chunk(i), 0))),
        scratch_shapes=[pltpu.VMEM((2, GDN_HEADS, GDN_HEAD_DIM, GDN_HEAD_DIM), F32)],
        compiler_params=_cparams(("parallel", "arbitrary")),
        name="gdnscan",
    )(qg, w, kd, u, aq, eg, qg, w, kd, u, aq, eg)


def _mixout_kernel(oa_ref, of_ref, ob_ref, z_ref, h_ref, mod_ref, gg_ref, wo_ref, gf_ref, rw_ref, rb_ref,
                   hn_ref, v_ref, te_ref, tg_ref):
    og = of_ref[0] + ob_ref[0]
    z = z_ref[0]
    parts = [oa_ref[0]]
    for hh in range(GDN_HEADS):
        sl = slice(hh * GDN_HEAD_DIM, (hh + 1) * GDN_HEAD_DIM)
        t = og[:, sl]
        t = t * lax.rsqrt(jnp.mean(t * t, axis=-1, keepdims=True) + EPS) * gg_ref[...]
        zz = z[:, sl]
        parts.append((t * (zz * _sigmoid(zz))).astype(BF16))
    mix_in = jnp.concatenate(parts, axis=-1)
    mix = _dot(mix_in, wo_ref[...])
    hn = h_ref[0] + mod_ref[0, 0, 2:3, :] * mix
    hn_ref[0] = hn
    y = hn * lax.rsqrt(jnp.mean(hn * hn, axis=-1, keepdims=True) + EPS) * gf_ref[...]
    v = y * (1.0 + mod_ref[0, 0, 4:5, :]) + mod_ref[0, 0, 3:4, :]
    v_ref[0] = v.astype(BF16)
    logits = _dot_hi(v, rw_ref[...]) + rb_ref[...]
    lane = lax.broadcasted_iota(jnp.int32, logits.shape, 1)
    cur = logits
    vals, idxs = [], []
    for _ in range(TOP_K):
        m = jnp.max(cur, axis=-1, keepdims=True)
        idx = jnp.min(jnp.where(cur == m, lane, LANES), axis=-1, keepdims=True)
        vals.append(m)
        idxs.append(idx)
        cur = jnp.where(lane == idx, -jnp.inf, cur)
    es = [jnp.exp(vv - vals[0]) for vv in vals]
    inv = 1.0 / (es[0] + es[1] + es[2] + es[3])
    te = jnp.zeros(logits.shape, jnp.int32)
    tg = jnp.zeros(logits.shape, F32)
    for kk in range(TOP_K):
        te = jnp.where(lane == kk, idxs[kk], te)
        tg = jnp.where(lane == kk, es[kk] * inv, tg)
    te_ref[0] = te
    tg_ref[0] = tg


def _mixout(o_att, o_f, o_b, z, h, modtab, gg, w_out, gf, rw, rb, n_lat_blocks):
    B, S, _ = h.shape
    tm = ROW_TILE
    row = lambda b, i: (b, i, 0)
    const = lambda b, i: (0, 0)
    return pl.pallas_call(
        _mixout_kernel,
        out_shape=(jax.ShapeDtypeStruct((B, S, D_MODEL), F32),
                   jax.ShapeDtypeStruct((B, S, D_MODEL), BF16),
                   jax.ShapeDtypeStruct((B, S, LANES), jnp.int32),
                   jax.ShapeDtypeStruct((B, S, LANES), F32)),
        grid=(B, S // tm),
        in_specs=[pl.BlockSpec((1, tm, ATT_WIDTH), row),
                  pl.BlockSpec((1, tm, GDN_WIDTH), row),
                  pl.BlockSpec((1, tm, GDN_WIDTH), row),
                  pl.BlockSpec((1, tm, GDN_WIDTH), row),
                  pl.BlockSpec((1, tm, D_MODEL), row),
                  pl.BlockSpec((1, 1, 8, D_MODEL), lambda b, i: (b, (i >= n_lat_blocks).astype(jnp.int32), 0, 0)),
                  pl.BlockSpec((1, LANES), const),
                  pl.BlockSpec((D_MODEL, D_MODEL), const),
                  pl.BlockSpec((1, D_MODEL), const),
                  pl.BlockSpec((D_MODEL, LANES), const),
                  pl.BlockSpec((1, LANES), const)],
        out_specs=(pl.BlockSpec((1, tm, D_MODEL), row),
                   pl.BlockSpec((1, tm, D_MODEL), row),
                   pl.BlockSpec((1, tm, LANES), row),
                   pl.BlockSpec((1, tm, LANES), row)),
        compiler_params=_cparams(("parallel", "parallel")),
        name="mixout",
    )(o_att, o_f, o_b, z, h, modtab, gg, w_out, gf, rw, rb)


def _expert_kernel(be_ref, nv_ref, x_ref, wgu_ref, bgu_ref, wd_ref, bd_ref, sg_ref, y_ref, wgu_sc, wd_sc):
    i = pl.program_id(0)
    new_expert = jnp.logical_or(i == 0, be_ref[i] != be_ref[jnp.maximum(i - 1, 0)])

    @pl.when(new_expert)
    def _():
        wgu_sc[...] = wgu_ref[0, 0].astype(BF16)
        wd_sc[...] = wd_ref[0, 0].astype(BF16)

    @pl.when(nv_ref[i] > 0)
    def _():
        gu = _dot(x_ref[...], wgu_sc[...]) + bgu_ref[0, 0]
        g_ = jnp.minimum(gu[:, :D_EXPERT], SWIGLU_LIMIT)
        up = jnp.clip(gu[:, D_EXPERT:], -SWIGLU_LIMIT, SWIGLU_LIMIT)
        glu = g_ * _sigmoid(SWIGLU_ALPHA * g_)
        act = ((up + 1.0) * glu).astype(BF16)
        y_ref[...] = ((_dot(act, wd_sc[...]) + bd_ref[0, 0]) * sg_ref[...]).astype(y_ref.dtype)

    @pl.when(nv_ref[i] == 0)
    def _():
        y_ref[...] = jnp.zeros(y_ref.shape, y_ref.dtype)


def _experts(block_e, n_valid, x_sorted, wgu, bgu, wd, bd, slot_gate, layer):
    n_slots = x_sorted.shape[0]
    nb = n_slots // MOE_BLOCK
    grid_spec = pltpu.PrefetchScalarGridSpec(
        num_scalar_prefetch=2,
        grid=(nb,),
        in_specs=[pl.BlockSpec((MOE_BLOCK, D_MODEL), lambda i, be, nv: (i, 0)),
                  pl.BlockSpec((1, 1, D_MODEL, 2 * D_EXPERT), lambda i, be, nv: (layer, be[i], 0, 0)),
                  pl.BlockSpec((1, 1, 1, 2 * D_EXPERT), lambda i, be, nv: (layer, be[i], 0, 0)),
                  pl.BlockSpec((1, 1, D_EXPERT, D_MODEL), lambda i, be, nv: (layer, be[i], 0, 0)),
                  pl.BlockSpec((1, 1, 1, D_MODEL), lambda i, be, nv: (layer, be[i], 0, 0)),
                  pl.BlockSpec((MOE_BLOCK, 1), lambda i, be, nv: (i, 0))],
        out_specs=pl.BlockSpec((MOE_BLOCK, D_MODEL), lambda i, be, nv: (i, 0)),
        scratch_shapes=[pltpu.VMEM((D_MODEL, 2 * D_EXPERT), BF16), pltpu.VMEM((D_EXPERT, D_MODEL), BF16)],
    )
    return pl.pallas_call(
        _expert_kernel,
        out_shape=jax.ShapeDtypeStruct((n_slots, D_MODEL), BF16),
        grid_spec=grid_spec,
        compiler_params=pltpu.CompilerParams(dimension_semantics=("arbitrary",),
                                             vmem_limit_bytes=EXPERT_VMEM_LIMIT),
        name="experts",
    )(block_e, n_valid, x_sorted, wgu, bgu, wd, bd, slot_gate)


def _moe_plan(top_e, top_g):
    T = top_e.shape[0]
    n_assign = T * TOP_K
    n_blocks = -(-n_assign // MOE_BLOCK) + N_EXPERTS
    n_slots = n_blocks * MOE_BLOCK
    i32 = jnp.int32
    flat_e = top_e.reshape(-1)
    gate_flat = top_g.reshape(-1)
    order = jnp.argsort(flat_e).astype(i32)
    rank = jnp.argsort(order).astype(i32)
    e_ids = jnp.arange(N_EXPERTS, dtype=i32)
    is_e = flat_e[:, None] == e_ids[None, :]
    counts = jnp.sum(is_e, axis=0, dtype=i32)
    start = jnp.cumsum(counts) - counts
    padded = (counts + MOE_BLOCK - 1) // MOE_BLOCK * MOE_BLOCK
    pad_end = jnp.cumsum(padded)
    pad_start = pad_end - padded
    blk0 = jnp.arange(n_blocks, dtype=i32) * MOE_BLOCK
    block_e = jnp.minimum(jnp.sum(pad_end[None, :] <= blk0[:, None], axis=1, dtype=i32), N_EXPERTS - 1)
    off = (blk0 - pad_start[block_e])[:, None] + jnp.arange(MOE_BLOCK, dtype=i32)[None, :]
    valid = off < counts[block_e][:, None]
    a_slot = order[jnp.clip(off + start[block_e][:, None], 0, n_assign - 1).reshape(-1)]
    valid_flat = valid.reshape(-1)
    slot_tok = jnp.where(valid_flat, a_slot // TOP_K, 0)
    slot_gate = jnp.where(valid_flat, gate_flat[a_slot], 0.0)
    n_valid = jnp.sum(valid, axis=1, dtype=i32)
    shift = jnp.sum(jnp.where(is_e, (pad_start - start)[None, :], 0), axis=1, dtype=i32)
    slot_of = rank + shift
    return slot_tok, slot_gate.reshape(n_slots, 1), block_e, n_valid, slot_of


def _combine_kernel(y_ref, h_ref, mod_ref, o_ref):
    y = (y_ref[0, 0].astype(F32) + y_ref[1, 0].astype(F32)) + (y_ref[2, 0].astype(F32) + y_ref[3, 0].astype(F32))
    o_ref[0] = h_ref[0] + mod_ref[0, 0, 5:6, :] * y


def _combine(y4, h, modtab, b, n_lat_blocks):
    B, S, _ = h.shape
    tm = ROW_TILE
    return pl.pallas_call(
        _combine_kernel,
        out_shape=jax.ShapeDtypeStruct((B, S, D_MODEL), F32),
        grid=(1, S // tm),
        in_specs=[pl.BlockSpec((TOP_K, 1, tm, D_MODEL), lambda _, i: (0, 0, i, 0)),
                  pl.BlockSpec((1, tm, D_MODEL), lambda _, i: (b, i, 0)),
                  pl.BlockSpec((1, 1, 8, D_MODEL), lambda _, i: (b, (i >= n_lat_blocks).astype(jnp.int32), 0, 0))],
        out_specs=pl.BlockSpec((1, tm, D_MODEL), lambda _, i: (b, i, 0)),
        input_output_aliases={1: 0},
        compiler_params=_cparams(("parallel", "parallel")),
        name="combine",
    )(y4, h, modtab)


def _moe(v_ffn, top_e, top_g, h_new, modtab, wgu, bgu, wd, bd, layer, n_lat_blocks):
    B, S, _ = v_ffn.shape
    h = h_new
    for b in range(B):
        slot_tok, slot_gate, block_e, n_valid, slot_of = _moe_plan(top_e[b, :, :TOP_K], top_g[b, :, :TOP_K])
        x_sorted = v_ffn[b][slot_tok]
        y_sorted = _experts(block_e, n_valid, x_sorted, wgu, bgu, wd, bd, slot_gate, layer)
        y4 = y_sorted[slot_of.reshape(S, TOP_K).T.reshape(-1)].reshape(TOP_K, 1, S, D_MODEL)
        h = _combine(y4, h, modtab, b, n_lat_blocks)
    return h


def _pick_tile(n, cands):
    for t in cands:
        if n % t == 0:
            return t
    raise ValueError(f"no tile for {n}")


def _rope_tables(L, Lc):
    rows = L // GRID_W
    row = jnp.repeat(jnp.arange(rows, dtype=F32), GRID_W)
    col = (jnp.arange(L, dtype=jnp.int32) % GRID_W).astype(F32)
    inv_freq = ROPE_BASE ** (-jnp.arange(ROPE_PAIRS, dtype=F32) / ROPE_PAIRS)
    ar = row[:, None] * inv_freq
    ac = col[:, None] * inv_freq
    cos64 = jnp.concatenate([jnp.cos(ar), jnp.cos(ar), jnp.cos(ac), jnp.cos(ac)], axis=-1)
    sin64 = jnp.concatenate([-jnp.sin(ar), jnp.sin(ar), -jnp.sin(ac), jnp.sin(ac)], axis=-1)
    cos_t = jnp.concatenate([jnp.tile(cos64, (1, 2)), jnp.ones((Lc, LANES), F32)], axis=0)
    sin_t = jnp.concatenate([jnp.tile(sin64, (1, 2)), jnp.zeros((Lc, LANES), F32)], axis=0)
    return cos_t, sin_t


def _pad_lanes(v):
    v = v.reshape(1, -1).astype(F32)
    return jnp.pad(v, ((0, 0), (0, LANES - v.shape[1])))


def kernel(x, c, ctx, c_ctx, w_mod, b_mod, norm_mix_g, w_in, q_norm_g, k_norm_g, lam_q1, lam_k1, lam_q2, lam_k2, subln_g, conv_w, a_log, dt_bias, gdn_norm_g, w_out, norm_ffn_g, router_w, router_b, w_gate_up, b_gate_up, w_down, b_down):
    B, L, D = x.shape
    Lc = ctx.shape[1]
    S = L + Lc
    depth = w_mod.shape[0]
    tm = ROW_TILE
    n_lat_blocks = L // tm
    cos_t, sin_t = _rope_tables(L, Lc)

    c_rows = jnp.zeros((8, D), F32).at[:B].set(c).at[B].set(c_ctx)
    h = jnp.concatenate([x, ctx], axis=1)

    for layer in range(depth):
        mod = _adaln(c_rows, w_mod, b_mod, layer)
        mod6 = mod.reshape(8, 6, D)
        lat_mod = mod6[:B]
        ctx_mod = jnp.broadcast_to(mod6[B][None], (B, 6, D))
        modtab = jnp.pad(jnp.stack([lat_mod, ctx_mod], axis=1), ((0, 0), (0, 0), (0, 2), (0, 0)))

        lam_init = 0.8 - 0.6 * math.exp(-0.3 * layer)
        lam_full = (jnp.exp(jnp.sum(lam_q1[layer] * lam_k1[layer]))
                    - jnp.exp(jnp.sum(lam_q2[layer] * lam_k2[layer])) + lam_init).reshape(1).astype(F32)

        w_l = w_in[layer]
        w_main = w_l[:, :IN_MAIN].astype(BF16)
        w_ab = jnp.pad(w_l[:, IN_MAIN:], ((0, 0), (0, LANES - (w_l.shape[1] - IN_MAIN)))).astype(BF16)
        p_qk, v_att, p_gdn, z, ab = _inproj(h, modtab, norm_mix_g[layer].reshape(1, D), w_main, w_ab,
                                            n_lat_blocks)

        gq = jnp.tile(q_norm_g[layer].reshape(1, ATT_HEAD_DIM), (1, 2))
        gk = jnp.tile(k_norm_g[layer].reshape(1, ATT_HEAD_DIM), (1, 2))
        qz, k_att = _qkprep(p_qk, cos_t, sin_t, gq, gk)
        g_sub = (subln_g[layer] * (1.0 - lam_init)).reshape(1, LANES).astype(F32)
        o_att = _attention_lat(lam_full, qz, k_att, v_att, g_sub, L)
        o_att = _attention_ctx(lam_full, qz, k_att, v_att, g_sub, L, o_att)

        conv_w8 = jnp.pad(conv_w[layer], ((0, 8 - CONV_W), (0, 0)))
        gq_g, gk_g, gv_g, gates = _gdnprep(p_gdn, conv_w8, ab, _pad_lanes(a_log[layer]),
                                           _pad_lanes(dt_bias[layer]), n_lat_blocks)
        qg, w_g, kd, u_g, aq, eg = _gdnchunk(gq_g, gk_g, gv_g, gates)
        o_f, o_b = _gdnscan(qg, w_g, kd, u_g, aq, eg, L // GDN_CHUNK, Lc // GDN_CHUNK)

        rw = jnp.pad(router_w[layer], ((0, 0), (0, LANES - N_EXPERTS)))
        rb = jnp.pad(router_b[layer].reshape(1, N_EXPERTS).astype(F32), ((0, 0), (0, LANES - N_EXPERTS)),
                     constant_values=-1e30)
        h_new, v_ffn, top_e, top_g = _mixout(
            o_att, o_f, o_b, z, h, modtab, gdn_norm_g[layer].reshape(1, LANES), w_out[layer].astype(BF16),
            norm_ffn_g[layer].reshape(1, D), rw, rb, n_lat_blocks)

        h = _moe(v_ffn, top_e, top_g, h_new, modtab,
                 w_gate_up, b_gate_up.reshape(depth, N_EXPERTS, 1, 2 * D_EXPERT),
                 w_down, b_down.reshape(depth, N_EXPERTS, 1, D), layer, n_lat_blocks)
    return h[:, :L]
```

```python
import functools
import math

import jax
import jax.numpy as jnp
from jax import lax
from jax.experimental import pallas as pl
from jax.experimental.pallas import tpu as pltpu

F32 = jnp.float32
BF16 = jnp.bfloat16
HIGHEST = lax.Precision.HIGHEST

D_MODEL = 1024
GRID_W = 64
EPS = 1e-6
ATT_WIDTH = 512
ATT_HEAD_DIM = 64
ATT_HEADS = 4
ROPE_BASE = 10000.0
ROPE_PAIRS = ATT_HEAD_DIM // 4
GDN_WIDTH = 512
GDN_HEAD_DIM = 128
GDN_HEADS = 4
GDN_CHUNK = 64
CONV_W = 5
IN_MAIN = 3 * ATT_WIDTH + 4 * GDN_WIDTH
N_EXPERTS = 32
TOP_K = 4
D_EXPERT = 1024
SWIGLU_ALPHA = 1.702
SWIGLU_LIMIT = 7.0
MOE_BLOCK = 512
COMBINE_TILE = 256
COMBINE_CHUNK = 16
COMBINE_MAX_CHUNKS = COMBINE_TILE * TOP_K // COMBINE_CHUNK + 2 * N_EXPERTS

LANES = 128
ROW_TILE = 256
KV_CHUNK = 256
Q_SCALE = ATT_HEAD_DIM ** -0.5 * math.log2(math.e)
VMEM_LIMIT = 48 * 1024 * 1024
EXPERT_VMEM_LIMIT = 56 * 1024 * 1024


def _cparams(sem):
    return pltpu.CompilerParams(dimension_semantics=sem, vmem_limit_bytes=VMEM_LIMIT)


def _dot(a, b):
    return jnp.dot(a, b, preferred_element_type=F32)


def _dot_nt(a, b):
    return lax.dot_general(a, b, (((1,), (1,)), ((), ())), preferred_element_type=F32)


def _dot_tn(a, b):
    return lax.dot_general(a, b, (((0,), (0,)), ((), ())), preferred_element_type=F32)


def _dot_hi(a, b):
    return jnp.dot(a, b, preferred_element_type=F32, precision=HIGHEST)


def _sigmoid(x):
    return 1.0 / (1.0 + jnp.exp(-x))


def _adaln_kernel(c_ref, w_ref, b_ref, o_ref):
    c = c_ref[...]
    s = c * _sigmoid(c)
    o_ref[...] = _dot_hi(s, w_ref[0]) + b_ref[0]


def _adaln(c_rows, w, b, layer):
    depth, _, n = w.shape
    tn = 1024
    return pl.pallas_call(
        _adaln_kernel,
        out_shape=jax.ShapeDtypeStruct((8, n), F32),
        grid=(n // tn,),
        in_specs=[pl.BlockSpec((8, D_MODEL), lambda j: (0, 0)),
                  pl.BlockSpec((1, D_MODEL, tn), lambda j: (layer, 0, j)),
                  pl.BlockSpec((1, 1, tn), lambda j: (layer, 0, j))],
        out_specs=pl.BlockSpec((8, tn), lambda j: (0, j)),
        compiler_params=_cparams(("arbitrary",)),
        name="adaln",
    )(c_rows, w, b.reshape(depth, 1, n))


def _inproj_kernel(h_ref, mod_ref, g_ref, w_ref, wab_ref, qk_ref, v_ref, gdn_ref, z_ref, ab_ref):
    x = h_ref[0]
    ms = jnp.mean(x * x, axis=-1, keepdims=True)
    y = x * lax.rsqrt(ms + EPS) * g_ref[...]
    shift = mod_ref[0, 0, 0:1, :]
    scale = mod_ref[0, 0, 1:2, :]
    u = (y * (1.0 + scale) + shift).astype(BF16)
    qk_ref[0] = _dot(u, w_ref[:, 0:2 * ATT_WIDTH])
    vv = _dot(u, w_ref[:, 2 * ATT_WIDTH:3 * ATT_WIDTH])
    for hh in range(ATT_HEADS):
        v_ref[0, hh] = vv[:, hh * LANES:(hh + 1) * LANES].T.astype(BF16)
    off = 3 * ATT_WIDTH
    gdn_ref[0] = _dot(u, w_ref[:, off:off + 3 * GDN_WIDTH])
    z_ref[0] = _dot(u, w_ref[:, off + 3 * GDN_WIDTH:off + 4 * GDN_WIDTH])
    ab_ref[0] = _dot(u, wab_ref[...])


def _inproj(h, modtab, g, w_main, w_ab, n_lat_blocks):
    B, S, _ = h.shape
    tm = ROW_TILE
    row = lambda b, i: (b, i, 0)
    return pl.pallas_call(
        _inproj_kernel,
        out_shape=(jax.ShapeDtypeStruct((B, S, 2 * ATT_WIDTH), F32),
                   jax.ShapeDtypeStruct((B, ATT_HEADS, LANES, S), BF16),
                   jax.ShapeDtypeStruct((B, S, 3 * GDN_WIDTH), F32),
                   jax.ShapeDtypeStruct((B, S, GDN_WIDTH), F32),
                   jax.ShapeDtypeStruct((B, S, LANES), F32)),
        grid=(B, S // tm),
        in_specs=[pl.BlockSpec((1, tm, D_MODEL), row),
                  pl.BlockSpec((1, 1, 8, D_MODEL), lambda b, i: (b, (i >= n_lat_blocks).astype(jnp.int32), 0, 0)),
                  pl.BlockSpec((1, D_MODEL), lambda b, i: (0, 0)),
                  pl.BlockSpec((D_MODEL, IN_MAIN), lambda b, i: (0, 0)),
                  pl.BlockSpec((D_MODEL, LANES), lambda b, i: (0, 0))],
        out_specs=(pl.BlockSpec((1, tm, 2 * ATT_WIDTH), row),
                   pl.BlockSpec((1, ATT_HEADS, LANES, tm), lambda b, i: (b, 0, 0, i)),
                   pl.BlockSpec((1, tm, 3 * GDN_WIDTH), row),
                   pl.BlockSpec((1, tm, GDN_WIDTH), row),
                   pl.BlockSpec((1, tm, LANES), row)),
        compiler_params=_cparams(("parallel", "parallel")),
        name="inproj",
    )(h, modtab, g, w_main, w_ab)


def _qkprep_kernel(p_ref, cos_ref, sin_ref, gq_ref, gk_ref, qz_ref, k_ref):
    tm = p_ref.shape[1]
    lane = lax.broadcasted_iota(jnp.int32, (tm, LANES), 1)
    hi16 = (lane & 16) != 0
    first = lax.broadcasted_iota(jnp.int32, (LANES, tm), 0) < ATT_HEAD_DIM
    r = lax.broadcasted_iota(jnp.int32, (LANES, LANES), 0) >> 6
    c = lax.broadcasted_iota(jnp.int32, (LANES, LANES), 1) >> 6
    gmat = jnp.where(r == c, 1.0 / ATT_HEAD_DIM, 0.0).astype(F32)
    cosv = cos_ref[...]
    sinv = sin_ref[...]
    for j in range(2 * ATT_HEADS):
        x = p_ref[0, :, j * LANES:(j + 1) * LANES]
        ms = _dot_hi(x * x, gmat)
        g = gq_ref[...] if j < ATT_HEADS else gk_ref[...]
        y = x * lax.rsqrt(ms + EPS) * g
        sw = jnp.where(hi16, pltpu.roll(y, 16, 1), pltpu.roll(y, LANES - 16, 1))
        y = y * cosv + sw * sinv
        if j < ATT_HEADS:
            yt = (y * Q_SCALE).T
            qz_ref[0, j, 0] = jnp.where(first, yt, 0.0).astype(BF16)
            qz_ref[0, j, 1] = jnp.where(first, 0.0, yt).astype(BF16)
        else:
            k_ref[0, j - ATT_HEADS] = y.astype(BF16)


def _qkprep(p_qk, cos_t, sin_t, gq, gk):
    B, S, _ = p_qk.shape
    tm = ROW_TILE
    return pl.pallas_call(
        _qkprep_kernel,
        out_shape=(jax.ShapeDtypeStruct((B, ATT_HEADS, 2, LANES, S), BF16),
                   jax.ShapeDtypeStruct((B, ATT_HEADS, S, LANES), BF16)),
        grid=(B, S // tm),
        in_specs=[pl.BlockSpec((1, tm, 2 * ATT_WIDTH), lambda b, i: (b, i, 0)),
                  pl.BlockSpec((tm, LANES), lambda b, i: (i, 0)),
                  pl.BlockSpec((tm, LANES), lambda b, i: (i, 0)),
                  pl.BlockSpec((1, LANES), lambda b, i: (0, 0)),
                  pl.BlockSpec((1, LANES), lambda b, i: (0, 0))],
        out_specs=(pl.BlockSpec((1, ATT_HEADS, 2, LANES, tm), lambda b, i: (b, 0, 0, 0, i)),
                   pl.BlockSpec((1, ATT_HEADS, tm, LANES), lambda b, i: (b, 0, i, 0))),
        compiler_params=_cparams(("parallel", "parallel")),
        name="qkprep",
    )(p_qk, cos_t, sin_t, gq, gk)


def _attn_kernel(lam_ref, qz_ref, k_ref, v_ref, g_ref, *rest, aliased, tk, n_kv):
    if aliased:
        rest = rest[1:]
    o_ref, s_sc, mt_sc, m_sc, l_sc, acc_sc = rest
    m_sc[...] = jnp.full(m_sc.shape, -jnp.inf, F32)
    l_sc[...] = jnp.zeros(l_sc.shape, F32)
    acc_sc[...] = jnp.zeros(acc_sc.shape, F32)

    def rows(j):
        if isinstance(j, int):
            return pl.ds(j * tk, tk)
        return pl.ds(pl.multiple_of(j * tk, tk), tk)

    def qk_tile(j, slot, ps=(0, 1)):
        kt = k_ref[0, 0, rows(j), :]
        for p in ps:
            s = _dot(kt, qz_ref[0, 0, p])
            s_sc[slot, p] = s
            mt_sc[slot, p] = jnp.max(s, axis=0, keepdims=True)

    def pv_tile(j, slot, ps=(0, 1)):
        for p in ps:
            m_prev = m_sc[p]
            m_new = jnp.maximum(m_prev, mt_sc[slot, p])
            alpha = jnp.exp2(m_prev - m_new)
            lsum = None
            acc = None
            for c in range(tk // KV_CHUNK):
                cs = slice(c * KV_CHUNK, (c + 1) * KV_CHUNK)
                pe = jnp.exp2(s_sc[slot, p, cs, :] - m_new)
                ps = jnp.sum(pe, axis=0, keepdims=True)
                pv = _dot(v_ref[0, 0, j, :, cs], pe.astype(BF16))
                lsum = ps if lsum is None else lsum + ps
                acc = pv if acc is None else acc + pv
            l_sc[p] = alpha * l_sc[p] + lsum
            acc_sc[p] = alpha * acc_sc[p] + acc
            m_sc[p] = m_new

    def fused_tile(jq, slot_q, jp, slot_p):
        m_new = [jnp.maximum(m_sc[p], mt_sc[slot_p, p]) for p in range(2)]
        alpha = [jnp.exp2(m_sc[p] - m_new[p]) for p in range(2)]
        mx, lsum, acc = [None, None], [None, None], [None, None]
        for c in range(tk // KV_CHUNK):
            cs = slice(c * KV_CHUNK, (c + 1) * KV_CHUNK)
            if isinstance(jq, int):
                kr = pl.ds(jq * tk + c * KV_CHUNK, KV_CHUNK)
            else:
                kr = pl.ds(pl.multiple_of(jq * tk + c * KV_CHUNK, KV_CHUNK), KV_CHUNK)
            kc = k_ref[0, 0, kr, :]
            vc = v_ref[0, 0, jp, :, cs]
            for p in range(2):
                s = _dot(kc, qz_ref[0, 0, p])
                s_sc[slot_q, p, cs, :] = s
                cm = jnp.max(s, axis=0, keepdims=True)
                mx[p] = cm if mx[p] is None else jnp.maximum(mx[p], cm)
                pe = jnp.exp2(s_sc[slot_p, p, cs, :] - m_new[p])
                ps = jnp.sum(pe, axis=0, keepdims=True)
                pv = _dot(vc, pe.astype(BF16))
                lsum[p] = ps if lsum[p] is None else lsum[p] + ps
                acc[p] = pv if acc[p] is None else acc[p] + pv
        for p in range(2):
            mt_sc[slot_q, p] = mx[p]
            l_sc[p] = alpha[p] * l_sc[p] + lsum[p]
            acc_sc[p] = alpha[p] * acc_sc[p] + acc[p]
            m_sc[p] = m_new[p]

    qk_tile(0, 0)
    n_pairs = (n_kv - 1) // 2

    def pair(jj, carry):
        j = 2 * jj
        fused_tile(j + 1, 1, j, 0)
        fused_tile(j + 2, 0, j + 1, 1)
        return carry

    if n_pairs > 0:
        lax.fori_loop(0, n_pairs, pair, 0)
    j = 2 * n_pairs
    if j == n_kv - 1:
        pv_tile(j, 0)
    else:
        qk_tile(j + 1, 1)
        pv_tile(j, 0)
        pv_tile(j + 1, 1)

    o = acc_sc[0] / l_sc[0] - lam_ref[0] * (acc_sc[1] / l_sc[1])
    ms = jnp.mean(o * o, axis=0, keepdims=True)
    o_ref[0] = ((o * lax.rsqrt(ms + EPS)).T * g_ref[...]).astype(BF16)


def _attention(lam, qz, k, vt, g, *, tq, tk, q_blk0, n_q, kv_blk0, n_kv, prev_out=None, name="attn_ctx"):
    B, H, _, _, S = qz.shape
    aliased = prev_out is not None
    skv = tk * n_kv
    in_specs = [pl.BlockSpec(memory_space=pltpu.SMEM),
                pl.BlockSpec((1, 1, 2, LANES, tq), lambda b, h, i: (b, h, 0, 0, q_blk0 + i)),
                pl.BlockSpec((1, 1, skv, LANES), lambda b, h, i: (b, h, kv_blk0, 0)),
                pl.BlockSpec((1, 1, n_kv, LANES, tk), lambda b, h, i: (b, h, 0, 0, 0)),
                pl.BlockSpec((1, LANES), lambda b, h, i: (0, 0))]
    args = [lam, qz, k, vt, g]
    aliases = {}
    if aliased:
        in_specs.append(pl.BlockSpec(memory_space=pl.ANY))
        args.append(prev_out)
        aliases = {5: 0}
    return pl.pallas_call(
        functools.partial(_attn_kernel, aliased=aliased, tk=tk, n_kv=n_kv),
        out_shape=jax.ShapeDtypeStruct((B, S, ATT_WIDTH), BF16),
        grid=(B, H, n_q),
        in_specs=in_specs,
        out_specs=pl.BlockSpec((1, tq, LANES), lambda b, h, i: (b, q_blk0 + i, h)),
        scratch_shapes=[pltpu.VMEM((2, 2, tk, tq), F32), pltpu.VMEM((2, 2, 1, tq), F32),
                        pltpu.VMEM((2, 1, tq), F32), pltpu.VMEM((2, 1, tq), F32),
                        pltpu.VMEM((2, LANES, tq), F32)],
        input_output_aliases=aliases,
        compiler_params=_cparams(("parallel", "parallel", "arbitrary")),
        name=name,
    )(*args)


def _attention_lat(lam, qz, k, vt, g, L):
    B, H, _, S = vt.shape
    tq = _pick_tile(L, (512, 256))
    tk = _pick_tile(S, (1280, 640, 256))
    n_kv = S // tk
    vt_tiles = vt.reshape(B, H, LANES, n_kv, tk).transpose(0, 1, 3, 2, 4)
    out0 = jnp.zeros((B, S, ATT_WIDTH), BF16)
    return _attention(lam, qz, k, vt_tiles, g, tq=tq, tk=tk, q_blk0=0, n_q=L // tq, kv_blk0=0, n_kv=n_kv,
                      prev_out=out0, name="attn_lat")


def _attention_ctx(lam, qz, k, vt, g, L, prev_out):
    B, H, _, S = vt.shape
    Lc = S - L
    vt_ctx = vt[:, :, :, L:].reshape(B, H, 1, LANES, Lc)
    return _attention(lam, qz, k, vt_ctx, g, tq=Lc, tk=Lc, q_blk0=L // Lc, n_q=1, kv_blk0=L // Lc, n_kv=1,
                      prev_out=prev_out)


def _gdnprep_kernel(x_ref, prev_ref, next_ref, cw_ref, ab_ref, alog_ref, dtb_ref,
                    q_ref, k_ref, v_ref, gate_ref, ext_sc, *, nb_lat, nb_all):
    i = pl.program_id(1)
    tm = x_ref.shape[1]
    first = jnp.logical_or(i == 0, i == nb_lat)
    last = jnp.logical_or(i == nb_lat - 1, i == nb_all - 1)
    keep_prev = jnp.where(first, 0.0, 1.0)
    keep_next = jnp.where(last, 0.0, 1.0)
    ext_sc[0:8, :] = prev_ref[0] * keep_prev
    ext_sc[8:8 + tm, :] = x_ref[0]
    ext_sc[8 + tm:16 + tm, :] = next_ref[0] * keep_next
    acc = None
    for j in range(CONV_W):
        term = ext_sc[pl.ds(8 + j - CONV_W // 2, tm), :] * cw_ref[j:j + 1, :]
        acc = term if acc is None else acc + term
    y = acc * _sigmoid(acc)
    for part, ref in ((0, q_ref), (1, k_ref)):
        for hh in range(GDN_HEADS):
            lo = part * GDN_WIDTH + hh * GDN_HEAD_DIM
            t = y[:, lo:lo + GDN_HEAD_DIM]
            ref[0, :, hh * GDN_HEAD_DIM:(hh + 1) * GDN_HEAD_DIM] = (
                t * lax.rsqrt(jnp.sum(t * t, axis=-1, keepdims=True) + EPS))
    v_ref[0] = y[:, 2 * GDN_WIDTH:3 * GDN_WIDTH]
    ab = ab_ref[0]
    xs = ab + dtb_ref[...]
    sp = jnp.maximum(xs, 0.0) + jnp.log(1.0 + jnp.exp(-jnp.abs(xs)))
    g = -jnp.exp(alog_ref[...]) * sp
    lane = lax.broadcasted_iota(jnp.int32, ab.shape, 1)
    gate_ref[0] = jnp.where(lane < 2 * GDN_HEADS, g, _sigmoid(ab))


def _gdnprep(p_gdn, conv_w8, ab, alog_row, dtb_row, n_lat_blocks):
    B, S, W = p_gdn.shape
    tm = ROW_TILE
    nb = S // tm
    r8 = tm // 8
    row = lambda b, i: (b, i, 0)
    return pl.pallas_call(
        functools.partial(_gdnprep_kernel, nb_lat=n_lat_blocks, nb_all=nb),
        out_shape=(jax.ShapeDtypeStruct((B, S, GDN_WIDTH), F32),
                   jax.ShapeDtypeStruct((B, S, GDN_WIDTH), F32),
                   jax.ShapeDtypeStruct((B, S, GDN_WIDTH), F32),
                   jax.ShapeDtypeStruct((B, S, LANES), F32)),
        grid=(B, nb),
        in_specs=[pl.BlockSpec((1, tm, W), row),
                  pl.BlockSpec((1, 8, W), lambda b, i: (b, jnp.maximum(i * r8 - 1, 0), 0)),
                  pl.BlockSpec((1, 8, W), lambda b, i: (b, jnp.minimum((i + 1) * r8, nb * r8 - 1), 0)),
                  pl.BlockSpec((8, W), lambda b, i: (0, 0)),
                  pl.BlockSpec((1, tm, LANES), row),
                  pl.BlockSpec((1, LANES), lambda b, i: (0, 0)),
                  pl.BlockSpec((1, LANES), lambda b, i: (0, 0))],
        out_specs=(pl.BlockSpec((1, tm, GDN_WIDTH), row),
                   pl.BlockSpec((1, tm, GDN_WIDTH), row),
                   pl.BlockSpec((1, tm, GDN_WIDTH), row),
                   pl.BlockSpec((1, tm, LANES), row)),
        scratch_shapes=[pltpu.VMEM((tm + 16, W), F32)],
        compiler_params=_cparams(("parallel", "parallel")),
        name="gdnprep",
    )(p_gdn, p_gdn, p_gdn, conv_w8, ab, alog_row, dtb_row)


def _gdnchunk_kernel(q_ref, k_ref, v_ref, gate_ref, qg_ref, w_ref, kd_ref, u_ref, aq_ref, eg_ref):
    tm = q_ref.shape[1]
    nc = tm // GDN_CHUNK
    gates = gate_ref[0]
    ri = lax.broadcasted_iota(jnp.int32, (tm, tm), 0)
    ci = lax.broadcasted_iota(jnp.int32, (tm, tm), 1)
    same = (ri >> 6) == (ci >> 6)
    eye = ri == ci
    tot = _dot_hi(jnp.where(same, 1.0, 0.0).astype(F32), gates)
    eye_f = jnp.where(eye, 1.0, 0.0).astype(F32)

    heads = []
    for hh in range(GDN_HEADS):
        sl = slice(hh * GDN_HEAD_DIM, (hh + 1) * GDN_HEAD_DIM)
        k = k_ref[0, :, sl]
        k16 = k.astype(BF16)
        q = q_ref[0, :, sl] * (GDN_HEAD_DIM ** -0.5)
        heads.append((sl, k, q, _dot_nt(k16, k16), _dot_nt(q.astype(BF16), k16)))

    chains = []
    for d in range(2):
        incl = jnp.logical_and(same, (ci <= ri) if d == 0 else (ci >= ri))
        strict = jnp.logical_and(incl, jnp.logical_not(eye))
        gc = _dot_hi(jnp.where(incl, 1.0, 0.0).astype(F32), gates)
        gc_t = gc.T
        for hh in range(GDN_HEADS):
            ln = d * GDN_HEADS + hh
            gcol = gc[:, ln:ln + 1]
            bcol = gates[:, 2 * GDN_HEADS + ln:2 * GDN_HEADS + ln + 1]
            tcol = tot[:, ln:ln + 1]
            diff = gcol - gc_t[ln:ln + 1, :]
            decay = jnp.where(incl, jnp.exp(jnp.where(incl, diff, 0.0)), 0.0)
            a = jnp.where(strict, heads[hh][3] * bcol * decay, 0.0)
            aqk = heads[hh][4] * decay
            chains.append(dict(d=d, hh=hh, x=-a, t=eye_f - a, aqk=aqk, gcol=gcol, bcol=bcol, tcol=tcol))

    for _ in range(5):
        for ch in chains:
            x16 = ch["x"].astype(BF16)
            ch["x"] = _dot(x16, x16)
        for ch in chains:
            ch["t"] = ch["t"] + _dot(ch["t"].astype(BF16), ch["x"].astype(BF16))

    for ch in chains:
        d, hh, gcol, bcol, tcol = ch["d"], ch["hh"], ch["gcol"], ch["bcol"], ch["tcol"]
        sl, k, q = heads[hh][0], heads[hh][1], heads[hh][2]
        t16 = ch["t"].astype(BF16)
        egc = jnp.exp(gcol)
        kb = k * bcol
        u_ref[0, d, :, sl] = _dot(t16, (v_ref[0, :, sl] * bcol).astype(BF16))
        w_ref[0, d, :, sl] = _dot(t16, (kb * egc).astype(BF16)).astype(BF16)
        kd_ref[0, d, :, sl] = (k * jnp.exp(tcol - gcol)).astype(BF16)
        qg_ref[0, d, :, sl] = (q * egc).astype(BF16)
        for cc in range(nc):
            rs = slice(cc * GDN_CHUNK, (cc + 1) * GDN_CHUNK)
            aq_ref[0, d, rs, hh * GDN_CHUNK:(hh + 1) * GDN_CHUNK] = ch["aqk"][rs, rs].astype(BF16)
            eg_ref[0, d, cc, hh:hh + 1, :] = jnp.broadcast_to(
                jnp.exp(tcol[cc * GDN_CHUNK:cc * GDN_CHUNK + 1, :]), (1, LANES))


def _gdnchunk(gq, gk, gv, gates):
    B, S, W = gq.shape
    tm = ROW_TILE
    nc = tm // GDN_CHUNK
    row = lambda b, i: (b, i, 0)
    drow = lambda b, i: (b, 0, i, 0)
    big = lambda dt: jax.ShapeDtypeStruct((B, 2, S, W), dt)
    return pl.pallas_call(
        _gdnchunk_kernel,
        out_shape=(big(BF16), big(BF16), big(BF16), big(F32),
                   jax.ShapeDtypeStruct((B, 2, S, GDN_HEADS * GDN_CHUNK), BF16),
                   jax.ShapeDtypeStruct((B, 2, S // GDN_CHUNK, GDN_HEADS, LANES), F32)),
        grid=(B, S // tm),
        in_specs=[pl.BlockSpec((1, tm, W), row), pl.BlockSpec((1, tm, W), row),
                  pl.BlockSpec((1, tm, W), row), pl.BlockSpec((1, tm, LANES), row)],
        out_specs=(pl.BlockSpec((1, 2, tm, W), drow), pl.BlockSpec((1, 2, tm, W), drow),
                   pl.BlockSpec((1, 2, tm, W), drow), pl.BlockSpec((1, 2, tm, W), drow),
                   pl.BlockSpec((1, 2, tm, GDN_HEADS * GDN_CHUNK), drow),
                   pl.BlockSpec((1, 2, nc, GDN_HEADS, LANES), lambda b, i: (b, 0, i, 0, 0))),
        compiler_params=_cparams(("parallel", "parallel")),
        name="gdnchunk",
    )(gq, gk, gv, gates)


def _gdnscan_kernel(qg0, w0, kd0, u0, aq0, eg0, qg1, w1, kd1, u1, aq1, eg1, of_ref, ob_ref, s_sc):
    @pl.when(pl.program_id(1) == 0)
    def _():
        s_sc[...] = jnp.zeros(s_sc.shape, F32)

    dirs = ((qg0, w0, kd0, u0, aq0, eg0, of_ref), (qg1, w1, kd1, u1, aq1, eg1, ob_ref))
    chains = [(d, hh) for d in range(2) for hh in range(GDN_HEADS)]
    st, st16, vn16, qs = {}, {}, {}, {}
    for d, hh in chains:
        st[d, hh] = s_sc[d, hh]
        st16[d, hh] = st[d, hh].astype(BF16)
    for d, hh in chains:
        qg, w, kd, u, aq, eg, o_ref = dirs[d]
        sl = slice(hh * GDN_HEAD_DIM, (hh + 1) * GDN_HEAD_DIM)
        vn16[d, hh] = (u[0, 0, :, sl] - _dot(w[0, 0, :, sl], st16[d, hh])).astype(BF16)
        qs[d, hh] = _dot(qg[0, 0, :, sl], st16[d, hh])
    for d, hh in chains:
        qg, w, kd, u, aq, eg, o_ref = dirs[d]
        sl = slice(hh * GDN_HEAD_DIM, (hh + 1) * GDN_HEAD_DIM)
        o_ref[0, :, sl] = qs[d, hh] + _dot(aq[0, 0, :, hh * GDN_CHUNK:(hh + 1) * GDN_CHUNK], vn16[d, hh])
        s_sc[d, hh] = st[d, hh] * eg[0, 0, 0, hh:hh + 1, :] + _dot_tn(kd[0, 0, :, sl], vn16[d, hh])


def _gdnscan(qg, w, kd, u, aq, eg, n_lat_chunks, n_ctx_chunks):
    B, _, S, W = qg.shape
    C = GDN_CHUNK
    n = S // C

    def fwd_chunk(i):
        return jnp.where(i < n_ctx_chunks, n_lat_chunks + i, i - n_ctx_chunks)

    def bwd_chunk(i):
        return jnp.where(i < n_ctx_chunks, n_lat_chunks + n_ctx_chunks - 1 - i, n - 1 - i)

    def specs(d, chunk_of):
        big = pl.BlockSpec((1, 1, C, W), lambda b, i: (b, d, chunk_of(i), 0))
        return [big, big, big, big,
                pl.BlockSpec((1, 1, C, GDN_HEADS * C), lambda b, i: (b, d, chunk_of(i), 0)),
                pl.BlockSpec((1, 1, 1, GDN_HEADS, LANES), lambda b, i: (b, d, chunk_of(i), 0, 0))]

    return pl.pallas_call(
        _gdnscan_kernel,
        out_shape=(jax.ShapeDtypeStruct((B, S, W), F32), jax.ShapeDtypeStruct((B, S, W), F32)),
        grid=(B, n),
        in_specs=specs(0, fwd_chunk) + specs(1, bwd_chunk),
        out_specs=(pl.BlockSpec((1, C, W), lambda b, i: (b, fwd_chunk(i), 0)),
                   pl.BlockSpec((1, C, W), lambda b, i: (b, bwd_chunk(i), 0))),
        scratch_shapes=[pltpu.VMEM((2, GDN_HEADS, GDN_HEAD_DIM, GDN_HEAD_DIM), F32)],
        compiler_params=_cparams(("parallel", "arbitrary")),
        name="gdnscan",
    )(qg, w, kd, u, aq, eg, qg, w, kd, u, aq, eg)


def _mixout_kernel(oa_ref, of_ref, ob_ref, z_ref, h_ref, mod_ref, gg_ref, wo_ref, gf_ref, rw_ref, rb_ref,
                   hn_ref, v_ref, te_ref, tg_ref):
    og = of_ref[0] + ob_ref[0]
    z = z_ref[0]
    parts = [oa_ref[0]]
    for hh in range(GDN_HEADS):
        sl = slice(hh * GDN_HEAD_DIM, (hh + 1) * GDN_HEAD_DIM)
        t = og[:, sl]
        t = t * lax.rsqrt(jnp.mean(t * t, axis=-1, keepdims=True) + EPS) * gg_ref[...]
        zz = z[:, sl]
        parts.append((t * (zz * _sigmoid(zz))).astype(BF16))
    mix_in = jnp.concatenate(parts, axis=-1)
    mix = _dot(mix_in, wo_ref[...])
    hn = h_ref[0] + mod_ref[0, 0, 2:3, :] * mix
    hn_ref[0] = hn
    y = hn * lax.rsqrt(jnp.mean(hn * hn, axis=-1, keepdims=True) + EPS) * gf_ref[...]
    v = y * (1.0 + mod_ref[0, 0, 4:5, :]) + mod_ref[0, 0, 3:4, :]
    v_ref[0] = v.astype(BF16)
    logits = _dot_hi(v, rw_ref[...]) + rb_ref[...]
    lane = lax.broadcasted_iota(jnp.int32, logits.shape, 1)
    cur = logits
    vals, idxs = [], []
    for _ in range(TOP_K):
        m = jnp.max(cur, axis=-1, keepdims=True)
        idx = jnp.min(jnp.where(cur == m, lane, LANES), axis=-1, keepdims=True)
        vals.append(m)
        idxs.append(idx)
        cur = jnp.where(lane == idx, -jnp.inf, cur)
    es = [jnp.exp(vv - vals[0]) for vv in vals]
    inv = 1.0 / (es[0] + es[1] + es[2] + es[3])
    te = jnp.zeros(logits.shape, jnp.int32)
    tg = jnp.zeros(logits.shape, F32)
    for kk in range(TOP_K):
        te = jnp.where(lane == kk, idxs[kk], te)
        tg = jnp.where(lane == kk, es[kk] * inv, tg)
    te_ref[0] = te
    tg_ref[0] = tg


def _mixout(o_att, o_f, o_b, z, h, modtab, gg, w_out, gf, rw, rb, n_lat_blocks):
    B, S, _ = h.shape
    tm = ROW_TILE
    row = lambda b, i: (b, i, 0)
    const = lambda b, i: (0, 0)
    return pl.pallas_call(
        _mixout_kernel,
        out_shape=(jax.ShapeDtypeStruct((B, S, D_MODEL), F32),
                   jax.ShapeDtypeStruct((B, S, D_MODEL), BF16),
                   jax.ShapeDtypeStruct((B, S, LANES), jnp.int32),
                   jax.ShapeDtypeStruct((B, S, LANES), F32)),
        grid=(B, S // tm),
        in_specs=[pl.BlockSpec((1, tm, ATT_WIDTH), row),
                  pl.BlockSpec((1, tm, GDN_WIDTH), row),
                  pl.BlockSpec((1, tm, GDN_WIDTH), row),
                  pl.BlockSpec((1, tm, GDN_WIDTH), row),
                  pl.BlockSpec((1, tm, D_MODEL), row),
                  pl.BlockSpec((1, 1, 8, D_MODEL), lambda b, i: (b, (i >= n_lat_blocks).astype(jnp.int32), 0, 0)),
                  pl.BlockSpec((1, LANES), const),
                  pl.BlockSpec((D_MODEL, D_MODEL), const),
                  pl.BlockSpec((1, D_MODEL), const),
                  pl.BlockSpec((D_MODEL, LANES), const),
                  pl.BlockSpec((1, LANES), const)],
        out_specs=(pl.BlockSpec((1, tm, D_MODEL), row),
                   pl.BlockSpec((1, tm, D_MODEL), row),
                   pl.BlockSpec((1, tm, LANES), row),
                   pl.BlockSpec((1, tm, LANES), row)),
        compiler_params=_cparams(("parallel", "parallel")),
        name="mixout",
    )(o_att, o_f, o_b, z, h, modtab, gg, w_out, gf, rw, rb)


def _expert_kernel(be_ref, nv_ref, x_ref, wgu_ref, bgu_ref, wd_ref, bd_ref, sg_ref, y_ref, wgu_sc, wd_sc):
    i = pl.program_id(0)
    new_expert = jnp.logical_or(i == 0, be_ref[i] != be_ref[jnp.maximum(i - 1, 0)])

    @pl.when(new_expert)
    def _():
        wgu_sc[...] = wgu_ref[0, 0].astype(BF16)
        wd_sc[...] = wd_ref[0, 0].astype(BF16)

    @pl.when(nv_ref[i] > 0)
    def _():
        gu = _dot(x_ref[...], wgu_sc[...]) + bgu_ref[0, 0]
        g_ = jnp.minimum(gu[:, :D_EXPERT], SWIGLU_LIMIT)
        up = jnp.clip(gu[:, D_EXPERT:], -SWIGLU_LIMIT, SWIGLU_LIMIT)
        glu = g_ * _sigmoid(SWIGLU_ALPHA * g_)
        act = ((up + 1.0) * glu).astype(BF16)
        y_ref[...] = ((_dot(act, wd_sc[...]) + bd_ref[0, 0]) * sg_ref[...]).astype(y_ref.dtype)

    @pl.when(nv_ref[i] == 0)
    def _():
        y_ref[...] = jnp.zeros(y_ref.shape, y_ref.dtype)


def _experts(block_e, n_valid, x_sorted, wgu, bgu, wd, bd, slot_gate, layer):
    n_slots = x_sorted.shape[0]
    nb = n_slots // MOE_BLOCK
    grid_spec = pltpu.PrefetchScalarGridSpec(
        num_scalar_prefetch=2,
        grid=(nb,),
        in_specs=[pl.BlockSpec((MOE_BLOCK, D_MODEL), lambda i, be, nv: (i, 0)),
                  pl.BlockSpec((1, 1, D_MODEL, 2 * D_EXPERT), lambda i, be, nv: (layer, be[i], 0, 0)),
                  pl.BlockSpec((1, 1, 1, 2 * D_EXPERT), lambda i, be, nv: (layer, be[i], 0, 0)),
                  pl.BlockSpec((1, 1, D_EXPERT, D_MODEL), lambda i, be, nv: (layer, be[i], 0, 0)),
                  pl.BlockSpec((1, 1, 1, D_MODEL), lambda i, be, nv: (layer, be[i], 0, 0)),
                  pl.BlockSpec((MOE_BLOCK, 1), lambda i, be, nv: (i, 0))],
        out_specs=pl.BlockSpec((MOE_BLOCK, D_MODEL), lambda i, be, nv: (i, 0)),
        scratch_shapes=[pltpu.VMEM((D_MODEL, 2 * D_EXPERT), BF16), pltpu.VMEM((D_EXPERT, D_MODEL), BF16)],
    )
    return pl.pallas_call(
        _expert_kernel,
        out_shape=jax.ShapeDtypeStruct((n_slots, D_MODEL), BF16),
        grid_spec=grid_spec,
        compiler_params=pltpu.CompilerParams(dimension_semantics=("arbitrary",),
                                             vmem_limit_bytes=EXPERT_VMEM_LIMIT),
        name="experts",
    )(block_e, n_valid, x_sorted, wgu, bgu, wd, bd, slot_gate)


def _moe_plan(top_e, top_g):
    T = top_e.shape[0]
    n_assign = T * TOP_K
    n_blocks = -(-n_assign // MOE_BLOCK) + N_EXPERTS
    n_slots = n_blocks * MOE_BLOCK
    i32 = jnp.int32
    flat_e = top_e.reshape(-1)
    gate_flat = top_g.reshape(-1)
    order = jnp.argsort(flat_e, stable=True).astype(i32)
    e_ids = jnp.arange(N_EXPERTS, dtype=i32)
    is_e = flat_e[:, None] == e_ids[None, :]
    counts = jnp.sum(is_e, axis=0, dtype=i32)
    start = jnp.cumsum(counts) - counts
    padded = (counts + MOE_BLOCK - 1) // MOE_BLOCK * MOE_BLOCK
    pad_end = jnp.cumsum(padded)
    pad_start = pad_end - padded
    blk0 = jnp.arange(n_blocks, dtype=i32) * MOE_BLOCK
    block_e = jnp.minimum(jnp.sum(pad_end[None, :] <= blk0[:, None], axis=1, dtype=i32), N_EXPERTS - 1)
    off = (blk0 - pad_start[block_e])[:, None] + jnp.arange(MOE_BLOCK, dtype=i32)[None, :]
    valid = off < counts[block_e][:, None]
    a_slot = order[jnp.clip(off + start[block_e][:, None], 0, n_assign - 1).reshape(-1)]
    valid_flat = valid.reshape(-1)
    slot_tok = jnp.where(valid_flat, a_slot // TOP_K, 0)
    slot_gate = jnp.where(valid_flat, gate_flat[a_slot], 0.0)
    n_valid = jnp.sum(valid, axis=1, dtype=i32)

    n_tiles = T // COMBINE_TILE
    cnt = jnp.sum(top_e.reshape(n_tiles, COMBINE_TILE * TOP_K)[:, :, None] == e_ids[None, None, :], axis=1, dtype=i32)
    run_start = pad_start[None, :] + jnp.cumsum(cnt, axis=0) - cnt
    q_first = run_start // COMBINE_CHUNK
    n_ch = jnp.where(cnt > 0, (run_start + cnt - 1) // COMBINE_CHUNK - q_first + 1, 0)
    ch_end = jnp.cumsum(n_ch, axis=1)
    ch_off = ch_end - n_ch
    j = jnp.arange(COMBINE_MAX_CHUNKS, dtype=i32)
    e_j = jnp.minimum(jnp.sum(ch_end[:, None, :] <= j[None, :, None], axis=2, dtype=i32), N_EXPERTS - 1)
    used = j[None, :] < ch_end[:, -1:]
    chunk_id = jnp.where(used, jnp.take_along_axis(q_first, e_j, axis=1)
                         + j[None, :] - jnp.take_along_axis(ch_off, e_j, axis=1), 0)
    slot_tokid = jnp.where(valid_flat, a_slot // TOP_K, -1).reshape(n_slots // COMBINE_CHUNK, COMBINE_CHUNK)
    row_tok = jnp.where(used[:, :, None], slot_tokid[chunk_id], -1)
    row_tok = row_tok.reshape(n_tiles, 1, COMBINE_MAX_CHUNKS * COMBINE_CHUNK)
    return slot_tok, slot_gate.reshape(n_slots, 1), block_e, n_valid, chunk_id, row_tok


def _combine_kernel(cid_ref, tok_ref, h_ref, mod_ref, y_hbm, o_ref, ybuf, sem):
    i = pl.program_id(0)
    n = pl.num_programs(0)
    slot = lax.rem(i, 2)

    def chunk_copy(c, s, jj):
        return pltpu.make_async_copy(
            y_hbm.at[pl.ds(pl.multiple_of(c * COMBINE_CHUNK, COMBINE_CHUNK), COMBINE_CHUNK)],
            ybuf.at[s, pl.ds(jj * COMBINE_CHUNK, COMBINE_CHUNK)], sem.at[s])

    def start_tile(t, s):
        for jj in range(COMBINE_MAX_CHUNKS):
            chunk_copy(cid_ref[t, jj], s, jj).start()

    @pl.when(i == 0)
    def _():
        start_tile(0, 0)

    @pl.when(i + 1 < n)
    def _():
        start_tile(i + 1, 1 - slot)

    for jj in range(COMBINE_MAX_CHUNKS):
        chunk_copy(0, slot, jj).wait()

    tok = tok_ref[0]
    t_ids = i * COMBINE_TILE + lax.broadcasted_iota(jnp.int32, (COMBINE_TILE, tok.shape[1]), 0)
    onehot = jnp.where(tok == t_ids, 1.0, 0.0).astype(BF16)
    y = _dot(onehot, ybuf[slot])
    o_ref[...] = h_ref[...] + mod_ref[0, 0, 5:6, :] * y


def _combine(chunk_id, row_tok, y_sorted, h, modtab, n_lat_blocks):
    B, S, _ = h.shape
    T = B * S
    ct = COMBINE_TILE
    tiles_per_batch = S // ct
    n_lat_tiles = n_lat_blocks * ROW_TILE // ct
    n_rows = COMBINE_MAX_CHUNKS * COMBINE_CHUNK
    grid_spec = pltpu.PrefetchScalarGridSpec(
        num_scalar_prefetch=1,
        grid=(T // ct,),
        in_specs=[pl.BlockSpec((1, 1, n_rows), lambda i, cid: (i, 0, 0)),
                  pl.BlockSpec((ct, D_MODEL), lambda i, cid: (i, 0)),
                  pl.BlockSpec((1, 1, 8, D_MODEL),
                               lambda i, cid: (i // tiles_per_batch,
                                               (lax.rem(i, tiles_per_batch) >= n_lat_tiles).astype(jnp.int32), 0, 0)),
                  pl.BlockSpec(memory_space=pl.ANY)],
        out_specs=pl.BlockSpec((ct, D_MODEL), lambda i, cid: (i, 0)),
        scratch_shapes=[pltpu.VMEM((2, n_rows, D_MODEL), BF16), pltpu.SemaphoreType.DMA((2,))],
    )
    out = pl.pallas_call(
        _combine_kernel,
        out_shape=jax.ShapeDtypeStruct((T, D_MODEL), F32),
        grid_spec=grid_spec,
        compiler_params=_cparams(("arbitrary",)),
        name="combine",
    )(chunk_id, row_tok, h.reshape(T, D_MODEL), modtab, y_sorted)
    return out.reshape(B, S, D_MODEL)


def _moe(v_ffn, top_e, top_g, h_new, modtab, wgu, bgu, wd, bd, layer, n_lat_blocks):
    B, S, _ = v_ffn.shape
    T = B * S
    slot_tok, slot_gate, block_e, n_valid, chunk_id, row_tok = _moe_plan(
        top_e.reshape(T, LANES)[:, :TOP_K], top_g.reshape(T, LANES)[:, :TOP_K])
    x_sorted = v_ffn.reshape(T, D_MODEL)[slot_tok]
    y_sorted = _experts(block_e, n_valid, x_sorted, wgu, bgu, wd, bd, slot_gate, layer)
    return _combine(chunk_id, row_tok, y_sorted, h_new, modtab, n_lat_blocks)


def _pick_tile(n, cands):
    for t in cands:
        if n % t == 0:
            return t
    raise ValueError(f"no tile for {n}")


def _rope_tables(L, Lc):
    rows = L // GRID_W
    row = jnp.repeat(jnp.arange(rows, dtype=F32), GRID_W)
    col = (jnp.arange(L, dtype=jnp.int32) % GRID_W).astype(F32)
    inv_freq = ROPE_BASE ** (-jnp.arange(ROPE_PAIRS, dtype=F32) / ROPE_PAIRS)
    ar = row[:, None] * inv_freq
    ac = col[:, None] * inv_freq
    cos64 = jnp.concatenate([jnp.cos(ar), jnp.cos(ar), jnp.cos(ac), jnp.cos(ac)], axis=-1)
    sin64 = jnp.concatenate([-jnp.sin(ar), jnp.sin(ar), -jnp.sin(ac), jnp.sin(ac)], axis=-1)
    cos_t = jnp.concatenate([jnp.tile(cos64, (1, 2)), jnp.ones((Lc, LANES), F32)], axis=0)
    sin_t = jnp.concatenate([jnp.tile(sin64, (1, 2)), jnp.zeros((Lc, LANES), F32)], axis=0)
    return cos_t, sin_t


def _pad_lanes(v):
    v = v.reshape(1, -1).astype(F32)
    return jnp.pad(v, ((0, 0), (0, LANES - v.shape[1])))


def kernel(x, c, ctx, c_ctx, w_mod, b_mod, norm_mix_g, w_in, q_norm_g, k_norm_g, lam_q1, lam_k1, lam_q2, lam_k2, subln_g, conv_w, a_log, dt_bias, gdn_norm_g, w_out, norm_ffn_g, router_w, router_b, w_gate_up, b_gate_up, w_down, b_down):
    B, L, D = x.shape
    Lc = ctx.shape[1]
    S = L + Lc
    depth = w_mod.shape[0]
    tm = ROW_TILE
    n_lat_blocks = L // tm
    cos_t, sin_t = _rope_tables(L, Lc)

    c_rows = jnp.zeros((8, D), F32).at[:B].set(c).at[B].set(c_ctx)
    h = jnp.concatenate([x, ctx], axis=1)

    for layer in range(depth):
        mod = _adaln(c_rows, w_mod, b_mod, layer)
        mod6 = mod.reshape(8, 6, D)
        lat_mod = mod6[:B]
        ctx_mod = jnp.broadcast_to(mod6[B][None], (B, 6, D))
        modtab = jnp.pad(jnp.stack([lat_mod, ctx_mod], axis=1), ((0, 0), (0, 0), (0, 2), (0, 0)))

        lam_init = 0.8 - 0.6 * math.exp(-0.3 * layer)
        lam_full = (jnp.exp(jnp.sum(lam_q1[layer] * lam_k1[layer]))
                    - jnp.exp(jnp.sum(lam_q2[layer] * lam_k2[layer])) + lam_init).reshape(1).astype(F32)

        w_l = w_in[layer]
        w_main = w_l[:, :IN_MAIN].astype(BF16)
        w_ab = jnp.pad(w_l[:, IN_MAIN:], ((0, 0), (0, LANES - (w_l.shape[1] - IN_MAIN)))).astype(BF16)
        p_qk, v_att, p_gdn, z, ab = _inproj(h, modtab, norm_mix_g[layer].reshape(1, D), w_main, w_ab,
                                            n_lat_blocks)

        gq = jnp.tile(q_norm_g[layer].reshape(1, ATT_HEAD_DIM), (1, 2))
        gk = jnp.tile(k_norm_g[layer].reshape(1, ATT_HEAD_DIM), (1, 2))
        qz, k_att = _qkprep(p_qk, cos_t, sin_t, gq, gk)
        g_sub = (subln_g[layer] * (1.0 - lam_init)).reshape(1, LANES).astype(F32)
        o_att = _attention_lat(lam_full, qz, k_att, v_att, g_sub, L)
        o_att = _attention_ctx(lam_full, qz, k_att, v_att, g_sub, L, o_att)

        conv_w8 = jnp.pad(conv_w[layer], ((0, 8 - CONV_W), (0, 0)))
        gq_g, gk_g, gv_g, gates = _gdnprep(p_gdn, conv_w8, ab, _pad_lanes(a_log[layer]),
                                           _pad_lanes(dt_bias[layer]), n_lat_blocks)
        qg, w_g, kd, u_g, aq, eg = _gdnchunk(gq_g, gk_g, gv_g, gates)
        o_f, o_b = _gdnscan(qg, w_g, kd, u_g, aq, eg, L // GDN_CHUNK, Lc // GDN_CHUNK)

        rw = jnp.pad(router_w[layer], ((0, 0), (0, LANES - N_EXPERTS)))
        rb = jnp.pad(router_b[layer].reshape(1, N_EXPERTS).astype(F32), ((0, 0), (0, LANES - N_EXPERTS)),
                     constant_values=-1e30)
        h_new, v_ffn, top_e, top_g = _mixout(
            o_att, o_f, o_b, z, h, modtab, gdn_norm_g[layer].reshape(1, LANES), w_out[layer].astype(BF16),
            norm_ffn_g[layer].reshape(1, D), rw, rb, n_lat_blocks)

        h = _moe(v_ffn, top_e, top_g, h_new, modtab,
                 w_gate_up, b_gate_up.reshape(depth, N_EXPERTS, 1, 2 * D_EXPERT),
                 w_down, b_down.reshape(depth, N_EXPERTS, 1, D), layer, n_lat_blocks)
    return h[:, :L]
```

```python
import functools
import math

import jax
import jax.numpy as jnp
from jax import lax
from jax.experimental import pallas as pl
from jax.experimental.pallas import tpu as pltpu

F32 = jnp.float32
BF16 = jnp.bfloat16
HIGHEST = lax.Precision.HIGHEST

D_MODEL = 1024
GRID_W = 64
EPS = 1e-6
ATT_WIDTH = 512
ATT_HEAD_DIM = 64
ATT_HEADS = 4
ROPE_BASE = 10000.0
ROPE_PAIRS = ATT_HEAD_DIM // 4
GDN_WIDTH = 512
GDN_HEAD_DIM = 128
GDN_HEADS = 4
GDN_CHUNK = 64
CONV_W = 5
IN_MAIN = 3 * ATT_WIDTH + 4 * GDN_WIDTH
N_EXPERTS = 32
TOP_K = 4
D_EXPERT = 1024
SWIGLU_ALPHA = 1.702
SWIGLU_LIMIT = 7.0
MOE_BLOCK = 512
COMBINE_TILE = 256
COMBINE_CHUNK = 16
COMBINE_MAX_CHUNKS = COMBINE_TILE * TOP_K // COMBINE_CHUNK + 2 * N_EXPERTS

LANES = 128
ROW_TILE = 256
KV_CHUNK = 256
Q_SCALE = ATT_HEAD_DIM ** -0.5 * math.log2(math.e)
VMEM_LIMIT = 48 * 1024 * 1024
EXPERT_VMEM_LIMIT = 56 * 1024 * 1024


def _cparams(sem):
    return pltpu.CompilerParams(dimension_semantics=sem, vmem_limit_bytes=VMEM_LIMIT)


def _dot(a, b):
    return jnp.dot(a, b, preferred_element_type=F32)


def _dot_nt(a, b):
    return lax.dot_general(a, b, (((1,), (1,)), ((), ())), preferred_element_type=F32)


def _dot_tn(a, b):
    return lax.dot_general(a, b, (((0,), (0,)), ((), ())), preferred_element_type=F32)


def _dot_hi(a, b):
    return jnp.dot(a, b, preferred_element_type=F32, precision=HIGHEST)


def _sigmoid(x):
    return 1.0 / (1.0 + jnp.exp(-x))


def _adaln_kernel(c_ref, w_ref, b_ref, o_ref):
    c = c_ref[...]
    s = c * _sigmoid(c)
    o_ref[...] = _dot_hi(s, w_ref[0]) + b_ref[0]


def _adaln(c_rows, w, b, layer):
    depth, _, n = w.shape
    tn = 1024
    return pl.pallas_call(
        _adaln_kernel,
        out_shape=jax.ShapeDtypeStruct((8, n), F32),
        grid=(n // tn,),
        in_specs=[pl.BlockSpec((8, D_MODEL), lambda j: (0, 0)),
                  pl.BlockSpec((1, D_MODEL, tn), lambda j: (layer, 0, j)),
                  pl.BlockSpec((1, 1, tn), lambda j: (layer, 0, j))],
        out_specs=pl.BlockSpec((8, tn), lambda j: (0, j)),
        compiler_params=_cparams(("arbitrary",)),
        name="adaln",
    )(c_rows, w, b.reshape(depth, 1, n))


def _inproj_kernel(h_ref, mod_ref, g_ref, w_ref, wab_ref, qk_ref, v_ref, gdn_ref, z_ref, ab_ref):
    x = h_ref[0]
    ms = jnp.mean(x * x, axis=-1, keepdims=True)
    y = x * lax.rsqrt(ms + EPS) * g_ref[...]
    shift = mod_ref[0, 0, 0:1, :]
    scale = mod_ref[0, 0, 1:2, :]
    u = (y * (1.0 + scale) + shift).astype(BF16)
    qk_ref[0] = _dot(u, w_ref[:, 0:2 * ATT_WIDTH])
    vv = _dot(u, w_ref[:, 2 * ATT_WIDTH:3 * ATT_WIDTH])
    for hh in range(ATT_HEADS):
        v_ref[0, hh] = vv[:, hh * LANES:(hh + 1) * LANES].T.astype(BF16)
    off = 3 * ATT_WIDTH
    gdn_ref[0] = _dot(u, w_ref[:, off:off + 3 * GDN_WIDTH])
    z_ref[0] = _dot(u, w_ref[:, off + 3 * GDN_WIDTH:off + 4 * GDN_WIDTH])
    ab_ref[0] = _dot(u, wab_ref[...])


def _inproj(h, modtab, g, w_main, w_ab, n_lat_blocks):
    B, S, _ = h.shape
    tm = ROW_TILE
    row = lambda b, i: (b, i, 0)
    return pl.pallas_call(
        _inproj_kernel,
        out_shape=(jax.ShapeDtypeStruct((B, S, 2 * ATT_WIDTH), F32),
                   jax.ShapeDtypeStruct((B, ATT_HEADS, LANES, S), BF16),
                   jax.ShapeDtypeStruct((B, S, 3 * GDN_WIDTH), F32),
                   jax.ShapeDtypeStruct((B, S, GDN_WIDTH), F32),
                   jax.ShapeDtypeStruct((B, S, LANES), F32)),
        grid=(B, S // tm),
        in_specs=[pl.BlockSpec((1, tm, D_MODEL), row),
                  pl.BlockSpec((1, 1, 8, D_MODEL), lambda b, i: (b, (i >= n_lat_blocks).astype(jnp.int32), 0, 0)),
                  pl.BlockSpec((1, D_MODEL), lambda b, i: (0, 0)),
                  pl.BlockSpec((D_MODEL, IN_MAIN), lambda b, i: (0, 0)),
                  pl.BlockSpec((D_MODEL, LANES), lambda b, i: (0, 0))],
        out_specs=(pl.BlockSpec((1, tm, 2 * ATT_WIDTH), row),
                   pl.BlockSpec((1, ATT_HEADS, LANES, tm), lambda b, i: (b, 0, 0, i)),
                   pl.BlockSpec((1, tm, 3 * GDN_WIDTH), row),
                   pl.BlockSpec((1, tm, GDN_WIDTH), row),
                   pl.BlockSpec((1, tm, LANES), row)),
        compiler_params=_cparams(("parallel", "parallel")),
        name="inproj",
    )(h, modtab, g, w_main, w_ab)


def _qkprep_kernel(p_ref, cos_ref, sin_ref, gq_ref, gk_ref, qz_ref, k_ref):
    tm = p_ref.shape[1]
    lane = lax.broadcasted_iota(jnp.int32, (tm, LANES), 1)
    hi16 = (lane & 16) != 0
    first = lax.broadcasted_iota(jnp.int32, (LANES, tm), 0) < ATT_HEAD_DIM
    r = lax.broadcasted_iota(jnp.int32, (LANES, LANES), 0) >> 6
    c = lax.broadcasted_iota(jnp.int32, (LANES, LANES), 1) >> 6
    gmat = jnp.where(r == c, 1.0 / ATT_HEAD_DIM, 0.0).astype(F32)
    cosv = cos_ref[...]
    sinv = sin_ref[...]
    for j in range(2 * ATT_HEADS):
        x = p_ref[0, :, j * LANES:(j + 1) * LANES]
        ms = _dot_hi(x * x, gmat)
        g = gq_ref[...] if j < ATT_HEADS else gk_ref[...]
        y = x * lax.rsqrt(ms + EPS) * g
        sw = jnp.where(hi16, pltpu.roll(y, 16, 1), pltpu.roll(y, LANES - 16, 1))
        y = y * cosv + sw * sinv
        if j < ATT_HEADS:
            yt = (y * Q_SCALE).T
            qz_ref[0, j, 0] = jnp.where(first, yt, 0.0).astype(BF16)
            qz_ref[0, j, 1] = jnp.where(first, 0.0, yt).astype(BF16)
        else:
            k_ref[0, j - ATT_HEADS] = y.astype(BF16)


def _qkprep(p_qk, cos_t, sin_t, gq, gk):
    B, S, _ = p_qk.shape
    tm = ROW_TILE
    return pl.pallas_call(
        _qkprep_kernel,
        out_shape=(jax.ShapeDtypeStruct((B, ATT_HEADS, 2, LANES, S), BF16),
                   jax.ShapeDtypeStruct((B, ATT_HEADS, S, LANES), BF16)),
        grid=(B, S // tm),
        in_specs=[pl.BlockSpec((1, tm, 2 * ATT_WIDTH), lambda b, i: (b, i, 0)),
                  pl.BlockSpec((tm, LANES), lambda b, i: (i, 0)),
                  pl.BlockSpec((tm, LANES), lambda b, i: (i, 0)),
                  pl.BlockSpec((1, LANES), lambda b, i: (0, 0)),
                  pl.BlockSpec((1, LANES), lambda b, i: (0, 0))],
        out_specs=(pl.BlockSpec((1, ATT_HEADS, 2, LANES, tm), lambda b, i: (b, 0, 0, 0, i)),
                   pl.BlockSpec((1, ATT_HEADS, tm, LANES), lambda b, i: (b, 0, i, 0))),
        compiler_params=_cparams(("parallel", "parallel")),
        name="qkprep",
    )(p_qk, cos_t, sin_t, gq, gk)


def _attn_kernel(lam_ref, qz_ref, k_ref, v_ref, g_ref, *rest, aliased, tk, n_kv):
    if aliased:
        rest = rest[1:]
    o_ref, s_sc, mt_sc, m_sc, l_sc, acc_sc = rest
    m_sc[...] = jnp.full(m_sc.shape, -jnp.inf, F32)
    l_sc[...] = jnp.zeros(l_sc.shape, F32)
    acc_sc[...] = jnp.zeros(acc_sc.shape, F32)

    def rows(j):
        if isinstance(j, int):
            return pl.ds(j * tk, tk)
        return pl.ds(pl.multiple_of(j * tk, tk), tk)

    def qk_tile(j, slot, ps=(0, 1)):
        kt = k_ref[0, 0, rows(j), :]
        for p in ps:
            s = _dot(kt, qz_ref[0, 0, p])
            s_sc[slot, p] = s
            mt_sc[slot, p] = jnp.max(s, axis=0, keepdims=True)

    def pv_tile(j, slot, ps=(0, 1)):
        for p in ps:
            m_prev = m_sc[p]
            m_new = jnp.maximum(m_prev, mt_sc[slot, p])
            alpha = jnp.exp2(m_prev - m_new)
            lsum = None
            acc = None
            for c in range(tk // KV_CHUNK):
                cs = slice(c * KV_CHUNK, (c + 1) * KV_CHUNK)
                pe = jnp.exp2(s_sc[slot, p, cs, :] - m_new)
                ps = jnp.sum(pe, axis=0, keepdims=True)
                pv = _dot(v_ref[0, 0, j, :, cs], pe.astype(BF16))
                lsum = ps if lsum is None else lsum + ps
                acc = pv if acc is None else acc + pv
            l_sc[p] = alpha * l_sc[p] + lsum
            acc_sc[p] = alpha * acc_sc[p] + acc
            m_sc[p] = m_new

    def fused_tile(jq, slot_q, jp, slot_p):
        m_new = [jnp.maximum(m_sc[p], mt_sc[slot_p, p]) for p in range(2)]
        alpha = [jnp.exp2(m_sc[p] - m_new[p]) for p in range(2)]
        mx, lsum, acc = [None, None], [None, None], [None, None]
        for c in range(tk // KV_CHUNK):
            cs = slice(c * KV_CHUNK, (c + 1) * KV_CHUNK)
            if isinstance(jq, int):
                kr = pl.ds(jq * tk + c * KV_CHUNK, KV_CHUNK)
            else:
                kr = pl.ds(pl.multiple_of(jq * tk + c * KV_CHUNK, KV_CHUNK), KV_CHUNK)
            kc = k_ref[0, 0, kr, :]
            vc = v_ref[0, 0, jp, :, cs]
            for p in range(2):
                s = _dot(kc, qz_ref[0, 0, p])
                s_sc[slot_q, p, cs, :] = s
                cm = jnp.max(s, axis=0, keepdims=True)
                mx[p] = cm if mx[p] is None else jnp.maximum(mx[p], cm)
                pe = jnp.exp2(s_sc[slot_p, p, cs, :] - m_new[p])
                ps = jnp.sum(pe, axis=0, keepdims=True)
                pv = _dot(vc, pe.astype(BF16))
                lsum[p] = ps if lsum[p] is None else lsum[p] + ps
                acc[p] = pv if acc[p] is None else acc[p] + pv
        for p in range(2):
            mt_sc[slot_q, p] = mx[p]
            l_sc[p] = alpha[p] * l_sc[p] + lsum[p]
            acc_sc[p] = alpha[p] * acc_sc[p] + acc[p]
            m_sc[p] = m_new[p]

    qk_tile(0, 0)
    n_pairs = (n_kv - 1) // 2

    def pair(jj, carry):
        j = 2 * jj
        fused_tile(j + 1, 1, j, 0)
        fused_tile(j + 2, 0, j + 1, 1)
        return carry

    if n_pairs > 0:
        lax.fori_loop(0, n_pairs, pair, 0)
    j = 2 * n_pairs
    if j == n_kv - 1:
        pv_tile(j, 0)
    else:
        qk_tile(j + 1, 1)
        pv_tile(j, 0)
        pv_tile(j + 1, 1)

    o = acc_sc[0] / l_sc[0] - lam_ref[0] * (acc_sc[1] / l_sc[1])
    ms = jnp.mean(o * o, axis=0, keepdims=True)
    o_ref[0] = ((o * lax.rsqrt(ms + EPS)).T * g_ref[...]).astype(BF16)


def _attention(lam, qz, k, vt, g, *, tq, tk, q_blk0, n_q, kv_blk0, n_kv, prev_out=None, name="attn_ctx"):
    B, H, _, _, S = qz.shape
    aliased = prev_out is not None
    skv = tk * n_kv
    in_specs = [pl.BlockSpec(memory_space=pltpu.SMEM),
                pl.BlockSpec((1, 1, 2, LANES, tq), lambda b, h, i: (b, h, 0, 0, q_blk0 + i)),
                pl.BlockSpec((1, 1, skv, LANES), lambda b, h, i: (b, h, kv_blk0, 0)),
                pl.BlockSpec((1, 1, n_kv, LANES, tk), lambda b, h, i: (b, h, 0, 0, 0)),
                pl.BlockSpec((1, LANES), lambda b, h, i: (0, 0))]
    args = [lam, qz, k, vt, g]
    aliases = {}
    if aliased:
        in_specs.append(pl.BlockSpec(memory_space=pl.ANY))
        args.append(prev_out)
        aliases = {5: 0}
    return pl.pallas_call(
        functools.partial(_attn_kernel, aliased=aliased, tk=tk, n_kv=n_kv),
        out_shape=jax.ShapeDtypeStruct((B, S, ATT_WIDTH), BF16),
        grid=(B, H, n_q),
        in_specs=in_specs,
        out_specs=pl.BlockSpec((1, tq, LANES), lambda b, h, i: (b, q_blk0 + i, h)),
        scratch_shapes=[pltpu.VMEM((2, 2, tk, tq), F32), pltpu.VMEM((2, 2, 1, tq), F32),
                        pltpu.VMEM((2, 1, tq), F32), pltpu.VMEM((2, 1, tq), F32),
                        pltpu.VMEM((2, LANES, tq), F32)],
        input_output_aliases=aliases,
        compiler_params=_cparams(("parallel", "parallel", "arbitrary")),
        name=name,
    )(*args)


def _attention_lat(lam, qz, k, vt, g, L):
    B, H, _, S = vt.shape
    tq = _pick_tile(L, (512, 256))
    tk = _pick_tile(S, (1280, 640, 256))
    n_kv = S // tk
    vt_tiles = vt.reshape(B, H, LANES, n_kv, tk).transpose(0, 1, 3, 2, 4)
    out0 = jnp.zeros((B, S, ATT_WIDTH), BF16)
    return _attention(lam, qz, k, vt_tiles, g, tq=tq, tk=tk, q_blk0=0, n_q=L // tq, kv_blk0=0, n_kv=n_kv,
                      prev_out=out0, name="attn_lat")


def _attention_ctx(lam, qz, k, vt, g, L, prev_out):
    B, H, _, S = vt.shape
    Lc = S - L
    vt_ctx = vt[:, :, :, L:].reshape(B, H, 1, LANES, Lc)
    return _attention(lam, qz, k, vt_ctx, g, tq=Lc, tk=Lc, q_blk0=L // Lc, n_q=1, kv_blk0=L // Lc, n_kv=1,
                      prev_out=prev_out)


def _gdnprep_kernel(x_ref, prev_ref, next_ref, cw_ref, ab_ref, alog_ref, dtb_ref,
                    q_ref, k_ref, v_ref, gate_ref, ext_sc, *, nb_lat, nb_all):
    i = pl.program_id(1)
    tm = x_ref.shape[1]
    first = jnp.logical_or(i == 0, i == nb_lat)
    last = jnp.logical_or(i == nb_lat - 1, i == nb_all - 1)
    keep_prev = jnp.where(first, 0.0, 1.0)
    keep_next = jnp.where(last, 0.0, 1.0)
    ext_sc[0:8, :] = prev_ref[0] * keep_prev
    ext_sc[8:8 + tm, :] = x_ref[0]
    ext_sc[8 + tm:16 + tm, :] = next_ref[0] * keep_next
    acc = None
    for j in range(CONV_W):
        term = ext_sc[pl.ds(8 + j - CONV_W // 2, tm), :] * cw_ref[j:j + 1, :]
        acc = term if acc is None else acc + term
    y = acc * _sigmoid(acc)
    for part, ref in ((0, q_ref), (1, k_ref)):
        for hh in range(GDN_HEADS):
            lo = part * GDN_WIDTH + hh * GDN_HEAD_DIM
            t = y[:, lo:lo + GDN_HEAD_DIM]
            ref[0, :, hh * GDN_HEAD_DIM:(hh + 1) * GDN_HEAD_DIM] = (
                t * lax.rsqrt(jnp.sum(t * t, axis=-1, keepdims=True) + EPS))
    v_ref[0] = y[:, 2 * GDN_WIDTH:3 * GDN_WIDTH]
    ab = ab_ref[0]
    xs = ab + dtb_ref[...]
    sp = jnp.maximum(xs, 0.0) + jnp.log(1.0 + jnp.exp(-jnp.abs(xs)))
    g = -jnp.exp(alog_ref[...]) * sp
    lane = lax.broadcasted_iota(jnp.int32, ab.shape, 1)
    gate_ref[0] = jnp.where(lane < 2 * GDN_HEADS, g, _sigmoid(ab))


def _gdnprep(p_gdn, conv_w8, ab, alog_row, dtb_row, n_lat_blocks):
    B, S, W = p_gdn.shape
    tm = ROW_TILE
    nb = S // tm
    r8 = tm // 8
    row = lambda b, i: (b, i, 0)
    return pl.pallas_call(
        functools.partial(_gdnprep_kernel, nb_lat=n_lat_blocks, nb_all=nb),
        out_shape=(jax.ShapeDtypeStruct((B, S, GDN_WIDTH), F32),
                   jax.ShapeDtypeStruct((B, S, GDN_WIDTH), F32),
                   jax.ShapeDtypeStruct((B, S, GDN_WIDTH), F32),
                   jax.ShapeDtypeStruct((B, S, LANES), F32)),
        grid=(B, nb),
        in_specs=[pl.BlockSpec((1, tm, W), row),
                  pl.BlockSpec((1, 8, W), lambda b, i: (b, jnp.maximum(i * r8 - 1, 0), 0)),
                  pl.BlockSpec((1, 8, W), lambda b, i: (b, jnp.minimum((i + 1) * r8, nb * r8 - 1), 0)),
                  pl.BlockSpec((8, W), lambda b, i: (0, 0)),
                  pl.BlockSpec((1, tm, LANES), row),
                  pl.BlockSpec((1, LANES), lambda b, i: (0, 0)),
                  pl.BlockSpec((1, LANES), lambda b, i: (0, 0))],
        out_specs=(pl.BlockSpec((1, tm, GDN_WIDTH), row),
                   pl.BlockSpec((1, tm, GDN_WIDTH), row),
                   pl.BlockSpec((1, tm, GDN_WIDTH), row),
                   pl.BlockSpec((1, tm, LANES), row)),
        scratch_shapes=[pltpu.VMEM((tm + 16, W), F32)],
        compiler_params=_cparams(("parallel", "parallel")),
        name="gdnprep",
    )(p_gdn, p_gdn, p_gdn, conv_w8, ab, alog_row, dtb_row)


def _gdnchunk_kernel(q_ref, k_ref, v_ref, gate_ref, qg_ref, w_ref, kd_ref, u_ref, aq_ref, eg_ref):
    tm = q_ref.shape[1]
    nc = tm // GDN_CHUNK
    gates = gate_ref[0]
    ri = lax.broadcasted_iota(jnp.int32, (tm, tm), 0)
    ci = lax.broadcasted_iota(jnp.int32, (tm, tm), 1)
    same = (ri >> 6) == (ci >> 6)
    eye = ri == ci
    tot = _dot_hi(jnp.where(same, 1.0, 0.0).astype(F32), gates)
    eye_f = jnp.where(eye, 1.0, 0.0).astype(F32)

    heads = []
    for hh in range(GDN_HEADS):
        sl = slice(hh * GDN_HEAD_DIM, (hh + 1) * GDN_HEAD_DIM)
        k = k_ref[0, :, sl]
        k16 = k.astype(BF16)
        q = q_ref[0, :, sl] * (GDN_HEAD_DIM ** -0.5)
        heads.append((sl, k, q, _dot_nt(k16, k16), _dot_nt(q.astype(BF16), k16)))

    chains = []
    for d in range(2):
        incl = jnp.logical_and(same, (ci <= ri) if d == 0 else (ci >= ri))
        strict = jnp.logical_and(incl, jnp.logical_not(eye))
        gc = _dot_hi(jnp.where(incl, 1.0, 0.0).astype(F32), gates)
        gc_t = gc.T
        for hh in range(GDN_HEADS):
            ln = d * GDN_HEADS + hh
            gcol = gc[:, ln:ln + 1]
            bcol = gates[:, 2 * GDN_HEADS + ln:2 * GDN_HEADS + ln + 1]
            tcol = tot[:, ln:ln + 1]
            diff = gcol - gc_t[ln:ln + 1, :]
            decay = jnp.where(incl, jnp.exp(jnp.where(incl, diff, 0.0)), 0.0)
            a = jnp.where(strict, heads[hh][3] * bcol * decay, 0.0)
            aqk = heads[hh][4] * decay
            chains.append(dict(d=d, hh=hh, x=-a, t=eye_f - a, aqk=aqk, gcol=gcol, bcol=bcol, tcol=tcol))

    for _ in range(5):
        for ch in chains:
            x16 = ch["x"].astype(BF16)
            ch["x"] = _dot(x16, x16)
        for ch in chains:
            ch["t"] = ch["t"] + _dot(ch["t"].astype(BF16), ch["x"].astype(BF16))

    for ch in chains:
        d, hh, gcol, bcol, tcol = ch["d"], ch["hh"], ch["gcol"], ch["bcol"], ch["tcol"]
        sl, k, q = heads[hh][0], heads[hh][1], heads[hh][2]
        t16 = ch["t"].astype(BF16)
        egc = jnp.exp(gcol)
        kb = k * bcol
        u_ref[0, d, :, sl] = _dot(t16, (v_ref[0, :, sl] * bcol).astype(BF16))
        w_ref[0, d, :, sl] = _dot(t16, (kb * egc).astype(BF16)).astype(BF16)
        kd_ref[0, d, :, sl] = (k * jnp.exp(tcol - gcol)).astype(BF16)
        qg_ref[0, d, :, sl] = (q * egc).astype(BF16)
        for cc in range(nc):
            rs = slice(cc * GDN_CHUNK, (cc + 1) * GDN_CHUNK)
            aq_ref[0, d, rs, hh * GDN_CHUNK:(hh + 1) * GDN_CHUNK] = ch["aqk"][rs, rs].astype(BF16)
            eg_ref[0, d, cc, hh:hh + 1, :] = jnp.broadcast_to(
                jnp.exp(tcol[cc * GDN_CHUNK:cc * GDN_CHUNK + 1, :]), (1, LANES))


def _gdnchunk(gq, gk, gv, gates):
    B, S, W = gq.shape
    tm = ROW_TILE
    nc = tm // GDN_CHUNK
    row = lambda b, i: (b, i, 0)
    drow = lambda b, i: (b, 0, i, 0)
    big = lambda dt: jax.ShapeDtypeStruct((B, 2, S, W), dt)
    return pl.pallas_call(
        _gdnchunk_kernel,
        out_shape=(big(BF16), big(BF16), big(BF16), big(F32),
                   jax.ShapeDtypeStruct((B, 2, S, GDN_HEADS * GDN_CHUNK), BF16),
                   jax.ShapeDtypeStruct((B, 2, S // GDN_CHUNK, GDN_HEADS, LANES), F32)),
        grid=(B, S // tm),
        in_specs=[pl.BlockSpec((1, tm, W), row), pl.BlockSpec((1, tm, W), row),
                  pl.BlockSpec((1, tm, W), row), pl.BlockSpec((1, tm, LANES), row)],
        out_specs=(pl.BlockSpec((1, 2, tm, W), drow), pl.BlockSpec((1, 2, tm, W), drow),
                   pl.BlockSpec((1, 2, tm, W), drow), pl.BlockSpec((1, 2, tm, W), drow),
                   pl.BlockSpec((1, 2, tm, GDN_HEADS * GDN_CHUNK), drow),
                   pl.BlockSpec((1, 2, nc, GDN_HEADS, LANES), lambda b, i: (b, 0, i, 0, 0))),
        compiler_params=_cparams(("parallel", "parallel")),
        name="gdnchunk",
    )(gq, gk, gv, gates)


def _gdnscan_kernel(qg0, w0, kd0, u0, aq0, eg0, qg1, w1, kd1, u1, aq1, eg1, of_ref, ob_ref, s_sc):
    @pl.when(pl.program_id(1) == 0)
    def _():
        s_sc[...] = jnp.zeros(s_sc.shape, F32)

    dirs = ((qg0, w0, kd0, u0, aq0, eg0, of_ref), (qg1, w1, kd1, u1, aq1, eg1, ob_ref))
    chains = [(d, hh) for d in range(2) for hh in range(GDN_HEADS)]
    st, st16, vn16, qs = {}, {}, {}, {}
    for d, hh in chains:
        st[d, hh] = s_sc[d, hh]
        st16[d, hh] = st[d, hh].astype(BF16)
    for d, hh in chains:
        qg, w, kd, u, aq, eg, o_ref = dirs[d]
        sl = slice(hh * GDN_HEAD_DIM, (hh + 1) * GDN_HEAD_DIM)
        vn16[d, hh] = (u[0, 0, :, sl] - _dot(w[0, 0, :, sl], st16[d, hh])).astype(BF16)
        qs[d, hh] = _dot(qg[0, 0, :, sl], st16[d, hh])
    for d, hh in chains:
        qg, w, kd, u, aq, eg, o_ref = dirs[d]
        sl = slice(hh * GDN_HEAD_DIM, (hh + 1) * GDN_HEAD_DIM)
        o_ref[0, :, sl] = qs[d, hh] + _dot(aq[0, 0, :, hh * GDN_CHUNK:(hh + 1) * GDN_CHUNK], vn16[d, hh])
        s_sc[d, hh] = st[d, hh] * eg[0, 0, 0, hh:hh + 1, :] + _dot_tn(kd[0, 0, :, sl], vn16[d, hh])


def _gdnscan(qg, w, kd, u, aq, eg, n_lat_chunks, n_ctx_chunks):
    B, _, S, W = qg.shape
    C = GDN_CHUNK
    n = S // C

    def fwd_chunk(i):
        return jnp.where(i < n_ctx_chunks, n_lat_chunks + i, i - n_ctx_chunks)

    def bwd_chunk(i):
        return jnp.where(i < n_ctx_chunks, n_lat_chunks + n_ctx_chunks - 1 - i, n - 1 - i)

    def specs(d, chunk_of):
        big = pl.BlockSpec((1, 1, C, W), lambda b, i: (b, d, chunk_of(i), 0))
        return [big, big, big, big,
                pl.BlockSpec((1, 1, C, GDN_HEADS * C), lambda b, i: (b, d, chunk_of(i), 0)),
                pl.BlockSpec((1, 1, 1, GDN_HEADS, LANES), lambda b, i: (b, d, chunk_of(i), 0, 0))]

    return pl.pallas_call(
        _gdnscan_kernel,
        out_shape=(jax.ShapeDtypeStruct((B, S, W), F32), jax.ShapeDtypeStruct((B, S, W), F32)),
        grid=(B, n),
        in_specs=specs(0, fwd_chunk) + specs(1, bwd_chunk),
        out_specs=(pl.BlockSpec((1, C, W), lambda b, i: (b, fwd_chunk(i), 0)),
                   pl.BlockSpec((1, C, W), lambda b, i: (b, bwd_chunk(i), 0))),
        scratch_shapes=[pltpu.VMEM((2, GDN_HEADS, GDN_HEAD_DIM, GDN_HEAD_DIM), F32)],
        compiler_params=_cparams(("parallel", "arbitrary")),
        name="gdnscan",
    )(qg, w, kd, u, aq, eg, qg, w, kd, u, aq, eg)


def _mixout_kernel(oa_ref, of_ref, ob_ref, z_ref, h_ref, mod_ref, gg_ref, wo_ref, gf_ref, rw_ref, rb_ref,
                   hn_ref, v_ref, te_ref, tg_ref):
    og = of_ref[0] + ob_ref[0]
    z = z_ref[0]
    parts = [oa_ref[0]]
    for hh in range(GDN_HEADS):
        sl = slice(hh * GDN_HEAD_DIM, (hh + 1) * GDN_HEAD_DIM)
        t = og[:, sl]
        t = t * lax.rsqrt(jnp.mean(t * t, axis=-1, keepdims=True) + EPS) * gg_ref[...]
        zz = z[:, sl]
        parts.append((t * (zz * _sigmoid(zz))).astype(BF16))
    mix_in = jnp.concatenate(parts, axis=-1)
    mix = _dot(mix_in, wo_ref[...])
    hn = h_ref[0] + mod_ref[0, 0, 2:3, :] * mix
    hn_ref[0] = hn
    y = hn * lax.rsqrt(jnp.mean(hn * hn, axis=-1, keepdims=True) + EPS) * gf_ref[...]
    v = y * (1.0 + mod_ref[0, 0, 4:5, :]) + mod_ref[0, 0, 3:4, :]
    v_ref[0] = v.astype(BF16)
    logits = _dot_hi(v, rw_ref[...]) + rb_ref[...]
    lane = lax.broadcasted_iota(jnp.int32, logits.shape, 1)
    cur = logits
    vals, idxs = [], []
    for _ in range(TOP_K):
        m = jnp.max(cur, axis=-1, keepdims=True)
        idx = jnp.min(jnp.where(cur == m, lane, LANES), axis=-1, keepdims=True)
        vals.append(m)
        idxs.append(idx)
        cur = jnp.where(lane == idx, -jnp.inf, cur)
    es = [jnp.exp(vv - vals[0]) for vv in vals]
    inv = 1.0 / (es[0] + es[1] + es[2] + es[3])
    te = jnp.zeros(logits.shape, jnp.int32)
    tg = jnp.zeros(logits.shape, F32)
    for kk in range(TOP_K):
        te = jnp.where(lane == kk, idxs[kk], te)
        tg = jnp.where(lane == kk, es[kk] * inv, tg)
    te_ref[0] = te
    tg_ref[0] = tg


def _mixout(o_att, o_f, o_b, z, h, modtab, gg, w_out, gf, rw, rb, n_lat_blocks):
    B, S, _ = h.shape
    tm = ROW_TILE
    row = lambda b, i: (b, i, 0)
    const = lambda b, i: (0, 0)
    return pl.pallas_call(
        _mixout_kernel,
        out_shape=(jax.ShapeDtypeStruct((B, S, D_MODEL), F32),
                   jax.ShapeDtypeStruct((B, S, D_MODEL), BF16),
                   jax.ShapeDtypeStruct((B, S, LANES), jnp.int32),
                   jax.ShapeDtypeStruct((B, S, LANES), F32)),
        grid=(B, S // tm),
        in_specs=[pl.BlockSpec((1, tm, ATT_WIDTH), row),
                  pl.BlockSpec((1, tm, GDN_WIDTH), row),
                  pl.BlockSpec((1, tm, GDN_WIDTH), row),
                  pl.BlockSpec((1, tm, GDN_WIDTH), row),
                  pl.BlockSpec((1, tm, D_MODEL), row),
                  pl.BlockSpec((1, 1, 8, D_MODEL), lambda b, i: (b, (i >= n_lat_blocks).astype(jnp.int32), 0, 0)),
                  pl.BlockSpec((1, LANES), const),
                  pl.BlockSpec((D_MODEL, D_MODEL), const),
                  pl.BlockSpec((1, D_MODEL), const),
                  pl.BlockSpec((D_MODEL, LANES), const),
                  pl.BlockSpec((1, LANES), const)],
        out_specs=(pl.BlockSpec((1, tm, D_MODEL), row),
                   pl.BlockSpec((1, tm, D_MODEL), row),
                   pl.BlockSpec((1, tm, LANES), row),
                   pl.BlockSpec((1, tm, LANES), row)),
        compiler_params=_cparams(("parallel", "parallel")),
        name="mixout",
    )(o_att, o_f, o_b, z, h, modtab, gg, w_out, gf, rw, rb)


def _expert_kernel(be_ref, nv_ref, x_ref, wgu_ref, bgu_ref, wd_ref, bd_ref, sg_ref, y_ref, wgu_sc, wd_sc):
    i = pl.program_id(0)
    new_expert = jnp.logical_or(i == 0, be_ref[i] != be_ref[jnp.maximum(i - 1, 0)])

    @pl.when(new_expert)
    def _():
        wgu_sc[...] = wgu_ref[0, 0].astype(BF16)
        wd_sc[...] = wd_ref[0, 0].astype(BF16)

    @pl.when(nv_ref[i] > 0)
    def _():
        gu = _dot(x_ref[...], wgu_sc[...]) + bgu_ref[0, 0]
        g_ = jnp.minimum(gu[:, :D_EXPERT], SWIGLU_LIMIT)
        up = jnp.clip(gu[:, D_EXPERT:], -SWIGLU_LIMIT, SWIGLU_LIMIT)
        glu = g_ * _sigmoid(SWIGLU_ALPHA * g_)
        act = ((up + 1.0) * glu).astype(BF16)
        y_ref[...] = ((_dot(act, wd_sc[...]) + bd_ref[0, 0]) * sg_ref[...]).astype(y_ref.dtype)

    @pl.when(nv_ref[i] == 0)
    def _():
        y_ref[...] = jnp.zeros(y_ref.shape, y_ref.dtype)


def _experts(block_e, n_valid, x_sorted, wgu, bgu, wd, bd, slot_gate, layer):
    n_slots = x_sorted.shape[0]
    nb = n_slots // MOE_BLOCK
    grid_spec = pltpu.PrefetchScalarGridSpec(
        num_scalar_prefetch=2,
        grid=(nb,),
        in_specs=[pl.BlockSpec((MOE_BLOCK, D_MODEL), lambda i, be, nv: (i, 0)),
                  pl.BlockSpec((1, 1, D_MODEL, 2 * D_EXPERT), lambda i, be, nv: (layer, be[i], 0, 0)),
                  pl.BlockSpec((1, 1, 1, 2 * D_EXPERT), lambda i, be, nv: (layer, be[i], 0, 0)),
                  pl.BlockSpec((1, 1, D_EXPERT, D_MODEL), lambda i, be, nv: (layer, be[i], 0, 0)),
                  pl.BlockSpec((1, 1, 1, D_MODEL), lambda i, be, nv: (layer, be[i], 0, 0)),
                  pl.BlockSpec((MOE_BLOCK, 1), lambda i, be, nv: (i, 0))],
        out_specs=pl.BlockSpec((MOE_BLOCK, D_MODEL), lambda i, be, nv: (i, 0)),
        scratch_shapes=[pltpu.VMEM((D_MODEL, 2 * D_EXPERT), BF16), pltpu.VMEM((D_EXPERT, D_MODEL), BF16)],
    )
    return pl.pallas_call(
        _expert_kernel,
        out_shape=jax.ShapeDtypeStruct((n_slots, D_MODEL), BF16),
        grid_spec=grid_spec,
        compiler_params=pltpu.CompilerParams(dimension_semantics=("arbitrary",),
                                             vmem_limit_bytes=EXPERT_VMEM_LIMIT),
        name="experts",
    )(block_e, n_valid, x_sorted, wgu, bgu, wd, bd, slot_gate)


def _moe_plan(top_e, top_g):
    T = top_e.shape[0]
    n_assign = T * TOP_K
    n_blocks = -(-n_assign // MOE_BLOCK) + N_EXPERTS
    n_slots = n_blocks * MOE_BLOCK
    i32 = jnp.int32
    flat_e = top_e.reshape(-1)
    gate_flat = top_g.reshape(-1)
    order = jnp.argsort(flat_e, stable=True).astype(i32)
    e_ids = jnp.arange(N_EXPERTS, dtype=i32)
    is_e = flat_e[:, None] == e_ids[None, :]
    counts = jnp.sum(is_e, axis=0, dtype=i32)
    start = jnp.cumsum(counts) - counts
    padded = (counts + MOE_BLOCK - 1) // MOE_BLOCK * MOE_BLOCK
    pad_end = jnp.cumsum(padded)
    pad_start = pad_end - padded
    blk0 = jnp.arange(n_blocks, dtype=i32) * MOE_BLOCK
    block_e = jnp.minimum(jnp.sum(pad_end[None, :] <= blk0[:, None], axis=1, dtype=i32), N_EXPERTS - 1)
    off = (blk0 - pad_start[block_e])[:, None] + jnp.arange(MOE_BLOCK, dtype=i32)[None, :]
    valid = off < counts[block_e][:, None]
    a_slot = order[jnp.clip(off + start[block_e][:, None], 0, n_assign - 1).reshape(-1)]
    valid_flat = valid.reshape(-1)
    slot_tok = jnp.where(valid_flat, a_slot // TOP_K, jnp.arange(n_slots, dtype=i32) % T)
    slot_gate = jnp.where(valid_flat, gate_flat[a_slot], 0.0)
    n_valid = jnp.sum(valid, axis=1, dtype=i32)

    n_tiles = T // COMBINE_TILE
    cnt = jnp.sum(top_e.reshape(n_tiles, COMBINE_TILE * TOP_K)[:, :, None] == e_ids[None, None, :], axis=1, dtype=i32)
    run_start = pad_start[None, :] + jnp.cumsum(cnt, axis=0) - cnt
    q_first = run_start // COMBINE_CHUNK
    n_ch = jnp.where(cnt > 0, (run_start + cnt - 1) // COMBINE_CHUNK - q_first + 1, 0)
    ch_end = jnp.cumsum(n_ch, axis=1)
    ch_off = ch_end - n_ch
    j = jnp.arange(COMBINE_MAX_CHUNKS, dtype=i32)
    e_j = jnp.minimum(jnp.sum(ch_end[:, None, :] <= j[None, :, None], axis=2, dtype=i32), N_EXPERTS - 1)
    used = j[None, :] < ch_end[:, -1:]
    chunk_id = jnp.where(used, jnp.take_along_axis(q_first, e_j, axis=1)
                         + j[None, :] - jnp.take_along_axis(ch_off, e_j, axis=1), 0)
    slot_tokid = jnp.where(valid_flat, a_slot // TOP_K, -1).reshape(n_slots // COMBINE_CHUNK, COMBINE_CHUNK)
    row_tok = jnp.where(used[:, :, None], slot_tokid[chunk_id], -1)
    row_tok = row_tok.reshape(n_tiles, 1, COMBINE_MAX_CHUNKS * COMBINE_CHUNK)
    return slot_tok, slot_gate.reshape(n_slots, 1), block_e, n_valid, chunk_id, row_tok


def _combine_kernel(cid_ref, tok_ref, h_ref, mod_ref, y_hbm, o_ref, ybuf, sem):
    i = pl.program_id(0)
    n = pl.num_programs(0)
    slot = lax.rem(i, 2)

    def chunk_copy(c, s, jj):
        return pltpu.make_async_copy(
            y_hbm.at[pl.ds(pl.multiple_of(c * COMBINE_CHUNK, COMBINE_CHUNK), COMBINE_CHUNK)],
            ybuf.at[s, pl.ds(jj * COMBINE_CHUNK, COMBINE_CHUNK)], sem.at[s])

    def start_tile(t, s):
        for jj in range(COMBINE_MAX_CHUNKS):
            chunk_copy(cid_ref[t, jj], s, jj).start()

    @pl.when(i == 0)
    def _():
        start_tile(0, 0)

    @pl.when(i + 1 < n)
    def _():
        start_tile(i + 1, 1 - slot)

    for jj in range(COMBINE_MAX_CHUNKS):
        chunk_copy(0, slot, jj).wait()

    tok = tok_ref[0]
    t_ids = i * COMBINE_TILE + lax.broadcasted_iota(jnp.int32, (COMBINE_TILE, tok.shape[1]), 0)
    onehot = jnp.where(tok == t_ids, 1.0, 0.0).astype(BF16)
    y = _dot(onehot, ybuf[slot])
    o_ref[...] = h_ref[...] + mod_ref[0, 0, 5:6, :] * y


def _combine(chunk_id, row_tok, y_sorted, h, modtab, n_lat_blocks):
    B, S, _ = h.shape
    T = B * S
    ct = COMBINE_TILE
    tiles_per_batch = S // ct
    n_lat_tiles = n_lat_blocks * ROW_TILE // ct
    n_rows = COMBINE_MAX_CHUNKS * COMBINE_CHUNK
    grid_spec = pltpu.PrefetchScalarGridSpec(
        num_scalar_prefetch=1,
        grid=(T // ct,),
        in_specs=[pl.BlockSpec((1, 1, n_rows), lambda i, cid: (i, 0, 0)),
                  pl.BlockSpec((ct, D_MODEL), lambda i, cid: (i, 0)),
                  pl.BlockSpec((1, 1, 8, D_MODEL),
                               lambda i, cid: (i // tiles_per_batch,
                                               (lax.rem(i, tiles_per_batch) >= n_lat_tiles).astype(jnp.int32), 0, 0)),
                  pl.BlockSpec(memory_space=pl.ANY)],
        out_specs=pl.BlockSpec((ct, D_MODEL), lambda i, cid: (i, 0)),
        scratch_shapes=[pltpu.VMEM((2, n_rows, D_MODEL), BF16), pltpu.SemaphoreType.DMA((2,))],
    )
    out = pl.pallas_call(
        _combine_kernel,
        out_shape=jax.ShapeDtypeStruct((T, D_MODEL), F32),
        grid_spec=grid_spec,
        compiler_params=_cparams(("arbitrary",)),
        name="combine",
    )(chunk_id, row_tok, h.reshape(T, D_MODEL), modtab, y_sorted)
    return out.reshape(B, S, D_MODEL)


def _moe(v_ffn, top_e, top_g, h_new, modtab, wgu, bgu, wd, bd, layer, n_lat_blocks):
    B, S, _ = v_ffn.shape
    T = B * S
    slot_tok, slot_gate, block_e, n_valid, chunk_id, row_tok = _moe_plan(
        top_e.reshape(T, LANES)[:, :TOP_K], top_g.reshape(T, LANES)[:, :TOP_K])
    x_sorted = v_ffn.reshape(T, D_MODEL)[slot_tok]
    y_sorted = _experts(block_e, n_valid, x_sorted, wgu, bgu, wd, bd, slot_gate, layer)
    return _combine(chunk_id, row_tok, y_sorted, h_new, modtab, n_lat_blocks)


def _pick_tile(n, cands):
    for t in cands:
        if n % t == 0:
            return t
    raise ValueError(f"no tile for {n}")


def _rope_tables(L, Lc):
    rows = L // GRID_W
    row = jnp.repeat(jnp.arange(rows, dtype=F32), GRID_W)
    col = (jnp.arange(L, dtype=jnp.int32) % GRID_W).astype(F32)
    inv_freq = ROPE_BASE ** (-jnp.arange(ROPE_PAIRS, dtype=F32) / ROPE_PAIRS)
    ar = row[:, None] * inv_freq
    ac = col[:, None] * inv_freq
    cos64 = jnp.concatenate([jnp.cos(ar), jnp.cos(ar), jnp.cos(ac), jnp.cos(ac)], axis=-1)
    sin64 = jnp.concatenate([-jnp.sin(ar), jnp.sin(ar), -jnp.sin(ac), jnp.sin(ac)], axis=-1)
    cos_t = jnp.concatenate([jnp.tile(cos64, (1, 2)), jnp.ones((Lc, LANES), F32)], axis=0)
    sin_t = jnp.concatenate([jnp.tile(sin64, (1, 2)), jnp.zeros((Lc, LANES), F32)], axis=0)
    return cos_t, sin_t


def _pad_lanes(v):
    v = v.reshape(1, -1).astype(F32)
    return jnp.pad(v, ((0, 0), (0, LANES - v.shape[1])))


def kernel(x, c, ctx, c_ctx, w_mod, b_mod, norm_mix_g, w_in, q_norm_g, k_norm_g, lam_q1, lam_k1, lam_q2, lam_k2, subln_g, conv_w, a_log, dt_bias, gdn_norm_g, w_out, norm_ffn_g, router_w, router_b, w_gate_up, b_gate_up, w_down, b_down):
    B, L, D = x.shape
    Lc = ctx.shape[1]
    S = L + Lc
    depth = w_mod.shape[0]
    tm = ROW_TILE
    n_lat_blocks = L // tm
    cos_t, sin_t = _rope_tables(L, Lc)

    c_rows = jnp.zeros((8, D), F32).at[:B].set(c).at[B].set(c_ctx)
    h = jnp.concatenate([x, ctx], axis=1)

    for layer in range(depth):
        mod = _adaln(c_rows, w_mod, b_mod, layer)
        mod6 = mod.reshape(8, 6, D)
        lat_mod = mod6[:B]
        ctx_mod = jnp.broadcast_to(mod6[B][None], (B, 6, D))
        modtab = jnp.pad(jnp.stack([lat_mod, ctx_mod], axis=1), ((0, 0), (0, 0), (0, 2), (0, 0)))

        lam_init = 0.8 - 0.6 * math.exp(-0.3 * layer)
        lam_full = (jnp.exp(jnp.sum(lam_q1[layer] * lam_k1[layer]))
                    - jnp.exp(jnp.sum(lam_q2[layer] * lam_k2[layer])) + lam_init).reshape(1).astype(F32)

        w_l = w_in[layer]
        w_main = w_l[:, :IN_MAIN].astype(BF16)
        w_ab = jnp.pad(w_l[:, IN_MAIN:], ((0, 0), (0, LANES - (w_l.shape[1] - IN_MAIN)))).astype(BF16)
        p_qk, v_att, p_gdn, z, ab = _inproj(h, modtab, norm_mix_g[layer].reshape(1, D), w_main, w_ab,
                                            n_lat_blocks)

        gq = jnp.tile(q_norm_g[layer].reshape(1, ATT_HEAD_DIM), (1, 2))
        gk = jnp.tile(k_norm_g[layer].reshape(1, ATT_HEAD_DIM), (1, 2))
        qz, k_att = _qkprep(p_qk, cos_t, sin_t, gq, gk)
        g_sub = (subln_g[layer] * (1.0 - lam_init)).reshape(1, LANES).astype(F32)
        o_att = _attention_lat(lam_full, qz, k_att, v_att, g_sub, L)
        o_att = _attention_ctx(lam_full, qz, k_att, v_att, g_sub, L, o_att)

        conv_w8 = jnp.pad(conv_w[layer], ((0, 8 - CONV_W), (0, 0)))
        gq_g, gk_g, gv_g, gates = _gdnprep(p_gdn, conv_w8, ab, _pad_lanes(a_log[layer]),
                                           _pad_lanes(dt_bias[layer]), n_lat_blocks)
        qg, w_g, kd, u_g, aq, eg = _gdnchunk(gq_g, gk_g, gv_g, gates)
        o_f, o_b = _gdnscan(qg, w_g, kd, u_g, aq, eg, L // GDN_CHUNK, Lc // GDN_CHUNK)

        rw = jnp.pad(router_w[layer], ((0, 0), (0, LANES - N_EXPERTS)))
        rb = jnp.pad(router_b[layer].reshape(1, N_EXPERTS).astype(F32), ((0, 0), (0, LANES - N_EXPERTS)),
                     constant_values=-1e30)
        h_new, v_ffn, top_e, top_g = _mixout(
            o_att, o_f, o_b, z, h, modtab, gdn_norm_g[layer].reshape(1, LANES), w_out[layer].astype(BF16),
            norm_ffn_g[layer].reshape(1, D), rw, rb, n_lat_blocks)

        h = _moe(v_ffn, top_e, top_g, h_new, modtab,
                 w_gate_up, b_gate_up.reshape(depth, N_EXPERTS, 1, 2 * D_EXPERT),
                 w_down, b_down.reshape(depth, N_EXPERTS, 1, D), layer, n_lat_blocks)
    return h[:, :L]
```

```python
import functools
import math

import jax
import jax.numpy as jnp
from jax import lax
from jax.experimental import pallas as pl
from jax.experimental.pallas import tpu as pltpu

F32 = jnp.float32
BF16 = jnp.bfloat16
HIGHEST = lax.Precision.HIGHEST

D_MODEL = 1024
GRID_W = 64
EPS = 1e-6
ATT_WIDTH = 512
ATT_HEAD_DIM = 64
ATT_HEADS = 4
ROPE_BASE = 10000.0
ROPE_PAIRS = ATT_HEAD_DIM // 4
GDN_WIDTH = 512
GDN_HEAD_DIM = 128
GDN_HEADS = 4
GDN_CHUNK = 64
CONV_W = 5
IN_MAIN = 3 * ATT_WIDTH + 4 * GDN_WIDTH
N_EXPERTS = 32
TOP_K = 4
D_EXPERT = 1024
SWIGLU_ALPHA = 1.702
SWIGLU_LIMIT = 7.0
MOE_BLOCK = 512
SCAN_GROUP = 4
COMBINE_TILE = 256
COMBINE_CHUNK = 16
COMBINE_MAX_CHUNKS = COMBINE_TILE * TOP_K // COMBINE_CHUNK + 2 * N_EXPERTS

LANES = 128
ROW_TILE = 256
KV_CHUNK = 256
Q_SCALE = ATT_HEAD_DIM ** -0.5 * math.log2(math.e)
VMEM_LIMIT = 48 * 1024 * 1024
EXPERT_VMEM_LIMIT = 56 * 1024 * 1024


def _cparams(sem):
    return pltpu.CompilerParams(dimension_semantics=sem, vmem_limit_bytes=VMEM_LIMIT)


def _dot(a, b):
    return jnp.dot(a, b, preferred_element_type=F32)


def _dot_nt(a, b):
    return lax.dot_general(a, b, (((1,), (1,)), ((), ())), preferred_element_type=F32)


def _dot_tn(a, b):
    return lax.dot_general(a, b, (((0,), (0,)), ((), ())), preferred_element_type=F32)


def _dot_hi(a, b):
    return jnp.dot(a, b, preferred_element_type=F32, precision=HIGHEST)


def _sigmoid(x):
    return 1.0 / (1.0 + jnp.exp(-x))


def _adaln_kernel(c_ref, w_ref, b_ref, o_ref):
    c = c_ref[...]
    s = c * _sigmoid(c)
    o_ref[...] = _dot_hi(s, w_ref[0]) + b_ref[0]


def _adaln(c_rows, w, b, layer):
    depth, _, n = w.shape
    tn = 1024
    return pl.pallas_call(
        _adaln_kernel,
        out_shape=jax.ShapeDtypeStruct((8, n), F32),
        grid=(n // tn,),
        in_specs=[pl.BlockSpec((8, D_MODEL), lambda j: (0, 0)),
                  pl.BlockSpec((1, D_MODEL, tn), lambda j: (layer, 0, j)),
                  pl.BlockSpec((1, 1, tn), lambda j: (layer, 0, j))],
        out_specs=pl.BlockSpec((8, tn), lambda j: (0, j)),
        compiler_params=_cparams(("arbitrary",)),
        name="adaln",
    )(c_rows, w, b.reshape(depth, 1, n))


def _inproj_kernel(h_ref, mod_ref, g_ref, w_ref, wab_ref, qk_ref, v_ref, gdn_ref, z_ref, ab_ref):
    x = h_ref[0]
    ms = jnp.mean(x * x, axis=-1, keepdims=True)
    y = x * lax.rsqrt(ms + EPS) * g_ref[...]
    shift = mod_ref[0, 0, 0:1, :]
    scale = mod_ref[0, 0, 1:2, :]
    u = (y * (1.0 + scale) + shift).astype(BF16)
    qk_ref[0] = _dot(u, w_ref[:, 0:2 * ATT_WIDTH])
    vv = _dot(u, w_ref[:, 2 * ATT_WIDTH:3 * ATT_WIDTH])
    for hh in range(ATT_HEADS):
        v_ref[0, hh] = vv[:, hh * LANES:(hh + 1) * LANES].T.astype(BF16)
    off = 3 * ATT_WIDTH
    gdn_ref[0] = _dot(u, w_ref[:, off:off + 3 * GDN_WIDTH])
    z_ref[0] = _dot(u, w_ref[:, off + 3 * GDN_WIDTH:off + 4 * GDN_WIDTH])
    ab_ref[0] = _dot(u, wab_ref[...])


def _inproj(h, modtab, g, w_main, w_ab, n_lat_blocks):
    B, S, _ = h.shape
    tm = ROW_TILE
    row = lambda b, i: (b, i, 0)
    return pl.pallas_call(
        _inproj_kernel,
        out_shape=(jax.ShapeDtypeStruct((B, S, 2 * ATT_WIDTH), F32),
                   jax.ShapeDtypeStruct((B, ATT_HEADS, LANES, S), BF16),
                   jax.ShapeDtypeStruct((B, S, 3 * GDN_WIDTH), F32),
                   jax.ShapeDtypeStruct((B, S, GDN_WIDTH), F32),
                   jax.ShapeDtypeStruct((B, S, LANES), F32)),
        grid=(B, S // tm),
        in_specs=[pl.BlockSpec((1, tm, D_MODEL), row),
                  pl.BlockSpec((1, 1, 8, D_MODEL), lambda b, i: (b, (i >= n_lat_blocks).astype(jnp.int32), 0, 0)),
                  pl.BlockSpec((1, D_MODEL), lambda b, i: (0, 0)),
                  pl.BlockSpec((D_MODEL, IN_MAIN), lambda b, i: (0, 0)),
                  pl.BlockSpec((D_MODEL, LANES), lambda b, i: (0, 0))],
        out_specs=(pl.BlockSpec((1, tm, 2 * ATT_WIDTH), row),
                   pl.BlockSpec((1, ATT_HEADS, LANES, tm), lambda b, i: (b, 0, 0, i)),
                   pl.BlockSpec((1, tm, 3 * GDN_WIDTH), row),
                   pl.BlockSpec((1, tm, GDN_WIDTH), row),
                   pl.BlockSpec((1, tm, LANES), row)),
        compiler_params=_cparams(("parallel", "parallel")),
        name="inproj",
    )(h, modtab, g, w_main, w_ab)


def _qkprep_kernel(p_ref, cos_ref, sin_ref, gq_ref, gk_ref, qz_ref, k_ref):
    tm = p_ref.shape[1]
    lane = lax.broadcasted_iota(jnp.int32, (tm, LANES), 1)
    hi16 = (lane & 16) != 0
    first = lax.broadcasted_iota(jnp.int32, (LANES, tm), 0) < ATT_HEAD_DIM
    r = lax.broadcasted_iota(jnp.int32, (LANES, LANES), 0) >> 6
    c = lax.broadcasted_iota(jnp.int32, (LANES, LANES), 1) >> 6
    gmat = jnp.where(r == c, 1.0 / ATT_HEAD_DIM, 0.0).astype(F32)
    cosv = cos_ref[...]
    sinv = sin_ref[...]
    for j in range(2 * ATT_HEADS):
        x = p_ref[0, :, j * LANES:(j + 1) * LANES]
        ms = _dot_hi(x * x, gmat)
        g = gq_ref[...] if j < ATT_HEADS else gk_ref[...]
        y = x * lax.rsqrt(ms + EPS) * g
        sw = jnp.where(hi16, pltpu.roll(y, 16, 1), pltpu.roll(y, LANES - 16, 1))
        y = y * cosv + sw * sinv
        if j < ATT_HEADS:
            yt = (y * Q_SCALE).T
            qz_ref[0, j, 0] = jnp.where(first, yt, 0.0).astype(BF16)
            qz_ref[0, j, 1] = jnp.where(first, 0.0, yt).astype(BF16)
        else:
            k_ref[0, j - ATT_HEADS] = y.astype(BF16)


def _qkprep(p_qk, cos_t, sin_t, gq, gk):
    B, S, _ = p_qk.shape
    tm = ROW_TILE
    return pl.pallas_call(
        _qkprep_kernel,
        out_shape=(jax.ShapeDtypeStruct((B, ATT_HEADS, 2, LANES, S), BF16),
                   jax.ShapeDtypeStruct((B, ATT_HEADS, S, LANES), BF16)),
        grid=(B, S // tm),
        in_specs=[pl.BlockSpec((1, tm, 2 * ATT_WIDTH), lambda b, i: (b, i, 0)),
                  pl.BlockSpec((tm, LANES), lambda b, i: (i, 0)),
                  pl.BlockSpec((tm, LANES), lambda b, i: (i, 0)),
                  pl.BlockSpec((1, LANES), lambda b, i: (0, 0)),
                  pl.BlockSpec((1, LANES), lambda b, i: (0, 0))],
        out_specs=(pl.BlockSpec((1, ATT_HEADS, 2, LANES, tm), lambda b, i: (b, 0, 0, 0, i)),
                   pl.BlockSpec((1, ATT_HEADS, tm, LANES), lambda b, i: (b, 0, i, 0))),
        compiler_params=_cparams(("parallel", "parallel")),
        name="qkprep",
    )(p_qk, cos_t, sin_t, gq, gk)


def _attn_kernel(lam_ref, qz_ref, k_ref, v_ref, g_ref, *rest, aliased, tk, n_kv):
    if aliased:
        rest = rest[1:]
    o_ref, s_sc, mt_sc, m_sc, l_sc, acc_sc = rest
    m_sc[...] = jnp.full(m_sc.shape, -jnp.inf, F32)
    l_sc[...] = jnp.zeros(l_sc.shape, F32)
    acc_sc[...] = jnp.zeros(acc_sc.shape, F32)

    def rows(j):
        if isinstance(j, int):
            return pl.ds(j * tk, tk)
        return pl.ds(pl.multiple_of(j * tk, tk), tk)

    def qk_tile(j, slot, ps=(0, 1)):
        kt = k_ref[0, 0, rows(j), :]
        for p in ps:
            s = _dot(kt, qz_ref[0, 0, p])
            s_sc[slot, p] = s
            mt_sc[slot, p] = jnp.max(s, axis=0, keepdims=True)

    def pv_tile(j, slot, ps=(0, 1)):
        for p in ps:
            m_prev = m_sc[p]
            m_new = jnp.maximum(m_prev, mt_sc[slot, p])
            alpha = jnp.exp2(m_prev - m_new)
            lsum = None
            acc = None
            for c in range(tk // KV_CHUNK):
                cs = slice(c * KV_CHUNK, (c + 1) * KV_CHUNK)
                pe = jnp.exp2(s_sc[slot, p, cs, :] - m_new)
                ps = jnp.sum(pe, axis=0, keepdims=True)
                pv = _dot(v_ref[0, 0, j, :, cs], pe.astype(BF16))
                lsum = ps if lsum is None else lsum + ps
                acc = pv if acc is None else acc + pv
            l_sc[p] = alpha * l_sc[p] + lsum
            acc_sc[p] = alpha * acc_sc[p] + acc
            m_sc[p] = m_new

    def fused_tile(jq, slot_q, jp, slot_p):
        m_new = [jnp.maximum(m_sc[p], mt_sc[slot_p, p]) for p in range(2)]
        alpha = [jnp.exp2(m_sc[p] - m_new[p]) for p in range(2)]
        mx, lsum, acc = [None, None], [None, None], [None, None]
        for c in range(tk // KV_CHUNK):
            cs = slice(c * KV_CHUNK, (c + 1) * KV_CHUNK)
            if isinstance(jq, int):
                kr = pl.ds(jq * tk + c * KV_CHUNK, KV_CHUNK)
            else:
                kr = pl.ds(pl.multiple_of(jq * tk + c * KV_CHUNK, KV_CHUNK), KV_CHUNK)
            kc = k_ref[0, 0, kr, :]
            vc = v_ref[0, 0, jp, :, cs]
            for p in range(2):
                s = _dot(kc, qz_ref[0, 0, p])
                s_sc[slot_q, p, cs, :] = s
                cm = jnp.max(s, axis=0, keepdims=True)
                mx[p] = cm if mx[p] is None else jnp.maximum(mx[p], cm)
                pe = jnp.exp2(s_sc[slot_p, p, cs, :] - m_new[p])
                ps = jnp.sum(pe, axis=0, keepdims=True)
                pv = _dot(vc, pe.astype(BF16))
                lsum[p] = ps if lsum[p] is None else lsum[p] + ps
                acc[p] = pv if acc[p] is None else acc[p] + pv
        for p in range(2):
            mt_sc[slot_q, p] = mx[p]
            l_sc[p] = alpha[p] * l_sc[p] + lsum[p]
            acc_sc[p] = alpha[p] * acc_sc[p] + acc[p]
            m_sc[p] = m_new[p]

    qk_tile(0, 0)
    n_pairs = (n_kv - 1) // 2

    def pair(jj, carry):
        j = 2 * jj
        fused_tile(j + 1, 1, j, 0)
        fused_tile(j + 2, 0, j + 1, 1)
        return carry

    if n_pairs > 0:
        lax.fori_loop(0, n_pairs, pair, 0)
    j = 2 * n_pairs
    if j == n_kv - 1:
        pv_tile(j, 0)
    else:
        qk_tile(j + 1, 1)
        pv_tile(j, 0)
        pv_tile(j + 1, 1)

    o = acc_sc[0] / l_sc[0] - lam_ref[0] * (acc_sc[1] / l_sc[1])
    ms = jnp.mean(o * o, axis=0, keepdims=True)
    o_ref[0] = ((o * lax.rsqrt(ms + EPS)).T * g_ref[...]).astype(BF16)


def _attention(lam, qz, k, vt, g, *, tq, tk, q_blk0, n_q, kv_blk0, n_kv, prev_out=None, name="attn_ctx"):
    B, H, _, _, S = qz.shape
    aliased = prev_out is not None
    skv = tk * n_kv
    in_specs = [pl.BlockSpec(memory_space=pltpu.SMEM),
                pl.BlockSpec((1, 1, 2, LANES, tq), lambda b, h, i: (b, h, 0, 0, q_blk0 + i)),
                pl.BlockSpec((1, 1, skv, LANES), lambda b, h, i: (b, h, kv_blk0, 0)),
                pl.BlockSpec((1, 1, n_kv, LANES, tk), lambda b, h, i: (b, h, 0, 0, 0)),
                pl.BlockSpec((1, LANES), lambda b, h, i: (0, 0))]
    args = [lam, qz, k, vt, g]
    aliases = {}
    if aliased:
        in_specs.append(pl.BlockSpec(memory_space=pl.ANY))
        args.append(prev_out)
        aliases = {5: 0}
    return pl.pallas_call(
        functools.partial(_attn_kernel, aliased=aliased, tk=tk, n_kv=n_kv),
        out_shape=jax.ShapeDtypeStruct((B, S, ATT_WIDTH), BF16),
        grid=(B, H, n_q),
        in_specs=in_specs,
        out_specs=pl.BlockSpec((1, tq, LANES), lambda b, h, i: (b, q_blk0 + i, h)),
        scratch_shapes=[pltpu.VMEM((2, 2, tk, tq), F32), pltpu.VMEM((2, 2, 1, tq), F32),
                        pltpu.VMEM((2, 1, tq), F32), pltpu.VMEM((2, 1, tq), F32),
                        pltpu.VMEM((2, LANES, tq), F32)],
        input_output_aliases=aliases,
        compiler_params=_cparams(("parallel", "parallel", "arbitrary")),
        name=name,
    )(*args)


def _attention_lat(lam, qz, k, vt, g, L):
    B, H, _, S = vt.shape
    tq = _pick_tile(L, (512, 256))
    tk = _pick_tile(S, (1280, 640, 256))
    n_kv = S // tk
    vt_tiles = vt.reshape(B, H, LANES, n_kv, tk).transpose(0, 1, 3, 2, 4)
    out0 = jnp.zeros((B, S, ATT_WIDTH), BF16)
    return _attention(lam, qz, k, vt_tiles, g, tq=tq, tk=tk, q_blk0=0, n_q=L // tq, kv_blk0=0, n_kv=n_kv,
                      prev_out=out0, name="attn_lat")


def _attention_ctx(lam, qz, k, vt, g, L, prev_out):
    B, H, _, S = vt.shape
    Lc = S - L
    vt_ctx = vt[:, :, :, L:].reshape(B, H, 1, LANES, Lc)
    return _attention(lam, qz, k, vt_ctx, g, tq=Lc, tk=Lc, q_blk0=L // Lc, n_q=1, kv_blk0=L // Lc, n_kv=1,
                      prev_out=prev_out)


def _gdnprep_kernel(x_ref, prev_ref, next_ref, cw_ref, ab_ref, alog_ref, dtb_ref,
                    q_ref, k_ref, v_ref, gate_ref, ext_sc, *, nb_lat, nb_all):
    i = pl.program_id(1)
    tm = x_ref.shape[1]
    first = jnp.logical_or(i == 0, i == nb_lat)
    last = jnp.logical_or(i == nb_lat - 1, i == nb_all - 1)
    keep_prev = jnp.where(first, 0.0, 1.0)
    keep_next = jnp.where(last, 0.0, 1.0)
    ext_sc[0:8, :] = prev_ref[0] * keep_prev
    ext_sc[8:8 + tm, :] = x_ref[0]
    ext_sc[8 + tm:16 + tm, :] = next_ref[0] * keep_next
    acc = None
    for j in range(CONV_W):
        term = ext_sc[pl.ds(8 + j - CONV_W // 2, tm), :] * cw_ref[j:j + 1, :]
        acc = term if acc is None else acc + term
    y = acc * _sigmoid(acc)
    for part, ref in ((0, q_ref), (1, k_ref)):
        for hh in range(GDN_HEADS):
            lo = part * GDN_WIDTH + hh * GDN_HEAD_DIM
            t = y[:, lo:lo + GDN_HEAD_DIM]
            ref[0, :, hh * GDN_HEAD_DIM:(hh + 1) * GDN_HEAD_DIM] = (
                t * lax.rsqrt(jnp.sum(t * t, axis=-1, keepdims=True) + EPS))
    v_ref[0] = y[:, 2 * GDN_WIDTH:3 * GDN_WIDTH]
    ab = ab_ref[0]
    xs = ab + dtb_ref[...]
    sp = jnp.maximum(xs, 0.0) + jnp.log(1.0 + jnp.exp(-jnp.abs(xs)))
    g = -jnp.exp(alog_ref[...]) * sp
    lane = lax.broadcasted_iota(jnp.int32, ab.shape, 1)
    gate_ref[0] = jnp.where(lane < 2 * GDN_HEADS, g, _sigmoid(ab))


def _gdnprep(p_gdn, conv_w8, ab, alog_row, dtb_row, n_lat_blocks):
    B, S, W = p_gdn.shape
    tm = ROW_TILE
    nb = S // tm
    r8 = tm // 8
    row = lambda b, i: (b, i, 0)
    return pl.pallas_call(
        functools.partial(_gdnprep_kernel, nb_lat=n_lat_blocks, nb_all=nb),
        out_shape=(jax.ShapeDtypeStruct((B, S, GDN_WIDTH), F32),
                   jax.ShapeDtypeStruct((B, S, GDN_WIDTH), F32),
                   jax.ShapeDtypeStruct((B, S, GDN_WIDTH), F32),
                   jax.ShapeDtypeStruct((B, S, LANES), F32)),
        grid=(B, nb),
        in_specs=[pl.BlockSpec((1, tm, W), row),
                  pl.BlockSpec((1, 8, W), lambda b, i: (b, jnp.maximum(i * r8 - 1, 0), 0)),
                  pl.BlockSpec((1, 8, W), lambda b, i: (b, jnp.minimum((i + 1) * r8, nb * r8 - 1), 0)),
                  pl.BlockSpec((8, W), lambda b, i: (0, 0)),
                  pl.BlockSpec((1, tm, LANES), row),
                  pl.BlockSpec((1, LANES), lambda b, i: (0, 0)),
                  pl.BlockSpec((1, LANES), lambda b, i: (0, 0))],
        out_specs=(pl.BlockSpec((1, tm, GDN_WIDTH), row),
                   pl.BlockSpec((1, tm, GDN_WIDTH), row),
                   pl.BlockSpec((1, tm, GDN_WIDTH), row),
                   pl.BlockSpec((1, tm, LANES), row)),
        scratch_shapes=[pltpu.VMEM((tm + 16, W), F32)],
        compiler_params=_cparams(("parallel", "parallel")),
        name="gdnprep",
    )(p_gdn, p_gdn, p_gdn, conv_w8, ab, alog_row, dtb_row)


def _gdnchunk_kernel(q_ref, k_ref, v_ref, gate_ref, qg_ref, w_ref, kd_ref, u_ref, aq_ref, eg_ref):
    tm = q_ref.shape[1]
    nc = tm // GDN_CHUNK
    gates = gate_ref[0]
    ri = lax.broadcasted_iota(jnp.int32, (tm, tm), 0)
    ci = lax.broadcasted_iota(jnp.int32, (tm, tm), 1)
    same = (ri >> 6) == (ci >> 6)
    eye = ri == ci
    eye_f = jnp.where(eye, 1.0, 0.0).astype(F32)

    g1 = gates.astype(BF16)
    r1 = gates - g1.astype(F32)
    g2 = r1.astype(BF16)
    g3 = (r1 - g2.astype(F32)).astype(BF16)
    gparts = jnp.concatenate([g1, g2, g3], axis=1)

    def seg_sum(mask):
        r = _dot(jnp.where(mask, 1.0, 0.0).astype(BF16), gparts)
        return (r[:, :LANES] + r[:, LANES:2 * LANES]) + r[:, 2 * LANES:]

    tot = seg_sum(same)

    heads = []
    for hh in range(GDN_HEADS):
        sl = slice(hh * GDN_HEAD_DIM, (hh + 1) * GDN_HEAD_DIM)
        k = k_ref[0, :, sl]
        k16 = k.astype(BF16)
        q = q_ref[0, :, sl] * (GDN_HEAD_DIM ** -0.5)
        heads.append((sl, k, q, _dot_nt(k16, k16), _dot_nt(q.astype(BF16), k16)))

    chains = []
    for d in range(2):
        incl = jnp.logical_and(same, (ci <= ri) if d == 0 else (ci >= ri))
        strict = jnp.logical_and(incl, jnp.logical_not(eye))
        gc = seg_sum(incl)
        gc_t = gc.T
        for hh in range(GDN_HEADS):
            ln = d * GDN_HEADS + hh
            gcol = gc[:, ln:ln + 1]
            bcol = gates[:, 2 * GDN_HEADS + ln:2 * GDN_HEADS + ln + 1]
            tcol = tot[:, ln:ln + 1]
            diff = gcol - gc_t[ln:ln + 1, :]
            decay = jnp.where(incl, jnp.exp(jnp.where(incl, diff, 0.0)), 0.0)
            a = jnp.where(strict, heads[hh][3] * bcol * decay, 0.0)
            aqk = heads[hh][4] * decay
            chains.append(dict(d=d, hh=hh, x=-a, t=eye_f - a, aqk=aqk, gcol=gcol, bcol=bcol, tcol=tcol))

    for _ in range(5):
        for ch in chains:
            x16 = ch["x"].astype(BF16)
            ch["x"] = _dot(x16, x16)
        for ch in chains:
            ch["t"] = ch["t"] + _dot(ch["t"].astype(BF16), ch["x"].astype(BF16))

    for ch in chains:
        d, hh, gcol, bcol, tcol = ch["d"], ch["hh"], ch["gcol"], ch["bcol"], ch["tcol"]
        sl, k, q = heads[hh][0], heads[hh][1], heads[hh][2]
        t16 = ch["t"].astype(BF16)
        egc = jnp.exp(gcol)
        kb = k * bcol
        u_ref[0, d, :, sl] = _dot(t16, (v_ref[0, :, sl] * bcol).astype(BF16))
        w_ref[0, d, :, sl] = _dot(t16, (kb * egc).astype(BF16)).astype(BF16)
        kd_ref[0, d, :, sl] = (k * jnp.exp(tcol - gcol)).astype(BF16)
        qg_ref[0, d, :, sl] = (q * egc).astype(BF16)
        for cc in range(nc):
            rs = slice(cc * GDN_CHUNK, (cc + 1) * GDN_CHUNK)
            aq_ref[0, d, rs, hh * GDN_CHUNK:(hh + 1) * GDN_CHUNK] = ch["aqk"][rs, rs].astype(BF16)
            eg_ref[0, d, cc, hh:hh + 1, :] = jnp.broadcast_to(
                jnp.exp(tcol[cc * GDN_CHUNK:cc * GDN_CHUNK + 1, :]), (1, LANES))


def _gdnchunk(gq, gk, gv, gates):
    B, S, W = gq.shape
    tm = ROW_TILE
    nc = tm // GDN_CHUNK
    row = lambda b, i: (b, i, 0)
    drow = lambda b, i: (b, 0, i, 0)
    big = lambda dt: jax.ShapeDtypeStruct((B, 2, S, W), dt)
    return pl.pallas_call(
        _gdnchunk_kernel,
        out_shape=(big(BF16), big(BF16), big(BF16), big(F32),
                   jax.ShapeDtypeStruct((B, 2, S, GDN_HEADS * GDN_CHUNK), BF16),
                   jax.ShapeDtypeStruct((B, 2, S // GDN_CHUNK, GDN_HEADS, LANES), F32)),
        grid=(B, S // tm),
        in_specs=[pl.BlockSpec((1, tm, W), row), pl.BlockSpec((1, tm, W), row),
                  pl.BlockSpec((1, tm, W), row), pl.BlockSpec((1, tm, LANES), row)],
        out_specs=(pl.BlockSpec((1, 2, tm, W), drow), pl.BlockSpec((1, 2, tm, W), drow),
                   pl.BlockSpec((1, 2, tm, W), drow), pl.BlockSpec((1, 2, tm, W), drow),
                   pl.BlockSpec((1, 2, tm, GDN_HEADS * GDN_CHUNK), drow),
                   pl.BlockSpec((1, 2, nc, GDN_HEADS, LANES), lambda b, i: (b, 0, i, 0, 0))),
        compiler_params=_cparams(("parallel", "parallel")),
        name="gdnchunk",
    )(gq, gk, gv, gates)


def _gdnscan_kernel(qg0, w0, kd0, u0, aq0, eg0, qg1, w1, kd1, u1, aq1, eg1, of_ref, ob_ref, s_sc):
    @pl.when(pl.program_id(1) == 0)
    def _():
        s_sc[...] = jnp.zeros(s_sc.shape, F32)

    dirs = ((qg0, w0, kd0, u0, aq0, eg0, of_ref), (qg1, w1, kd1, u1, aq1, eg1, ob_ref))
    chains = [(d, hh) for d in range(2) for hh in range(GDN_HEADS)]
    st = {ch: s_sc[ch[0], ch[1]] for ch in chains}
    for step in range(SCAN_GROUP):
        st16, vn16, qs = {}, {}, {}
        for d, hh in chains:
            st16[d, hh] = st[d, hh].astype(BF16)
        for d, hh in chains:
            qg, w, kd, u, aq, eg, o_ref = dirs[d]
            cc = step if d == 0 else SCAN_GROUP - 1 - step
            rs = slice(cc * GDN_CHUNK, (cc + 1) * GDN_CHUNK)
            sl = slice(hh * GDN_HEAD_DIM, (hh + 1) * GDN_HEAD_DIM)
            vn16[d, hh] = (u[0, 0, rs, sl] - _dot(w[0, 0, rs, sl], st16[d, hh])).astype(BF16)
            qs[d, hh] = _dot(qg[0, 0, rs, sl], st16[d, hh])
        for d, hh in chains:
            qg, w, kd, u, aq, eg, o_ref = dirs[d]
            cc = step if d == 0 else SCAN_GROUP - 1 - step
            rs = slice(cc * GDN_CHUNK, (cc + 1) * GDN_CHUNK)
            sl = slice(hh * GDN_HEAD_DIM, (hh + 1) * GDN_HEAD_DIM)
            o_ref[0, rs, sl] = qs[d, hh] + _dot(aq[0, 0, rs, hh * GDN_CHUNK:(hh + 1) * GDN_CHUNK], vn16[d, hh])
            st[d, hh] = st[d, hh] * eg[0, 0, cc, hh:hh + 1, :] + _dot_tn(kd[0, 0, rs, sl], vn16[d, hh])
    for d, hh in chains:
        s_sc[d, hh] = st[d, hh]


def _gdnscan(qg, w, kd, u, aq, eg, n_lat_chunks, n_ctx_chunks):
    B, _, S, W = qg.shape
    G = SCAN_GROUP
    R = GDN_CHUNK * G
    n = S // R
    n_lat, n_ctx = n_lat_chunks // G, n_ctx_chunks // G
    assert n_lat * G == n_lat_chunks and n_ctx * G == n_ctx_chunks

    def fwd_blk(i):
        return jnp.where(i < n_ctx, n_lat + i, i - n_ctx)

    def bwd_blk(i):
        return jnp.where(i < n_ctx, n_lat + n_ctx - 1 - i, n - 1 - i)

    def specs(d, blk_of):
        big = pl.BlockSpec((1, 1, R, W), lambda b, i: (b, d, blk_of(i), 0))
        return [big, big, big, big,
                pl.BlockSpec((1, 1, R, GDN_HEADS * GDN_CHUNK), lambda b, i: (b, d, blk_of(i), 0)),
                pl.BlockSpec((1, 1, G, GDN_HEADS, LANES), lambda b, i: (b, d, blk_of(i), 0, 0))]

    return pl.pallas_call(
        _gdnscan_kernel,
        out_shape=(jax.ShapeDtypeStruct((B, S, W), F32), jax.ShapeDtypeStruct((B, S, W), F32)),
        grid=(B, n),
        in_specs=specs(0, fwd_blk) + specs(1, bwd_blk),
        out_specs=(pl.BlockSpec((1, R, W), lambda b, i: (b, fwd_blk(i), 0)),
                   pl.BlockSpec((1, R, W), lambda b, i: (b, bwd_blk(i), 0))),
        scratch_shapes=[pltpu.VMEM((2, GDN_HEADS, GDN_HEAD_DIM, GDN_HEAD_DIM), F32)],
        compiler_params=_cparams(("parallel", "arbitrary")),
        name="gdnscan",
    )(qg, w, kd, u, aq, eg, qg, w, kd, u, aq, eg)


def _mixout_kernel(oa_ref, of_ref, ob_ref, z_ref, h_ref, mod_ref, gg_ref, wo_ref, gf_ref, rw_ref, rb_ref,
                   hn_ref, v_ref, te_ref, tg_ref):
    og = of_ref[0] + ob_ref[0]
    z = z_ref[0]
    parts = [oa_ref[0]]
    for hh in range(GDN_HEADS):
        sl = slice(hh * GDN_HEAD_DIM, (hh + 1) * GDN_HEAD_DIM)
        t = og[:, sl]
        t = t * lax.rsqrt(jnp.mean(t * t, axis=-1, keepdims=True) + EPS) * gg_ref[...]
        zz = z[:, sl]
        parts.append((t * (zz * _sigmoid(zz))).astype(BF16))
    mix_in = jnp.concatenate(parts, axis=-1)
    mix = _dot(mix_in, wo_ref[...])
    hn = h_ref[0] + mod_ref[0, 0, 2:3, :] * mix
    hn_ref[0] = hn
    y = hn * lax.rsqrt(jnp.mean(hn * hn, axis=-1, keepdims=True) + EPS) * gf_ref[...]
    v = y * (1.0 + mod_ref[0, 0, 4:5, :]) + mod_ref[0, 0, 3:4, :]
    v_ref[0] = v.astype(BF16)
    v_hi = v.astype(BF16)
    v_lo = (v - v_hi.astype(F32)).astype(BF16)
    logits = (_dot(v_hi, rw_ref[0]) + _dot(v_lo, rw_ref[0]) + _dot(v_hi, rw_ref[1])) + rb_ref[...]
    lane = lax.broadcasted_iota(jnp.int32, logits.shape, 1)
    cur = logits
    vals, idxs = [], []
    for _ in range(TOP_K):
        m = jnp.max(cur, axis=-1, keepdims=True)
        idx = jnp.min(jnp.where(cur == m, lane, LANES), axis=-1, keepdims=True)
        vals.append(m)
        idxs.append(idx)
        cur = jnp.where(lane == idx, -jnp.inf, cur)
    es = [jnp.exp(vv - vals[0]) for vv in vals]
    inv = 1.0 / (es[0] + es[1] + es[2] + es[3])
    te = jnp.zeros(logits.shape, jnp.int32)
    tg = jnp.zeros(logits.shape, F32)
    for kk in range(TOP_K):
        te = jnp.where(lane == kk, idxs[kk], te)
        tg = jnp.where(lane == kk, es[kk] * inv, tg)
    te_ref[0] = te
    tg_ref[0] = tg


def _mixout(o_att, o_f, o_b, z, h, modtab, gg, w_out, gf, rw, rb, n_lat_blocks):
    B, S, _ = h.shape
    tm = ROW_TILE
    row = lambda b, i: (b, i, 0)
    const = lambda b, i: (0, 0)
    return pl.pallas_call(
        _mixout_kernel,
        out_shape=(jax.ShapeDtypeStruct((B, S, D_MODEL), F32),
                   jax.ShapeDtypeStruct((B, S, D_MODEL), BF16),
                   jax.ShapeDtypeStruct((B, S, LANES), jnp.int32),
                   jax.ShapeDtypeStruct((B, S, LANES), F32)),
        grid=(B, S // tm),
        in_specs=[pl.BlockSpec((1, tm, ATT_WIDTH), row),
                  pl.BlockSpec((1, tm, GDN_WIDTH), row),
                  pl.BlockSpec((1, tm, GDN_WIDTH), row),
                  pl.BlockSpec((1, tm, GDN_WIDTH), row),
                  pl.BlockSpec((1, tm, D_MODEL), row),
                  pl.BlockSpec((1, 1, 8, D_MODEL), lambda b, i: (b, (i >= n_lat_blocks).astype(jnp.int32), 0, 0)),
                  pl.BlockSpec((1, LANES), const),
                  pl.BlockSpec((D_MODEL, D_MODEL), const),
                  pl.BlockSpec((1, D_MODEL), const),
                  pl.BlockSpec((2, D_MODEL, LANES), lambda b, i: (0, 0, 0)),
                  pl.BlockSpec((1, LANES), const)],
        out_specs=(pl.BlockSpec((1, tm, D_MODEL), row),
                   pl.BlockSpec((1, tm, D_MODEL), row),
                   pl.BlockSpec((1, tm, LANES), row),
                   pl.BlockSpec((1, tm, LANES), row)),
        compiler_params=_cparams(("parallel", "parallel")),
        name="mixout",
    )(o_att, o_f, o_b, z, h, modtab, gg, w_out, gf, rw, rb)


def _expert_kernel(be_ref, nv_ref, x_ref, wgu_ref, bgu_ref, wd_ref, bd_ref, sg_ref, y_ref, wgu_sc, wd_sc):
    i = pl.program_id(0)
    new_expert = jnp.logical_or(i == 0, be_ref[i] != be_ref[jnp.maximum(i - 1, 0)])

    @pl.when(new_expert)
    def _():
        wgu_sc[...] = wgu_ref[0, 0].astype(BF16)
        wd_sc[...] = wd_ref[0, 0].astype(BF16)

    @pl.when(nv_ref[i] > 0)
    def _():
        gu = _dot(x_ref[...], wgu_sc[...]) + bgu_ref[0, 0]
        g_ = jnp.minimum(gu[:, :D_EXPERT], SWIGLU_LIMIT)
        up = jnp.clip(gu[:, D_EXPERT:], -SWIGLU_LIMIT, SWIGLU_LIMIT)
        glu = g_ * _sigmoid(SWIGLU_ALPHA * g_)
        act = ((up + 1.0) * glu).astype(BF16)
        y_ref[...] = ((_dot(act, wd_sc[...]) + bd_ref[0, 0]) * sg_ref[...]).astype(y_ref.dtype)

    @pl.when(nv_ref[i] == 0)
    def _():
        y_ref[...] = jnp.zeros(y_ref.shape, y_ref.dtype)


def _experts(block_e, n_valid, x_sorted, wgu, bgu, wd, bd, slot_gate, layer):
    n_slots = x_sorted.shape[0]
    nb = n_slots // MOE_BLOCK
    grid_spec = pltpu.PrefetchScalarGridSpec(
        num_scalar_prefetch=2,
        grid=(nb,),
        in_specs=[pl.BlockSpec((MOE_BLOCK, D_MODEL), lambda i, be, nv: (i, 0)),
                  pl.BlockSpec((1, 1, D_MODEL, 2 * D_EXPERT), lambda i, be, nv: (layer, be[i], 0, 0)),
                  pl.BlockSpec((1, 1, 1, 2 * D_EXPERT), lambda i, be, nv: (layer, be[i], 0, 0)),
                  pl.BlockSpec((1, 1, D_EXPERT, D_MODEL), lambda i, be, nv: (layer, be[i], 0, 0)),
                  pl.BlockSpec((1, 1, 1, D_MODEL), lambda i, be, nv: (layer, be[i], 0, 0)),
                  pl.BlockSpec((MOE_BLOCK, 1), lambda i, be, nv: (i, 0))],
        out_specs=pl.BlockSpec((MOE_BLOCK, D_MODEL), lambda i, be, nv: (i, 0)),
        scratch_shapes=[pltpu.VMEM((D_MODEL, 2 * D_EXPERT), BF16), pltpu.VMEM((D_EXPERT, D_MODEL), BF16)],
    )
    return pl.pallas_call(
        _expert_kernel,
        out_shape=jax.ShapeDtypeStruct((n_slots, D_MODEL), BF16),
        grid_spec=grid_spec,
        compiler_params=pltpu.CompilerParams(dimension_semantics=("arbitrary",),
                                             vmem_limit_bytes=EXPERT_VMEM_LIMIT),
        name="experts",
    )(block_e, n_valid, x_sorted, wgu, bgu, wd, bd, slot_gate)


def _moe_plan(top_e, top_g):
    T = top_e.shape[0]
    n_assign = T * TOP_K
    n_blocks = -(-n_assign // MOE_BLOCK) + N_EXPERTS
    n_slots = n_blocks * MOE_BLOCK
    i32 = jnp.int32
    flat_e = top_e.reshape(-1)
    gate_flat = top_g.reshape(-1)
    order = jnp.argsort(flat_e, stable=True).astype(i32)
    e_ids = jnp.arange(N_EXPERTS, dtype=i32)
    is_e = flat_e[:, None] == e_ids[None, :]
    counts = jnp.sum(is_e, axis=0, dtype=i32)
    start = jnp.cumsum(counts) - counts
    padded = (counts + MOE_BLOCK - 1) // MOE_BLOCK * MOE_BLOCK
    pad_end = jnp.cumsum(padded)
    pad_start = pad_end - padded
    blk0 = jnp.arange(n_blocks, dtype=i32) * MOE_BLOCK
    block_e = jnp.minimum(jnp.sum(pad_end[None, :] <= blk0[:, None], axis=1, dtype=i32), N_EXPERTS - 1)
    off = (blk0 - pad_start[block_e])[:, None] + jnp.arange(MOE_BLOCK, dtype=i32)[None, :]
    valid = off < counts[block_e][:, None]
    a_slot = order[jnp.clip(off + start[block_e][:, None], 0, n_assign - 1).reshape(-1)]
    valid_flat = valid.reshape(-1)
    slot_tok = jnp.where(valid_flat, a_slot // TOP_K, jnp.arange(n_slots, dtype=i32) % T)
    slot_gate = jnp.where(valid_flat, gate_flat[a_slot], 0.0)
    n_valid = jnp.sum(valid, axis=1, dtype=i32)

    n_tiles = T // COMBINE_TILE
    cnt = jnp.sum(top_e.reshape(n_tiles, COMBINE_TILE * TOP_K)[:, :, None] == e_ids[None, None, :], axis=1, dtype=i32)
    run_start = pad_start[None, :] + jnp.cumsum(cnt, axis=0) - cnt
    q_first = run_start // COMBINE_CHUNK
    n_ch = jnp.where(cnt > 0, (run_start + cnt - 1) // COMBINE_CHUNK - q_first + 1, 0)
    ch_end = jnp.cumsum(n_ch, axis=1)
    ch_off = ch_end - n_ch
    j = jnp.arange(COMBINE_MAX_CHUNKS, dtype=i32)
    e_j = jnp.minimum(jnp.sum(ch_end[:, None, :] <= j[None, :, None], axis=2, dtype=i32), N_EXPERTS - 1)
    used = j[None, :] < ch_end[:, -1:]
    chunk_id = jnp.where(used, jnp.take_along_axis(q_first, e_j, axis=1)
                         + j[None, :] - jnp.take_along_axis(ch_off, e_j, axis=1), 0)
    slot_tokid = jnp.where(valid_flat, a_slot // TOP_K, -1).reshape(n_slots // COMBINE_CHUNK, COMBINE_CHUNK)
    row_tok = jnp.where(used[:, :, None], slot_tokid[chunk_id], -1)
    row_tok = row_tok.reshape(n_tiles, 1, COMBINE_MAX_CHUNKS * COMBINE_CHUNK)
    return slot_tok, slot_gate.reshape(n_slots, 1), block_e, n_valid, chunk_id, row_tok


def _combine_kernel(cid_ref, tok_ref, h_ref, mod_ref, y_hbm, o_ref, ybuf, sem):
    i = pl.program_id(0)
    n = pl.num_programs(0)
    slot = lax.rem(i, 2)

    def chunk_copy(c, s, jj):
        return pltpu.make_async_copy(
            y_hbm.at[pl.ds(pl.multiple_of(c * COMBINE_CHUNK, COMBINE_CHUNK), COMBINE_CHUNK)],
            ybuf.at[s, pl.ds(jj * COMBINE_CHUNK, COMBINE_CHUNK)], sem.at[s])

    def start_tile(t, s):
        for jj in range(COMBINE_MAX_CHUNKS):
            chunk_copy(cid_ref[t, jj], s, jj).start()

    @pl.when(i == 0)
    def _():
        start_tile(0, 0)

    @pl.when(i + 1 < n)
    def _():
        start_tile(i + 1, 1 - slot)

    for jj in range(COMBINE_MAX_CHUNKS):
        chunk_copy(0, slot, jj).wait()

    tok = tok_ref[0]
    t_ids = i * COMBINE_TILE + lax.broadcasted_iota(jnp.int32, (COMBINE_TILE, tok.shape[1]), 0)
    onehot = jnp.where(tok == t_ids, 1.0, 0.0).astype(BF16)
    y = _dot(onehot, ybuf[slot])
    o_ref[...] = h_ref[...] + mod_ref[0, 0, 5:6, :] * y


def _combine(chunk_id, row_tok, y_sorted, h, modtab, n_lat_blocks):
    B, S, _ = h.shape
    T = B * S
    ct = COMBINE_TILE
    tiles_per_batch = S // ct
    n_lat_tiles = n_lat_blocks * ROW_TILE // ct
    n_rows = COMBINE_MAX_CHUNKS * COMBINE_CHUNK
    grid_spec = pltpu.PrefetchScalarGridSpec(
        num_scalar_prefetch=1,
        grid=(T // ct,),
        in_specs=[pl.BlockSpec((1, 1, n_rows), lambda i, cid: (i, 0, 0)),
                  pl.BlockSpec((ct, D_MODEL), lambda i, cid: (i, 0)),
                  pl.BlockSpec((1, 1, 8, D_MODEL),
                               lambda i, cid: (i // tiles_per_batch,
                                               (lax.rem(i, tiles_per_batch) >= n_lat_tiles).astype(jnp.int32), 0, 0)),
                  pl.BlockSpec(memory_space=pl.ANY)],
        out_specs=pl.BlockSpec((ct, D_MODEL), lambda i, cid: (i, 0)),
        scratch_shapes=[pltpu.VMEM((2, n_rows, D_MODEL), BF16), pltpu.SemaphoreType.DMA((2,))],
    )
    out = pl.pallas_call(
        _combine_kernel,
        out_shape=jax.ShapeDtypeStruct((T, D_MODEL), F32),
        grid_spec=grid_spec,
        compiler_params=_cparams(("arbitrary",)),
        name="combine",
    )(chunk_id, row_tok, h.reshape(T, D_MODEL), modtab, y_sorted)
    return out.reshape(B, S, D_MODEL)


def _moe(v_ffn, top_e, top_g, h_new, modtab, wgu, bgu, wd, bd, layer, n_lat_blocks):
    B, S, _ = v_ffn.shape
    T = B * S
    slot_tok, slot_gate, block_e, n_valid, chunk_id, row_tok = _moe_plan(
        top_e.reshape(T, LANES)[:, :TOP_K], top_g.reshape(T, LANES)[:, :TOP_K])
    x_sorted = v_ffn.reshape(T, D_MODEL)[slot_tok]
    y_sorted = _experts(block_e, n_valid, x_sorted, wgu, bgu, wd, bd, slot_gate, layer)
    return _combine(chunk_id, row_tok, y_sorted, h_new, modtab, n_lat_blocks)


def _pick_tile(n, cands):
    for t in cands:
        if n % t == 0:
            return t
    raise ValueError(f"no tile for {n}")


def _rope_tables(L, Lc):
    rows = L // GRID_W
    row = jnp.repeat(jnp.arange(rows, dtype=F32), GRID_W)
    col = (jnp.arange(L, dtype=jnp.int32) % GRID_W).astype(F32)
    inv_freq = ROPE_BASE ** (-jnp.arange(ROPE_PAIRS, dtype=F32) / ROPE_PAIRS)
    ar = row[:, None] * inv_freq
    ac = col[:, None] * inv_freq
    cos64 = jnp.concatenate([jnp.cos(ar), jnp.cos(ar), jnp.cos(ac), jnp.cos(ac)], axis=-1)
    sin64 = jnp.concatenate([-jnp.sin(ar), jnp.sin(ar), -jnp.sin(ac), jnp.sin(ac)], axis=-1)
    cos_t = jnp.concatenate([jnp.tile(cos64, (1, 2)), jnp.ones((Lc, LANES), F32)], axis=0)
    sin_t = jnp.concatenate([jnp.tile(sin64, (1, 2)), jnp.zeros((Lc, LANES), F32)], axis=0)
    return cos_t, sin_t


def _pad_lanes(v):
    v = v.reshape(1, -1).astype(F32)
    return jnp.pad(v, ((0, 0), (0, LANES - v.shape[1])))


def kernel(x, c, ctx, c_ctx, w_mod, b_mod, norm_mix_g, w_in, q_norm_g, k_norm_g, lam_q1, lam_k1, lam_q2, lam_k2, subln_g, conv_w, a_log, dt_bias, gdn_norm_g, w_out, norm_ffn_g, router_w, router_b, w_gate_up, b_gate_up, w_down, b_down):
    B, L, D = x.shape
    Lc = ctx.shape[1]
    S = L + Lc
    depth = w_mod.shape[0]
    tm = ROW_TILE
    n_lat_blocks = L // tm
    cos_t, sin_t = _rope_tables(L, Lc)

    c_rows = jnp.zeros((8, D), F32).at[:B].set(c).at[B].set(c_ctx)
    h = jnp.concatenate([x, ctx], axis=1)

    for layer in range(depth):
        mod = _adaln(c_rows, w_mod, b_mod, layer)
        mod6 = mod.reshape(8, 6, D)
        lat_mod = mod6[:B]
        ctx_mod = jnp.broadcast_to(mod6[B][None], (B, 6, D))
        modtab = jnp.pad(jnp.stack([lat_mod, ctx_mod], axis=1), ((0, 0), (0, 0), (0, 2), (0, 0)))

        lam_init = 0.8 - 0.6 * math.exp(-0.3 * layer)
        lam_full = (jnp.exp(jnp.sum(lam_q1[layer] * lam_k1[layer]))
                    - jnp.exp(jnp.sum(lam_q2[layer] * lam_k2[layer])) + lam_init).reshape(1).astype(F32)

        w_l = w_in[layer]
        w_main = w_l[:, :IN_MAIN].astype(BF16)
        w_ab = jnp.pad(w_l[:, IN_MAIN:], ((0, 0), (0, LANES - (w_l.shape[1] - IN_MAIN)))).astype(BF16)
        p_qk, v_att, p_gdn, z, ab = _inproj(h, modtab, norm_mix_g[layer].reshape(1, D), w_main, w_ab,
                                            n_lat_blocks)

        gq = jnp.tile(q_norm_g[layer].reshape(1, ATT_HEAD_DIM), (1, 2))
        gk = jnp.tile(k_norm_g[layer].reshape(1, ATT_HEAD_DIM), (1, 2))
        qz, k_att = _qkprep(p_qk, cos_t, sin_t, gq, gk)
        g_sub = (subln_g[layer] * (1.0 - lam_init)).reshape(1, LANES).astype(F32)
        o_att = _attention_lat(lam_full, qz, k_att, v_att, g_sub, L)
        o_att = _attention_ctx(lam_full, qz, k_att, v_att, g_sub, L, o_att)

        conv_w8 = jnp.pad(conv_w[layer], ((0, 8 - CONV_W), (0, 0)))
        gq_g, gk_g, gv_g, gates = _gdnprep(p_gdn, conv_w8, ab, _pad_lanes(a_log[layer]),
                                           _pad_lanes(dt_bias[layer]), n_lat_blocks)
        qg, w_g, kd, u_g, aq, eg = _gdnchunk(gq_g, gk_g, gv_g, gates)
        o_f, o_b = _gdnscan(qg, w_g, kd, u_g, aq, eg, L // GDN_CHUNK, Lc // GDN_CHUNK)

        rw = jnp.pad(router_w[layer].astype(F32), ((0, 0), (0, LANES - N_EXPERTS)))
        rw_hi = rw.astype(BF16)
        rw = jnp.stack([rw_hi, (rw - rw_hi.astype(F32)).astype(BF16)])
        rb = jnp.pad(router_b[layer].reshape(1, N_EXPERTS).astype(F32), ((0, 0), (0, LANES - N_EXPERTS)),
                     constant_values=-1e30)
        h_new, v_ffn, top_e, top_g = _mixout(
            o_att, o_f, o_b, z, h, modtab, gdn_norm_g[layer].reshape(1, LANES), w_out[layer].astype(BF16),
            norm_ffn_g[layer].reshape(1, D), rw, rb, n_lat_blocks)

        h = _moe(v_ffn, top_e, top_g, h_new, modtab,
                 w_gate_up, b_gate_up.reshape(depth, N_EXPERTS, 1, 2 * D_EXPERT),
                 w_down, b_down.reshape(depth, N_EXPERTS, 1, D), layer, n_lat_blocks)
    return h[:, :L]
```

```python
import functools
import math

import jax
import jax.numpy as jnp
from jax import lax
from jax.experimental import pallas as pl
from jax.experimental.pallas import tpu as pltpu

F32 = jnp.float32
BF16 = jnp.bfloat16
HIGHEST = lax.Precision.HIGHEST

D_MODEL = 1024
GRID_W = 64
EPS = 1e-6
ATT_WIDTH = 512
ATT_HEAD_DIM = 64
ATT_HEADS = 4
ROPE_BASE = 10000.0
ROPE_PAIRS = ATT_HEAD_DIM // 4
GDN_WIDTH = 512
GDN_HEAD_DIM = 128
GDN_HEADS = 4
GDN_CHUNK = 64
CONV_W = 5
IN_MAIN = 3 * ATT_WIDTH + 4 * GDN_WIDTH
N_EXPERTS = 32
TOP_K = 4
D_EXPERT = 1024
SWIGLU_ALPHA = 1.702
SWIGLU_LIMIT = 7.0
MOE_BLOCK = 512
SCAN_GROUP = 4
NO_TOKEN = 0xFFFF
COMBINE_TILE = 256
COMBINE_CHUNK = 16
COMBINE_MAX_CHUNKS = COMBINE_TILE * TOP_K // COMBINE_CHUNK + 2 * N_EXPERTS

LANES = 128
ROW_TILE = 256
KV_CHUNK = 256
Q_SCALE = ATT_HEAD_DIM ** -0.5 * math.log2(math.e)
VMEM_LIMIT = 48 * 1024 * 1024
EXPERT_VMEM_LIMIT = 56 * 1024 * 1024


def _cparams(sem):
    return pltpu.CompilerParams(dimension_semantics=sem, vmem_limit_bytes=VMEM_LIMIT)


def _dot(a, b):
    return jnp.dot(a, b, preferred_element_type=F32)


def _dot_nt(a, b):
    return lax.dot_general(a, b, (((1,), (1,)), ((), ())), preferred_element_type=F32)


def _dot_tn(a, b):
    return lax.dot_general(a, b, (((0,), (0,)), ((), ())), preferred_element_type=F32)


def _dot_hi(a, b):
    return jnp.dot(a, b, preferred_element_type=F32, precision=HIGHEST)


def _sigmoid(x):
    return 1.0 / (1.0 + jnp.exp(-x))


def _adaln_kernel(c_ref, w_ref, b_ref, o_ref):
    c = c_ref[...]
    s = c * _sigmoid(c)
    o_ref[...] = _dot_hi(s, w_ref[0]) + b_ref[0]


def _adaln(c_rows, w, b, layer):
    depth, _, n = w.shape
    tn = 1024
    return pl.pallas_call(
        _adaln_kernel,
        out_shape=jax.ShapeDtypeStruct((8, n), F32),
        grid=(n // tn,),
        in_specs=[pl.BlockSpec((8, D_MODEL), lambda j: (0, 0)),
                  pl.BlockSpec((1, D_MODEL, tn), lambda j: (layer, 0, j)),
                  pl.BlockSpec((1, 1, tn), lambda j: (layer, 0, j))],
        out_specs=pl.BlockSpec((8, tn), lambda j: (0, j)),
        compiler_params=_cparams(("arbitrary",)),
        name="adaln",
    )(c_rows, w, b.reshape(depth, 1, n))


def _inproj_kernel(h_ref, mod_ref, g_ref, w_ref, wab_ref, qk_ref, v_ref, gdn_ref, z_ref, ab_ref):
    x = h_ref[0]
    ms = jnp.mean(x * x, axis=-1, keepdims=True)
    y = x * lax.rsqrt(ms + EPS) * g_ref[...]
    shift = mod_ref[0, 0, 0:1, :]
    scale = mod_ref[0, 0, 1:2, :]
    u = (y * (1.0 + scale) + shift).astype(BF16)
    qk_ref[0] = _dot(u, w_ref[:, 0:2 * ATT_WIDTH])
    vv = _dot(u, w_ref[:, 2 * ATT_WIDTH:3 * ATT_WIDTH])
    for hh in range(ATT_HEADS):
        v_ref[0, hh] = vv[:, hh * LANES:(hh + 1) * LANES].T.astype(BF16)
    off = 3 * ATT_WIDTH
    gdn_ref[0] = _dot(u, w_ref[:, off:off + 3 * GDN_WIDTH])
    z_ref[0] = _dot(u, w_ref[:, off + 3 * GDN_WIDTH:off + 4 * GDN_WIDTH])
    ab_ref[0] = _dot(u, wab_ref[...])


def _inproj(h, modtab, g, w_main, w_ab, n_lat_blocks):
    B, S, _ = h.shape
    tm = ROW_TILE
    row = lambda b, i: (b, i, 0)
    return pl.pallas_call(
        _inproj_kernel,
        out_shape=(jax.ShapeDtypeStruct((B, S, 2 * ATT_WIDTH), F32),
                   jax.ShapeDtypeStruct((B, ATT_HEADS, LANES, S), BF16),
                   jax.ShapeDtypeStruct((B, S, 3 * GDN_WIDTH), F32),
                   jax.ShapeDtypeStruct((B, S, GDN_WIDTH), F32),
                   jax.ShapeDtypeStruct((B, S, LANES), F32)),
        grid=(B, S // tm),
        in_specs=[pl.BlockSpec((1, tm, D_MODEL), row),
                  pl.BlockSpec((1, 1, 8, D_MODEL), lambda b, i: (b, (i >= n_lat_blocks).astype(jnp.int32), 0, 0)),
                  pl.BlockSpec((1, D_MODEL), lambda b, i: (0, 0)),
                  pl.BlockSpec((D_MODEL, IN_MAIN), lambda b, i: (0, 0)),
                  pl.BlockSpec((D_MODEL, LANES), lambda b, i: (0, 0))],
        out_specs=(pl.BlockSpec((1, tm, 2 * ATT_WIDTH), row),
                   pl.BlockSpec((1, ATT_HEADS, LANES, tm), lambda b, i: (b, 0, 0, i)),
                   pl.BlockSpec((1, tm, 3 * GDN_WIDTH), row),
                   pl.BlockSpec((1, tm, GDN_WIDTH), row),
                   pl.BlockSpec((1, tm, LANES), row)),
        compiler_params=_cparams(("parallel", "parallel")),
        name="inproj",
    )(h, modtab, g, w_main, w_ab)


def _qkprep_kernel(p_ref, cos_ref, sin_ref, gq_ref, gk_ref, qz_ref, k_ref):
    tm = p_ref.shape[1]
    lane = lax.broadcasted_iota(jnp.int32, (tm, LANES), 1)
    hi16 = (lane & 16) != 0
    first = lax.broadcasted_iota(jnp.int32, (LANES, tm), 0) < ATT_HEAD_DIM
    r = lax.broadcasted_iota(jnp.int32, (LANES, LANES), 0) >> 6
    c = lax.broadcasted_iota(jnp.int32, (LANES, LANES), 1) >> 6
    gmat = jnp.where(r == c, 1.0 / ATT_HEAD_DIM, 0.0).astype(F32)
    cosv = cos_ref[...]
    sinv = sin_ref[...]
    for j in range(2 * ATT_HEADS):
        x = p_ref[0, :, j * LANES:(j + 1) * LANES]
        ms = _dot_hi(x * x, gmat)
        g = gq_ref[...] if j < ATT_HEADS else gk_ref[...]
        y = x * lax.rsqrt(ms + EPS) * g
        sw = jnp.where(hi16, pltpu.roll(y, 16, 1), pltpu.roll(y, LANES - 16, 1))
        y = y * cosv + sw * sinv
        if j < ATT_HEADS:
            yt = (y * Q_SCALE).T
            qz_ref[0, j, 0] = jnp.where(first, yt, 0.0).astype(BF16)
            qz_ref[0, j, 1] = jnp.where(first, 0.0, yt).astype(BF16)
        else:
            k_ref[0, j - ATT_HEADS] = y.astype(BF16)


def _qkprep(p_qk, cos_t, sin_t, gq, gk):
    B, S, _ = p_qk.shape
    tm = ROW_TILE
    return pl.pallas_call(
        _qkprep_kernel,
        out_shape=(jax.ShapeDtypeStruct((B, ATT_HEADS, 2, LANES, S), BF16),
                   jax.ShapeDtypeStruct((B, ATT_HEADS, S, LANES), BF16)),
        grid=(B, S // tm),
        in_specs=[pl.BlockSpec((1, tm, 2 * ATT_WIDTH), lambda b, i: (b, i, 0)),
                  pl.BlockSpec((tm, LANES), lambda b, i: (i, 0)),
                  pl.BlockSpec((tm, LANES), lambda b, i: (i, 0)),
                  pl.BlockSpec((1, LANES), lambda b, i: (0, 0)),
                  pl.BlockSpec((1, LANES), lambda b, i: (0, 0))],
        out_specs=(pl.BlockSpec((1, ATT_HEADS, 2, LANES, tm), lambda b, i: (b, 0, 0, 0, i)),
                   pl.BlockSpec((1, ATT_HEADS, tm, LANES), lambda b, i: (b, 0, i, 0))),
        compiler_params=_cparams(("parallel", "parallel")),
        name="qkprep",
    )(p_qk, cos_t, sin_t, gq, gk)


def _attn_kernel(lam_ref, qz_ref, k_ref, v_ref, g_ref, *rest, aliased, tk, n_kv):
    if aliased:
        rest = rest[1:]
    o_ref, s_sc, mt_sc, m_sc, l_sc, acc_sc = rest
    m_sc[...] = jnp.full(m_sc.shape, -jnp.inf, F32)
    l_sc[...] = jnp.zeros(l_sc.shape, F32)
    acc_sc[...] = jnp.zeros(acc_sc.shape, F32)

    def rows(j):
        if isinstance(j, int):
            return pl.ds(j * tk, tk)
        return pl.ds(pl.multiple_of(j * tk, tk), tk)

    def qk_tile(j, slot, ps=(0, 1)):
        kt = k_ref[0, 0, rows(j), :]
        for p in ps:
            s = _dot(kt, qz_ref[0, 0, p])
            s_sc[slot, p] = s
            mt_sc[slot, p] = jnp.max(s, axis=0, keepdims=True)

    def pv_tile(j, slot, ps=(0, 1)):
        for p in ps:
            m_prev = m_sc[p]
            m_new = jnp.maximum(m_prev, mt_sc[slot, p])
            alpha = jnp.exp2(m_prev - m_new)
            lsum = None
            acc = None
            for c in range(tk // KV_CHUNK):
                cs = slice(c * KV_CHUNK, (c + 1) * KV_CHUNK)
                pe = jnp.exp2(s_sc[slot, p, cs, :] - m_new)
                ps = jnp.sum(pe, axis=0, keepdims=True)
                pv = _dot(v_ref[0, 0, j, :, cs], pe.astype(BF16))
                lsum = ps if lsum is None else lsum + ps
                acc = pv if acc is None else acc + pv
            l_sc[p] = alpha * l_sc[p] + lsum
            acc_sc[p] = alpha * acc_sc[p] + acc
            m_sc[p] = m_new

    def fused_tile(jq, slot_q, jp, slot_p):
        m_new = [jnp.maximum(m_sc[p], mt_sc[slot_p, p]) for p in range(2)]
        alpha = [jnp.exp2(m_sc[p] - m_new[p]) for p in range(2)]
        mx, lsum, acc = [None, None], [None, None], [None, None]
        for c in range(tk // KV_CHUNK):
            cs = slice(c * KV_CHUNK, (c + 1) * KV_CHUNK)
            if isinstance(jq, int):
                kr = pl.ds(jq * tk + c * KV_CHUNK, KV_CHUNK)
            else:
                kr = pl.ds(pl.multiple_of(jq * tk + c * KV_CHUNK, KV_CHUNK), KV_CHUNK)
            kc = k_ref[0, 0, kr, :]
            vc = v_ref[0, 0, jp, :, cs]
            for p in range(2):
                s = _dot(kc, qz_ref[0, 0, p])
                s_sc[slot_q, p, cs, :] = s
                cm = jnp.max(s, axis=0, keepdims=True)
                mx[p] = cm if mx[p] is None else jnp.maximum(mx[p], cm)
                pe = jnp.exp2(s_sc[slot_p, p, cs, :] - m_new[p])
                ps = jnp.sum(pe, axis=0, keepdims=True)
                pv = _dot(vc, pe.astype(BF16))
                lsum[p] = ps if lsum[p] is None else lsum[p] + ps
                acc[p] = pv if acc[p] is None else acc[p] + pv
        for p in range(2):
            mt_sc[slot_q, p] = mx[p]
            l_sc[p] = alpha[p] * l_sc[p] + lsum[p]
            acc_sc[p] = alpha[p] * acc_sc[p] + acc[p]
            m_sc[p] = m_new[p]

    qk_tile(0, 0)
    n_pairs = (n_kv - 1) // 2

    def pair(jj, carry):
        j = 2 * jj
        fused_tile(j + 1, 1, j, 0)
        fused_tile(j + 2, 0, j + 1, 1)
        return carry

    if n_pairs > 0:
        lax.fori_loop(0, n_pairs, pair, 0)
    j = 2 * n_pairs
    if j == n_kv - 1:
        pv_tile(j, 0)
    else:
        qk_tile(j + 1, 1)
        pv_tile(j, 0)
        pv_tile(j + 1, 1)

    o = acc_sc[0] / l_sc[0] - lam_ref[0] * (acc_sc[1] / l_sc[1])
    ms = jnp.mean(o * o, axis=0, keepdims=True)
    o_ref[0] = ((o * lax.rsqrt(ms + EPS)).T * g_ref[...]).astype(BF16)


def _attention(lam, qz, k, vt, g, *, tq, tk, q_blk0, n_q, kv_blk0, n_kv, prev_out=None, name="attn_ctx"):
    B, H, _, _, S = qz.shape
    aliased = prev_out is not None
    skv = tk * n_kv
    in_specs = [pl.BlockSpec(memory_space=pltpu.SMEM),
                pl.BlockSpec((1, 1, 2, LANES, tq), lambda b, h, i: (b, h, 0, 0, q_blk0 + i)),
                pl.BlockSpec((1, 1, skv, LANES), lambda b, h, i: (b, h, kv_blk0, 0)),
                pl.BlockSpec((1, 1, n_kv, LANES, tk), lambda b, h, i: (b, h, 0, 0, 0)),
                pl.BlockSpec((1, LANES), lambda b, h, i: (0, 0))]
    args = [lam, qz, k, vt, g]
    aliases = {}
    if aliased:
        in_specs.append(pl.BlockSpec(memory_space=pl.ANY))
        args.append(prev_out)
        aliases = {5: 0}
    return pl.pallas_call(
        functools.partial(_attn_kernel, aliased=aliased, tk=tk, n_kv=n_kv),
        out_shape=jax.ShapeDtypeStruct((B, S, ATT_WIDTH), BF16),
        grid=(B, H, n_q),
        in_specs=in_specs,
        out_specs=pl.BlockSpec((1, tq, LANES), lambda b, h, i: (b, q_blk0 + i, h)),
        scratch_shapes=[pltpu.VMEM((2, 2, tk, tq), F32), pltpu.VMEM((2, 2, 1, tq), F32),
                        pltpu.VMEM((2, 1, tq), F32), pltpu.VMEM((2, 1, tq), F32),
                        pltpu.VMEM((2, LANES, tq), F32)],
        input_output_aliases=aliases,
        compiler_params=_cparams(("parallel", "parallel", "arbitrary")),
        name=name,
    )(*args)


def _attention_lat(lam, qz, k, vt, g, L):
    B, H, _, S = vt.shape
    tq = _pick_tile(L, (512, 256))
    tk = _pick_tile(S, (1280, 640, 256))
    n_kv = S // tk
    vt_tiles = vt.reshape(B, H, LANES, n_kv, tk).transpose(0, 1, 3, 2, 4)
    out0 = jnp.zeros((B, S, ATT_WIDTH), BF16)
    return _attention(lam, qz, k, vt_tiles, g, tq=tq, tk=tk, q_blk0=0, n_q=L // tq, kv_blk0=0, n_kv=n_kv,
                      prev_out=out0, name="attn_lat")


def _attention_ctx(lam, qz, k, vt, g, L, prev_out):
    B, H, _, S = vt.shape
    Lc = S - L
    vt_ctx = vt[:, :, :, L:].reshape(B, H, 1, LANES, Lc)
    return _attention(lam, qz, k, vt_ctx, g, tq=Lc, tk=Lc, q_blk0=L // Lc, n_q=1, kv_blk0=L // Lc, n_kv=1,
                      prev_out=prev_out)


def _gdnprep_kernel(x_ref, prev_ref, next_ref, cw_ref, ab_ref, alog_ref, dtb_ref,
                    q_ref, k_ref, v_ref, gate_ref, ext_sc, *, nb_lat, nb_all):
    i = pl.program_id(1)
    tm = x_ref.shape[1]
    first = jnp.logical_or(i == 0, i == nb_lat)
    last = jnp.logical_or(i == nb_lat - 1, i == nb_all - 1)
    keep_prev = jnp.where(first, 0.0, 1.0)
    keep_next = jnp.where(last, 0.0, 1.0)
    ext_sc[0:8, :] = prev_ref[0] * keep_prev
    ext_sc[8:8 + tm, :] = x_ref[0]
    ext_sc[8 + tm:16 + tm, :] = next_ref[0] * keep_next
    acc = None
    for j in range(CONV_W):
        term = ext_sc[pl.ds(8 + j - CONV_W // 2, tm), :] * cw_ref[j:j + 1, :]
        acc = term if acc is None else acc + term
    y = acc * _sigmoid(acc)
    for part, ref in ((0, q_ref), (1, k_ref)):
        for hh in range(GDN_HEADS):
            lo = part * GDN_WIDTH + hh * GDN_HEAD_DIM
            t = y[:, lo:lo + GDN_HEAD_DIM]
            ref[0, :, hh * GDN_HEAD_DIM:(hh + 1) * GDN_HEAD_DIM] = (
                t * lax.rsqrt(jnp.sum(t * t, axis=-1, keepdims=True) + EPS))
    v_ref[0] = y[:, 2 * GDN_WIDTH:3 * GDN_WIDTH]
    ab = ab_ref[0]
    xs = ab + dtb_ref[...]
    sp = jnp.maximum(xs, 0.0) + jnp.log(1.0 + jnp.exp(-jnp.abs(xs)))
    g = -jnp.exp(alog_ref[...]) * sp
    lane = lax.broadcasted_iota(jnp.int32, ab.shape, 1)
    gate_ref[0] = jnp.where(lane < 2 * GDN_HEADS, g, _sigmoid(ab))


def _gdnprep(p_gdn, conv_w8, ab, alog_row, dtb_row, n_lat_blocks):
    B, S, W = p_gdn.shape
    tm = ROW_TILE
    nb = S // tm
    r8 = tm // 8
    row = lambda b, i: (b, i, 0)
    return pl.pallas_call(
        functools.partial(_gdnprep_kernel, nb_lat=n_lat_blocks, nb_all=nb),
        out_shape=(jax.ShapeDtypeStruct((B, S, GDN_WIDTH), F32),
                   jax.ShapeDtypeStruct((B, S, GDN_WIDTH), F32),
                   jax.ShapeDtypeStruct((B, S, GDN_WIDTH), F32),
                   jax.ShapeDtypeStruct((B, S, LANES), F32)),
        grid=(B, nb),
        in_specs=[pl.BlockSpec((1, tm, W), row),
                  pl.BlockSpec((1, 8, W), lambda b, i: (b, jnp.maximum(i * r8 - 1, 0), 0)),
                  pl.BlockSpec((1, 8, W), lambda b, i: (b, jnp.minimum((i + 1) * r8, nb * r8 - 1), 0)),
                  pl.BlockSpec((8, W), lambda b, i: (0, 0)),
                  pl.BlockSpec((1, tm, LANES), row),
                  pl.BlockSpec((1, LANES), lambda b, i: (0, 0)),
                  pl.BlockSpec((1, LANES), lambda b, i: (0, 0))],
        out_specs=(pl.BlockSpec((1, tm, GDN_WIDTH), row),
                   pl.BlockSpec((1, tm, GDN_WIDTH), row),
                   pl.BlockSpec((1, tm, GDN_WIDTH), row),
                   pl.BlockSpec((1, tm, LANES), row)),
        scratch_shapes=[pltpu.VMEM((tm + 16, W), F32)],
        compiler_params=_cparams(("parallel", "parallel")),
        name="gdnprep",
    )(p_gdn, p_gdn, p_gdn, conv_w8, ab, alog_row, dtb_row)


def _gdnchunk_kernel(q_ref, k_ref, v_ref, gate_ref, qg_ref, w_ref, kd_ref, u_ref, aq_ref, eg_ref):
    tm = q_ref.shape[1]
    nc = tm // GDN_CHUNK
    gates = gate_ref[0]
    ri = lax.broadcasted_iota(jnp.int32, (tm, tm), 0)
    ci = lax.broadcasted_iota(jnp.int32, (tm, tm), 1)
    same = (ri >> 6) == (ci >> 6)
    eye = ri == ci
    eye_f = jnp.where(eye, 1.0, 0.0).astype(F32)

    g1 = gates.astype(BF16)
    r1 = gates - g1.astype(F32)
    g2 = r1.astype(BF16)
    g3 = (r1 - g2.astype(F32)).astype(BF16)
    gparts = jnp.concatenate([g1, g2, g3], axis=1)

    def seg_sum(mask):
        r = _dot(jnp.where(mask, 1.0, 0.0).astype(BF16), gparts)
        return (r[:, :LANES] + r[:, LANES:2 * LANES]) + r[:, 2 * LANES:]

    tot = seg_sum(same)

    heads = []
    for hh in range(GDN_HEADS):
        sl = slice(hh * GDN_HEAD_DIM, (hh + 1) * GDN_HEAD_DIM)
        k = k_ref[0, :, sl]
        k16 = k.astype(BF16)
        q = q_ref[0, :, sl] * (GDN_HEAD_DIM ** -0.5)
        heads.append((sl, k, q, _dot_nt(k16, k16), _dot_nt(q.astype(BF16), k16)))

    chains = []
    for d in range(2):
        incl = jnp.logical_and(same, (ci <= ri) if d == 0 else (ci >= ri))
        strict = jnp.logical_and(incl, jnp.logical_not(eye))
        gc = seg_sum(incl)
        gc_t = gc.T
        for hh in range(GDN_HEADS):
            ln = d * GDN_HEADS + hh
            gcol = gc[:, ln:ln + 1]
            bcol = gates[:, 2 * GDN_HEADS + ln:2 * GDN_HEADS + ln + 1]
            tcol = tot[:, ln:ln + 1]
            diff = gcol - gc_t[ln:ln + 1, :]
            decay = jnp.where(incl, jnp.exp(jnp.where(incl, diff, 0.0)), 0.0)
            a = jnp.where(strict, heads[hh][3] * bcol * decay, 0.0)
            aqk = heads[hh][4] * decay
            chains.append(dict(d=d, hh=hh, x=-a, t=eye_f - a, aqk=aqk, gcol=gcol, bcol=bcol, tcol=tcol))

    for _ in range(5):
        for ch in chains:
            x16 = ch["x"].astype(BF16)
            ch["x"] = _dot(x16, x16)
        for ch in chains:
            ch["t"] = ch["t"] + _dot(ch["t"].astype(BF16), ch["x"].astype(BF16))

    for ch in chains:
        d, hh, gcol, bcol, tcol = ch["d"], ch["hh"], ch["gcol"], ch["bcol"], ch["tcol"]
        sl, k, q = heads[hh][0], heads[hh][1], heads[hh][2]
        t16 = ch["t"].astype(BF16)
        egc = jnp.exp(gcol)
        kb = k * bcol
        u_ref[0, d, :, sl] = _dot(t16, (v_ref[0, :, sl] * bcol).astype(BF16))
        w_ref[0, d, :, sl] = _dot(t16, (kb * egc).astype(BF16)).astype(BF16)
        kd_ref[0, d, :, sl] = (k * jnp.exp(tcol - gcol)).astype(BF16)
        qg_ref[0, d, :, sl] = (q * egc).astype(BF16)
        for cc in range(nc):
            rs = slice(cc * GDN_CHUNK, (cc + 1) * GDN_CHUNK)
            aq_ref[0, d, rs, hh * GDN_CHUNK:(hh + 1) * GDN_CHUNK] = ch["aqk"][rs, rs].astype(BF16)
            eg_ref[0, d, cc, hh:hh + 1, :] = jnp.broadcast_to(
                jnp.exp(tcol[cc * GDN_CHUNK:cc * GDN_CHUNK + 1, :]), (1, LANES))


def _gdnchunk(gq, gk, gv, gates):
    B, S, W = gq.shape
    tm = ROW_TILE
    nc = tm // GDN_CHUNK
    row = lambda b, i: (b, i, 0)
    drow = lambda b, i: (b, 0, i, 0)
    big = lambda dt: jax.ShapeDtypeStruct((B, 2, S, W), dt)
    return pl.pallas_call(
        _gdnchunk_kernel,
        out_shape=(big(BF16), big(BF16), big(BF16), big(F32),
                   jax.ShapeDtypeStruct((B, 2, S, GDN_HEADS * GDN_CHUNK), BF16),
                   jax.ShapeDtypeStruct((B, 2, S // GDN_CHUNK, GDN_HEADS, LANES), F32)),
        grid=(B, S // tm),
        in_specs=[pl.BlockSpec((1, tm, W), row), pl.BlockSpec((1, tm, W), row),
                  pl.BlockSpec((1, tm, W), row), pl.BlockSpec((1, tm, LANES), row)],
        out_specs=(pl.BlockSpec((1, 2, tm, W), drow), pl.BlockSpec((1, 2, tm, W), drow),
                   pl.BlockSpec((1, 2, tm, W), drow), pl.BlockSpec((1, 2, tm, W), drow),
                   pl.BlockSpec((1, 2, tm, GDN_HEADS * GDN_CHUNK), drow),
                   pl.BlockSpec((1, 2, nc, GDN_HEADS, LANES), lambda b, i: (b, 0, i, 0, 0))),
        compiler_params=_cparams(("parallel", "parallel")),
        name="gdnchunk",
    )(gq, gk, gv, gates)


def _gdnscan_kernel(qg0, w0, kd0, u0, aq0, eg0, qg1, w1, kd1, u1, aq1, eg1, of_ref, ob_ref, s_sc):
    @pl.when(pl.program_id(1) == 0)
    def _():
        s_sc[...] = jnp.zeros(s_sc.shape, F32)

    dirs = ((qg0, w0, kd0, u0, aq0, eg0, of_ref), (qg1, w1, kd1, u1, aq1, eg1, ob_ref))
    chains = [(d, hh) for d in range(2) for hh in range(GDN_HEADS)]
    st = {ch: s_sc[ch[0], ch[1]] for ch in chains}
    for step in range(SCAN_GROUP):
        st16, vn16, qs = {}, {}, {}
        for d, hh in chains:
            st16[d, hh] = st[d, hh].astype(BF16)
        for d, hh in chains:
            qg, w, kd, u, aq, eg, o_ref = dirs[d]
            cc = step if d == 0 else SCAN_GROUP - 1 - step
            rs = slice(cc * GDN_CHUNK, (cc + 1) * GDN_CHUNK)
            sl = slice(hh * GDN_HEAD_DIM, (hh + 1) * GDN_HEAD_DIM)
            vn16[d, hh] = (u[0, 0, rs, sl] - _dot(w[0, 0, rs, sl], st16[d, hh])).astype(BF16)
            qs[d, hh] = _dot(qg[0, 0, rs, sl], st16[d, hh])
        for d, hh in chains:
            qg, w, kd, u, aq, eg, o_ref = dirs[d]
            cc = step if d == 0 else SCAN_GROUP - 1 - step
            rs = slice(cc * GDN_CHUNK, (cc + 1) * GDN_CHUNK)
            sl = slice(hh * GDN_HEAD_DIM, (hh + 1) * GDN_HEAD_DIM)
            o_ref[0, rs, sl] = qs[d, hh] + _dot(aq[0, 0, rs, hh * GDN_CHUNK:(hh + 1) * GDN_CHUNK], vn16[d, hh])
            st[d, hh] = st[d, hh] * eg[0, 0, cc, hh:hh + 1, :] + _dot_tn(kd[0, 0, rs, sl], vn16[d, hh])
    for d, hh in chains:
        s_sc[d, hh] = st[d, hh]


def _gdnscan(qg, w, kd, u, aq, eg, n_lat_chunks, n_ctx_chunks):
    B, _, S, W = qg.shape
    G = SCAN_GROUP
    R = GDN_CHUNK * G
    n = S // R
    n_lat, n_ctx = n_lat_chunks // G, n_ctx_chunks // G
    assert n_lat * G == n_lat_chunks and n_ctx * G == n_ctx_chunks

    def fwd_blk(i):
        return jnp.where(i < n_ctx, n_lat + i, i - n_ctx)

    def bwd_blk(i):
        return jnp.where(i < n_ctx, n_lat + n_ctx - 1 - i, n - 1 - i)

    def specs(d, blk_of):
        big = pl.BlockSpec((1, 1, R, W), lambda b, i: (b, d, blk_of(i), 0))
        return [big, big, big, big,
                pl.BlockSpec((1, 1, R, GDN_HEADS * GDN_CHUNK), lambda b, i: (b, d, blk_of(i), 0)),
                pl.BlockSpec((1, 1, G, GDN_HEADS, LANES), lambda b, i: (b, d, blk_of(i), 0, 0))]

    return pl.pallas_call(
        _gdnscan_kernel,
        out_shape=(jax.ShapeDtypeStruct((B, S, W), F32), jax.ShapeDtypeStruct((B, S, W), F32)),
        grid=(B, n),
        in_specs=specs(0, fwd_blk) + specs(1, bwd_blk),
        out_specs=(pl.BlockSpec((1, R, W), lambda b, i: (b, fwd_blk(i), 0)),
                   pl.BlockSpec((1, R, W), lambda b, i: (b, bwd_blk(i), 0))),
        scratch_shapes=[pltpu.VMEM((2, GDN_HEADS, GDN_HEAD_DIM, GDN_HEAD_DIM), F32)],
        compiler_params=_cparams(("parallel", "arbitrary")),
        name="gdnscan",
    )(qg, w, kd, u, aq, eg, qg, w, kd, u, aq, eg)


def _mixout_kernel(oa_ref, of_ref, ob_ref, z_ref, h_ref, mod_ref, gg_ref, wo_ref, gf_ref, rw_ref, rb_ref,
                   hn_ref, v_ref, te_ref, tg_ref):
    og = of_ref[0] + ob_ref[0]
    z = z_ref[0]
    parts = [oa_ref[0]]
    for hh in range(GDN_HEADS):
        sl = slice(hh * GDN_HEAD_DIM, (hh + 1) * GDN_HEAD_DIM)
        t = og[:, sl]
        t = t * lax.rsqrt(jnp.mean(t * t, axis=-1, keepdims=True) + EPS) * gg_ref[...]
        zz = z[:, sl]
        parts.append((t * (zz * _sigmoid(zz))).astype(BF16))
    mix_in = jnp.concatenate(parts, axis=-1)
    mix = _dot(mix_in, wo_ref[...])
    hn = h_ref[0] + mod_ref[0, 0, 2:3, :] * mix
    hn_ref[0] = hn
    y = hn * lax.rsqrt(jnp.mean(hn * hn, axis=-1, keepdims=True) + EPS) * gf_ref[...]
    v = y * (1.0 + mod_ref[0, 0, 4:5, :]) + mod_ref[0, 0, 3:4, :]
    v_ref[0] = v.astype(BF16)
    v_hi = v.astype(BF16)
    v_lo = (v - v_hi.astype(F32)).astype(BF16)
    logits = (_dot(v_hi, rw_ref[0]) + _dot(v_lo, rw_ref[0]) + _dot(v_hi, rw_ref[1])) + rb_ref[...]
    lane = lax.broadcasted_iota(jnp.int32, logits.shape, 1)
    cur = logits
    vals, idxs = [], []
    for _ in range(TOP_K):
        m = jnp.max(cur, axis=-1, keepdims=True)
        idx = jnp.min(jnp.where(cur == m, lane, LANES), axis=-1, keepdims=True)
        vals.append(m)
        idxs.append(idx)
        cur = jnp.where(lane == idx, -jnp.inf, cur)
    es = [jnp.exp(vv - vals[0]) for vv in vals]
    inv = 1.0 / (es[0] + es[1] + es[2] + es[3])
    te = jnp.zeros(logits.shape, jnp.int32)
    tg = jnp.zeros(logits.shape, F32)
    for kk in range(TOP_K):
        te = jnp.where(lane == kk, idxs[kk], te)
        tg = jnp.where(lane == kk, es[kk] * inv, tg)
    te_ref[0] = te
    tg_ref[0] = tg


def _mixout(o_att, o_f, o_b, z, h, modtab, gg, w_out, gf, rw, rb, n_lat_blocks):
    B, S, _ = h.shape
    tm = ROW_TILE
    row = lambda b, i: (b, i, 0)
    const = lambda b, i: (0, 0)
    return pl.pallas_call(
        _mixout_kernel,
        out_shape=(jax.ShapeDtypeStruct((B, S, D_MODEL), F32),
                   jax.ShapeDtypeStruct((B, S, D_MODEL), BF16),
                   jax.ShapeDtypeStruct((B, S, LANES), jnp.int32),
                   jax.ShapeDtypeStruct((B, S, LANES), F32)),
        grid=(B, S // tm),
        in_specs=[pl.BlockSpec((1, tm, ATT_WIDTH), row),
                  pl.BlockSpec((1, tm, GDN_WIDTH), row),
                  pl.BlockSpec((1, tm, GDN_WIDTH), row),
                  pl.BlockSpec((1, tm, GDN_WIDTH), row),
                  pl.BlockSpec((1, tm, D_MODEL), row),
                  pl.BlockSpec((1, 1, 8, D_MODEL), lambda b, i: (b, (i >= n_lat_blocks).astype(jnp.int32), 0, 0)),
                  pl.BlockSpec((1, LANES), const),
                  pl.BlockSpec((D_MODEL, D_MODEL), const),
                  pl.BlockSpec((1, D_MODEL), const),
                  pl.BlockSpec((2, D_MODEL, LANES), lambda b, i: (0, 0, 0)),
                  pl.BlockSpec((1, LANES), const)],
        out_specs=(pl.BlockSpec((1, tm, D_MODEL), row),
                   pl.BlockSpec((1, tm, D_MODEL), row),
                   pl.BlockSpec((1, tm, LANES), row),
                   pl.BlockSpec((1, tm, LANES), row)),
        compiler_params=_cparams(("parallel", "parallel")),
        name="mixout",
    )(o_att, o_f, o_b, z, h, modtab, gg, w_out, gf, rw, rb)


def _expert_kernel(be_ref, nv_ref, x_ref, wgu_ref, bgu_ref, wd_ref, bd_ref, y_ref, wgu_sc, wd_sc):
    i = pl.program_id(0)
    new_expert = jnp.logical_or(i == 0, be_ref[i] != be_ref[jnp.maximum(i - 1, 0)])

    @pl.when(new_expert)
    def _():
        wgu_sc[...] = wgu_ref[0, 0].astype(BF16)
        wd_sc[...] = wd_ref[0, 0].astype(BF16)

    @pl.when(nv_ref[i] > 0)
    def _():
        gu = _dot(x_ref[...], wgu_sc[...]) + bgu_ref[0, 0]
        g_ = jnp.minimum(gu[:, :D_EXPERT], SWIGLU_LIMIT)
        up = jnp.clip(gu[:, D_EXPERT:], -SWIGLU_LIMIT, SWIGLU_LIMIT)
        glu = g_ * _sigmoid(SWIGLU_ALPHA * g_)
        act = ((up + 1.0) * glu).astype(BF16)
        y_ref[...] = (_dot(act, wd_sc[...]) + bd_ref[0, 0]).astype(y_ref.dtype)

    @pl.when(nv_ref[i] == 0)
    def _():
        y_ref[...] = jnp.zeros(y_ref.shape, y_ref.dtype)


def _experts(block_e, n_valid, x_sorted, wgu, bgu, wd, bd, layer):
    n_slots = x_sorted.shape[0]
    nb = n_slots // MOE_BLOCK
    grid_spec = pltpu.PrefetchScalarGridSpec(
        num_scalar_prefetch=2,
        grid=(nb,),
        in_specs=[pl.BlockSpec((MOE_BLOCK, D_MODEL), lambda i, be, nv: (i, 0)),
                  pl.BlockSpec((1, 1, D_MODEL, 2 * D_EXPERT), lambda i, be, nv: (layer, be[i], 0, 0)),
                  pl.BlockSpec((1, 1, 1, 2 * D_EXPERT), lambda i, be, nv: (layer, be[i], 0, 0)),
                  pl.BlockSpec((1, 1, D_EXPERT, D_MODEL), lambda i, be, nv: (layer, be[i], 0, 0)),
                  pl.BlockSpec((1, 1, 1, D_MODEL), lambda i, be, nv: (layer, be[i], 0, 0))],
        out_specs=pl.BlockSpec((MOE_BLOCK, D_MODEL), lambda i, be, nv: (i, 0)),
        scratch_shapes=[pltpu.VMEM((D_MODEL, 2 * D_EXPERT), BF16), pltpu.VMEM((D_EXPERT, D_MODEL), BF16)],
    )
    return pl.pallas_call(
        _expert_kernel,
        out_shape=jax.ShapeDtypeStruct((n_slots, D_MODEL), BF16),
        grid_spec=grid_spec,
        compiler_params=pltpu.CompilerParams(dimension_semantics=("arbitrary",),
                                             vmem_limit_bytes=EXPERT_VMEM_LIMIT),
        name="experts",
    )(block_e, n_valid, x_sorted, wgu, bgu, wd, bd)


def _moe_plan(top_e, top_g):
    T = top_e.shape[0]
    n_assign = T * TOP_K
    n_blocks = -(-n_assign // MOE_BLOCK) + N_EXPERTS
    n_slots = n_blocks * MOE_BLOCK
    i32 = jnp.int32
    flat_e = top_e.reshape(-1)
    gate_flat = top_g.reshape(-1)
    order = jnp.argsort(flat_e, stable=True).astype(i32)
    e_ids = jnp.arange(N_EXPERTS, dtype=i32)
    is_e = flat_e[:, None] == e_ids[None, :]
    counts = jnp.sum(is_e, axis=0, dtype=i32)
    start = jnp.cumsum(counts) - counts
    padded = (counts + MOE_BLOCK - 1) // MOE_BLOCK * MOE_BLOCK
    pad_end = jnp.cumsum(padded)
    pad_start = pad_end - padded
    blk0 = jnp.arange(n_blocks, dtype=i32) * MOE_BLOCK
    block_e = jnp.minimum(jnp.sum(pad_end[None, :] <= blk0[:, None], axis=1, dtype=i32), N_EXPERTS - 1)
    off = (blk0 - pad_start[block_e])[:, None] + jnp.arange(MOE_BLOCK, dtype=i32)[None, :]
    valid = off < counts[block_e][:, None]
    a_slot = order[jnp.clip(off + start[block_e][:, None], 0, n_assign - 1).reshape(-1)]
    valid_flat = valid.reshape(-1)
    slot_tok = jnp.where(valid_flat, a_slot // TOP_K, jnp.arange(n_slots, dtype=i32) % T)
    n_valid = jnp.sum(valid, axis=1, dtype=i32)
    assert T < NO_TOKEN
    gate_bits = lax.bitcast_convert_type(gate_flat.astype(BF16), jnp.uint16).astype(jnp.uint32)
    tok_pack = (gate_bits << 16) | (jnp.arange(n_assign, dtype=jnp.uint32) // TOP_K)
    slot_pack = jnp.where(valid_flat, tok_pack[a_slot], jnp.uint32(NO_TOKEN))

    n_tiles = T // COMBINE_TILE
    cnt = jnp.sum(top_e.reshape(n_tiles, COMBINE_TILE * TOP_K)[:, :, None] == e_ids[None, None, :], axis=1, dtype=i32)
    run_start = pad_start[None, :] + jnp.cumsum(cnt, axis=0) - cnt
    q_first = run_start // COMBINE_CHUNK
    n_ch = jnp.where(cnt > 0, (run_start + cnt - 1) // COMBINE_CHUNK - q_first + 1, 0)
    ch_end = jnp.cumsum(n_ch, axis=1)
    ch_off = ch_end - n_ch
    j = jnp.arange(COMBINE_MAX_CHUNKS, dtype=i32)
    e_j = jnp.minimum(jnp.sum(ch_end[:, None, :] <= j[None, :, None], axis=2, dtype=i32), N_EXPERTS - 1)
    used = j[None, :] < ch_end[:, -1:]
    chunk_id = jnp.where(used, jnp.take_along_axis(q_first, e_j, axis=1)
                         + j[None, :] - jnp.take_along_axis(ch_off, e_j, axis=1), 0)
    slot_pack = slot_pack.reshape(n_slots // COMBINE_CHUNK, COMBINE_CHUNK)
    row_pack = jnp.where(used[:, :, None], slot_pack[chunk_id], jnp.uint32(NO_TOKEN))
    row_pack = lax.bitcast_convert_type(row_pack, i32).reshape(n_tiles, 1, COMBINE_MAX_CHUNKS * COMBINE_CHUNK)
    return slot_tok, block_e, n_valid, chunk_id, row_pack


def _combine_kernel(cid_ref, tok_ref, h_ref, mod_ref, y_hbm, o_ref, ybuf, sem):
    i = pl.program_id(0)
    n = pl.num_programs(0)
    slot = lax.rem(i, 2)

    def chunk_copy(c, s, jj):
        return pltpu.make_async_copy(
            y_hbm.at[pl.ds(pl.multiple_of(c * COMBINE_CHUNK, COMBINE_CHUNK), COMBINE_CHUNK)],
            ybuf.at[s, pl.ds(jj * COMBINE_CHUNK, COMBINE_CHUNK)], sem.at[s])

    def start_tile(t, s):
        for jj in range(COMBINE_MAX_CHUNKS):
            chunk_copy(cid_ref[t, jj], s, jj).start()

    @pl.when(i == 0)
    def _():
        start_tile(0, 0)

    @pl.when(i + 1 < n)
    def _():
        start_tile(i + 1, 1 - slot)

    for jj in range(COMBINE_MAX_CHUNKS):
        chunk_copy(0, slot, jj).wait()

    packed = tok_ref[0]
    tok = packed & NO_TOKEN
    gate = lax.bitcast_convert_type(packed & ~NO_TOKEN, F32)
    t_ids = i * COMBINE_TILE + lax.broadcasted_iota(jnp.int32, (COMBINE_TILE, tok.shape[1]), 0)
    sel = jnp.where(tok == t_ids, gate, 0.0).astype(BF16)
    y = _dot(sel, ybuf[slot])
    o_ref[...] = h_ref[...] + mod_ref[0, 0, 5:6, :] * y


def _combine(chunk_id, row_tok, y_sorted, h, modtab, n_lat_blocks):
    B, S, _ = h.shape
    T = B * S
    ct = COMBINE_TILE
    tiles_per_batch = S // ct
    n_lat_tiles = n_lat_blocks * ROW_TILE // ct
    n_rows = COMBINE_MAX_CHUNKS * COMBINE_CHUNK
    grid_spec = pltpu.PrefetchScalarGridSpec(
        num_scalar_prefetch=1,
        grid=(T // ct,),
        in_specs=[pl.BlockSpec((1, 1, n_rows), lambda i, cid: (i, 0, 0)),
                  pl.BlockSpec((ct, D_MODEL), lambda i, cid: (i, 0)),
                  pl.BlockSpec((1, 1, 8, D_MODEL),
                               lambda i, cid: (i // tiles_per_batch,
                                               (lax.rem(i, tiles_per_batch) >= n_lat_tiles).astype(jnp.int32), 0, 0)),
                  pl.BlockSpec(memory_space=pl.ANY)],
        out_specs=pl.BlockSpec((ct, D_MODEL), lambda i, cid: (i, 0)),
        scratch_shapes=[pltpu.VMEM((2, n_rows, D_MODEL), BF16), pltpu.SemaphoreType.DMA((2,))],
    )
    out = pl.pallas_call(
        _combine_kernel,
        out_shape=jax.ShapeDtypeStruct((T, D_MODEL), F32),
        grid_spec=grid_spec,
        compiler_params=_cparams(("arbitrary",)),
        name="combine",
    )(chunk_id, row_tok, h.reshape(T, D_MODEL), modtab, y_sorted)
    return out.reshape(B, S, D_MODEL)


def _moe(v_ffn, top_e, top_g, h_new, modtab, wgu, bgu, wd, bd, layer, n_lat_blocks):
    B, S, _ = v_ffn.shape
    T = B * S
    slot_tok, block_e, n_valid, chunk_id, row_tok = _moe_plan(
        top_e.reshape(T, LANES)[:, :TOP_K], top_g.reshape(T, LANES)[:, :TOP_K])
    x_sorted = v_ffn.reshape(T, D_MODEL)[slot_tok]
    y_sorted = _experts(block_e, n_valid, x_sorted, wgu, bgu, wd, bd, layer)
    return _combine(chunk_id, row_tok, y_sorted, h_new, modtab, n_lat_blocks)


def _pick_tile(n, cands):
    for t in cands:
        if n % t == 0:
            return t
    raise ValueError(f"no tile for {n}")


def _rope_tables(L, Lc):
    rows = L // GRID_W
    row = jnp.repeat(jnp.arange(rows, dtype=F32), GRID_W)
    col = (jnp.arange(L, dtype=jnp.int32) % GRID_W).astype(F32)
    inv_freq = ROPE_BASE ** (-jnp.arange(ROPE_PAIRS, dtype=F32) / ROPE_PAIRS)
    ar = row[:, None] * inv_freq
    ac = col[:, None] * inv_freq
    cos64 = jnp.concatenate([jnp.cos(ar), jnp.cos(ar), jnp.cos(ac), jnp.cos(ac)], axis=-1)
    sin64 = jnp.concatenate([-jnp.sin(ar), jnp.sin(ar), -jnp.sin(ac), jnp.sin(ac)], axis=-1)
    cos_t = jnp.concatenate([jnp.tile(cos64, (1, 2)), jnp.ones((Lc, LANES), F32)], axis=0)
    sin_t = jnp.concatenate([jnp.tile(sin64, (1, 2)), jnp.zeros((Lc, LANES), F32)], axis=0)
    return cos_t, sin_t


def _pad_lanes(v):
    v = v.reshape(1, -1).astype(F32)
    return jnp.pad(v, ((0, 0), (0, LANES - v.shape[1])))


def kernel(x, c, ctx, c_ctx, w_mod, b_mod, norm_mix_g, w_in, q_norm_g, k_norm_g, lam_q1, lam_k1, lam_q2, lam_k2, subln_g, conv_w, a_log, dt_bias, gdn_norm_g, w_out, norm_ffn_g, router_w, router_b, w_gate_up, b_gate_up, w_down, b_down):
    B, L, D = x.shape
    Lc = ctx.shape[1]
    S = L + Lc
    depth = w_mod.shape[0]
    tm = ROW_TILE
    n_lat_blocks = L // tm
    cos_t, sin_t = _rope_tables(L, Lc)

    c_rows = jnp.zeros((8, D), F32).at[:B].set(c).at[B].set(c_ctx)
    h = jnp.concatenate([x, ctx], axis=1)

    for layer in range(depth):
        mod = _adaln(c_rows, w_mod, b_mod, layer)
        mod6 = mod.reshape(8, 6, D)
        lat_mod = mod6[:B]
        ctx_mod = jnp.broadcast_to(mod6[B][None], (B, 6, D))
        modtab = jnp.pad(jnp.stack([lat_mod, ctx_mod], axis=1), ((0, 0), (0, 0), (0, 2), (0, 0)))

        lam_init = 0.8 - 0.6 * math.exp(-0.3 * layer)
        lam_full = (jnp.exp(jnp.sum(lam_q1[layer] * lam_k1[layer]))
                    - jnp.exp(jnp.sum(lam_q2[layer] * lam_k2[layer])) + lam_init).reshape(1).astype(F32)

        w_l = w_in[layer]
        w_main = w_l[:, :IN_MAIN].astype(BF16)
        w_ab = jnp.pad(w_l[:, IN_MAIN:], ((0, 0), (0, LANES - (w_l.shape[1] - IN_MAIN)))).astype(BF16)
        p_qk, v_att, p_gdn, z, ab = _inproj(h, modtab, norm_mix_g[layer].reshape(1, D), w_main, w_ab,
                                            n_lat_blocks)

        gq = jnp.tile(q_norm_g[layer].reshape(1, ATT_HEAD_DIM), (1, 2))
        gk = jnp.tile(k_norm_g[layer].reshape(1, ATT_HEAD_DIM), (1, 2))
        qz, k_att = _qkprep(p_qk, cos_t, sin_t, gq, gk)
        g_sub = (subln_g[layer] * (1.0 - lam_init)).reshape(1, LANES).astype(F32)
        o_att = _attention_lat(lam_full, qz, k_att, v_att, g_sub, L)
        o_att = _attention_ctx(lam_full, qz, k_att, v_att, g_sub, L, o_att)

        conv_w8 = jnp.pad(conv_w[layer], ((0, 8 - CONV_W), (0, 0)))
        gq_g, gk_g, gv_g, gates = _gdnprep(p_gdn, conv_w8, ab, _pad_lanes(a_log[layer]),
                                           _pad_lanes(dt_bias[layer]), n_lat_blocks)
        qg, w_g, kd, u_g, aq, eg = _gdnchunk(gq_g, gk_g, gv_g, gates)
        o_f, o_b = _gdnscan(qg, w_g, kd, u_g, aq, eg, L // GDN_CHUNK, Lc // GDN_CHUNK)

        rw = jnp.pad(router_w[layer].astype(F32), ((0, 0), (0, LANES - N_EXPERTS)))
        rw_hi = rw.astype(BF16)
        rw = jnp.stack([rw_hi, (rw - rw_hi.astype(F32)).astype(BF16)])
        rb = jnp.pad(router_b[layer].reshape(1, N_EXPERTS).astype(F32), ((0, 0), (0, LANES - N_EXPERTS)),
                     constant_values=-1e30)
        h_new, v_ffn, top_e, top_g = _mixout(
            o_att, o_f, o_b, z, h, modtab, gdn_norm_g[layer].reshape(1, LANES), w_out[layer].astype(BF16),
            norm_ffn_g[layer].reshape(1, D), rw, rb, n_lat_blocks)

        h = _moe(v_ffn, top_e, top_g, h_new, modtab,
                 w_gate_up, b_gate_up.reshape(depth, N_EXPERTS, 1, 2 * D_EXPERT),
                 w_down, b_down.reshape(depth, N_EXPERTS, 1, D), layer, n_lat_blocks)
    return h[:, :L]
```

```python
import functools
import math

import jax
import jax.numpy as jnp
from jax import lax
from jax.experimental import pallas as pl
from jax.experimental.pallas import tpu as pltpu

F32 = jnp.float32
BF16 = jnp.bfloat16
HIGHEST = lax.Precision.HIGHEST

D_MODEL = 1024
GRID_W = 64
EPS = 1e-6
ATT_WIDTH = 512
ATT_HEAD_DIM = 64
ATT_HEADS = 4
ROPE_BASE = 10000.0
ROPE_PAIRS = ATT_HEAD_DIM // 4
GDN_WIDTH = 512
GDN_HEAD_DIM = 128
GDN_HEADS = 4
GDN_CHUNK = 64
CONV_W = 5
IN_MAIN = 3 * ATT_WIDTH + 4 * GDN_WIDTH
N_EXPERTS = 32
TOP_K = 4
D_EXPERT = 1024
SWIGLU_ALPHA = 1.702
SWIGLU_LIMIT = 7.0
MOE_BLOCK = 512
SCAN_GROUP = 4
NO_TOKEN = 0xFFFF
COMBINE_TILE = 256
COMBINE_CHUNK = 16
COMBINE_MAX_CHUNKS = COMBINE_TILE * TOP_K // COMBINE_CHUNK + 2 * N_EXPERTS

LANES = 128
ROW_TILE = 256
KV_CHUNK = 256
Q_SCALE = ATT_HEAD_DIM ** -0.5 * math.log2(math.e)
VMEM_LIMIT = 48 * 1024 * 1024
EXPERT_VMEM_LIMIT = 56 * 1024 * 1024


def _cparams(sem):
    return pltpu.CompilerParams(dimension_semantics=sem, vmem_limit_bytes=VMEM_LIMIT)


def _dot(a, b):
    return jnp.dot(a, b, preferred_element_type=F32)


def _dot_nt(a, b):
    return lax.dot_general(a, b, (((1,), (1,)), ((), ())), preferred_element_type=F32)


def _dot_tn(a, b):
    return lax.dot_general(a, b, (((0,), (0,)), ((), ())), preferred_element_type=F32)


def _dot_hi(a, b):
    return jnp.dot(a, b, preferred_element_type=F32, precision=HIGHEST)


def _sigmoid(x):
    return 1.0 / (1.0 + jnp.exp(-x))


def _adaln_kernel(c_ref, w_ref, b_ref, o_ref):
    c = c_ref[...]
    s = c * _sigmoid(c)
    o_ref[...] = _dot_hi(s, w_ref[0]) + b_ref[0]


def _adaln(c_rows, w, b, layer):
    depth, _, n = w.shape
    tn = 1024
    return pl.pallas_call(
        _adaln_kernel,
        out_shape=jax.ShapeDtypeStruct((8, n), F32),
        grid=(n // tn,),
        in_specs=[pl.BlockSpec((8, D_MODEL), lambda j: (0, 0)),
                  pl.BlockSpec((1, D_MODEL, tn), lambda j: (layer, 0, j)),
                  pl.BlockSpec((1, 1, tn), lambda j: (layer, 0, j))],
        out_specs=pl.BlockSpec((8, tn), lambda j: (0, j)),
        compiler_params=_cparams(("arbitrary",)),
        name="adaln",
    )(c_rows, w, b.reshape(depth, 1, n))


def _qk_norm_rope(p_qk, cos_ref, sin_ref, gq_ref, gk_ref, qz_ref, k_ref):
    tm = p_qk.shape[0]
    lane = lax.broadcasted_iota(jnp.int32, (tm, LANES), 1)
    hi16 = (lane & 16) != 0
    first = lax.broadcasted_iota(jnp.int32, (LANES, tm), 0) < ATT_HEAD_DIM
    r = lax.broadcasted_iota(jnp.int32, (LANES, LANES), 0) >> 6
    c = lax.broadcasted_iota(jnp.int32, (LANES, LANES), 1) >> 6
    gmat = jnp.where(r == c, 1.0 / ATT_HEAD_DIM, 0.0).astype(F32)
    cosv = cos_ref[...]
    sinv = sin_ref[...]
    for j in range(2 * ATT_HEADS):
        x = p_qk[:, j * LANES:(j + 1) * LANES]
        ms = _dot_hi(x * x, gmat)
        g = gq_ref[...] if j < ATT_HEADS else gk_ref[...]
        y = x * lax.rsqrt(ms + EPS) * g
        sw = jnp.where(hi16, pltpu.roll(y, 16, 1), pltpu.roll(y, LANES - 16, 1))
        y = y * cosv + sw * sinv
        if j < ATT_HEADS:
            yt = (y * Q_SCALE).T
            qz_ref[0, j, 0] = jnp.where(first, yt, 0.0).astype(BF16)
            qz_ref[0, j, 1] = jnp.where(first, 0.0, yt).astype(BF16)
        else:
            k_ref[0, j - ATT_HEADS] = y.astype(BF16)


def _inproj_kernel(h_ref, mod_ref, g_ref, w_ref, wab_ref, cos_ref, sin_ref, gq_ref, gk_ref,
                   qz_ref, k_ref, v_ref, gdn_ref, z_ref, ab_ref):
    x = h_ref[0]
    ms = jnp.mean(x * x, axis=-1, keepdims=True)
    y = x * lax.rsqrt(ms + EPS) * g_ref[...]
    shift = mod_ref[0, 0, 0:1, :]
    scale = mod_ref[0, 0, 1:2, :]
    u = (y * (1.0 + scale) + shift).astype(BF16)
    _qk_norm_rope(_dot(u, w_ref[:, 0:2 * ATT_WIDTH]), cos_ref, sin_ref, gq_ref, gk_ref, qz_ref, k_ref)
    vv = _dot(u, w_ref[:, 2 * ATT_WIDTH:3 * ATT_WIDTH])
    for hh in range(ATT_HEADS):
        v_ref[0, hh] = vv[:, hh * LANES:(hh + 1) * LANES].T.astype(BF16)
    off = 3 * ATT_WIDTH
    gdn_ref[0] = _dot(u, w_ref[:, off:off + 3 * GDN_WIDTH])
    z_ref[0] = _dot(u, w_ref[:, off + 3 * GDN_WIDTH:off + 4 * GDN_WIDTH])
    ab_ref[0] = _dot(u, wab_ref[...])


def _inproj(h, modtab, g, w_main, w_ab, cos_t, sin_t, gq, gk, n_lat_blocks):
    B, S, _ = h.shape
    tm = ROW_TILE
    row = lambda b, i: (b, i, 0)
    vec = lambda b, i: (0, 0)
    return pl.pallas_call(
        _inproj_kernel,
        out_shape=(jax.ShapeDtypeStruct((B, ATT_HEADS, 2, LANES, S), BF16),
                   jax.ShapeDtypeStruct((B, ATT_HEADS, S, LANES), BF16),
                   jax.ShapeDtypeStruct((B, ATT_HEADS, LANES, S), BF16),
                   jax.ShapeDtypeStruct((B, S, 3 * GDN_WIDTH), F32),
                   jax.ShapeDtypeStruct((B, S, GDN_WIDTH), F32),
                   jax.ShapeDtypeStruct((B, S, LANES), F32)),
        grid=(B, S // tm),
        in_specs=[pl.BlockSpec((1, tm, D_MODEL), row),
                  pl.BlockSpec((1, 1, 8, D_MODEL), lambda b, i: (b, (i >= n_lat_blocks).astype(jnp.int32), 0, 0)),
                  pl.BlockSpec((1, D_MODEL), lambda b, i: (0, 0)),
                  pl.BlockSpec((D_MODEL, IN_MAIN), vec),
                  pl.BlockSpec((D_MODEL, LANES), vec),
                  pl.BlockSpec((tm, LANES), lambda b, i: (i, 0)),
                  pl.BlockSpec((tm, LANES), lambda b, i: (i, 0)),
                  pl.BlockSpec((1, LANES), vec),
                  pl.BlockSpec((1, LANES), vec)],
        out_specs=(pl.BlockSpec((1, ATT_HEADS, 2, LANES, tm), lambda b, i: (b, 0, 0, 0, i)),
                   pl.BlockSpec((1, ATT_HEADS, tm, LANES), lambda b, i: (b, 0, i, 0)),
                   pl.BlockSpec((1, ATT_HEADS, LANES, tm), lambda b, i: (b, 0, 0, i)),
                   pl.BlockSpec((1, tm, 3 * GDN_WIDTH), row),
                   pl.BlockSpec((1, tm, GDN_WIDTH), row),
                   pl.BlockSpec((1, tm, LANES), row)),
        compiler_params=_cparams(("parallel", "parallel")),
        name="inproj",
    )(h, modtab, g, w_main, w_ab, cos_t, sin_t, gq, gk)


def _attn_kernel(lam_ref, qz_ref, k_ref, v_ref, g_ref, *rest, aliased, tk, n_kv):
    if aliased:
        rest = rest[1:]
    o_ref, s_sc, mt_sc, m_sc, l_sc, acc_sc = rest
    m_sc[...] = jnp.full(m_sc.shape, -jnp.inf, F32)
    l_sc[...] = jnp.zeros(l_sc.shape, F32)
    acc_sc[...] = jnp.zeros(acc_sc.shape, F32)

    def rows(j):
        if isinstance(j, int):
            return pl.ds(j * tk, tk)
        return pl.ds(pl.multiple_of(j * tk, tk), tk)

    def qk_tile(j, slot, ps=(0, 1)):
        kt = k_ref[0, 0, rows(j), :]
        for p in ps:
            s = _dot(kt, qz_ref[0, 0, p])
            s_sc[slot, p] = s
            mt_sc[slot, p] = jnp.max(s, axis=0, keepdims=True)

    def pv_tile(j, slot, ps=(0, 1)):
        for p in ps:
            m_prev = m_sc[p]
            m_new = jnp.maximum(m_prev, mt_sc[slot, p])
            alpha = jnp.exp2(m_prev - m_new)
            lsum = None
            acc = None
            for c in range(tk // KV_CHUNK):
                cs = slice(c * KV_CHUNK, (c + 1) * KV_CHUNK)
                pe = jnp.exp2(s_sc[slot, p, cs, :] - m_new)
                ps = jnp.sum(pe, axis=0, keepdims=True)
                pv = _dot(v_ref[0, 0, j, :, cs], pe.astype(BF16))
                lsum = ps if lsum is None else lsum + ps
                acc = pv if acc is None else acc + pv
            l_sc[p] = alpha * l_sc[p] + lsum
            acc_sc[p] = alpha * acc_sc[p] + acc
            m_sc[p] = m_new

    def fused_tile(jq, slot_q, jp, slot_p):
        m_new = [jnp.maximum(m_sc[p], mt_sc[slot_p, p]) for p in range(2)]
        alpha = [jnp.exp2(m_sc[p] - m_new[p]) for p in range(2)]
        mx, lsum, acc = [None, None], [None, None], [None, None]
        for c in range(tk // KV_CHUNK):
            cs = slice(c * KV_CHUNK, (c + 1) * KV_CHUNK)
            if isinstance(jq, int):
                kr = pl.ds(jq * tk + c * KV_CHUNK, KV_CHUNK)
            else:
                kr = pl.ds(pl.multiple_of(jq * tk + c * KV_CHUNK, KV_CHUNK), KV_CHUNK)
            kc = k_ref[0, 0, kr, :]
            vc = v_ref[0, 0, jp, :, cs]
            for p in range(2):
                s = _dot(kc, qz_ref[0, 0, p])
                s_sc[slot_q, p, cs, :] = s
                cm = jnp.max(s, axis=0, keepdims=True)
                mx[p] = cm if mx[p] is None else jnp.maximum(mx[p], cm)
                pe = jnp.exp2(s_sc[slot_p, p, cs, :] - m_new[p])
                ps = jnp.sum(pe, axis=0, keepdims=True)
                pv = _dot(vc, pe.astype(BF16))
                lsum[p] = ps if lsum[p] is None else lsum[p] + ps
                acc[p] = pv if acc[p] is None else acc[p] + pv
        for p in range(2):
            mt_sc[slot_q, p] = mx[p]
            l_sc[p] = alpha[p] * l_sc[p] + lsum[p]
            acc_sc[p] = alpha[p] * acc_sc[p] + acc[p]
            m_sc[p] = m_new[p]

    qk_tile(0, 0)
    n_pairs = (n_kv - 1) // 2

    def pair(jj, carry):
        j = 2 * jj
        fused_tile(j + 1, 1, j, 0)
        fused_tile(j + 2, 0, j + 1, 1)
        return carry

    if n_pairs > 0:
        lax.fori_loop(0, n_pairs, pair, 0)
    j = 2 * n_pairs
    if j == n_kv - 1:
        pv_tile(j, 0)
    else:
        qk_tile(j + 1, 1)
        pv_tile(j, 0)
        pv_tile(j + 1, 1)

    o = acc_sc[0] / l_sc[0] - lam_ref[0] * (acc_sc[1] / l_sc[1])
    ms = jnp.mean(o * o, axis=0, keepdims=True)
    o_ref[0] = ((o * lax.rsqrt(ms + EPS)).T * g_ref[...]).astype(BF16)


def _attention(lam, qz, k, vt, g, *, tq, tk, q_blk0, n_q, kv_blk0, n_kv, prev_out=None, name="attn_ctx"):
    B, H, _, _, S = qz.shape
    aliased = prev_out is not None
    skv = tk * n_kv
    in_specs = [pl.BlockSpec(memory_space=pltpu.SMEM),
                pl.BlockSpec((1, 1, 2, LANES, tq), lambda b, h, i: (b, h, 0, 0, q_blk0 + i)),
                pl.BlockSpec((1, 1, skv, LANES), lambda b, h, i: (b, h, kv_blk0, 0)),
                pl.BlockSpec((1, 1, n_kv, LANES, tk), lambda b, h, i: (b, h, 0, 0, 0)),
                pl.BlockSpec((1, LANES), lambda b, h, i: (0, 0))]
    args = [lam, qz, k, vt, g]
    aliases = {}
    if aliased:
        in_specs.append(pl.BlockSpec(memory_space=pl.ANY))
        args.append(prev_out)
        aliases = {5: 0}
    return pl.pallas_call(
        functools.partial(_attn_kernel, aliased=aliased, tk=tk, n_kv=n_kv),
        out_shape=jax.ShapeDtypeStruct((B, S, ATT_WIDTH), BF16),
        grid=(B, H, n_q),
        in_specs=in_specs,
        out_specs=pl.BlockSpec((1, tq, LANES), lambda b, h, i: (b, q_blk0 + i, h)),
        scratch_shapes=[pltpu.VMEM((2, 2, tk, tq), F32), pltpu.VMEM((2, 2, 1, tq), F32),
                        pltpu.VMEM((2, 1, tq), F32), pltpu.VMEM((2, 1, tq), F32),
                        pltpu.VMEM((2, LANES, tq), F32)],
        input_output_aliases=aliases,
        compiler_params=_cparams(("parallel", "parallel", "arbitrary")),
        name=name,
    )(*args)


def _attention_lat(lam, qz, k, vt, g, L):
    B, H, _, S = vt.shape
    tq = _pick_tile(L, (512, 256))
    tk = _pick_tile(S, (1280, 640, 256))
    n_kv = S // tk
    vt_tiles = vt.reshape(B, H, LANES, n_kv, tk).transpose(0, 1, 3, 2, 4)
    out0 = jnp.zeros((B, S, ATT_WIDTH), BF16)
    return _attention(lam, qz, k, vt_tiles, g, tq=tq, tk=tk, q_blk0=0, n_q=L // tq, kv_blk0=0, n_kv=n_kv,
                      prev_out=out0, name="attn_lat")


def _attention_ctx(lam, qz, k, vt, g, L, prev_out):
    B, H, _, S = vt.shape
    Lc = S - L
    vt_ctx = vt[:, :, :, L:].reshape(B, H, 1, LANES, Lc)
    return _attention(lam, qz, k, vt_ctx, g, tq=Lc, tk=Lc, q_blk0=L // Lc, n_q=1, kv_blk0=L // Lc, n_kv=1,
                      prev_out=prev_out)


def _gdnprep_kernel(x_ref, prev_ref, next_ref, cw_ref, ab_ref, alog_ref, dtb_ref,
                    q_ref, k_ref, v_ref, gate_ref, ext_sc, *, nb_lat, nb_all):
    i = pl.program_id(1)
    tm = x_ref.shape[1]
    first = jnp.logical_or(i == 0, i == nb_lat)
    last = jnp.logical_or(i == nb_lat - 1, i == nb_all - 1)
    keep_prev = jnp.where(first, 0.0, 1.0)
    keep_next = jnp.where(last, 0.0, 1.0)
    ext_sc[0:8, :] = prev_ref[0] * keep_prev
    ext_sc[8:8 + tm, :] = x_ref[0]
    ext_sc[8 + tm:16 + tm, :] = next_ref[0] * keep_next
    acc = None
    for j in range(CONV_W):
        term = ext_sc[pl.ds(8 + j - CONV_W // 2, tm), :] * cw_ref[j:j + 1, :]
        acc = term if acc is None else acc + term
    y = acc * _sigmoid(acc)
    for part, ref in ((0, q_ref), (1, k_ref)):
        for hh in range(GDN_HEADS):
            lo = part * GDN_WIDTH + hh * GDN_HEAD_DIM
            t = y[:, lo:lo + GDN_HEAD_DIM]
            ref[0, :, hh * GDN_HEAD_DIM:(hh + 1) * GDN_HEAD_DIM] = (
                t * lax.rsqrt(jnp.sum(t * t, axis=-1, keepdims=True) + EPS))
    v_ref[0] = y[:, 2 * GDN_WIDTH:3 * GDN_WIDTH]
    ab = ab_ref[0]
    xs = ab + dtb_ref[...]
    sp = jnp.maximum(xs, 0.0) + jnp.log(1.0 + jnp.exp(-jnp.abs(xs)))
    g = -jnp.exp(alog_ref[...]) * sp
    lane = lax.broadcasted_iota(jnp.int32, ab.shape, 1)
    gate_ref[0] = jnp.where(lane < 2 * GDN_HEADS, g, _sigmoid(ab))


def _gdnprep(p_gdn, conv_w8, ab, alog_row, dtb_row, n_lat_blocks):
    B, S, W = p_gdn.shape
    tm = ROW_TILE
    nb = S // tm
    r8 = tm // 8
    row = lambda b, i: (b, i, 0)
    return pl.pallas_call(
        functools.partial(_gdnprep_kernel, nb_lat=n_lat_blocks, nb_all=nb),
        out_shape=(jax.ShapeDtypeStruct((B, S, GDN_WIDTH), F32),
                   jax.ShapeDtypeStruct((B, S, GDN_WIDTH), F32),
                   jax.ShapeDtypeStruct((B, S, GDN_WIDTH), F32),
                   jax.ShapeDtypeStruct((B, S, LANES), F32)),
        grid=(B, nb),
        in_specs=[pl.BlockSpec((1, tm, W), row),
                  pl.BlockSpec((1, 8, W), lambda b, i: (b, jnp.maximum(i * r8 - 1, 0), 0)),
                  pl.BlockSpec((1, 8, W), lambda b, i: (b, jnp.minimum((i + 1) * r8, nb * r8 - 1), 0)),
                  pl.BlockSpec((8, W), lambda b, i: (0, 0)),
                  pl.BlockSpec((1, tm, LANES), row),
                  pl.BlockSpec((1, LANES), lambda b, i: (0, 0)),
                  pl.BlockSpec((1, LANES), lambda b, i: (0, 0))],
        out_specs=(pl.BlockSpec((1, tm, GDN_WIDTH), row),
                   pl.BlockSpec((1, tm, GDN_WIDTH), row),
                   pl.BlockSpec((1, tm, GDN_WIDTH), row),
                   pl.BlockSpec((1, tm, LANES), row)),
        scratch_shapes=[pltpu.VMEM((tm + 16, W), F32)],
        compiler_params=_cparams(("parallel", "parallel")),
        name="gdnprep",
    )(p_gdn, p_gdn, p_gdn, conv_w8, ab, alog_row, dtb_row)


def _gdnchunk_kernel(q_ref, k_ref, v_ref, gate_ref, qg_ref, w_ref, kd_ref, u_ref, aq_ref, eg_ref):
    tm = q_ref.shape[1]
    nc = tm // GDN_CHUNK
    gates = gate_ref[0]
    ri = lax.broadcasted_iota(jnp.int32, (tm, tm), 0)
    ci = lax.broadcasted_iota(jnp.int32, (tm, tm), 1)
    same = (ri >> 6) == (ci >> 6)
    eye = ri == ci
    eye_f = jnp.where(eye, 1.0, 0.0).astype(F32)

    g1 = gates.astype(BF16)
    r1 = gates - g1.astype(F32)
    g2 = r1.astype(BF16)
    g3 = (r1 - g2.astype(F32)).astype(BF16)
    gparts = jnp.concatenate([g1, g2, g3], axis=1)

    def seg_sum(mask):
        r = _dot(jnp.where(mask, 1.0, 0.0).astype(BF16), gparts)
        return (r[:, :LANES] + r[:, LANES:2 * LANES]) + r[:, 2 * LANES:]

    tot = seg_sum(same)

    heads = []
    for hh in range(GDN_HEADS):
        sl = slice(hh * GDN_HEAD_DIM, (hh + 1) * GDN_HEAD_DIM)
        k = k_ref[0, :, sl]
        k16 = k.astype(BF16)
        q = q_ref[0, :, sl] * (GDN_HEAD_DIM ** -0.5)
        heads.append((sl, k, q, _dot_nt(k16, k16), _dot_nt(q.astype(BF16), k16)))

    chains = []
    for d in range(2):
        incl = jnp.logical_and(same, (ci <= ri) if d == 0 else (ci >= ri))
        strict = jnp.logical_and(incl, jnp.logical_not(eye))
        gc = seg_sum(incl)
        gc_t = gc.T
        for hh in range(GDN_HEADS):
            ln = d * GDN_HEADS + hh
            gcol = gc[:, ln:ln + 1]
            bcol = gates[:, 2 * GDN_HEADS + ln:2 * GDN_HEADS + ln + 1]
            tcol = tot[:, ln:ln + 1]
            diff = gcol - gc_t[ln:ln + 1, :]
            decay = jnp.where(incl, jnp.exp(jnp.where(incl, diff, 0.0)), 0.0)
            a = jnp.where(strict, heads[hh][3] * bcol * decay, 0.0)
            aqk = heads[hh][4] * decay
            chains.append(dict(d=d, hh=hh, x=-a, t=eye_f - a, aqk=aqk, gcol=gcol, bcol=bcol, tcol=tcol))

    for _ in range(5):
        for ch in chains:
            x16 = ch["x"].astype(BF16)
            ch["x"] = _dot(x16, x16)
        for ch in chains:
            ch["t"] = ch["t"] + _dot(ch["t"].astype(BF16), ch["x"].astype(BF16))

    for ch in chains:
        d, hh, gcol, bcol, tcol = ch["d"], ch["hh"], ch["gcol"], ch["bcol"], ch["tcol"]
        sl, k, q = heads[hh][0], heads[hh][1], heads[hh][2]
        t16 = ch["t"].astype(BF16)
        egc = jnp.exp(gcol)
        kb = k * bcol
        u_ref[0, d, :, sl] = _dot(t16, (v_ref[0, :, sl] * bcol).astype(BF16))
        w_ref[0, d, :, sl] = _dot(t16, (kb * egc).astype(BF16)).astype(BF16)
        kd_ref[0, d, :, sl] = (k * jnp.exp(tcol - gcol)).astype(BF16)
        qg_ref[0, d, :, sl] = (q * egc).astype(BF16)
        for cc in range(nc):
            rs = slice(cc * GDN_CHUNK, (cc + 1) * GDN_CHUNK)
            aq_ref[0, d, rs, hh * GDN_CHUNK:(hh + 1) * GDN_CHUNK] = ch["aqk"][rs, rs].astype(BF16)
            eg_ref[0, d, cc, hh:hh + 1, :] = jnp.broadcast_to(
                jnp.exp(tcol[cc * GDN_CHUNK:cc * GDN_CHUNK + 1, :]), (1, LANES))


def _gdnchunk(gq, gk, gv, gates):
    B, S, W = gq.shape
    tm = ROW_TILE
    nc = tm // GDN_CHUNK
    row = lambda b, i: (b, i, 0)
    drow = lambda b, i: (b, 0, i, 0)
    big = lambda dt: jax.ShapeDtypeStruct((B, 2, S, W), dt)
    return pl.pallas_call(
        _gdnchunk_kernel,
        out_shape=(big(BF16), big(BF16), big(BF16), big(F32),
                   jax.ShapeDtypeStruct((B, 2, S, GDN_HEADS * GDN_CHUNK), BF16),
                   jax.ShapeDtypeStruct((B, 2, S // GDN_CHUNK, GDN_HEADS, LANES), F32)),
        grid=(B, S // tm),
        in_specs=[pl.BlockSpec((1, tm, W), row), pl.BlockSpec((1, tm, W), row),
                  pl.BlockSpec((1, tm, W), row), pl.BlockSpec((1, tm, LANES), row)],
        out_specs=(pl.BlockSpec((1, 2, tm, W), drow), pl.BlockSpec((1, 2, tm, W), drow),
                   pl.BlockSpec((1, 2, tm, W), drow), pl.BlockSpec((1, 2, tm, W), drow),
                   pl.BlockSpec((1, 2, tm, GDN_HEADS * GDN_CHUNK), drow),
                   pl.BlockSpec((1, 2, nc, GDN_HEADS, LANES), lambda b, i: (b, 0, i, 0, 0))),
        compiler_params=_cparams(("parallel", "parallel")),
        name="gdnchunk",
    )(gq, gk, gv, gates)


def _gdnscan_kernel(qg0, w0, kd0, u0, aq0, eg0, qg1, w1, kd1, u1, aq1, eg1, of_ref, ob_ref, s_sc):
    @pl.when(pl.program_id(1) == 0)
    def _():
        s_sc[...] = jnp.zeros(s_sc.shape, F32)

    dirs = ((qg0, w0, kd0, u0, aq0, eg0, of_ref), (qg1, w1, kd1, u1, aq1, eg1, ob_ref))
    chains = [(d, hh) for d in range(2) for hh in range(GDN_HEADS)]
    st = {ch: s_sc[ch[0], ch[1]] for ch in chains}
    for step in range(SCAN_GROUP):
        st16, vn16, qs = {}, {}, {}
        for d, hh in chains:
            st16[d, hh] = st[d, hh].astype(BF16)
        for d, hh in chains:
            qg, w, kd, u, aq, eg, o_ref = dirs[d]
            cc = step if d == 0 else SCAN_GROUP - 1 - step
            rs = slice(cc * GDN_CHUNK, (cc + 1) * GDN_CHUNK)
            sl = slice(hh * GDN_HEAD_DIM, (hh + 1) * GDN_HEAD_DIM)
            vn16[d, hh] = (u[0, 0, rs, sl] - _dot(w[0, 0, rs, sl], st16[d, hh])).astype(BF16)
            qs[d, hh] = _dot(qg[0, 0, rs, sl], st16[d, hh])
        for d, hh in chains:
            qg, w, kd, u, aq, eg, o_ref = dirs[d]
            cc = step if d == 0 else SCAN_GROUP - 1 - step
            rs = slice(cc * GDN_CHUNK, (cc + 1) * GDN_CHUNK)
            sl = slice(hh * GDN_HEAD_DIM, (hh + 1) * GDN_HEAD_DIM)
            o_ref[0, rs, sl] = qs[d, hh] + _dot(aq[0, 0, rs, hh * GDN_CHUNK:(hh + 1) * GDN_CHUNK], vn16[d, hh])
            st[d, hh] = st[d, hh] * eg[0, 0, cc, hh:hh + 1, :] + _dot_tn(kd[0, 0, rs, sl], vn16[d, hh])
    for d, hh in chains:
        s_sc[d, hh] = st[d, hh]


def _gdnscan(qg, w, kd, u, aq, eg, n_lat_chunks, n_ctx_chunks):
    B, _, S, W = qg.shape
    G = SCAN_GROUP
    R = GDN_CHUNK * G
    n = S // R
    n_lat, n_ctx = n_lat_chunks // G, n_ctx_chunks // G
    assert n_lat * G == n_lat_chunks and n_ctx * G == n_ctx_chunks

    def fwd_blk(i):
        return jnp.where(i < n_ctx, n_lat + i, i - n_ctx)

    def bwd_blk(i):
        return jnp.where(i < n_ctx, n_lat + n_ctx - 1 - i, n - 1 - i)

    def specs(d, blk_of):
        big = pl.BlockSpec((1, 1, R, W), lambda b, i: (b, d, blk_of(i), 0))
        return [big, big, big, big,
                pl.BlockSpec((1, 1, R, GDN_HEADS * GDN_CHUNK), lambda b, i: (b, d, blk_of(i), 0)),
                pl.BlockSpec((1, 1, G, GDN_HEADS, LANES), lambda b, i: (b, d, blk_of(i), 0, 0))]

    return pl.pallas_call(
        _gdnscan_kernel,
        out_shape=(jax.ShapeDtypeStruct((B, S, W), F32), jax.ShapeDtypeStruct((B, S, W), F32)),
        grid=(B, n),
        in_specs=specs(0, fwd_blk) + specs(1, bwd_blk),
        out_specs=(pl.BlockSpec((1, R, W), lambda b, i: (b, fwd_blk(i), 0)),
                   pl.BlockSpec((1, R, W), lambda b, i: (b, bwd_blk(i), 0))),
        scratch_shapes=[pltpu.VMEM((2, GDN_HEADS, GDN_HEAD_DIM, GDN_HEAD_DIM), F32)],
        compiler_params=_cparams(("parallel", "arbitrary")),
        name="gdnscan",
    )(qg, w, kd, u, aq, eg, qg, w, kd, u, aq, eg)


def _mixout_kernel(oa_ref, of_ref, ob_ref, z_ref, h_ref, mod_ref, gg_ref, wo_ref, gf_ref, rw_ref, rb_ref,
                   hn_ref, v_ref, te_ref, tg_ref):
    og = of_ref[0] + ob_ref[0]
    z = z_ref[0]
    parts = [oa_ref[0]]
    for hh in range(GDN_HEADS):
        sl = slice(hh * GDN_HEAD_DIM, (hh + 1) * GDN_HEAD_DIM)
        t = og[:, sl]
        t = t * lax.rsqrt(jnp.mean(t * t, axis=-1, keepdims=True) + EPS) * gg_ref[...]
        zz = z[:, sl]
        parts.append((t * (zz * _sigmoid(zz))).astype(BF16))
    mix_in = jnp.concatenate(parts, axis=-1)
    mix = _dot(mix_in, wo_ref[...])
    hn = h_ref[0] + mod_ref[0, 0, 2:3, :] * mix
    hn_ref[0] = hn
    y = hn * lax.rsqrt(jnp.mean(hn * hn, axis=-1, keepdims=True) + EPS) * gf_ref[...]
    v = y * (1.0 + mod_ref[0, 0, 4:5, :]) + mod_ref[0, 0, 3:4, :]
    v_ref[0] = v.astype(BF16)
    v_hi = v.astype(BF16)
    v_lo = (v - v_hi.astype(F32)).astype(BF16)
    logits = (_dot(v_hi, rw_ref[0]) + _dot(v_lo, rw_ref[0]) + _dot(v_hi, rw_ref[1])) + rb_ref[...]
    lane = lax.broadcasted_iota(jnp.int32, logits.shape, 1)
    cur = logits
    vals, idxs = [], []
    for _ in range(TOP_K):
        m = jnp.max(cur, axis=-1, keepdims=True)
        idx = jnp.min(jnp.where(cur == m, lane, LANES), axis=-1, keepdims=True)
        vals.append(m)
        idxs.append(idx)
        cur = jnp.where(lane == idx, -jnp.inf, cur)
    es = [jnp.exp(vv - vals[0]) for vv in vals]
    inv = 1.0 / (es[0] + es[1] + es[2] + es[3])
    te = jnp.zeros(logits.shape, jnp.int32)
    tg = jnp.zeros(logits.shape, F32)
    for kk in range(TOP_K):
        te = jnp.where(lane == kk, idxs[kk], te)
        tg = jnp.where(lane == kk, es[kk] * inv, tg)
    te_ref[0] = te
    tg_ref[0] = tg


def _mixout(o_att, o_f, o_b, z, h, modtab, gg, w_out, gf, rw, rb, n_lat_blocks):
    B, S, _ = h.shape
    tm = ROW_TILE
    row = lambda b, i: (b, i, 0)
    const = lambda b, i: (0, 0)
    return pl.pallas_call(
        _mixout_kernel,
        out_shape=(jax.ShapeDtypeStruct((B, S, D_MODEL), F32),
                   jax.ShapeDtypeStruct((B, S, D_MODEL), BF16),
                   jax.ShapeDtypeStruct((B, S, LANES), jnp.int32),
                   jax.ShapeDtypeStruct((B, S, LANES), F32)),
        grid=(B, S // tm),
        in_specs=[pl.BlockSpec((1, tm, ATT_WIDTH), row),
                  pl.BlockSpec((1, tm, GDN_WIDTH), row),
                  pl.BlockSpec((1, tm, GDN_WIDTH), row),
                  pl.BlockSpec((1, tm, GDN_WIDTH), row),
                  pl.BlockSpec((1, tm, D_MODEL), row),
                  pl.BlockSpec((1, 1, 8, D_MODEL), lambda b, i: (b, (i >= n_lat_blocks).astype(jnp.int32), 0, 0)),
                  pl.BlockSpec((1, LANES), const),
                  pl.BlockSpec((D_MODEL, D_MODEL), const),
                  pl.BlockSpec((1, D_MODEL), const),
                  pl.BlockSpec((2, D_MODEL, LANES), lambda b, i: (0, 0, 0)),
                  pl.BlockSpec((1, LANES), const)],
        out_specs=(pl.BlockSpec((1, tm, D_MODEL), row),
                   pl.BlockSpec((1, tm, D_MODEL), row),
                   pl.BlockSpec((1, tm, LANES), row),
                   pl.BlockSpec((1, tm, LANES), row)),
        compiler_params=_cparams(("parallel", "parallel")),
        name="mixout",
    )(o_att, o_f, o_b, z, h, modtab, gg, w_out, gf, rw, rb)


def _expert_kernel(be_ref, nv_ref, x_ref, wgu_ref, bgu_ref, wd_ref, bd_ref, y_ref, wgu_sc, wd_sc):
    i = pl.program_id(0)
    new_expert = jnp.logical_or(i == 0, be_ref[i] != be_ref[jnp.maximum(i - 1, 0)])

    @pl.when(new_expert)
    def _():
        wgu_sc[...] = wgu_ref[0, 0].astype(BF16)
        wd_sc[...] = wd_ref[0, 0].astype(BF16)

    @pl.when(nv_ref[i] > 0)
    def _():
        gu = _dot(x_ref[...], wgu_sc[...]) + bgu_ref[0, 0]
        g_ = jnp.minimum(gu[:, :D_EXPERT], SWIGLU_LIMIT)
        up = jnp.clip(gu[:, D_EXPERT:], -SWIGLU_LIMIT, SWIGLU_LIMIT)
        glu = g_ * _sigmoid(SWIGLU_ALPHA * g_)
        act = ((up + 1.0) * glu).astype(BF16)
        y_ref[...] = (_dot(act, wd_sc[...]) + bd_ref[0, 0]).astype(y_ref.dtype)

    @pl.when(nv_ref[i] == 0)
    def _():
        y_ref[...] = jnp.zeros(y_ref.shape, y_ref.dtype)


def _experts(block_e, n_valid, x_sorted, wgu, bgu, wd, bd, layer):
    n_slots = x_sorted.shape[0]
    nb = n_slots // MOE_BLOCK
    grid_spec = pltpu.PrefetchScalarGridSpec(
        num_scalar_prefetch=2,
        grid=(nb,),
        in_specs=[pl.BlockSpec((MOE_BLOCK, D_MODEL), lambda i, be, nv: (i, 0)),
                  pl.BlockSpec((1, 1, D_MODEL, 2 * D_EXPERT), lambda i, be, nv: (layer, be[i], 0, 0)),
                  pl.BlockSpec((1, 1, 1, 2 * D_EXPERT), lambda i, be, nv: (layer, be[i], 0, 0)),
                  pl.BlockSpec((1, 1, D_EXPERT, D_MODEL), lambda i, be, nv: (layer, be[i], 0, 0)),
                  pl.BlockSpec((1, 1, 1, D_MODEL), lambda i, be, nv: (layer, be[i], 0, 0))],
        out_specs=pl.BlockSpec((MOE_BLOCK, D_MODEL), lambda i, be, nv: (i, 0)),
        scratch_shapes=[pltpu.VMEM((D_MODEL, 2 * D_EXPERT), BF16), pltpu.VMEM((D_EXPERT, D_MODEL), BF16)],
    )
    return pl.pallas_call(
        _expert_kernel,
        out_shape=jax.ShapeDtypeStruct((n_slots, D_MODEL), BF16),
        grid_spec=grid_spec,
        compiler_params=pltpu.CompilerParams(dimension_semantics=("arbitrary",),
                                             vmem_limit_bytes=EXPERT_VMEM_LIMIT),
        name="experts",
    )(block_e, n_valid, x_sorted, wgu, bgu, wd, bd)


def _moe_plan(top_e, top_g):
    T = top_e.shape[0]
    n_assign = T * TOP_K
    n_blocks = -(-n_assign // MOE_BLOCK) + N_EXPERTS
    n_slots = n_blocks * MOE_BLOCK
    i32 = jnp.int32
    flat_e = top_e.reshape(-1)
    gate_flat = top_g.reshape(-1)
    assert T < NO_TOKEN
    gate_bits = lax.bitcast_convert_type(gate_flat.astype(BF16), jnp.uint16).astype(jnp.uint32)
    tok_pack = (gate_bits << 16) | (jnp.arange(n_assign, dtype=jnp.uint32) // TOP_K)
    _, sorted_pack = lax.sort((flat_e, tok_pack), num_keys=1, is_stable=True)
    e_ids = jnp.arange(N_EXPERTS, dtype=i32)
    is_e = flat_e[:, None] == e_ids[None, :]
    counts = jnp.sum(is_e, axis=0, dtype=i32)
    start = jnp.cumsum(counts) - counts
    padded = (counts + MOE_BLOCK - 1) // MOE_BLOCK * MOE_BLOCK
    pad_end = jnp.cumsum(padded)
    pad_start = pad_end - padded
    blk0 = jnp.arange(n_blocks, dtype=i32) * MOE_BLOCK
    block_e = jnp.minimum(jnp.sum(pad_end[None, :] <= blk0[:, None], axis=1, dtype=i32), N_EXPERTS - 1)
    off = (blk0 - pad_start[block_e])[:, None] + jnp.arange(MOE_BLOCK, dtype=i32)[None, :]
    valid = off < counts[block_e][:, None]
    valid_flat = valid.reshape(-1)
    slot_pack = jnp.where(valid_flat,
                          sorted_pack[jnp.clip(off + start[block_e][:, None], 0, n_assign - 1).reshape(-1)],
                          jnp.uint32(NO_TOKEN))
    slot_tok = jnp.where(valid_flat, (slot_pack & NO_TOKEN).astype(i32), jnp.arange(n_slots, dtype=i32) % T)
    n_valid = jnp.sum(valid, axis=1, dtype=i32)

    n_tiles = T // COMBINE_TILE
    cnt = jnp.sum(top_e.reshape(n_tiles, COMBINE_TILE * TOP_K)[:, :, None] == e_ids[None, None, :], axis=1, dtype=i32)
    run_start = pad_start[None, :] + jnp.cumsum(cnt, axis=0) - cnt
    q_first = run_start // COMBINE_CHUNK
    n_ch = jnp.where(cnt > 0, (run_start + cnt - 1) // COMBINE_CHUNK - q_first + 1, 0)
    ch_end = jnp.cumsum(n_ch, axis=1)
    ch_off = ch_end - n_ch
    j = jnp.arange(COMBINE_MAX_CHUNKS, dtype=i32)
    e_j = jnp.minimum(jnp.sum(ch_end[:, None, :] <= j[None, :, None], axis=2, dtype=i32), N_EXPERTS - 1)
    used = j[None, :] < ch_end[:, -1:]
    chunk_id = jnp.where(used, jnp.take_along_axis(q_first, e_j, axis=1)
                         + j[None, :] - jnp.take_along_axis(ch_off, e_j, axis=1), 0)
    slot_pack = slot_pack.reshape(n_slots // COMBINE_CHUNK, COMBINE_CHUNK)
    row_pack = jnp.where(used[:, :, None], slot_pack[chunk_id], jnp.uint32(NO_TOKEN))
    row_pack = lax.bitcast_convert_type(row_pack, i32).reshape(n_tiles, 1, COMBINE_MAX_CHUNKS * COMBINE_CHUNK)
    return slot_tok, block_e, n_valid, chunk_id, row_pack


def _combine_kernel(cid_ref, tok_ref, h_ref, mod_ref, y_hbm, o_ref, ybuf, sem):
    i = pl.program_id(0)
    n = pl.num_programs(0)
    slot = lax.rem(i, 2)

    def chunk_copy(c, s, jj):
        return pltpu.make_async_copy(
            y_hbm.at[pl.ds(pl.multiple_of(c * COMBINE_CHUNK, COMBINE_CHUNK), COMBINE_CHUNK)],
            ybuf.at[s, pl.ds(jj * COMBINE_CHUNK, COMBINE_CHUNK)], sem.at[s])

    def start_tile(t, s):
        for jj in range(COMBINE_MAX_CHUNKS):
            chunk_copy(cid_ref[t, jj], s, jj).start()

    @pl.when(i == 0)
    def _():
        start_tile(0, 0)

    @pl.when(i + 1 < n)
    def _():
        start_tile(i + 1, 1 - slot)

    for jj in range(COMBINE_MAX_CHUNKS):
        chunk_copy(0, slot, jj).wait()

    packed = tok_ref[0]
    tok = packed & NO_TOKEN
    gate = lax.bitcast_convert_type(packed & ~NO_TOKEN, F32)
    t_ids = i * COMBINE_TILE + lax.broadcasted_iota(jnp.int32, (COMBINE_TILE, tok.shape[1]), 0)
    sel = jnp.where(tok == t_ids, gate, 0.0).astype(BF16)
    y = _dot(sel, ybuf[slot])
    o_ref[...] = h_ref[...] + mod_ref[0, 0, 5:6, :] * y


def _combine(chunk_id, row_tok, y_sorted, h, modtab, n_lat_blocks):
    B, S, _ = h.shape
    T = B * S
    ct = COMBINE_TILE
    tiles_per_batch = S // ct
    n_lat_tiles = n_lat_blocks * ROW_TILE // ct
    n_rows = COMBINE_MAX_CHUNKS * COMBINE_CHUNK
    grid_spec = pltpu.PrefetchScalarGridSpec(
        num_scalar_prefetch=1,
        grid=(T // ct,),
        in_specs=[pl.BlockSpec((1, 1, n_rows), lambda i, cid: (i, 0, 0)),
                  pl.BlockSpec((ct, D_MODEL), lambda i, cid: (i, 0)),
                  pl.BlockSpec((1, 1, 8, D_MODEL),
                               lambda i, cid: (i // tiles_per_batch,
                                               (lax.rem(i, tiles_per_batch) >= n_lat_tiles).astype(jnp.int32), 0, 0)),
                  pl.BlockSpec(memory_space=pl.ANY)],
        out_specs=pl.BlockSpec((ct, D_MODEL), lambda i, cid: (i, 0)),
        scratch_shapes=[pltpu.VMEM((2, n_rows, D_MODEL), BF16), pltpu.SemaphoreType.DMA((2,))],
    )
    out = pl.pallas_call(
        _combine_kernel,
        out_shape=jax.ShapeDtypeStruct((T, D_MODEL), F32),
        grid_spec=grid_spec,
        compiler_params=_cparams(("arbitrary",)),
        name="combine",
    )(chunk_id, row_tok, h.reshape(T, D_MODEL), modtab, y_sorted)
    return out.reshape(B, S, D_MODEL)


def _moe(v_ffn, top_e, top_g, h_new, modtab, wgu, bgu, wd, bd, layer, n_lat_blocks):
    B, S, _ = v_ffn.shape
    T = B * S
    slot_tok, block_e, n_valid, chunk_id, row_tok = _moe_plan(
        top_e.reshape(T, LANES)[:, :TOP_K], top_g.reshape(T, LANES)[:, :TOP_K])
    x_sorted = v_ffn.reshape(T, D_MODEL)[slot_tok]
    y_sorted = _experts(block_e, n_valid, x_sorted, wgu, bgu, wd, bd, layer)
    return _combine(chunk_id, row_tok, y_sorted, h_new, modtab, n_lat_blocks)


def _pick_tile(n, cands):
    for t in cands:
        if n % t == 0:
            return t
    raise ValueError(f"no tile for {n}")


def _rope_tables(L, Lc):
    rows = L // GRID_W
    row = jnp.repeat(jnp.arange(rows, dtype=F32), GRID_W)
    col = (jnp.arange(L, dtype=jnp.int32) % GRID_W).astype(F32)
    inv_freq = ROPE_BASE ** (-jnp.arange(ROPE_PAIRS, dtype=F32) / ROPE_PAIRS)
    ar = row[:, None] * inv_freq
    ac = col[:, None] * inv_freq
    cos64 = jnp.concatenate([jnp.cos(ar), jnp.cos(ar), jnp.cos(ac), jnp.cos(ac)], axis=-1)
    sin64 = jnp.concatenate([-jnp.sin(ar), jnp.sin(ar), -jnp.sin(ac), jnp.sin(ac)], axis=-1)
    cos_t = jnp.concatenate([jnp.tile(cos64, (1, 2)), jnp.ones((Lc, LANES), F32)], axis=0)
    sin_t = jnp.concatenate([jnp.tile(sin64, (1, 2)), jnp.zeros((Lc, LANES), F32)], axis=0)
    return cos_t, sin_t


def _pad_lanes(v):
    v = v.reshape(1, -1).astype(F32)
    return jnp.pad(v, ((0, 0), (0, LANES - v.shape[1])))


def kernel(x, c, ctx, c_ctx, w_mod, b_mod, norm_mix_g, w_in, q_norm_g, k_norm_g, lam_q1, lam_k1, lam_q2, lam_k2, subln_g, conv_w, a_log, dt_bias, gdn_norm_g, w_out, norm_ffn_g, router_w, router_b, w_gate_up, b_gate_up, w_down, b_down):
    B, L, D = x.shape
    Lc = ctx.shape[1]
    S = L + Lc
    depth = w_mod.shape[0]
    tm = ROW_TILE
    n_lat_blocks = L // tm
    cos_t, sin_t = _rope_tables(L, Lc)

    c_rows = jnp.zeros((8, D), F32).at[:B].set(c).at[B].set(c_ctx)
    h = jnp.concatenate([x, ctx], axis=1)

    for layer in range(depth):
        mod = _adaln(c_rows, w_mod, b_mod, layer)
        mod6 = mod.reshape(8, 6, D)
        lat_mod = mod6[:B]
        ctx_mod = jnp.broadcast_to(mod6[B][None], (B, 6, D))
        modtab = jnp.pad(jnp.stack([lat_mod, ctx_mod], axis=1), ((0, 0), (0, 0), (0, 2), (0, 0)))

        lam_init = 0.8 - 0.6 * math.exp(-0.3 * layer)
        lam_full = (jnp.exp(jnp.sum(lam_q1[layer] * lam_k1[layer]))
                    - jnp.exp(jnp.sum(lam_q2[layer] * lam_k2[layer])) + lam_init).reshape(1).astype(F32)

        w_l = w_in[layer]
        w_main = w_l[:, :IN_MAIN].astype(BF16)
        w_ab = jnp.pad(w_l[:, IN_MAIN:], ((0, 0), (0, LANES - (w_l.shape[1] - IN_MAIN)))).astype(BF16)
        gq = jnp.tile(q_norm_g[layer].reshape(1, ATT_HEAD_DIM), (1, 2))
        gk = jnp.tile(k_norm_g[layer].reshape(1, ATT_HEAD_DIM), (1, 2))
        qz, k_att, v_att, p_gdn, z, ab = _inproj(h, modtab, norm_mix_g[layer].reshape(1, D), w_main, w_ab,
                                                 cos_t, sin_t, gq, gk, n_lat_blocks)
        g_sub = (subln_g[layer] * (1.0 - lam_init)).reshape(1, LANES).astype(F32)
        o_att = _attention_lat(lam_full, qz, k_att, v_att, g_sub, L)
        o_att = _attention_ctx(lam_full, qz, k_att, v_att, g_sub, L, o_att)

        conv_w8 = jnp.pad(conv_w[layer], ((0, 8 - CONV_W), (0, 0)))
        gq_g, gk_g, gv_g, gates = _gdnprep(p_gdn, conv_w8, ab, _pad_lanes(a_log[layer]),
                                           _pad_lanes(dt_bias[layer]), n_lat_blocks)
        qg, w_g, kd, u_g, aq, eg = _gdnchunk(gq_g, gk_g, gv_g, gates)
        o_f, o_b = _gdnscan(qg, w_g, kd, u_g, aq, eg, L // GDN_CHUNK, Lc // GDN_CHUNK)

        rw = jnp.pad(router_w[layer].astype(F32), ((0, 0), (0, LANES - N_EXPERTS)))
        rw_hi = rw.astype(BF16)
        rw = jnp.stack([rw_hi, (rw - rw_hi.astype(F32)).astype(BF16)])
        rb = jnp.pad(router_b[layer].reshape(1, N_EXPERTS).astype(F32), ((0, 0), (0, LANES - N_EXPERTS)),
                     constant_values=-1e30)
        h_new, v_ffn, top_e, top_g = _mixout(
            o_att, o_f, o_b, z, h, modtab, gdn_norm_g[layer].reshape(1, LANES), w_out[layer].astype(BF16),
            norm_ffn_g[layer].reshape(1, D), rw, rb, n_lat_blocks)

        h = _moe(v_ffn, top_e, top_g, h_new, modtab,
                 w_gate_up, b_gate_up.reshape(depth, N_EXPERTS, 1, 2 * D_EXPERT),
                 w_down, b_down.reshape(depth, N_EXPERTS, 1, D), layer, n_lat_blocks)
    return h[:, :L]
```

```python
import functools
import math

import jax
import jax.numpy as jnp
from jax import lax
from jax.experimental import pallas as pl
from jax.experimental.pallas import tpu as pltpu

F32 = jnp.float32
BF16 = jnp.bfloat16
HIGHEST = lax.Precision.HIGHEST

D_MODEL = 1024
GRID_W = 64
EPS = 1e-6
ATT_WIDTH = 512
ATT_HEAD_DIM = 64
ATT_HEADS = 4
ROPE_BASE = 10000.0
ROPE_PAIRS = ATT_HEAD_DIM // 4
GDN_WIDTH = 512
GDN_HEAD_DIM = 128
GDN_HEADS = 4
GDN_CHUNK = 64
CONV_W = 5
IN_MAIN = 3 * ATT_WIDTH + 4 * GDN_WIDTH
N_EXPERTS = 32
TOP_K = 4
D_EXPERT = 1024
SWIGLU_ALPHA = 1.702
SWIGLU_LIMIT = 7.0
MOE_BLOCK = 512
SCAN_GROUP = 4
NO_TOKEN = 0xFFFF
COMBINE_TILE = 256
COMBINE_CHUNK = 16
COMBINE_MAX_CHUNKS = COMBINE_TILE * TOP_K // COMBINE_CHUNK + 2 * N_EXPERTS

LANES = 128
ROW_TILE = 256
KV_CHUNK = 256
Q_SCALE = ATT_HEAD_DIM ** -0.5 * math.log2(math.e)
VMEM_LIMIT = 48 * 1024 * 1024
EXPERT_VMEM_LIMIT = 56 * 1024 * 1024


def _cparams(sem):
    return pltpu.CompilerParams(dimension_semantics=sem, vmem_limit_bytes=VMEM_LIMIT)


def _dot(a, b):
    return jnp.dot(a, b, preferred_element_type=F32)


def _dot_nt(a, b):
    return lax.dot_general(a, b, (((1,), (1,)), ((), ())), preferred_element_type=F32)


def _dot_tn(a, b):
    return lax.dot_general(a, b, (((0,), (0,)), ((), ())), preferred_element_type=F32)


def _dot_hi(a, b):
    return jnp.dot(a, b, preferred_element_type=F32, precision=HIGHEST)


def _sigmoid(x):
    return 1.0 / (1.0 + jnp.exp(-x))


def _adaln_kernel(c_ref, w_ref, b_ref, o_ref):
    c = c_ref[...]
    s = c * _sigmoid(c)
    o_ref[...] = _dot_hi(s, w_ref[0]) + b_ref[0]


def _adaln(c_rows, w, b, layer):
    depth, _, n = w.shape
    tn = 1024
    return pl.pallas_call(
        _adaln_kernel,
        out_shape=jax.ShapeDtypeStruct((8, n), F32),
        grid=(n // tn,),
        in_specs=[pl.BlockSpec((8, D_MODEL), lambda j: (0, 0)),
                  pl.BlockSpec((1, D_MODEL, tn), lambda j: (layer, 0, j)),
                  pl.BlockSpec((1, 1, tn), lambda j: (layer, 0, j))],
        out_specs=pl.BlockSpec((8, tn), lambda j: (0, j)),
        compiler_params=_cparams(("arbitrary",)),
        name="adaln",
    )(c_rows, w, b.reshape(depth, 1, n))


def _qk_norm_rope(p_qk, cos_ref, sin_ref, gq_ref, gk_ref, qz_ref, k_ref):
    tm = p_qk.shape[0]
    lane = lax.broadcasted_iota(jnp.int32, (tm, LANES), 1)
    hi16 = (lane & 16) != 0
    first = lax.broadcasted_iota(jnp.int32, (LANES, tm), 0) < ATT_HEAD_DIM
    r = lax.broadcasted_iota(jnp.int32, (LANES, LANES), 0) >> 6
    c = lax.broadcasted_iota(jnp.int32, (LANES, LANES), 1) >> 6
    gmat = jnp.where(r == c, 1.0 / ATT_HEAD_DIM, 0.0).astype(F32)
    cosv = cos_ref[...]
    sinv = sin_ref[...]
    for j in range(2 * ATT_HEADS):
        x = p_qk[:, j * LANES:(j + 1) * LANES]
        ms = _dot_hi(x * x, gmat)
        g = gq_ref[...] if j < ATT_HEADS else gk_ref[...]
        y = x * lax.rsqrt(ms + EPS) * g
        sw = jnp.where(hi16, pltpu.roll(y, 16, 1), pltpu.roll(y, LANES - 16, 1))
        y = y * cosv + sw * sinv
        if j < ATT_HEADS:
            yt = (y * Q_SCALE).T
            qz_ref[0, j, 0] = jnp.where(first, yt, 0.0).astype(BF16)
            qz_ref[0, j, 1] = jnp.where(first, 0.0, yt).astype(BF16)
        else:
            k_ref[0, j - ATT_HEADS] = y.astype(BF16)


def _inproj_kernel(h_ref, mod_ref, g_ref, w_ref, wab_ref, cos_ref, sin_ref, gq_ref, gk_ref,
                   qz_ref, k_ref, v_ref, gdn_ref, z_ref, ab_ref):
    x = h_ref[0]
    ms = jnp.mean(x * x, axis=-1, keepdims=True)
    y = x * lax.rsqrt(ms + EPS) * g_ref[...]
    shift = mod_ref[0, 0, 0:1, :]
    scale = mod_ref[0, 0, 1:2, :]
    u = (y * (1.0 + scale) + shift).astype(BF16)
    _qk_norm_rope(_dot(u, w_ref[:, 0:2 * ATT_WIDTH]), cos_ref, sin_ref, gq_ref, gk_ref, qz_ref, k_ref)
    vv = _dot(u, w_ref[:, 2 * ATT_WIDTH:3 * ATT_WIDTH])
    for hh in range(ATT_HEADS):
        v_ref[0, hh] = vv[:, hh * LANES:(hh + 1) * LANES].T.astype(BF16)
    off = 3 * ATT_WIDTH
    gdn_ref[0] = _dot(u, w_ref[:, off:off + 3 * GDN_WIDTH])
    z_ref[0] = _dot(u, w_ref[:, off + 3 * GDN_WIDTH:off + 4 * GDN_WIDTH])
    ab_ref[0] = _dot(u, wab_ref[...])


def _inproj(h, modtab, g, w_main, w_ab, cos_t, sin_t, gq, gk, n_lat_blocks):
    B, S, _ = h.shape
    tm = ROW_TILE
    row = lambda b, i: (b, i, 0)
    vec = lambda b, i: (0, 0)
    return pl.pallas_call(
        _inproj_kernel,
        out_shape=(jax.ShapeDtypeStruct((B, ATT_HEADS, 2, LANES, S), BF16),
                   jax.ShapeDtypeStruct((B, ATT_HEADS, S, LANES), BF16),
                   jax.ShapeDtypeStruct((B, ATT_HEADS, LANES, S), BF16),
                   jax.ShapeDtypeStruct((B, S, 3 * GDN_WIDTH), F32),
                   jax.ShapeDtypeStruct((B, S, GDN_WIDTH), F32),
                   jax.ShapeDtypeStruct((B, S, LANES), F32)),
        grid=(B, S // tm),
        in_specs=[pl.BlockSpec((1, tm, D_MODEL), row),
                  pl.BlockSpec((1, 1, 8, D_MODEL), lambda b, i: (b, (i >= n_lat_blocks).astype(jnp.int32), 0, 0)),
                  pl.BlockSpec((1, D_MODEL), lambda b, i: (0, 0)),
                  pl.BlockSpec((D_MODEL, IN_MAIN), vec),
                  pl.BlockSpec((D_MODEL, LANES), vec),
                  pl.BlockSpec((tm, LANES), lambda b, i: (i, 0)),
                  pl.BlockSpec((tm, LANES), lambda b, i: (i, 0)),
                  pl.BlockSpec((1, LANES), vec),
                  pl.BlockSpec((1, LANES), vec)],
        out_specs=(pl.BlockSpec((1, ATT_HEADS, 2, LANES, tm), lambda b, i: (b, 0, 0, 0, i)),
                   pl.BlockSpec((1, ATT_HEADS, tm, LANES), lambda b, i: (b, 0, i, 0)),
                   pl.BlockSpec((1, ATT_HEADS, LANES, tm), lambda b, i: (b, 0, 0, i)),
                   pl.BlockSpec((1, tm, 3 * GDN_WIDTH), row),
                   pl.BlockSpec((1, tm, GDN_WIDTH), row),
                   pl.BlockSpec((1, tm, LANES), row)),
        compiler_params=_cparams(("parallel", "parallel")),
        name="inproj",
    )(h, modtab, g, w_main, w_ab, cos_t, sin_t, gq, gk)


def _attn_kernel(lam_ref, qz_ref, k_ref, v_ref, g_ref, *rest, aliased, tk, n_kv):
    if aliased:
        rest = rest[1:]
    o_ref, s_sc, mt_sc, m_sc, l_sc, acc_sc = rest
    m_sc[...] = jnp.full(m_sc.shape, -jnp.inf, F32)
    l_sc[...] = jnp.zeros(l_sc.shape, F32)
    acc_sc[...] = jnp.zeros(acc_sc.shape, F32)

    def rows(j):
        if isinstance(j, int):
            return pl.ds(j * tk, tk)
        return pl.ds(pl.multiple_of(j * tk, tk), tk)

    def qk_tile(j, slot, ps=(0, 1)):
        kt = k_ref[0, 0, rows(j), :]
        for p in ps:
            s = _dot(kt, qz_ref[0, 0, p])
            s_sc[slot, p] = s
            mt_sc[slot, p] = jnp.max(s, axis=0, keepdims=True)

    def pv_tile(j, slot, ps=(0, 1)):
        for p in ps:
            m_prev = m_sc[p]
            m_new = jnp.maximum(m_prev, mt_sc[slot, p])
            alpha = jnp.exp2(m_prev - m_new)
            lsum = None
            acc = None
            for c in range(tk // KV_CHUNK):
                cs = slice(c * KV_CHUNK, (c + 1) * KV_CHUNK)
                pe = jnp.exp2(s_sc[slot, p, cs, :] - m_new)
                ps = jnp.sum(pe, axis=0, keepdims=True)
                pv = _dot(v_ref[0, 0, j, :, cs], pe.astype(BF16))
                lsum = ps if lsum is None else lsum + ps
                acc = pv if acc is None else acc + pv
            l_sc[p] = alpha * l_sc[p] + lsum
            acc_sc[p] = alpha * acc_sc[p] + acc
            m_sc[p] = m_new

    def fused_tile(jq, slot_q, jp, slot_p):
        m_new = [jnp.maximum(m_sc[p], mt_sc[slot_p, p]) for p in range(2)]
        alpha = [jnp.exp2(m_sc[p] - m_new[p]) for p in range(2)]
        mx, lsum, acc = [None, None], [None, None], [None, None]
        for c in range(tk // KV_CHUNK):
            cs = slice(c * KV_CHUNK, (c + 1) * KV_CHUNK)
            if isinstance(jq, int):
                kr = pl.ds(jq * tk + c * KV_CHUNK, KV_CHUNK)
            else:
                kr = pl.ds(pl.multiple_of(jq * tk + c * KV_CHUNK, KV_CHUNK), KV_CHUNK)
            kc = k_ref[0, 0, kr, :]
            vc = v_ref[0, 0, jp, :, cs]
            for p in range(2):
                s = _dot(kc, qz_ref[0, 0, p])
                s_sc[slot_q, p, cs, :] = s
                cm = jnp.max(s, axis=0, keepdims=True)
                mx[p] = cm if mx[p] is None else jnp.maximum(mx[p], cm)
                pe = jnp.exp2(s_sc[slot_p, p, cs, :] - m_new[p])
                ps = jnp.sum(pe, axis=0, keepdims=True)
                pv = _dot(vc, pe.astype(BF16))
                lsum[p] = ps if lsum[p] is None else lsum[p] + ps
                acc[p] = pv if acc[p] is None else acc[p] + pv
        for p in range(2):
            mt_sc[slot_q, p] = mx[p]
            l_sc[p] = alpha[p] * l_sc[p] + lsum[p]
            acc_sc[p] = alpha[p] * acc_sc[p] + acc[p]
            m_sc[p] = m_new[p]

    qk_tile(0, 0)
    n_pairs = (n_kv - 1) // 2

    def pair(jj, carry):
        j = 2 * jj
        fused_tile(j + 1, 1, j, 0)
        fused_tile(j + 2, 0, j + 1, 1)
        return carry

    if n_pairs > 0:
        lax.fori_loop(0, n_pairs, pair, 0)
    j = 2 * n_pairs
    if j == n_kv - 1:
        pv_tile(j, 0)
    else:
        qk_tile(j + 1, 1)
        pv_tile(j, 0)
        pv_tile(j + 1, 1)

    o = acc_sc[0] / l_sc[0] - lam_ref[0] * (acc_sc[1] / l_sc[1])
    ms = jnp.mean(o * o, axis=0, keepdims=True)
    o_ref[0] = ((o * lax.rsqrt(ms + EPS)).T * g_ref[...]).astype(BF16)


def _attention(lam, qz, k, vt, g, *, tq, tk, q_blk0, n_q, kv_blk0, n_kv, prev_out=None, name="attn_ctx"):
    B, H, _, _, S = qz.shape
    aliased = prev_out is not None
    skv = tk * n_kv
    in_specs = [pl.BlockSpec(memory_space=pltpu.SMEM),
                pl.BlockSpec((1, 1, 2, LANES, tq), lambda b, h, i: (b, h, 0, 0, q_blk0 + i)),
                pl.BlockSpec((1, 1, skv, LANES), lambda b, h, i: (b, h, kv_blk0, 0)),
                pl.BlockSpec((1, 1, n_kv, LANES, tk), lambda b, h, i: (b, h, 0, 0, 0)),
                pl.BlockSpec((1, LANES), lambda b, h, i: (0, 0))]
    args = [lam, qz, k, vt, g]
    aliases = {}
    if aliased:
        in_specs.append(pl.BlockSpec(memory_space=pl.ANY))
        args.append(prev_out)
        aliases = {5: 0}
    return pl.pallas_call(
        functools.partial(_attn_kernel, aliased=aliased, tk=tk, n_kv=n_kv),
        out_shape=jax.ShapeDtypeStruct((B, S, ATT_WIDTH), BF16),
        grid=(B, H, n_q),
        in_specs=in_specs,
        out_specs=pl.BlockSpec((1, tq, LANES), lambda b, h, i: (b, q_blk0 + i, h)),
        scratch_shapes=[pltpu.VMEM((2, 2, tk, tq), F32), pltpu.VMEM((2, 2, 1, tq), F32),
                        pltpu.VMEM((2, 1, tq), F32), pltpu.VMEM((2, 1, tq), F32),
                        pltpu.VMEM((2, LANES, tq), F32)],
        input_output_aliases=aliases,
        compiler_params=_cparams(("parallel", "parallel", "arbitrary")),
        name=name,
    )(*args)


def _attention_lat(lam, qz, k, vt, g, L):
    B, H, _, S = vt.shape
    tq = _pick_tile(L, (512, 256))
    tk = _pick_tile(S, (1280, 640, 256))
    n_kv = S // tk
    vt_tiles = vt.reshape(B, H, LANES, n_kv, tk).transpose(0, 1, 3, 2, 4)
    out0 = jnp.zeros((B, S, ATT_WIDTH), BF16)
    return _attention(lam, qz, k, vt_tiles, g, tq=tq, tk=tk, q_blk0=0, n_q=L // tq, kv_blk0=0, n_kv=n_kv,
                      prev_out=out0, name="attn_lat")


def _attention_ctx(lam, qz, k, vt, g, L, prev_out):
    B, H, _, S = vt.shape
    Lc = S - L
    vt_ctx = vt[:, :, :, L:].reshape(B, H, 1, LANES, Lc)
    return _attention(lam, qz, k, vt_ctx, g, tq=Lc, tk=Lc, q_blk0=L // Lc, n_q=1, kv_blk0=L // Lc, n_kv=1,
                      prev_out=prev_out)


def _gdnprep_kernel(x_ref, prev_ref, next_ref, cw_ref, ab_ref, alog_ref, dtb_ref,
                    q_ref, k_ref, v_ref, gate_ref, ext_sc, *, nb_lat, nb_all):
    i = pl.program_id(1)
    tm = x_ref.shape[1]
    first = jnp.logical_or(i == 0, i == nb_lat)
    last = jnp.logical_or(i == nb_lat - 1, i == nb_all - 1)
    keep_prev = jnp.where(first, 0.0, 1.0)
    keep_next = jnp.where(last, 0.0, 1.0)
    ext_sc[0:8, :] = prev_ref[0] * keep_prev
    ext_sc[8:8 + tm, :] = x_ref[0]
    ext_sc[8 + tm:16 + tm, :] = next_ref[0] * keep_next
    acc = None
    for j in range(CONV_W):
        term = ext_sc[pl.ds(8 + j - CONV_W // 2, tm), :] * cw_ref[j:j + 1, :]
        acc = term if acc is None else acc + term
    y = acc * _sigmoid(acc)
    for part, ref in ((0, q_ref), (1, k_ref)):
        for hh in range(GDN_HEADS):
            lo = part * GDN_WIDTH + hh * GDN_HEAD_DIM
            t = y[:, lo:lo + GDN_HEAD_DIM]
            ref[0, :, hh * GDN_HEAD_DIM:(hh + 1) * GDN_HEAD_DIM] = (
                t * lax.rsqrt(jnp.sum(t * t, axis=-1, keepdims=True) + EPS))
    v_ref[0] = y[:, 2 * GDN_WIDTH:3 * GDN_WIDTH]
    ab = ab_ref[0]
    xs = ab + dtb_ref[...]
    sp = jnp.maximum(xs, 0.0) + jnp.log(1.0 + jnp.exp(-jnp.abs(xs)))
    g = -jnp.exp(alog_ref[...]) * sp
    lane = lax.broadcasted_iota(jnp.int32, ab.shape, 1)
    gate_ref[0] = jnp.where(lane < 2 * GDN_HEADS, g, _sigmoid(ab))


def _gdnprep(p_gdn, conv_w8, ab, alog_row, dtb_row, n_lat_blocks):
    B, S, W = p_gdn.shape
    tm = ROW_TILE
    nb = S // tm
    r8 = tm // 8
    row = lambda b, i: (b, i, 0)
    return pl.pallas_call(
        functools.partial(_gdnprep_kernel, nb_lat=n_lat_blocks, nb_all=nb),
        out_shape=(jax.ShapeDtypeStruct((B, S, GDN_WIDTH), F32),
                   jax.ShapeDtypeStruct((B, S, GDN_WIDTH), F32),
                   jax.ShapeDtypeStruct((B, S, GDN_WIDTH), F32),
                   jax.ShapeDtypeStruct((B, S, LANES), F32)),
        grid=(B, nb),
        in_specs=[pl.BlockSpec((1, tm, W), row),
                  pl.BlockSpec((1, 8, W), lambda b, i: (b, jnp.maximum(i * r8 - 1, 0), 0)),
                  pl.BlockSpec((1, 8, W), lambda b, i: (b, jnp.minimum((i + 1) * r8, nb * r8 - 1), 0)),
                  pl.BlockSpec((8, W), lambda b, i: (0, 0)),
                  pl.BlockSpec((1, tm, LANES), row),
                  pl.BlockSpec((1, LANES), lambda b, i: (0, 0)),
                  pl.BlockSpec((1, LANES), lambda b, i: (0, 0))],
        out_specs=(pl.BlockSpec((1, tm, GDN_WIDTH), row),
                   pl.BlockSpec((1, tm, GDN_WIDTH), row),
                   pl.BlockSpec((1, tm, GDN_WIDTH), row),
                   pl.BlockSpec((1, tm, LANES), row)),
        scratch_shapes=[pltpu.VMEM((tm + 16, W), F32)],
        compiler_params=_cparams(("parallel", "parallel")),
        name="gdnprep",
    )(p_gdn, p_gdn, p_gdn, conv_w8, ab, alog_row, dtb_row)


def _gdnchunk_kernel(q_ref, k_ref, v_ref, gate_ref, qg_ref, w_ref, kd_ref, u_ref, aq_ref, eg_ref):
    tm = q_ref.shape[1]
    nc = tm // GDN_CHUNK
    gates = gate_ref[0]
    ri = lax.broadcasted_iota(jnp.int32, (tm, tm), 0)
    ci = lax.broadcasted_iota(jnp.int32, (tm, tm), 1)
    same = (ri >> 6) == (ci >> 6)
    eye = ri == ci
    eye_f = jnp.where(eye, 1.0, 0.0).astype(F32)

    g1 = gates.astype(BF16)
    r1 = gates - g1.astype(F32)
    g2 = r1.astype(BF16)
    g3 = (r1 - g2.astype(F32)).astype(BF16)
    gparts = jnp.concatenate([g1, g2, g3], axis=1)

    def seg_sum(mask):
        r = _dot(jnp.where(mask, 1.0, 0.0).astype(BF16), gparts)
        return (r[:, :LANES] + r[:, LANES:2 * LANES]) + r[:, 2 * LANES:]

    tot = seg_sum(same)

    heads = []
    for hh in range(GDN_HEADS):
        sl = slice(hh * GDN_HEAD_DIM, (hh + 1) * GDN_HEAD_DIM)
        k = k_ref[0, :, sl]
        k16 = k.astype(BF16)
        q = q_ref[0, :, sl] * (GDN_HEAD_DIM ** -0.5)
        heads.append((sl, k, q, _dot_nt(k16, k16), _dot_nt(q.astype(BF16), k16)))

    chains = []
    for d in range(2):
        incl = jnp.logical_and(same, (ci <= ri) if d == 0 else (ci >= ri))
        strict = jnp.logical_and(incl, jnp.logical_not(eye))
        gc = seg_sum(incl)
        gc_t = gc.T
        for hh in range(GDN_HEADS):
            ln = d * GDN_HEADS + hh
            gcol = gc[:, ln:ln + 1]
            bcol = gates[:, 2 * GDN_HEADS + ln:2 * GDN_HEADS + ln + 1]
            tcol = tot[:, ln:ln + 1]
            diff = gcol - gc_t[ln:ln + 1, :]
            decay = jnp.where(incl, jnp.exp(jnp.where(incl, diff, 0.0)), 0.0)
            a = jnp.where(strict, heads[hh][3] * bcol * decay, 0.0)
            aqk = heads[hh][4] * decay
            chains.append(dict(d=d, hh=hh, x=-a, t=eye_f - a, aqk=aqk, gcol=gcol, bcol=bcol, tcol=tcol))

    for _ in range(5):
        for ch in chains:
            x16 = ch["x"].astype(BF16)
            ch["x"] = _dot(x16, x16)
        for ch in chains:
            ch["t"] = ch["t"] + _dot(ch["t"].astype(BF16), ch["x"].astype(BF16))

    for ch in chains:
        d, hh, gcol, bcol, tcol = ch["d"], ch["hh"], ch["gcol"], ch["bcol"], ch["tcol"]
        sl, k, q = heads[hh][0], heads[hh][1], heads[hh][2]
        t16 = ch["t"].astype(BF16)
        egc = jnp.exp(gcol)
        kb = k * bcol
        rhs = jnp.concatenate([(v_ref[0, :, sl] * bcol).astype(BF16), (kb * egc).astype(BF16)], axis=1)
        uw = _dot(t16, rhs)
        u_ref[0, d, :, sl] = uw[:, :GDN_HEAD_DIM]
        w_ref[0, d, :, sl] = uw[:, GDN_HEAD_DIM:].astype(BF16)
        kd_ref[0, d, :, sl] = (k * jnp.exp(tcol - gcol)).astype(BF16)
        qg_ref[0, d, :, sl] = (q * egc).astype(BF16)
        for cc in range(nc):
            rs = slice(cc * GDN_CHUNK, (cc + 1) * GDN_CHUNK)
            aq_ref[0, d, rs, hh * GDN_CHUNK:(hh + 1) * GDN_CHUNK] = ch["aqk"][rs, rs].astype(BF16)
            eg_ref[0, d, cc, hh:hh + 1, :] = jnp.broadcast_to(
                jnp.exp(tcol[cc * GDN_CHUNK:cc * GDN_CHUNK + 1, :]), (1, LANES))


def _gdnchunk(gq, gk, gv, gates):
    B, S, W = gq.shape
    tm = ROW_TILE
    nc = tm // GDN_CHUNK
    row = lambda b, i: (b, i, 0)
    drow = lambda b, i: (b, 0, i, 0)
    big = lambda dt: jax.ShapeDtypeStruct((B, 2, S, W), dt)
    return pl.pallas_call(
        _gdnchunk_kernel,
        out_shape=(big(BF16), big(BF16), big(BF16), big(F32),
                   jax.ShapeDtypeStruct((B, 2, S, GDN_HEADS * GDN_CHUNK), BF16),
                   jax.ShapeDtypeStruct((B, 2, S // GDN_CHUNK, GDN_HEADS, LANES), F32)),
        grid=(B, S // tm),
        in_specs=[pl.BlockSpec((1, tm, W), row), pl.BlockSpec((1, tm, W), row),
                  pl.BlockSpec((1, tm, W), row), pl.BlockSpec((1, tm, LANES), row)],
        out_specs=(pl.BlockSpec((1, 2, tm, W), drow), pl.BlockSpec((1, 2, tm, W), drow),
                   pl.BlockSpec((1, 2, tm, W), drow), pl.BlockSpec((1, 2, tm, W), drow),
                   pl.BlockSpec((1, 2, tm, GDN_HEADS * GDN_CHUNK), drow),
                   pl.BlockSpec((1, 2, nc, GDN_HEADS, LANES), lambda b, i: (b, 0, i, 0, 0))),
        compiler_params=_cparams(("parallel", "parallel")),
        name="gdnchunk",
    )(gq, gk, gv, gates)


def _gdnscan_kernel(qg0, w0, kd0, u0, aq0, eg0, qg1, w1, kd1, u1, aq1, eg1, of_ref, ob_ref, s_sc):
    @pl.when(pl.program_id(1) == 0)
    def _():
        s_sc[...] = jnp.zeros(s_sc.shape, F32)

    dirs = ((qg0, w0, kd0, u0, aq0, eg0, of_ref), (qg1, w1, kd1, u1, aq1, eg1, ob_ref))
    chains = [(d, hh) for d in range(2) for hh in range(GDN_HEADS)]
    st = {ch: s_sc[ch[0], ch[1]] for ch in chains}
    for step in range(SCAN_GROUP):
        st16, vn16, qs = {}, {}, {}
        for d, hh in chains:
            st16[d, hh] = st[d, hh].astype(BF16)
        for d, hh in chains:
            qg, w, kd, u, aq, eg, o_ref = dirs[d]
            cc = step if d == 0 else SCAN_GROUP - 1 - step
            rs = slice(cc * GDN_CHUNK, (cc + 1) * GDN_CHUNK)
            sl = slice(hh * GDN_HEAD_DIM, (hh + 1) * GDN_HEAD_DIM)
            vn16[d, hh] = (u[0, 0, rs, sl] - _dot(w[0, 0, rs, sl], st16[d, hh])).astype(BF16)
            qs[d, hh] = _dot(qg[0, 0, rs, sl], st16[d, hh])
        for d, hh in chains:
            qg, w, kd, u, aq, eg, o_ref = dirs[d]
            cc = step if d == 0 else SCAN_GROUP - 1 - step
            rs = slice(cc * GDN_CHUNK, (cc + 1) * GDN_CHUNK)
            sl = slice(hh * GDN_HEAD_DIM, (hh + 1) * GDN_HEAD_DIM)
            o_ref[0, rs, sl] = qs[d, hh] + _dot(aq[0, 0, rs, hh * GDN_CHUNK:(hh + 1) * GDN_CHUNK], vn16[d, hh])
            st[d, hh] = st[d, hh] * eg[0, 0, cc, hh:hh + 1, :] + _dot_tn(kd[0, 0, rs, sl], vn16[d, hh])
    for d, hh in chains:
        s_sc[d, hh] = st[d, hh]


def _gdnscan(qg, w, kd, u, aq, eg, n_lat_chunks, n_ctx_chunks):
    B, _, S, W = qg.shape
    G = SCAN_GROUP
    R = GDN_CHUNK * G
    n = S // R
    n_lat, n_ctx = n_lat_chunks // G, n_ctx_chunks // G
    assert n_lat * G == n_lat_chunks and n_ctx * G == n_ctx_chunks

    def fwd_blk(i):
        return jnp.where(i < n_ctx, n_lat + i, i - n_ctx)

    def bwd_blk(i):
        return jnp.where(i < n_ctx, n_lat + n_ctx - 1 - i, n - 1 - i)

    def specs(d, blk_of):
        big = pl.BlockSpec((1, 1, R, W), lambda b, i: (b, d, blk_of(i), 0))
        return [big, big, big, big,
                pl.BlockSpec((1, 1, R, GDN_HEADS * GDN_CHUNK), lambda b, i: (b, d, blk_of(i), 0)),
                pl.BlockSpec((1, 1, G, GDN_HEADS, LANES), lambda b, i: (b, d, blk_of(i), 0, 0))]

    return pl.pallas_call(
        _gdnscan_kernel,
        out_shape=(jax.ShapeDtypeStruct((B, S, W), F32), jax.ShapeDtypeStruct((B, S, W), F32)),
        grid=(B, n),
        in_specs=specs(0, fwd_blk) + specs(1, bwd_blk),
        out_specs=(pl.BlockSpec((1, R, W), lambda b, i: (b, fwd_blk(i), 0)),
                   pl.BlockSpec((1, R, W), lambda b, i: (b, bwd_blk(i), 0))),
        scratch_shapes=[pltpu.VMEM((2, GDN_HEADS, GDN_HEAD_DIM, GDN_HEAD_DIM), F32)],
        compiler_params=_cparams(("parallel", "arbitrary")),
        name="gdnscan",
    )(qg, w, kd, u, aq, eg, qg, w, kd, u, aq, eg)


def _mixout_kernel(oa_ref, of_ref, ob_ref, z_ref, h_ref, mod_ref, gg_ref, wo_ref, gf_ref, rw_ref, rb_ref,
                   hn_ref, v_ref, te_ref, tg_ref):
    og = of_ref[0] + ob_ref[0]
    z = z_ref[0]
    parts = [oa_ref[0]]
    for hh in range(GDN_HEADS):
        sl = slice(hh * GDN_HEAD_DIM, (hh + 1) * GDN_HEAD_DIM)
        t = og[:, sl]
        t = t * lax.rsqrt(jnp.mean(t * t, axis=-1, keepdims=True) + EPS) * gg_ref[...]
        zz = z[:, sl]
        parts.append((t * (zz * _sigmoid(zz))).astype(BF16))
    mix_in = jnp.concatenate(parts, axis=-1)
    mix = _dot(mix_in, wo_ref[...])
    hn = h_ref[0] + mod_ref[0, 0, 2:3, :] * mix
    hn_ref[0] = hn
    y = hn * lax.rsqrt(jnp.mean(hn * hn, axis=-1, keepdims=True) + EPS) * gf_ref[...]
    v = y * (1.0 + mod_ref[0, 0, 4:5, :]) + mod_ref[0, 0, 3:4, :]
    v_ref[0] = v.astype(BF16)
    v_hi = v.astype(BF16)
    v_lo = (v - v_hi.astype(F32)).astype(BF16)
    logits = (_dot(v_hi, rw_ref[0]) + _dot(v_lo, rw_ref[0]) + _dot(v_hi, rw_ref[1])) + rb_ref[...]
    lane = lax.broadcasted_iota(jnp.int32, logits.shape, 1)
    cur = logits
    vals, idxs = [], []
    for _ in range(TOP_K):
        m = jnp.max(cur, axis=-1, keepdims=True)
        idx = jnp.min(jnp.where(cur == m, lane, LANES), axis=-1, keepdims=True)
        vals.append(m)
        idxs.append(idx)
        cur = jnp.where(lane == idx, -jnp.inf, cur)
    es = [jnp.exp(vv - vals[0]) for vv in vals]
    inv = 1.0 / (es[0] + es[1] + es[2] + es[3])
    te = jnp.zeros(logits.shape, jnp.int32)
    tg = jnp.zeros(logits.shape, F32)
    for kk in range(TOP_K):
        te = jnp.where(lane == kk, idxs[kk], te)
        tg = jnp.where(lane == kk, es[kk] * inv, tg)
    te_ref[0] = te
    tg_ref[0] = tg


def _mixout(o_att, o_f, o_b, z, h, modtab, gg, w_out, gf, rw, rb, n_lat_blocks):
    B, S, _ = h.shape
    tm = ROW_TILE
    row = lambda b, i: (b, i, 0)
    const = lambda b, i: (0, 0)
    return pl.pallas_call(
        _mixout_kernel,
        out_shape=(jax.ShapeDtypeStruct((B, S, D_MODEL), F32),
                   jax.ShapeDtypeStruct((B, S, D_MODEL), BF16),
                   jax.ShapeDtypeStruct((B, S, LANES), jnp.int32),
                   jax.ShapeDtypeStruct((B, S, LANES), F32)),
        grid=(B, S // tm),
        in_specs=[pl.BlockSpec((1, tm, ATT_WIDTH), row),
                  pl.BlockSpec((1, tm, GDN_WIDTH), row),
                  pl.BlockSpec((1, tm, GDN_WIDTH), row),
                  pl.BlockSpec((1, tm, GDN_WIDTH), row),
                  pl.BlockSpec((1, tm, D_MODEL), row),
                  pl.BlockSpec((1, 1, 8, D_MODEL), lambda b, i: (b, (i >= n_lat_blocks).astype(jnp.int32), 0, 0)),
                  pl.BlockSpec((1, LANES), const),
                  pl.BlockSpec((D_MODEL, D_MODEL), const),
                  pl.BlockSpec((1, D_MODEL), const),
                  pl.BlockSpec((2, D_MODEL, LANES), lambda b, i: (0, 0, 0)),
                  pl.BlockSpec((1, LANES), const)],
        out_specs=(pl.BlockSpec((1, tm, D_MODEL), row),
                   pl.BlockSpec((1, tm, D_MODEL), row),
                   pl.BlockSpec((1, tm, LANES), row),
                   pl.BlockSpec((1, tm, LANES), row)),
        compiler_params=_cparams(("parallel", "parallel")),
        name="mixout",
    )(o_att, o_f, o_b, z, h, modtab, gg, w_out, gf, rw, rb)


def _expert_kernel(be_ref, nv_ref, x_ref, wgu_ref, bgu_ref, wd_ref, bd_ref, y_ref, wgu_sc, wd_sc):
    i = pl.program_id(0)
    new_expert = jnp.logical_or(i == 0, be_ref[i] != be_ref[jnp.maximum(i - 1, 0)])

    @pl.when(new_expert)
    def _():
        wgu_sc[...] = wgu_ref[0, 0].astype(BF16)
        wd_sc[...] = wd_ref[0, 0].astype(BF16)

    @pl.when(nv_ref[i] > 0)
    def _():
        gu = _dot(x_ref[...], wgu_sc[...]) + bgu_ref[0, 0]
        g_ = jnp.minimum(gu[:, :D_EXPERT], SWIGLU_LIMIT)
        up = jnp.clip(gu[:, D_EXPERT:], -SWIGLU_LIMIT, SWIGLU_LIMIT)
        glu = g_ * _sigmoid(SWIGLU_ALPHA * g_)
        act = ((up + 1.0) * glu).astype(BF16)
        y_ref[...] = (_dot(act, wd_sc[...]) + bd_ref[0, 0]).astype(y_ref.dtype)

    @pl.when(nv_ref[i] == 0)
    def _():
        y_ref[...] = jnp.zeros(y_ref.shape, y_ref.dtype)


def _experts(block_e, n_valid, x_sorted, wgu, bgu, wd, bd, layer):
    n_slots = x_sorted.shape[0]
    nb = n_slots // MOE_BLOCK
    grid_spec = pltpu.PrefetchScalarGridSpec(
        num_scalar_prefetch=2,
        grid=(nb,),
        in_specs=[pl.BlockSpec((MOE_BLOCK, D_MODEL), lambda i, be, nv: (i, 0)),
                  pl.BlockSpec((1, 1, D_MODEL, 2 * D_EXPERT), lambda i, be, nv: (layer, be[i], 0, 0)),
                  pl.BlockSpec((1, 1, 1, 2 * D_EXPERT), lambda i, be, nv: (layer, be[i], 0, 0)),
                  pl.BlockSpec((1, 1, D_EXPERT, D_MODEL), lambda i, be, nv: (layer, be[i], 0, 0)),
                  pl.BlockSpec((1, 1, 1, D_MODEL), lambda i, be, nv: (layer, be[i], 0, 0))],
        out_specs=pl.BlockSpec((MOE_BLOCK, D_MODEL), lambda i, be, nv: (i, 0)),
        scratch_shapes=[pltpu.VMEM((D_MODEL, 2 * D_EXPERT), BF16), pltpu.VMEM((D_EXPERT, D_MODEL), BF16)],
    )
    return pl.pallas_call(
        _expert_kernel,
        out_shape=jax.ShapeDtypeStruct((n_slots, D_MODEL), BF16),
        grid_spec=grid_spec,
        compiler_params=pltpu.CompilerParams(dimension_semantics=("arbitrary",),
                                             vmem_limit_bytes=EXPERT_VMEM_LIMIT),
        name="experts",
    )(block_e, n_valid, x_sorted, wgu, bgu, wd, bd)


def _moe_plan(top_e, top_g):
    T = top_e.shape[0]
    n_assign = T * TOP_K
    n_blocks = -(-n_assign // MOE_BLOCK) + N_EXPERTS
    n_slots = n_blocks * MOE_BLOCK
    i32 = jnp.int32
    flat_e = top_e.reshape(-1)
    gate_flat = top_g.reshape(-1)
    assert T < NO_TOKEN
    gate_bits = lax.bitcast_convert_type(gate_flat.astype(BF16), jnp.uint16).astype(jnp.uint32)
    tok_pack = (gate_bits << 16) | (jnp.arange(n_assign, dtype=jnp.uint32) // TOP_K)
    _, sorted_pack = lax.sort((flat_e, tok_pack), num_keys=1, is_stable=True)
    e_ids = jnp.arange(N_EXPERTS, dtype=i32)
    is_e = flat_e[:, None] == e_ids[None, :]
    counts = jnp.sum(is_e, axis=0, dtype=i32)
    start = jnp.cumsum(counts) - counts
    padded = (counts + MOE_BLOCK - 1) // MOE_BLOCK * MOE_BLOCK
    pad_end = jnp.cumsum(padded)
    pad_start = pad_end - padded
    blk0 = jnp.arange(n_blocks, dtype=i32) * MOE_BLOCK
    block_e = jnp.minimum(jnp.sum(pad_end[None, :] <= blk0[:, None], axis=1, dtype=i32), N_EXPERTS - 1)
    off = (blk0 - pad_start[block_e])[:, None] + jnp.arange(MOE_BLOCK, dtype=i32)[None, :]
    valid = off < counts[block_e][:, None]
    valid_flat = valid.reshape(-1)
    slot_pack = jnp.where(valid_flat,
                          sorted_pack[jnp.clip(off + start[block_e][:, None], 0, n_assign - 1).reshape(-1)],
                          jnp.uint32(NO_TOKEN))
    slot_tok = jnp.where(valid_flat, (slot_pack & NO_TOKEN).astype(i32), jnp.arange(n_slots, dtype=i32) % T)
    n_valid = jnp.sum(valid, axis=1, dtype=i32)

    n_tiles = T // COMBINE_TILE
    cnt = jnp.sum(top_e.reshape(n_tiles, COMBINE_TILE * TOP_K)[:, :, None] == e_ids[None, None, :], axis=1, dtype=i32)
    run_start = pad_start[None, :] + jnp.cumsum(cnt, axis=0) - cnt
    q_first = run_start // COMBINE_CHUNK
    n_ch = jnp.where(cnt > 0, (run_start + cnt - 1) // COMBINE_CHUNK - q_first + 1, 0)
    ch_end = jnp.cumsum(n_ch, axis=1)
    ch_off = ch_end - n_ch
    j = jnp.arange(COMBINE_MAX_CHUNKS, dtype=i32)
    e_j = jnp.minimum(jnp.sum(ch_end[:, None, :] <= j[None, :, None], axis=2, dtype=i32), N_EXPERTS - 1)
    used = j[None, :] < ch_end[:, -1:]
    base = jnp.sum(jnp.where(e_j[:, :, None] == e_ids[None, None, :], (q_first - ch_off)[:, None, :], 0),
                   axis=2, dtype=i32)
    chunk_id = jnp.where(used, base + j[None, :], 0)
    slot_pack = slot_pack.reshape(n_slots // COMBINE_CHUNK, COMBINE_CHUNK)
    row_pack = jnp.where(used[:, :, None], slot_pack[chunk_id], jnp.uint32(NO_TOKEN))
    row_pack = lax.bitcast_convert_type(row_pack, i32).reshape(n_tiles, 1, COMBINE_MAX_CHUNKS * COMBINE_CHUNK)
    return slot_tok, block_e, n_valid, chunk_id, row_pack


def _combine_kernel(cid_ref, tok_ref, h_ref, mod_ref, y_hbm, o_ref, ybuf, sem):
    i = pl.program_id(0)
    n = pl.num_programs(0)
    slot = lax.rem(i, 2)

    def chunk_copy(c, s, jj):
        return pltpu.make_async_copy(
            y_hbm.at[pl.ds(pl.multiple_of(c * COMBINE_CHUNK, COMBINE_CHUNK), COMBINE_CHUNK)],
            ybuf.at[s, pl.ds(jj * COMBINE_CHUNK, COMBINE_CHUNK)], sem.at[s])

    def start_tile(t, s):
        for jj in range(COMBINE_MAX_CHUNKS):
            chunk_copy(cid_ref[t, jj], s, jj).start()

    @pl.when(i == 0)
    def _():
        start_tile(0, 0)

    @pl.when(i + 1 < n)
    def _():
        start_tile(i + 1, 1 - slot)

    for jj in range(COMBINE_MAX_CHUNKS):
        chunk_copy(0, slot, jj).wait()

    packed = tok_ref[0]
    tok = packed & NO_TOKEN
    gate = lax.bitcast_convert_type(packed & ~NO_TOKEN, F32)
    t_ids = i * COMBINE_TILE + lax.broadcasted_iota(jnp.int32, (COMBINE_TILE, tok.shape[1]), 0)
    sel = jnp.where(tok == t_ids, gate, 0.0).astype(BF16)
    y = _dot(sel, ybuf[slot])
    o_ref[...] = h_ref[...] + mod_ref[0, 0, 5:6, :] * y


def _combine(chunk_id, row_tok, y_sorted, h, modtab, n_lat_blocks):
    B, S, _ = h.shape
    T = B * S
    ct = COMBINE_TILE
    tiles_per_batch = S // ct
    n_lat_tiles = n_lat_blocks * ROW_TILE // ct
    n_rows = COMBINE_MAX_CHUNKS * COMBINE_CHUNK
    grid_spec = pltpu.PrefetchScalarGridSpec(
        num_scalar_prefetch=1,
        grid=(T // ct,),
        in_specs=[pl.BlockSpec((1, 1, n_rows), lambda i, cid: (i, 0, 0)),
                  pl.BlockSpec((ct, D_MODEL), lambda i, cid: (i, 0)),
                  pl.BlockSpec((1, 1, 8, D_MODEL),
                               lambda i, cid: (i // tiles_per_batch,
                                               (lax.rem(i, tiles_per_batch) >= n_lat_tiles).astype(jnp.int32), 0, 0)),
                  pl.BlockSpec(memory_space=pl.ANY)],
        out_specs=pl.BlockSpec((ct, D_MODEL), lambda i, cid: (i, 0)),
        scratch_shapes=[pltpu.VMEM((2, n_rows, D_MODEL), BF16), pltpu.SemaphoreType.DMA((2,))],
    )
    out = pl.pallas_call(
        _combine_kernel,
        out_shape=jax.ShapeDtypeStruct((T, D_MODEL), F32),
        grid_spec=grid_spec,
        compiler_params=_cparams(("arbitrary",)),
        name="combine",
    )(chunk_id, row_tok, h.reshape(T, D_MODEL), modtab, y_sorted)
    return out.reshape(B, S, D_MODEL)


def _moe(v_ffn, top_e, top_g, h_new, modtab, wgu, bgu, wd, bd, layer, n_lat_blocks):
    B, S, _ = v_ffn.shape
    T = B * S
    slot_tok, block_e, n_valid, chunk_id, row_tok = _moe_plan(
        top_e.reshape(T, LANES)[:, :TOP_K], top_g.reshape(T, LANES)[:, :TOP_K])
    x_sorted = v_ffn.reshape(T, D_MODEL)[slot_tok]
    y_sorted = _experts(block_e, n_valid, x_sorted, wgu, bgu, wd, bd, layer)
    return _combine(chunk_id, row_tok, y_sorted, h_new, modtab, n_lat_blocks)


def _pick_tile(n, cands):
    for t in cands:
        if n % t == 0:
            return t
    raise ValueError(f"no tile for {n}")


def _rope_tables(L, Lc):
    rows = L // GRID_W
    row = jnp.repeat(jnp.arange(rows, dtype=F32), GRID_W)
    col = (jnp.arange(L, dtype=jnp.int32) % GRID_W).astype(F32)
    inv_freq = ROPE_BASE ** (-jnp.arange(ROPE_PAIRS, dtype=F32) / ROPE_PAIRS)
    ar = row[:, None] * inv_freq
    ac = col[:, None] * inv_freq
    cos64 = jnp.concatenate([jnp.cos(ar), jnp.cos(ar), jnp.cos(ac), jnp.cos(ac)], axis=-1)
    sin64 = jnp.concatenate([-jnp.sin(ar), jnp.sin(ar), -jnp.sin(ac), jnp.sin(ac)], axis=-1)
    cos_t = jnp.concatenate([jnp.tile(cos64, (1, 2)), jnp.ones((Lc, LANES), F32)], axis=0)
    sin_t = jnp.concatenate([jnp.tile(sin64, (1, 2)), jnp.zeros((Lc, LANES), F32)], axis=0)
    return cos_t, sin_t


def _pad_lanes(v):
    v = v.reshape(1, -1).astype(F32)
    return jnp.pad(v, ((0, 0), (0, LANES - v.shape[1])))


def kernel(x, c, ctx, c_ctx, w_mod, b_mod, norm_mix_g, w_in, q_norm_g, k_norm_g, lam_q1, lam_k1, lam_q2, lam_k2, subln_g, conv_w, a_log, dt_bias, gdn_norm_g, w_out, norm_ffn_g, router_w, router_b, w_gate_up, b_gate_up, w_down, b_down):
    B, L, D = x.shape
    Lc = ctx.shape[1]
    S = L + Lc
    depth = w_mod.shape[0]
    tm = ROW_TILE
    n_lat_blocks = L // tm
    cos_t, sin_t = _rope_tables(L, Lc)

    c_rows = jnp.zeros((8, D), F32).at[:B].set(c).at[B].set(c_ctx)
    h = jnp.concatenate([x, ctx], axis=1)

    for layer in range(depth):
        mod = _adaln(c_rows, w_mod, b_mod, layer)
        mod6 = mod.reshape(8, 6, D)
        lat_mod = mod6[:B]
        ctx_mod = jnp.broadcast_to(mod6[B][None], (B, 6, D))
        modtab = jnp.pad(jnp.stack([lat_mod, ctx_mod], axis=1), ((0, 0), (0, 0), (0, 2), (0, 0)))

        lam_init = 0.8 - 0.6 * math.exp(-0.3 * layer)
        lam_full = (jnp.exp(jnp.sum(lam_q1[layer] * lam_k1[layer]))
                    - jnp.exp(jnp.sum(lam_q2[layer] * lam_k2[layer])) + lam_init).reshape(1).astype(F32)

        w_l = w_in[layer]
        w_main = w_l[:, :IN_MAIN].astype(BF16)
        w_ab = jnp.pad(w_l[:, IN_MAIN:], ((0, 0), (0, LANES - (w_l.shape[1] - IN_MAIN)))).astype(BF16)
        gq = jnp.tile(q_norm_g[layer].reshape(1, ATT_HEAD_DIM), (1, 2))
        gk = jnp.tile(k_norm_g[layer].reshape(1, ATT_HEAD_DIM), (1, 2))
        qz, k_att, v_att, p_gdn, z, ab = _inproj(h, modtab, norm_mix_g[layer].reshape(1, D), w_main, w_ab,
                                                 cos_t, sin_t, gq, gk, n_lat_blocks)
        g_sub = (subln_g[layer] * (1.0 - lam_init)).reshape(1, LANES).astype(F32)
        o_att = _attention_lat(lam_full, qz, k_att, v_att, g_sub, L)
        o_att = _attention_ctx(lam_full, qz, k_att, v_att, g_sub, L, o_att)

        conv_w8 = jnp.pad(conv_w[layer], ((0, 8 - CONV_W), (0, 0)))
        gq_g, gk_g, gv_g, gates = _gdnprep(p_gdn, conv_w8, ab, _pad_lanes(a_log[layer]),
                                           _pad_lanes(dt_bias[layer]), n_lat_blocks)
        qg, w_g, kd, u_g, aq, eg = _gdnchunk(gq_g, gk_g, gv_g, gates)
        o_f, o_b = _gdnscan(qg, w_g, kd, u_g, aq, eg, L // GDN_CHUNK, Lc // GDN_CHUNK)

        rw = jnp.pad(router_w[layer].astype(F32), ((0, 0), (0, LANES - N_EXPERTS)))
        rw_hi = rw.astype(BF16)
        rw = jnp.stack([rw_hi, (rw - rw_hi.astype(F32)).astype(BF16)])
        rb = jnp.pad(router_b[layer].reshape(1, N_EXPERTS).astype(F32), ((0, 0), (0, LANES - N_EXPERTS)),
                     constant_values=-1e30)
        h_new, v_ffn, top_e, top_g = _mixout(
            o_att, o_f, o_b, z, h, modtab, gdn_norm_g[layer].reshape(1, LANES), w_out[layer].astype(BF16),
            norm_ffn_g[layer].reshape(1, D), rw, rb, n_lat_blocks)

        h = _moe(v_ffn, top_e, top_g, h_new, modtab,
                 w_gate_up, b_gate_up.reshape(depth, N_EXPERTS, 1, 2 * D_EXPERT),
                 w_down, b_down.reshape(depth, N_EXPERTS, 1, D), layer, n_lat_blocks)
    return h[:, :L]
```

```python
import functools
import math

import jax
import jax.numpy as jnp
from jax import lax
from jax.experimental import pallas as pl
from jax.experimental.pallas import tpu as pltpu

F32 = jnp.float32
BF16 = jnp.bfloat16
HIGHEST = lax.Precision.HIGHEST

D_MODEL = 1024
GRID_W = 64
EPS = 1e-6
ATT_WIDTH = 512
ATT_HEAD_DIM = 64
ATT_HEADS = 4
ROPE_BASE = 10000.0
ROPE_PAIRS = ATT_HEAD_DIM // 4
GDN_WIDTH = 512
GDN_HEAD_DIM = 128
GDN_HEADS = 4
GDN_CHUNK = 64
CONV_W = 5
IN_MAIN = 3 * ATT_WIDTH + 4 * GDN_WIDTH
N_EXPERTS = 32
TOP_K = 4
D_EXPERT = 1024
SWIGLU_ALPHA = 1.702
SWIGLU_LIMIT = 7.0
MOE_BLOCK = 512
SCAN_GROUP = 4
NO_TOKEN = 0xFFFF
COMBINE_TILE = 256
COMBINE_CHUNK = 16
COMBINE_MAX_CHUNKS = COMBINE_TILE * TOP_K // COMBINE_CHUNK + 2 * N_EXPERTS

LANES = 128
ROW_TILE = 256
KV_CHUNK = 256
Q_SCALE = ATT_HEAD_DIM ** -0.5 * math.log2(math.e)
VMEM_LIMIT = 48 * 1024 * 1024
EXPERT_VMEM_LIMIT = 56 * 1024 * 1024


def _cparams(sem):
    return pltpu.CompilerParams(dimension_semantics=sem, vmem_limit_bytes=VMEM_LIMIT)


def _dot(a, b):
    return jnp.dot(a, b, preferred_element_type=F32)


def _dot_nt(a, b):
    return lax.dot_general(a, b, (((1,), (1,)), ((), ())), preferred_element_type=F32)


def _dot_tn(a, b):
    return lax.dot_general(a, b, (((0,), (0,)), ((), ())), preferred_element_type=F32)


def _dot_hi(a, b):
    return jnp.dot(a, b, preferred_element_type=F32, precision=HIGHEST)


def _sigmoid(x):
    return 1.0 / (1.0 + jnp.exp(-x))


def _adaln_kernel(c_ref, w_ref, b_ref, o_ref):
    c = c_ref[...]
    s = c * _sigmoid(c)
    o_ref[...] = _dot_hi(s, w_ref[0]) + b_ref[0]


def _adaln(c_rows, w, b, layer):
    depth, _, n = w.shape
    tn = 1024
    return pl.pallas_call(
        _adaln_kernel,
        out_shape=jax.ShapeDtypeStruct((8, n), F32),
        grid=(n // tn,),
        in_specs=[pl.BlockSpec((8, D_MODEL), lambda j: (0, 0)),
                  pl.BlockSpec((1, D_MODEL, tn), lambda j: (layer, 0, j)),
                  pl.BlockSpec((1, 1, tn), lambda j: (layer, 0, j))],
        out_specs=pl.BlockSpec((8, tn), lambda j: (0, j)),
        compiler_params=_cparams(("arbitrary",)),
        name="adaln",
    )(c_rows, w, b.reshape(depth, 1, n))


def _qk_norm_rope(p_qk, cos_ref, sin_ref, gq_ref, gk_ref, qz_ref, k_ref):
    tm = p_qk.shape[0]
    lane = lax.broadcasted_iota(jnp.int32, (tm, LANES), 1)
    hi16 = (lane & 16) != 0
    first = lax.broadcasted_iota(jnp.int32, (LANES, tm), 0) < ATT_HEAD_DIM
    r = lax.broadcasted_iota(jnp.int32, (LANES, LANES), 0) >> 6
    c = lax.broadcasted_iota(jnp.int32, (LANES, LANES), 1) >> 6
    gmat = jnp.where(r == c, 1.0 / ATT_HEAD_DIM, 0.0).astype(BF16)
    cosv = cos_ref[...]
    sinv = sin_ref[...]
    sq = jnp.concatenate([jnp.square(p_qk[:, j * LANES:(j + 1) * LANES]) for j in range(2 * ATT_HEADS)], axis=0)
    sq_hi = sq.astype(BF16)
    sq_lo = (sq - sq_hi.astype(F32)).astype(BF16)
    ms_all = _dot(sq_hi, gmat) + _dot(sq_lo, gmat)
    for j in range(2 * ATT_HEADS):
        x = p_qk[:, j * LANES:(j + 1) * LANES]
        ms = ms_all[j * tm:(j + 1) * tm, :]
        g = gq_ref[...] if j < ATT_HEADS else gk_ref[...]
        y = x * lax.rsqrt(ms + EPS) * g
        sw = jnp.where(hi16, pltpu.roll(y, 16, 1), pltpu.roll(y, LANES - 16, 1))
        y = y * cosv + sw * sinv
        if j < ATT_HEADS:
            yt = (y * Q_SCALE).T
            qz_ref[0, j, 0] = jnp.where(first, yt, 0.0).astype(BF16)
            qz_ref[0, j, 1] = jnp.where(first, 0.0, yt).astype(BF16)
        else:
            k_ref[0, j - ATT_HEADS] = y.astype(BF16)


def _inproj_kernel(h_ref, mod_ref, g_ref, w_ref, wab_ref, cos_ref, sin_ref, gq_ref, gk_ref,
                   qz_ref, k_ref, v_ref, gdn_ref, z_ref, ab_ref):
    x = h_ref[0]
    ms = jnp.mean(x * x, axis=-1, keepdims=True)
    y = x * lax.rsqrt(ms + EPS) * g_ref[...]
    shift = mod_ref[0, 0, 0:1, :]
    scale = mod_ref[0, 0, 1:2, :]
    u = (y * (1.0 + scale) + shift).astype(BF16)
    _qk_norm_rope(_dot(u, w_ref[:, 0:2 * ATT_WIDTH]), cos_ref, sin_ref, gq_ref, gk_ref, qz_ref, k_ref)
    vv = _dot(u, w_ref[:, 2 * ATT_WIDTH:3 * ATT_WIDTH])
    for hh in range(ATT_HEADS):
        v_ref[0, hh] = vv[:, hh * LANES:(hh + 1) * LANES].T.astype(BF16)
    off = 3 * ATT_WIDTH
    gdn_ref[0] = _dot(u, w_ref[:, off:off + 3 * GDN_WIDTH])
    z_ref[0] = _dot(u, w_ref[:, off + 3 * GDN_WIDTH:off + 4 * GDN_WIDTH])
    ab_ref[0] = _dot(u, wab_ref[...])


def _inproj(h, modtab, g, w_main, w_ab, cos_t, sin_t, gq, gk, n_lat_blocks):
    B, S, _ = h.shape
    tm = ROW_TILE
    row = lambda b, i: (b, i, 0)
    vec = lambda b, i: (0, 0)
    return pl.pallas_call(
        _inproj_kernel,
        out_shape=(jax.ShapeDtypeStruct((B, ATT_HEADS, 2, LANES, S), BF16),
                   jax.ShapeDtypeStruct((B, ATT_HEADS, S, LANES), BF16),
                   jax.ShapeDtypeStruct((B, ATT_HEADS, LANES, S), BF16),
                   jax.ShapeDtypeStruct((B, S, 3 * GDN_WIDTH), F32),
                   jax.ShapeDtypeStruct((B, S, GDN_WIDTH), F32),
                   jax.ShapeDtypeStruct((B, S, LANES), F32)),
        grid=(B, S // tm),
        in_specs=[pl.BlockSpec((1, tm, D_MODEL), row),
                  pl.BlockSpec((1, 1, 8, D_MODEL), lambda b, i: (b, (i >= n_lat_blocks).astype(jnp.int32), 0, 0)),
                  pl.BlockSpec((1, D_MODEL), lambda b, i: (0, 0)),
                  pl.BlockSpec((D_MODEL, IN_MAIN), vec),
                  pl.BlockSpec((D_MODEL, LANES), vec),
                  pl.BlockSpec((tm, LANES), lambda b, i: (i, 0)),
                  pl.BlockSpec((tm, LANES), lambda b, i: (i, 0)),
                  pl.BlockSpec((1, LANES), vec),
                  pl.BlockSpec((1, LANES), vec)],
        out_specs=(pl.BlockSpec((1, ATT_HEADS, 2, LANES, tm), lambda b, i: (b, 0, 0, 0, i)),
                   pl.BlockSpec((1, ATT_HEADS, tm, LANES), lambda b, i: (b, 0, i, 0)),
                   pl.BlockSpec((1, ATT_HEADS, LANES, tm), lambda b, i: (b, 0, 0, i)),
                   pl.BlockSpec((1, tm, 3 * GDN_WIDTH), row),
                   pl.BlockSpec((1, tm, GDN_WIDTH), row),
                   pl.BlockSpec((1, tm, LANES), row)),
        compiler_params=_cparams(("parallel", "parallel")),
        name="inproj",
    )(h, modtab, g, w_main, w_ab, cos_t, sin_t, gq, gk)


def _attn_kernel(lam_ref, qz_ref, k_ref, v_ref, g_ref, *rest, aliased, tk, n_kv):
    if aliased:
        rest = rest[1:]
    o_ref, s_sc, mt_sc, m_sc, l_sc, acc_sc = rest
    m_sc[...] = jnp.full(m_sc.shape, -jnp.inf, F32)
    l_sc[...] = jnp.zeros(l_sc.shape, F32)
    acc_sc[...] = jnp.zeros(acc_sc.shape, F32)

    def rows(j):
        if isinstance(j, int):
            return pl.ds(j * tk, tk)
        return pl.ds(pl.multiple_of(j * tk, tk), tk)

    def qk_tile(j, slot, ps=(0, 1)):
        kt = k_ref[0, 0, rows(j), :]
        for p in ps:
            s = _dot(kt, qz_ref[0, 0, p])
            s_sc[slot, p] = s
            mt_sc[slot, p] = jnp.max(s, axis=0, keepdims=True)

    def pv_tile(j, slot, ps=(0, 1)):
        for p in ps:
            m_prev = m_sc[p]
            m_new = jnp.maximum(m_prev, mt_sc[slot, p])
            alpha = jnp.exp2(m_prev - m_new)
            lsum = None
            acc = None
            for c in range(tk // KV_CHUNK):
                cs = slice(c * KV_CHUNK, (c + 1) * KV_CHUNK)
                pe = jnp.exp2(s_sc[slot, p, cs, :] - m_new)
                ps = jnp.sum(pe, axis=0, keepdims=True)
                pv = _dot(v_ref[0, 0, j, :, cs], pe.astype(BF16))
                lsum = ps if lsum is None else lsum + ps
                acc = pv if acc is None else acc + pv
            l_sc[p] = alpha * l_sc[p] + lsum
            acc_sc[p] = alpha * acc_sc[p] + acc
            m_sc[p] = m_new

    def fused_tile(jq, slot_q, jp, slot_p):
        m_new = [jnp.maximum(m_sc[p], mt_sc[slot_p, p]) for p in range(2)]
        alpha = [jnp.exp2(m_sc[p] - m_new[p]) for p in range(2)]
        mx, lsum, acc = [None, None], [None, None], [None, None]
        for c in range(tk // KV_CHUNK):
            cs = slice(c * KV_CHUNK, (c + 1) * KV_CHUNK)
            if isinstance(jq, int):
                kr = pl.ds(jq * tk + c * KV_CHUNK, KV_CHUNK)
            else:
                kr = pl.ds(pl.multiple_of(jq * tk + c * KV_CHUNK, KV_CHUNK), KV_CHUNK)
            kc = k_ref[0, 0, kr, :]
            vc = v_ref[0, 0, jp, :, cs]
            for p in range(2):
                s = _dot(kc, qz_ref[0, 0, p])
                s_sc[slot_q, p, cs, :] = s
                cm = jnp.max(s, axis=0, keepdims=True)
                mx[p] = cm if mx[p] is None else jnp.maximum(mx[p], cm)
                pe = jnp.exp2(s_sc[slot_p, p, cs, :] - m_new[p])
                ps = jnp.sum(pe, axis=0, keepdims=True)
                pv = _dot(vc, pe.astype(BF16))
                lsum[p] = ps if lsum[p] is None else lsum[p] + ps
                acc[p] = pv if acc[p] is None else acc[p] + pv
        for p in range(2):
            mt_sc[slot_q, p] = mx[p]
            l_sc[p] = alpha[p] * l_sc[p] + lsum[p]
            acc_sc[p] = alpha[p] * acc_sc[p] + acc[p]
            m_sc[p] = m_new[p]

    qk_tile(0, 0)
    n_pairs = (n_kv - 1) // 2

    def pair(jj, carry):
        j = 2 * jj
        fused_tile(j + 1, 1, j, 0)
        fused_tile(j + 2, 0, j + 1, 1)
        return carry

    if n_pairs > 0:
        lax.fori_loop(0, n_pairs, pair, 0)
    j = 2 * n_pairs
    if j == n_kv - 1:
        pv_tile(j, 0)
    else:
        qk_tile(j + 1, 1)
        pv_tile(j, 0)
        pv_tile(j + 1, 1)

    o = acc_sc[0] / l_sc[0] - lam_ref[0] * (acc_sc[1] / l_sc[1])
    ms = jnp.mean(o * o, axis=0, keepdims=True)
    o_ref[0] = ((o * lax.rsqrt(ms + EPS)).T * g_ref[...]).astype(BF16)


def _attention(lam, qz, k, vt, g, *, tq, tk, q_blk0, n_q, kv_blk0, n_kv, prev_out=None, name="attn_ctx"):
    B, H, _, _, S = qz.shape
    aliased = prev_out is not None
    skv = tk * n_kv
    in_specs = [pl.BlockSpec(memory_space=pltpu.SMEM),
                pl.BlockSpec((1, 1, 2, LANES, tq), lambda b, h, i: (b, h, 0, 0, q_blk0 + i)),
                pl.BlockSpec((1, 1, skv, LANES), lambda b, h, i: (b, h, kv_blk0, 0)),
                pl.BlockSpec((1, 1, n_kv, LANES, tk), lambda b, h, i: (b, h, 0, 0, 0)),
                pl.BlockSpec((1, LANES), lambda b, h, i: (0, 0))]
    args = [lam, qz, k, vt, g]
    aliases = {}
    if aliased:
        in_specs.append(pl.BlockSpec(memory_space=pl.ANY))
        args.append(prev_out)
        aliases = {5: 0}
    return pl.pallas_call(
        functools.partial(_attn_kernel, aliased=aliased, tk=tk, n_kv=n_kv),
        out_shape=jax.ShapeDtypeStruct((B, S, ATT_WIDTH), BF16),
        grid=(B, H, n_q),
        in_specs=in_specs,
        out_specs=pl.BlockSpec((1, tq, LANES), lambda b, h, i: (b, q_blk0 + i, h)),
        scratch_shapes=[pltpu.VMEM((2, 2, tk, tq), F32), pltpu.VMEM((2, 2, 1, tq), F32),
                        pltpu.VMEM((2, 1, tq), F32), pltpu.VMEM((2, 1, tq), F32),
                        pltpu.VMEM((2, LANES, tq), F32)],
        input_output_aliases=aliases,
        compiler_params=_cparams(("parallel", "parallel", "arbitrary")),
        name=name,
    )(*args)


def _attention_lat(lam, qz, k, vt, g, L):
    B, H, _, S = vt.shape
    tq = _pick_tile(L, (512, 256))
    tk = _pick_tile(S, (1280, 640, 256))
    n_kv = S // tk
    vt_tiles = vt.reshape(B, H, LANES, n_kv, tk).transpose(0, 1, 3, 2, 4)
    out0 = jnp.zeros((B, S, ATT_WIDTH), BF16)
    return _attention(lam, qz, k, vt_tiles, g, tq=tq, tk=tk, q_blk0=0, n_q=L // tq, kv_blk0=0, n_kv=n_kv,
                      prev_out=out0, name="attn_lat")


def _attention_ctx(lam, qz, k, vt, g, L, prev_out):
    B, H, _, S = vt.shape
    Lc = S - L
    vt_ctx = vt[:, :, :, L:].reshape(B, H, 1, LANES, Lc)
    return _attention(lam, qz, k, vt_ctx, g, tq=Lc, tk=Lc, q_blk0=L // Lc, n_q=1, kv_blk0=L // Lc, n_kv=1,
                      prev_out=prev_out)


def _gdnprep_kernel(x_ref, prev_ref, next_ref, cw_ref, ab_ref, alog_ref, dtb_ref,
                    q_ref, k_ref, v_ref, gate_ref, ext_sc, *, nb_lat, nb_all):
    i = pl.program_id(1)
    tm = x_ref.shape[1]
    first = jnp.logical_or(i == 0, i == nb_lat)
    last = jnp.logical_or(i == nb_lat - 1, i == nb_all - 1)
    keep_prev = jnp.where(first, 0.0, 1.0)
    keep_next = jnp.where(last, 0.0, 1.0)
    ext_sc[0:8, :] = prev_ref[0] * keep_prev
    ext_sc[8:8 + tm, :] = x_ref[0]
    ext_sc[8 + tm:16 + tm, :] = next_ref[0] * keep_next
    acc = None
    for j in range(CONV_W):
        term = ext_sc[pl.ds(8 + j - CONV_W // 2, tm), :] * cw_ref[j:j + 1, :]
        acc = term if acc is None else acc + term
    y = acc * _sigmoid(acc)
    for part, ref in ((0, q_ref), (1, k_ref)):
        for hh in range(GDN_HEADS):
            lo = part * GDN_WIDTH + hh * GDN_HEAD_DIM
            t = y[:, lo:lo + GDN_HEAD_DIM]
            ref[0, :, hh * GDN_HEAD_DIM:(hh + 1) * GDN_HEAD_DIM] = (
                t * lax.rsqrt(jnp.sum(t * t, axis=-1, keepdims=True) + EPS))
    v_ref[0] = y[:, 2 * GDN_WIDTH:3 * GDN_WIDTH]
    ab = ab_ref[0]
    xs = ab + dtb_ref[...]
    sp = jnp.maximum(xs, 0.0) + jnp.log(1.0 + jnp.exp(-jnp.abs(xs)))
    g = -jnp.exp(alog_ref[...]) * sp
    lane = lax.broadcasted_iota(jnp.int32, ab.shape, 1)
    gate_ref[0] = jnp.where(lane < 2 * GDN_HEADS, g, _sigmoid(ab))


def _gdnprep(p_gdn, conv_w8, ab, alog_row, dtb_row, n_lat_blocks):
    B, S, W = p_gdn.shape
    tm = ROW_TILE
    nb = S // tm
    r8 = tm // 8
    row = lambda b, i: (b, i, 0)
    return pl.pallas_call(
        functools.partial(_gdnprep_kernel, nb_lat=n_lat_blocks, nb_all=nb),
        out_shape=(jax.ShapeDtypeStruct((B, S, GDN_WIDTH), F32),
                   jax.ShapeDtypeStruct((B, S, GDN_WIDTH), F32),
                   jax.ShapeDtypeStruct((B, S, GDN_WIDTH), F32),
                   jax.ShapeDtypeStruct((B, S, LANES), F32)),
        grid=(B, nb),
        in_specs=[pl.BlockSpec((1, tm, W), row),
                  pl.BlockSpec((1, 8, W), lambda b, i: (b, jnp.maximum(i * r8 - 1, 0), 0)),
                  pl.BlockSpec((1, 8, W), lambda b, i: (b, jnp.minimum((i + 1) * r8, nb * r8 - 1), 0)),
                  pl.BlockSpec((8, W), lambda b, i: (0, 0)),
                  pl.BlockSpec((1, tm, LANES), row),
                  pl.BlockSpec((1, LANES), lambda b, i: (0, 0)),
                  pl.BlockSpec((1, LANES), lambda b, i: (0, 0))],
        out_specs=(pl.BlockSpec((1, tm, GDN_WIDTH), row),
                   pl.BlockSpec((1, tm, GDN_WIDTH), row),
                   pl.BlockSpec((1, tm, GDN_WIDTH), row),
                   pl.BlockSpec((1, tm, LANES), row)),
        scratch_shapes=[pltpu.VMEM((tm + 16, W), F32)],
        compiler_params=_cparams(("parallel", "parallel")),
        name="gdnprep",
    )(p_gdn, p_gdn, p_gdn, conv_w8, ab, alog_row, dtb_row)


def _gdnchunk_kernel(q_ref, k_ref, v_ref, gate_ref, qg_ref, w_ref, kd_ref, u_ref, aq_ref, eg_ref):
    tm = q_ref.shape[1]
    nc = tm // GDN_CHUNK
    gates = gate_ref[0]
    ri = lax.broadcasted_iota(jnp.int32, (tm, tm), 0)
    ci = lax.broadcasted_iota(jnp.int32, (tm, tm), 1)
    same = (ri >> 6) == (ci >> 6)
    eye = ri == ci
    eye_f = jnp.where(eye, 1.0, 0.0).astype(F32)

    g1 = gates.astype(BF16)
    r1 = gates - g1.astype(F32)
    g2 = r1.astype(BF16)
    g3 = (r1 - g2.astype(F32)).astype(BF16)
    gparts = jnp.concatenate([g1, g2, g3], axis=1)

    def seg_sum(mask):
        r = _dot(jnp.where(mask, 1.0, 0.0).astype(BF16), gparts)
        return (r[:, :LANES] + r[:, LANES:2 * LANES]) + r[:, 2 * LANES:]

    tot = seg_sum(same)

    heads = []
    for hh in range(GDN_HEADS):
        sl = slice(hh * GDN_HEAD_DIM, (hh + 1) * GDN_HEAD_DIM)
        k = k_ref[0, :, sl]
        k16 = k.astype(BF16)
        q = q_ref[0, :, sl] * (GDN_HEAD_DIM ** -0.5)
        heads.append((sl, k, q, _dot_nt(k16, k16), _dot_nt(q.astype(BF16), k16)))

    chains = []
    for d in range(2):
        incl = jnp.logical_and(same, (ci <= ri) if d == 0 else (ci >= ri))
        strict = jnp.logical_and(incl, jnp.logical_not(eye))
        gc = seg_sum(incl)
        gc_t = gc.T
        for hh in range(GDN_HEADS):
            ln = d * GDN_HEADS + hh
            gcol = gc[:, ln:ln + 1]
            bcol = gates[:, 2 * GDN_HEADS + ln:2 * GDN_HEADS + ln + 1]
            tcol = tot[:, ln:ln + 1]
            diff = gcol - gc_t[ln:ln + 1, :]
            decay = jnp.where(incl, jnp.exp(jnp.where(incl, diff, 0.0)), 0.0)
            a = jnp.where(strict, heads[hh][3] * bcol * decay, 0.0)
            aqk = heads[hh][4] * decay
            chains.append(dict(d=d, hh=hh, x=-a, t=eye_f - a, aqk=aqk, gcol=gcol, bcol=bcol, tcol=tcol))

    for _ in range(5):
        for ch in chains:
            x16 = ch["x"].astype(BF16)
            ch["x"] = _dot(x16, x16)
        for ch in chains:
            ch["t"] = ch["t"] + _dot(ch["t"].astype(BF16), ch["x"].astype(BF16))

    for ch in chains:
        d, hh, gcol, bcol, tcol = ch["d"], ch["hh"], ch["gcol"], ch["bcol"], ch["tcol"]
        sl, k, q = heads[hh][0], heads[hh][1], heads[hh][2]
        t16 = ch["t"].astype(BF16)
        egc = jnp.exp(gcol)
        kb = k * bcol
        rhs = jnp.concatenate([(v_ref[0, :, sl] * bcol).astype(BF16), (kb * egc).astype(BF16)], axis=1)
        uw = _dot(t16, rhs)
        u_ref[0, d, :, sl] = uw[:, :GDN_HEAD_DIM]
        w_ref[0, d, :, sl] = uw[:, GDN_HEAD_DIM:].astype(BF16)
        kd_ref[0, d, :, sl] = (k * jnp.exp(tcol - gcol)).astype(BF16)
        qg_ref[0, d, :, sl] = (q * egc).astype(BF16)
        for cc in range(nc):
            rs = slice(cc * GDN_CHUNK, (cc + 1) * GDN_CHUNK)
            aq_ref[0, d, rs, hh * GDN_CHUNK:(hh + 1) * GDN_CHUNK] = ch["aqk"][rs, rs].astype(BF16)
            eg_ref[0, d, cc, hh:hh + 1, :] = jnp.broadcast_to(
                jnp.exp(tcol[cc * GDN_CHUNK:cc * GDN_CHUNK + 1, :]), (1, LANES))


def _gdnchunk(gq, gk, gv, gates):
    B, S, W = gq.shape
    tm = ROW_TILE
    nc = tm // GDN_CHUNK
    row = lambda b, i: (b, i, 0)
    drow = lambda b, i: (b, 0, i, 0)
    big = lambda dt: jax.ShapeDtypeStruct((B, 2, S, W), dt)
    return pl.pallas_call(
        _gdnchunk_kernel,
        out_shape=(big(BF16), big(BF16), big(BF16), big(F32),
                   jax.ShapeDtypeStruct((B, 2, S, GDN_HEADS * GDN_CHUNK), BF16),
                   jax.ShapeDtypeStruct((B, 2, S // GDN_CHUNK, GDN_HEADS, LANES), F32)),
        grid=(B, S // tm),
        in_specs=[pl.BlockSpec((1, tm, W), row), pl.BlockSpec((1, tm, W), row),
                  pl.BlockSpec((1, tm, W), row), pl.BlockSpec((1, tm, LANES), row)],
        out_specs=(pl.BlockSpec((1, 2, tm, W), drow), pl.BlockSpec((1, 2, tm, W), drow),
                   pl.BlockSpec((1, 2, tm, W), drow), pl.BlockSpec((1, 2, tm, W), drow),
                   pl.BlockSpec((1, 2, tm, GDN_HEADS * GDN_CHUNK), drow),
                   pl.BlockSpec((1, 2, nc, GDN_HEADS, LANES), lambda b, i: (b, 0, i, 0, 0))),
        compiler_params=_cparams(("parallel", "parallel")),
        name="gdnchunk",
    )(gq, gk, gv, gates)


def _gdnscan_kernel(qg0, w0, kd0, u0, aq0, eg0, qg1, w1, kd1, u1, aq1, eg1, of_ref, ob_ref, s_sc):
    @pl.when(pl.program_id(1) == 0)
    def _():
        s_sc[...] = jnp.zeros(s_sc.shape, F32)

    dirs = ((qg0, w0, kd0, u0, aq0, eg0, of_ref), (qg1, w1, kd1, u1, aq1, eg1, ob_ref))
    chains = [(d, hh) for d in range(2) for hh in range(GDN_HEADS)]
    st = {ch: s_sc[ch[0], ch[1]] for ch in chains}
    for step in range(SCAN_GROUP):
        st16, vn16, qs = {}, {}, {}
        for d, hh in chains:
            st16[d, hh] = st[d, hh].astype(BF16)
        for d, hh in chains:
            qg, w, kd, u, aq, eg, o_ref = dirs[d]
            cc = step if d == 0 else SCAN_GROUP - 1 - step
            rs = slice(cc * GDN_CHUNK, (cc + 1) * GDN_CHUNK)
            sl = slice(hh * GDN_HEAD_DIM, (hh + 1) * GDN_HEAD_DIM)
            vn16[d, hh] = (u[0, 0, rs, sl] - _dot(w[0, 0, rs, sl], st16[d, hh])).astype(BF16)
            qs[d, hh] = _dot(qg[0, 0, rs, sl], st16[d, hh])
        for d, hh in chains:
            qg, w, kd, u, aq, eg, o_ref = dirs[d]
            cc = step if d == 0 else SCAN_GROUP - 1 - step
            rs = slice(cc * GDN_CHUNK, (cc + 1) * GDN_CHUNK)
            sl = slice(hh * GDN_HEAD_DIM, (hh + 1) * GDN_HEAD_DIM)
            o_ref[0, rs, sl] = qs[d, hh] + _dot(aq[0, 0, rs, hh * GDN_CHUNK:(hh + 1) * GDN_CHUNK], vn16[d, hh])
            st[d, hh] = st[d, hh] * eg[0, 0, cc, hh:hh + 1, :] + _dot_tn(kd[0, 0, rs, sl], vn16[d, hh])
    for d, hh in chains:
        s_sc[d, hh] = st[d, hh]


def _gdnscan(qg, w, kd, u, aq, eg, n_lat_chunks, n_ctx_chunks):
    B, _, S, W = qg.shape
    G = SCAN_GROUP
    R = GDN_CHUNK * G
    n = S // R
    n_lat, n_ctx = n_lat_chunks // G, n_ctx_chunks // G
    assert n_lat * G == n_lat_chunks and n_ctx * G == n_ctx_chunks

    def fwd_blk(i):
        return jnp.where(i < n_ctx, n_lat + i, i - n_ctx)

    def bwd_blk(i):
        return jnp.where(i < n_ctx, n_lat + n_ctx - 1 - i, n - 1 - i)

    def specs(d, blk_of):
        big = pl.BlockSpec((1, 1, R, W), lambda b, i: (b, d, blk_of(i), 0))
        return [big, big, big, big,
                pl.BlockSpec((1, 1, R, GDN_HEADS * GDN_CHUNK), lambda b, i: (b, d, blk_of(i), 0)),
                pl.BlockSpec((1, 1, G, GDN_HEADS, LANES), lambda b, i: (b, d, blk_of(i), 0, 0))]

    return pl.pallas_call(
        _gdnscan_kernel,
        out_shape=(jax.ShapeDtypeStruct((B, S, W), F32), jax.ShapeDtypeStruct((B, S, W), F32)),
        grid=(B, n),
        in_specs=specs(0, fwd_blk) + specs(1, bwd_blk),
        out_specs=(pl.BlockSpec((1, R, W), lambda b, i: (b, fwd_blk(i), 0)),
                   pl.BlockSpec((1, R, W), lambda b, i: (b, bwd_blk(i), 0))),
        scratch_shapes=[pltpu.VMEM((2, GDN_HEADS, GDN_HEAD_DIM, GDN_HEAD_DIM), F32)],
        compiler_params=_cparams(("parallel", "arbitrary")),
        name="gdnscan",
    )(qg, w, kd, u, aq, eg, qg, w, kd, u, aq, eg)


def _mixout_kernel(oa_ref, of_ref, ob_ref, z_ref, h_ref, mod_ref, gg_ref, wo_ref, gf_ref, rw_ref, rb_ref,
                   hn_ref, v_ref, te_ref, tg_ref):
    og = of_ref[0] + ob_ref[0]
    z = z_ref[0]
    parts = [oa_ref[0]]
    for hh in range(GDN_HEADS):
        sl = slice(hh * GDN_HEAD_DIM, (hh + 1) * GDN_HEAD_DIM)
        t = og[:, sl]
        t = t * lax.rsqrt(jnp.mean(t * t, axis=-1, keepdims=True) + EPS) * gg_ref[...]
        zz = z[:, sl]
        parts.append((t * (zz * _sigmoid(zz))).astype(BF16))
    mix_in = jnp.concatenate(parts, axis=-1)
    mix = _dot(mix_in, wo_ref[...])
    hn = h_ref[0] + mod_ref[0, 0, 2:3, :] * mix
    hn_ref[0] = hn
    y = hn * lax.rsqrt(jnp.mean(hn * hn, axis=-1, keepdims=True) + EPS) * gf_ref[...]
    v = y * (1.0 + mod_ref[0, 0, 4:5, :]) + mod_ref[0, 0, 3:4, :]
    v_ref[0] = v.astype(BF16)
    v_hi = v.astype(BF16)
    v_lo = (v - v_hi.astype(F32)).astype(BF16)
    logits = (_dot(v_hi, rw_ref[0]) + _dot(v_lo, rw_ref[0]) + _dot(v_hi, rw_ref[1])) + rb_ref[...]
    lane = lax.broadcasted_iota(jnp.int32, logits.shape, 1)
    cur = logits
    vals, idxs = [], []
    for _ in range(TOP_K):
        m = jnp.max(cur, axis=-1, keepdims=True)
        idx = jnp.min(jnp.where(cur == m, lane, LANES), axis=-1, keepdims=True)
        vals.append(m)
        idxs.append(idx)
        cur = jnp.where(lane == idx, -jnp.inf, cur)
    es = [jnp.exp(vv - vals[0]) for vv in vals]
    inv = 1.0 / (es[0] + es[1] + es[2] + es[3])
    te = jnp.zeros(logits.shape, jnp.int32)
    tg = jnp.zeros(logits.shape, F32)
    for kk in range(TOP_K):
        te = jnp.where(lane == kk, idxs[kk], te)
        tg = jnp.where(lane == kk, es[kk] * inv, tg)
    te_ref[0] = te
    tg_ref[0] = tg


def _mixout(o_att, o_f, o_b, z, h, modtab, gg, w_out, gf, rw, rb, n_lat_blocks):
    B, S, _ = h.shape
    tm = ROW_TILE
    row = lambda b, i: (b, i, 0)
    const = lambda b, i: (0, 0)
    return pl.pallas_call(
        _mixout_kernel,
        out_shape=(jax.ShapeDtypeStruct((B, S, D_MODEL), F32),
                   jax.ShapeDtypeStruct((B, S, D_MODEL), BF16),
                   jax.ShapeDtypeStruct((B, S, LANES), jnp.int32),
                   jax.ShapeDtypeStruct((B, S, LANES), F32)),
        grid=(B, S // tm),
        in_specs=[pl.BlockSpec((1, tm, ATT_WIDTH), row),
                  pl.BlockSpec((1, tm, GDN_WIDTH), row),
                  pl.BlockSpec((1, tm, GDN_WIDTH), row),
                  pl.BlockSpec((1, tm, GDN_WIDTH), row),
                  pl.BlockSpec((1, tm, D_MODEL), row),
                  pl.BlockSpec((1, 1, 8, D_MODEL), lambda b, i: (b, (i >= n_lat_blocks).astype(jnp.int32), 0, 0)),
                  pl.BlockSpec((1, LANES), const),
                  pl.BlockSpec((D_MODEL, D_MODEL), const),
                  pl.BlockSpec((1, D_MODEL), const),
                  pl.BlockSpec((2, D_MODEL, LANES), lambda b, i: (0, 0, 0)),
                  pl.BlockSpec((1, LANES), const)],
        out_specs=(pl.BlockSpec((1, tm, D_MODEL), row),
                   pl.BlockSpec((1, tm, D_MODEL), row),
                   pl.BlockSpec((1, tm, LANES), row),
                   pl.BlockSpec((1, tm, LANES), row)),
        compiler_params=_cparams(("parallel", "parallel")),
        name="mixout",
    )(o_att, o_f, o_b, z, h, modtab, gg, w_out, gf, rw, rb)


def _expert_kernel(be_ref, nv_ref, x_ref, wgu_ref, bgu_ref, wd_ref, bd_ref, y_ref, wgu_sc, wd_sc):
    i = pl.program_id(0)
    new_expert = jnp.logical_or(i == 0, be_ref[i] != be_ref[jnp.maximum(i - 1, 0)])

    @pl.when(new_expert)
    def _():
        wgu_sc[...] = wgu_ref[0, 0].astype(BF16)
        wd_sc[...] = wd_ref[0, 0].astype(BF16)

    @pl.when(nv_ref[i] > 0)
    def _():
        gu = _dot(x_ref[...], wgu_sc[...]) + bgu_ref[0, 0]
        g_ = jnp.minimum(gu[:, :D_EXPERT], SWIGLU_LIMIT)
        up = jnp.clip(gu[:, D_EXPERT:], -SWIGLU_LIMIT, SWIGLU_LIMIT)
        glu = g_ * _sigmoid(SWIGLU_ALPHA * g_)
        act = ((up + 1.0) * glu).astype(BF16)
        y_ref[...] = (_dot(act, wd_sc[...]) + bd_ref[0, 0]).astype(y_ref.dtype)

    @pl.when(nv_ref[i] == 0)
    def _():
        y_ref[...] = jnp.zeros(y_ref.shape, y_ref.dtype)


def _experts(block_e, n_valid, x_sorted, wgu, bgu, wd, bd, layer):
    n_slots = x_sorted.shape[0]
    nb = n_slots // MOE_BLOCK
    grid_spec = pltpu.PrefetchScalarGridSpec(
        num_scalar_prefetch=2,
        grid=(nb,),
        in_specs=[pl.BlockSpec((MOE_BLOCK, D_MODEL), lambda i, be, nv: (i, 0)),
                  pl.BlockSpec((1, 1, D_MODEL, 2 * D_EXPERT), lambda i, be, nv: (layer, be[i], 0, 0)),
                  pl.BlockSpec((1, 1, 1, 2 * D_EXPERT), lambda i, be, nv: (layer, be[i], 0, 0)),
                  pl.BlockSpec((1, 1, D_EXPERT, D_MODEL), lambda i, be, nv: (layer, be[i], 0, 0)),
                  pl.BlockSpec((1, 1, 1, D_MODEL), lambda i, be, nv: (layer, be[i], 0, 0))],
        out_specs=pl.BlockSpec((MOE_BLOCK, D_MODEL), lambda i, be, nv: (i, 0)),
        scratch_shapes=[pltpu.VMEM((D_MODEL, 2 * D_EXPERT), BF16), pltpu.VMEM((D_EXPERT, D_MODEL), BF16)],
    )
    return pl.pallas_call(
        _expert_kernel,
        out_shape=jax.ShapeDtypeStruct((n_slots, D_MODEL), BF16),
        grid_spec=grid_spec,
        compiler_params=pltpu.CompilerParams(dimension_semantics=("arbitrary",),
                                             vmem_limit_bytes=EXPERT_VMEM_LIMIT),
        name="experts",
    )(block_e, n_valid, x_sorted, wgu, bgu, wd, bd)


def _moe_plan(top_e, top_g):
    T = top_e.shape[0]
    n_assign = T * TOP_K
    n_blocks = -(-n_assign // MOE_BLOCK) + N_EXPERTS
    n_slots = n_blocks * MOE_BLOCK
    i32 = jnp.int32
    flat_e = top_e.reshape(-1)
    gate_flat = top_g.reshape(-1)
    assert T < NO_TOKEN
    gate_bits = lax.bitcast_convert_type(gate_flat.astype(BF16), jnp.uint16).astype(jnp.uint32)
    tok_pack = (gate_bits << 16) | (jnp.arange(n_assign, dtype=jnp.uint32) // TOP_K)
    _, sorted_pack = lax.sort((flat_e, tok_pack), num_keys=1, is_stable=True)
    e_ids = jnp.arange(N_EXPERTS, dtype=i32)
    is_e = flat_e[:, None] == e_ids[None, :]
    counts = jnp.sum(is_e, axis=0, dtype=i32)
    start = jnp.cumsum(counts) - counts
    padded = (counts + MOE_BLOCK - 1) // MOE_BLOCK * MOE_BLOCK
    pad_end = jnp.cumsum(padded)
    pad_start = pad_end - padded
    blk0 = jnp.arange(n_blocks, dtype=i32) * MOE_BLOCK
    block_e = jnp.minimum(jnp.sum(pad_end[None, :] <= blk0[:, None], axis=1, dtype=i32), N_EXPERTS - 1)
    off = (blk0 - pad_start[block_e])[:, None] + jnp.arange(MOE_BLOCK, dtype=i32)[None, :]
    valid = off < counts[block_e][:, None]
    valid_flat = valid.reshape(-1)
    win0 = jnp.clip(blk0 - pad_start[block_e] + start[block_e], 0, n_assign)
    sorted_pad = jnp.concatenate([sorted_pack, jnp.zeros((MOE_BLOCK,), jnp.uint32)])
    windows = jax.vmap(lambda s0: lax.dynamic_slice(sorted_pad, (s0,), (MOE_BLOCK,)))(win0)
    slot_pack = jnp.where(valid_flat, windows.reshape(-1), jnp.uint32(NO_TOKEN))
    slot_tok = jnp.where(valid_flat, (slot_pack & NO_TOKEN).astype(i32), jnp.arange(n_slots, dtype=i32) % T)
    n_valid = jnp.sum(valid, axis=1, dtype=i32)

    n_tiles = T // COMBINE_TILE
    cnt = jnp.sum(top_e.reshape(n_tiles, COMBINE_TILE * TOP_K)[:, :, None] == e_ids[None, None, :], axis=1, dtype=i32)
    run_start = pad_start[None, :] + jnp.cumsum(cnt, axis=0) - cnt
    q_first = run_start // COMBINE_CHUNK
    n_ch = jnp.where(cnt > 0, (run_start + cnt - 1) // COMBINE_CHUNK - q_first + 1, 0)
    ch_end = jnp.cumsum(n_ch, axis=1)
    ch_off = ch_end - n_ch
    j = jnp.arange(COMBINE_MAX_CHUNKS, dtype=i32)
    e_j = jnp.minimum(jnp.sum(ch_end[:, None, :] <= j[None, :, None], axis=2, dtype=i32), N_EXPERTS - 1)
    used = j[None, :] < ch_end[:, -1:]
    base = jnp.sum(jnp.where(e_j[:, :, None] == e_ids[None, None, :], (q_first - ch_off)[:, None, :], 0),
                   axis=2, dtype=i32)
    chunk_id = jnp.where(used, base + j[None, :], 0)
    slot_pack = slot_pack.reshape(n_slots // COMBINE_CHUNK, COMBINE_CHUNK)
    row_pack = jnp.where(used[:, :, None], slot_pack[chunk_id], jnp.uint32(NO_TOKEN))
    row_pack = lax.bitcast_convert_type(row_pack, i32).reshape(n_tiles, 1, COMBINE_MAX_CHUNKS * COMBINE_CHUNK)
    return slot_tok, block_e, n_valid, chunk_id, row_pack


def _combine_kernel(cid_ref, tok_ref, h_ref, mod_ref, y_hbm, o_ref, ybuf, sem, *, lat_only, tiles_per_batch,
                    n_lat_tiles):
    i = pl.program_id(0)
    n = pl.num_programs(0)
    slot = lax.rem(i, 2)

    def chunk_copy(c, s, jj):
        return pltpu.make_async_copy(
            y_hbm.at[pl.ds(pl.multiple_of(c * COMBINE_CHUNK, COMBINE_CHUNK), COMBINE_CHUNK)],
            ybuf.at[s, pl.ds(jj * COMBINE_CHUNK, COMBINE_CHUNK)], sem.at[s])

    def start_tile(t, s):
        for jj in range(COMBINE_MAX_CHUNKS):
            chunk_copy(cid_ref[t, jj], s, jj).start()

    @pl.when(i == 0)
    def _():
        start_tile(0, 0)

    @pl.when(i + 1 < n)
    def _():
        start_tile(i + 1, 1 - slot)

    for jj in range(COMBINE_MAX_CHUNKS):
        chunk_copy(0, slot, jj).wait()

    packed = tok_ref[0]
    tok = packed & NO_TOKEN
    gate = lax.bitcast_convert_type(packed & ~NO_TOKEN, F32)
    t_ids = i * COMBINE_TILE + lax.broadcasted_iota(jnp.int32, (COMBINE_TILE, tok.shape[1]), 0)
    sel = jnp.where(tok == t_ids, gate, 0.0).astype(BF16)
    y = _dot(sel, ybuf[slot])
    out = h_ref[...] + mod_ref[0, 0, 5:6, :] * y
    if lat_only:
        @pl.when(lax.rem(i, tiles_per_batch) < n_lat_tiles)
        def _():
            o_ref[...] = out
    else:
        o_ref[...] = out


def _combine(chunk_id, row_tok, y_sorted, h, modtab, n_lat_blocks, lat_only):
    B, S, _ = h.shape
    T = B * S
    ct = COMBINE_TILE
    tiles_per_batch = S // ct
    n_lat_tiles = n_lat_blocks * ROW_TILE // ct
    n_rows = COMBINE_MAX_CHUNKS * COMBINE_CHUNK
    if lat_only:
        out_rows = B * n_lat_tiles * ct
        out_map = lambda i, cid: ((i // tiles_per_batch) * n_lat_tiles
                                  + jnp.minimum(lax.rem(i, tiles_per_batch), n_lat_tiles - 1), 0)
    else:
        out_rows = T
        out_map = lambda i, cid: (i, 0)
    grid_spec = pltpu.PrefetchScalarGridSpec(
        num_scalar_prefetch=1,
        grid=(T // ct,),
        in_specs=[pl.BlockSpec((1, 1, n_rows), lambda i, cid: (i, 0, 0)),
                  pl.BlockSpec((ct, D_MODEL), lambda i, cid: (i, 0)),
                  pl.BlockSpec((1, 1, 8, D_MODEL),
                               lambda i, cid: (i // tiles_per_batch,
                                               (lax.rem(i, tiles_per_batch) >= n_lat_tiles).astype(jnp.int32), 0, 0)),
                  pl.BlockSpec(memory_space=pl.ANY)],
        out_specs=pl.BlockSpec((ct, D_MODEL), out_map),
        scratch_shapes=[pltpu.VMEM((2, n_rows, D_MODEL), BF16), pltpu.SemaphoreType.DMA((2,))],
    )
    out = pl.pallas_call(
        functools.partial(_combine_kernel, lat_only=lat_only, tiles_per_batch=tiles_per_batch,
                          n_lat_tiles=n_lat_tiles),
        out_shape=jax.ShapeDtypeStruct((out_rows, D_MODEL), F32),
        grid_spec=grid_spec,
        compiler_params=_cparams(("arbitrary",)),
        name="combine",
    )(chunk_id, row_tok, h.reshape(T, D_MODEL), modtab, y_sorted)
    return out.reshape(B, out_rows // B, D_MODEL)


def _moe(v_ffn, top_e, top_g, h_new, modtab, wgu, bgu, wd, bd, layer, n_lat_blocks, lat_only):
    B, S, _ = v_ffn.shape
    T = B * S
    slot_tok, block_e, n_valid, chunk_id, row_tok = _moe_plan(
        top_e.reshape(T, LANES)[:, :TOP_K], top_g.reshape(T, LANES)[:, :TOP_K])
    x_sorted = v_ffn.reshape(T, D_MODEL)[slot_tok]
    y_sorted = _experts(block_e, n_valid, x_sorted, wgu, bgu, wd, bd, layer)
    return _combine(chunk_id, row_tok, y_sorted, h_new, modtab, n_lat_blocks, lat_only)


def _pick_tile(n, cands):
    for t in cands:
        if n % t == 0:
            return t
    raise ValueError(f"no tile for {n}")


def _rope_tables(L, Lc):
    rows = L // GRID_W
    row = jnp.repeat(jnp.arange(rows, dtype=F32), GRID_W)
    col = (jnp.arange(L, dtype=jnp.int32) % GRID_W).astype(F32)
    inv_freq = ROPE_BASE ** (-jnp.arange(ROPE_PAIRS, dtype=F32) / ROPE_PAIRS)
    ar = row[:, None] * inv_freq
    ac = col[:, None] * inv_freq
    cos64 = jnp.concatenate([jnp.cos(ar), jnp.cos(ar), jnp.cos(ac), jnp.cos(ac)], axis=-1)
    sin64 = jnp.concatenate([-jnp.sin(ar), jnp.sin(ar), -jnp.sin(ac), jnp.sin(ac)], axis=-1)
    cos_t = jnp.concatenate([jnp.tile(cos64, (1, 2)), jnp.ones((Lc, LANES), F32)], axis=0)
    sin_t = jnp.concatenate([jnp.tile(sin64, (1, 2)), jnp.zeros((Lc, LANES), F32)], axis=0)
    return cos_t, sin_t


def _pad_lanes(v):
    v = v.reshape(1, -1).astype(F32)
    return jnp.pad(v, ((0, 0), (0, LANES - v.shape[1])))


def kernel(x, c, ctx, c_ctx, w_mod, b_mod, norm_mix_g, w_in, q_norm_g, k_norm_g, lam_q1, lam_k1, lam_q2, lam_k2, subln_g, conv_w, a_log, dt_bias, gdn_norm_g, w_out, norm_ffn_g, router_w, router_b, w_gate_up, b_gate_up, w_down, b_down):
    B, L, D = x.shape
    Lc = ctx.shape[1]
    S = L + Lc
    depth = w_mod.shape[0]
    tm = ROW_TILE
    n_lat_blocks = L // tm
    cos_t, sin_t = _rope_tables(L, Lc)

    c_rows = jnp.zeros((8, D), F32).at[:B].set(c).at[B].set(c_ctx)
    h = jnp.concatenate([x, ctx], axis=1)

    for layer in range(depth):
        mod = _adaln(c_rows, w_mod, b_mod, layer)
        mod6 = mod.reshape(8, 6, D)
        lat_mod = mod6[:B]
        ctx_mod = jnp.broadcast_to(mod6[B][None], (B, 6, D))
        modtab = jnp.pad(jnp.stack([lat_mod, ctx_mod], axis=1), ((0, 0), (0, 0), (0, 2), (0, 0)))

        lam_init = 0.8 - 0.6 * math.exp(-0.3 * layer)
        lam_full = (jnp.exp(jnp.sum(lam_q1[layer] * lam_k1[layer]))
                    - jnp.exp(jnp.sum(lam_q2[layer] * lam_k2[layer])) + lam_init).reshape(1).astype(F32)

        w_l = w_in[layer]
        w_main = w_l[:, :IN_MAIN].astype(BF16)
        w_ab = jnp.pad(w_l[:, IN_MAIN:], ((0, 0), (0, LANES - (w_l.shape[1] - IN_MAIN)))).astype(BF16)
        gq = jnp.tile(q_norm_g[layer].reshape(1, ATT_HEAD_DIM), (1, 2))
        gk = jnp.tile(k_norm_g[layer].reshape(1, ATT_HEAD_DIM), (1, 2))
        qz, k_att, v_att, p_gdn, z, ab = _inproj(h, modtab, norm_mix_g[layer].reshape(1, D), w_main, w_ab,
                                                 cos_t, sin_t, gq, gk, n_lat_blocks)
        g_sub = (subln_g[layer] * (1.0 - lam_init)).reshape(1, LANES).astype(F32)
        o_att = _attention_lat(lam_full, qz, k_att, v_att, g_sub, L)
        o_att = _attention_ctx(lam_full, qz, k_att, v_att, g_sub, L, o_att)

        conv_w8 = jnp.pad(conv_w[layer], ((0, 8 - CONV_W), (0, 0)))
        gq_g, gk_g, gv_g, gates = _gdnprep(p_gdn, conv_w8, ab, _pad_lanes(a_log[layer]),
                                           _pad_lanes(dt_bias[layer]), n_lat_blocks)
        qg, w_g, kd, u_g, aq, eg = _gdnchunk(gq_g, gk_g, gv_g, gates)
        o_f, o_b = _gdnscan(qg, w_g, kd, u_g, aq, eg, L // GDN_CHUNK, Lc // GDN_CHUNK)

        rw = jnp.pad(router_w[layer].astype(F32), ((0, 0), (0, LANES - N_EXPERTS)))
        rw_hi = rw.astype(BF16)
        rw = jnp.stack([rw_hi, (rw - rw_hi.astype(F32)).astype(BF16)])
        rb = jnp.pad(router_b[layer].reshape(1, N_EXPERTS).astype(F32), ((0, 0), (0, LANES - N_EXPERTS)),
                     constant_values=-1e30)
        h_new, v_ffn, top_e, top_g = _mixout(
            o_att, o_f, o_b, z, h, modtab, gdn_norm_g[layer].reshape(1, LANES), w_out[layer].astype(BF16),
            norm_ffn_g[layer].reshape(1, D), rw, rb, n_lat_blocks)

        h = _moe(v_ffn, top_e, top_g, h_new, modtab,
                 w_gate_up, b_gate_up.reshape(depth, N_EXPERTS, 1, 2 * D_EXPERT),
                 w_down, b_down.reshape(depth, N_EXPERTS, 1, D), layer, n_lat_blocks,
                 lat_only=layer == depth - 1)
    return h
```

```python
import functools
import math

import jax
import jax.numpy as jnp
from jax import lax
from jax.experimental import pallas as pl
from jax.experimental.pallas import tpu as pltpu

F32 = jnp.float32
BF16 = jnp.bfloat16
HIGHEST = lax.Precision.HIGHEST

D_MODEL = 1024
GRID_W = 64
EPS = 1e-6
ATT_WIDTH = 512
ATT_HEAD_DIM = 64
ATT_HEADS = 4
ROPE_BASE = 10000.0
ROPE_PAIRS = ATT_HEAD_DIM // 4
GDN_WIDTH = 512
GDN_HEAD_DIM = 128
GDN_HEADS = 4
GDN_CHUNK = 64
CONV_W = 5
IN_MAIN = 3 * ATT_WIDTH + 4 * GDN_WIDTH
N_EXPERTS = 32
TOP_K = 4
D_EXPERT = 1024
SWIGLU_ALPHA = 1.702
SWIGLU_LIMIT = 7.0
MOE_BLOCK = 512
SCAN_GROUP = 4
NO_TOKEN = 0xFFFF
COMBINE_TILE = 256
COMBINE_CHUNK = 32
COMBINE_MAX_CHUNKS = COMBINE_TILE * TOP_K // COMBINE_CHUNK + 2 * N_EXPERTS

LANES = 128
ROW_TILE = 256
KV_CHUNK = 256
Q_SCALE = ATT_HEAD_DIM ** -0.5 * math.log2(math.e)
VMEM_LIMIT = 48 * 1024 * 1024
EXPERT_VMEM_LIMIT = 56 * 1024 * 1024


def _cparams(sem):
    return pltpu.CompilerParams(dimension_semantics=sem, vmem_limit_bytes=VMEM_LIMIT)


def _dot(a, b):
    return jnp.dot(a, b, preferred_element_type=F32)


def _dot_nt(a, b):
    return lax.dot_general(a, b, (((1,), (1,)), ((), ())), preferred_element_type=F32)


def _dot_tn(a, b):
    return lax.dot_general(a, b, (((0,), (0,)), ((), ())), preferred_element_type=F32)


def _dot_hi(a, b):
    return jnp.dot(a, b, preferred_element_type=F32, precision=HIGHEST)


def _sigmoid(x):
    return 1.0 / (1.0 + jnp.exp(-x))


def _adaln_kernel(c_ref, w_ref, b_ref, o_ref):
    c = c_ref[...]
    s = c * _sigmoid(c)
    o_ref[...] = _dot_hi(s, w_ref[0]) + b_ref[0]


def _adaln(c_rows, w, b, layer):
    depth, _, n = w.shape
    tn = 1024
    return pl.pallas_call(
        _adaln_kernel,
        out_shape=jax.ShapeDtypeStruct((8, n), F32),
        grid=(n // tn,),
        in_specs=[pl.BlockSpec((8, D_MODEL), lambda j: (0, 0)),
                  pl.BlockSpec((1, D_MODEL, tn), lambda j: (layer, 0, j)),
                  pl.BlockSpec((1, 1, tn), lambda j: (layer, 0, j))],
        out_specs=pl.BlockSpec((8, tn), lambda j: (0, j)),
        compiler_params=_cparams(("arbitrary",)),
        name="adaln",
    )(c_rows, w, b.reshape(depth, 1, n))


def _qk_norm_rope(p_qk, cos_ref, sin_ref, gq_ref, gk_ref, qz_ref, k_ref):
    tm = p_qk.shape[0]
    lane = lax.broadcasted_iota(jnp.int32, (tm, LANES), 1)
    hi16 = (lane & 16) != 0
    first = lax.broadcasted_iota(jnp.int32, (LANES, tm), 0) < ATT_HEAD_DIM
    r = lax.broadcasted_iota(jnp.int32, (LANES, LANES), 0) >> 6
    c = lax.broadcasted_iota(jnp.int32, (LANES, LANES), 1) >> 6
    gmat = jnp.where(r == c, 1.0 / ATT_HEAD_DIM, 0.0).astype(BF16)
    cosv = cos_ref[...]
    sinv = sin_ref[...]
    sq = jnp.concatenate([jnp.square(p_qk[:, j * LANES:(j + 1) * LANES]) for j in range(2 * ATT_HEADS)], axis=0)
    sq_hi = sq.astype(BF16)
    sq_lo = (sq - sq_hi.astype(F32)).astype(BF16)
    ms_all = _dot(sq_hi, gmat) + _dot(sq_lo, gmat)
    for j in range(2 * ATT_HEADS):
        x = p_qk[:, j * LANES:(j + 1) * LANES]
        ms = ms_all[j * tm:(j + 1) * tm, :]
        g = gq_ref[...] if j < ATT_HEADS else gk_ref[...]
        y = x * lax.rsqrt(ms + EPS) * g
        sw = jnp.where(hi16, pltpu.roll(y, 16, 1), pltpu.roll(y, LANES - 16, 1))
        y = y * cosv + sw * sinv
        if j < ATT_HEADS:
            yt = (y * Q_SCALE).T
            qz_ref[0, j, 0] = jnp.where(first, yt, 0.0).astype(BF16)
            qz_ref[0, j, 1] = jnp.where(first, 0.0, yt).astype(BF16)
        else:
            k_ref[0, j - ATT_HEADS] = y.astype(BF16)


def _inproj_kernel(h_ref, mod_ref, g_ref, w_ref, wab_ref, cos_ref, sin_ref, gq_ref, gk_ref,
                   qz_ref, k_ref, v_ref, gdn_ref, z_ref, ab_ref):
    x = h_ref[0]
    ms = jnp.mean(x * x, axis=-1, keepdims=True)
    y = x * lax.rsqrt(ms + EPS) * g_ref[...]
    shift = mod_ref[0, 0, 0:1, :]
    scale = mod_ref[0, 0, 1:2, :]
    u = (y * (1.0 + scale) + shift).astype(BF16)
    _qk_norm_rope(_dot(u, w_ref[:, 0:2 * ATT_WIDTH]), cos_ref, sin_ref, gq_ref, gk_ref, qz_ref, k_ref)
    vv = _dot(u, w_ref[:, 2 * ATT_WIDTH:3 * ATT_WIDTH])
    for hh in range(ATT_HEADS):
        v_ref[0, hh] = vv[:, hh * LANES:(hh + 1) * LANES].T.astype(BF16)
    off = 3 * ATT_WIDTH
    gdn_ref[0] = _dot(u, w_ref[:, off:off + 3 * GDN_WIDTH])
    z_ref[0] = _dot(u, w_ref[:, off + 3 * GDN_WIDTH:off + 4 * GDN_WIDTH])
    ab_ref[0] = _dot(u, wab_ref[...])


def _inproj(h, modtab, g, w_main, w_ab, cos_t, sin_t, gq, gk, n_lat_blocks):
    B, S, _ = h.shape
    tm = ROW_TILE
    row = lambda b, i: (b, i, 0)
    vec = lambda b, i: (0, 0)
    return pl.pallas_call(
        _inproj_kernel,
        out_shape=(jax.ShapeDtypeStruct((B, ATT_HEADS, 2, LANES, S), BF16),
                   jax.ShapeDtypeStruct((B, ATT_HEADS, S, LANES), BF16),
                   jax.ShapeDtypeStruct((B, ATT_HEADS, LANES, S), BF16),
                   jax.ShapeDtypeStruct((B, S, 3 * GDN_WIDTH), F32),
                   jax.ShapeDtypeStruct((B, S, GDN_WIDTH), F32),
                   jax.ShapeDtypeStruct((B, S, LANES), F32)),
        grid=(B, S // tm),
        in_specs=[pl.BlockSpec((1, tm, D_MODEL), row),
                  pl.BlockSpec((1, 1, 8, D_MODEL), lambda b, i: (b, (i >= n_lat_blocks).astype(jnp.int32), 0, 0)),
                  pl.BlockSpec((1, D_MODEL), lambda b, i: (0, 0)),
                  pl.BlockSpec((D_MODEL, IN_MAIN), vec),
                  pl.BlockSpec((D_MODEL, LANES), vec),
                  pl.BlockSpec((tm, LANES), lambda b, i: (i, 0)),
                  pl.BlockSpec((tm, LANES), lambda b, i: (i, 0)),
                  pl.BlockSpec((1, LANES), vec),
                  pl.BlockSpec((1, LANES), vec)],
        out_specs=(pl.BlockSpec((1, ATT_HEADS, 2, LANES, tm), lambda b, i: (b, 0, 0, 0, i)),
                   pl.BlockSpec((1, ATT_HEADS, tm, LANES), lambda b, i: (b, 0, i, 0)),
                   pl.BlockSpec((1, ATT_HEADS, LANES, tm), lambda b, i: (b, 0, 0, i)),
                   pl.BlockSpec((1, tm, 3 * GDN_WIDTH), row),
                   pl.BlockSpec((1, tm, GDN_WIDTH), row),
                   pl.BlockSpec((1, tm, LANES), row)),
        compiler_params=_cparams(("parallel", "parallel")),
        name="inproj",
    )(h, modtab, g, w_main, w_ab, cos_t, sin_t, gq, gk)


def _attn_kernel(lam_ref, qz_ref, k_ref, v_ref, g_ref, *rest, aliased, tk, n_kv):
    if aliased:
        rest = rest[1:]
    o_ref, s_sc, mt_sc, m_sc, l_sc, acc_sc = rest
    m_sc[...] = jnp.full(m_sc.shape, -jnp.inf, F32)
    l_sc[...] = jnp.zeros(l_sc.shape, F32)
    acc_sc[...] = jnp.zeros(acc_sc.shape, F32)

    def rows(j):
        if isinstance(j, int):
            return pl.ds(j * tk, tk)
        return pl.ds(pl.multiple_of(j * tk, tk), tk)

    def qk_tile(j, slot, ps=(0, 1)):
        kt = k_ref[0, 0, rows(j), :]
        for p in ps:
            s = _dot(kt, qz_ref[0, 0, p])
            s_sc[slot, p] = s
            mt_sc[slot, p] = jnp.max(s, axis=0, keepdims=True)

    def pv_tile(j, slot, ps=(0, 1)):
        for p in ps:
            m_prev = m_sc[p]
            m_new = jnp.maximum(m_prev, mt_sc[slot, p])
            alpha = jnp.exp2(m_prev - m_new)
            lsum = None
            acc = None
            for c in range(tk // KV_CHUNK):
                cs = slice(c * KV_CHUNK, (c + 1) * KV_CHUNK)
                pe = jnp.exp2(s_sc[slot, p, cs, :] - m_new)
                ps = jnp.sum(pe, axis=0, keepdims=True)
                pv = _dot(v_ref[0, 0, j, :, cs], pe.astype(BF16))
                lsum = ps if lsum is None else lsum + ps
                acc = pv if acc is None else acc + pv
            l_sc[p] = alpha * l_sc[p] + lsum
            acc_sc[p] = alpha * acc_sc[p] + acc
            m_sc[p] = m_new

    def fused_tile(jq, slot_q, jp, slot_p):
        m_new = [jnp.maximum(m_sc[p], mt_sc[slot_p, p]) for p in range(2)]
        alpha = [jnp.exp2(m_sc[p] - m_new[p]) for p in range(2)]
        mx, lsum, acc = [None, None], [None, None], [None, None]
        for c in range(tk // KV_CHUNK):
            cs = slice(c * KV_CHUNK, (c + 1) * KV_CHUNK)
            if isinstance(jq, int):
                kr = pl.ds(jq * tk + c * KV_CHUNK, KV_CHUNK)
            else:
                kr = pl.ds(pl.multiple_of(jq * tk + c * KV_CHUNK, KV_CHUNK), KV_CHUNK)
            kc = k_ref[0, 0, kr, :]
            vc = v_ref[0, 0, jp, :, cs]
            for p in range(2):
                s = _dot(kc, qz_ref[0, 0, p])
                s_sc[slot_q, p, cs, :] = s
                cm = jnp.max(s, axis=0, keepdims=True)
                mx[p] = cm if mx[p] is None else jnp.maximum(mx[p], cm)
                pe = jnp.exp2(s_sc[slot_p, p, cs, :] - m_new[p])
                ps = jnp.sum(pe, axis=0, keepdims=True)
                pv = _dot(vc, pe.astype(BF16))
                lsum[p] = ps if lsum[p] is None else lsum[p] + ps
                acc[p] = pv if acc[p] is None else acc[p] + pv
        for p in range(2):
            mt_sc[slot_q, p] = mx[p]
            l_sc[p] = alpha[p] * l_sc[p] + lsum[p]
            acc_sc[p] = alpha[p] * acc_sc[p] + acc[p]
            m_sc[p] = m_new[p]

    qk_tile(0, 0)
    n_pairs = (n_kv - 1) // 2

    def pair(jj, carry):
        j = 2 * jj
        fused_tile(j + 1, 1, j, 0)
        fused_tile(j + 2, 0, j + 1, 1)
        return carry

    if n_pairs > 0:
        lax.fori_loop(0, n_pairs, pair, 0)
    j = 2 * n_pairs
    if j == n_kv - 1:
        pv_tile(j, 0)
    else:
        qk_tile(j + 1, 1)
        pv_tile(j, 0)
        pv_tile(j + 1, 1)

    o = acc_sc[0] / l_sc[0] - lam_ref[0] * (acc_sc[1] / l_sc[1])
    ms = jnp.mean(o * o, axis=0, keepdims=True)
    o_ref[0] = ((o * lax.rsqrt(ms + EPS)).T * g_ref[...]).astype(BF16)


def _attention(lam, qz, k, vt, g, *, tq, tk, q_blk0, n_q, kv_blk0, n_kv, prev_out=None, name="attn_ctx"):
    B, H, _, _, S = qz.shape
    aliased = prev_out is not None
    skv = tk * n_kv
    in_specs = [pl.BlockSpec(memory_space=pltpu.SMEM),
                pl.BlockSpec((1, 1, 2, LANES, tq), lambda b, h, i: (b, h, 0, 0, q_blk0 + i)),
                pl.BlockSpec((1, 1, skv, LANES), lambda b, h, i: (b, h, kv_blk0, 0)),
                pl.BlockSpec((1, 1, n_kv, LANES, tk), lambda b, h, i: (b, h, 0, 0, 0)),
                pl.BlockSpec((1, LANES), lambda b, h, i: (0, 0))]
    args = [lam, qz, k, vt, g]
    aliases = {}
    if aliased:
        in_specs.append(pl.BlockSpec(memory_space=pl.ANY))
        args.append(prev_out)
        aliases = {5: 0}
    return pl.pallas_call(
        functools.partial(_attn_kernel, aliased=aliased, tk=tk, n_kv=n_kv),
        out_shape=jax.ShapeDtypeStruct((B, S, ATT_WIDTH), BF16),
        grid=(B, H, n_q),
        in_specs=in_specs,
        out_specs=pl.BlockSpec((1, tq, LANES), lambda b, h, i: (b, q_blk0 + i, h)),
        scratch_shapes=[pltpu.VMEM((2, 2, tk, tq), F32), pltpu.VMEM((2, 2, 1, tq), F32),
                        pltpu.VMEM((2, 1, tq), F32), pltpu.VMEM((2, 1, tq), F32),
                        pltpu.VMEM((2, LANES, tq), F32)],
        input_output_aliases=aliases,
        compiler_params=_cparams(("parallel", "parallel", "arbitrary")),
        name=name,
    )(*args)


def _attention_lat(lam, qz, k, vt, g, L):
    B, H, _, S = vt.shape
    tq = _pick_tile(L, (512, 256))
    tk = _pick_tile(S, (1280, 640, 256))
    n_kv = S // tk
    vt_tiles = vt.reshape(B, H, LANES, n_kv, tk).transpose(0, 1, 3, 2, 4)
    out0 = jnp.zeros((B, S, ATT_WIDTH), BF16)
    return _attention(lam, qz, k, vt_tiles, g, tq=tq, tk=tk, q_blk0=0, n_q=L // tq, kv_blk0=0, n_kv=n_kv,
                      prev_out=out0, name="attn_lat")


def _attention_ctx(lam, qz, k, vt, g, L, prev_out):
    B, H, _, S = vt.shape
    Lc = S - L
    vt_ctx = vt[:, :, :, L:].reshape(B, H, 1, LANES, Lc)
    return _attention(lam, qz, k, vt_ctx, g, tq=Lc, tk=Lc, q_blk0=L // Lc, n_q=1, kv_blk0=L // Lc, n_kv=1,
                      prev_out=prev_out)


def _gdnprep_kernel(x_ref, prev_ref, next_ref, cw_ref, ab_ref, alog_ref, dtb_ref,
                    q_ref, k_ref, v_ref, gate_ref, ext_sc, *, nb_lat, nb_all):
    i = pl.program_id(1)
    tm = x_ref.shape[1]
    first = jnp.logical_or(i == 0, i == nb_lat)
    last = jnp.logical_or(i == nb_lat - 1, i == nb_all - 1)
    keep_prev = jnp.where(first, 0.0, 1.0)
    keep_next = jnp.where(last, 0.0, 1.0)
    ext_sc[0:8, :] = prev_ref[0] * keep_prev
    ext_sc[8:8 + tm, :] = x_ref[0]
    ext_sc[8 + tm:16 + tm, :] = next_ref[0] * keep_next
    acc = None
    for j in range(CONV_W):
        term = ext_sc[pl.ds(8 + j - CONV_W // 2, tm), :] * cw_ref[j:j + 1, :]
        acc = term if acc is None else acc + term
    y = acc * _sigmoid(acc)
    for part, ref in ((0, q_ref), (1, k_ref)):
        for hh in range(GDN_HEADS):
            lo = part * GDN_WIDTH + hh * GDN_HEAD_DIM
            t = y[:, lo:lo + GDN_HEAD_DIM]
            ref[0, :, hh * GDN_HEAD_DIM:(hh + 1) * GDN_HEAD_DIM] = (
                t * lax.rsqrt(jnp.sum(t * t, axis=-1, keepdims=True) + EPS))
    v_ref[0] = y[:, 2 * GDN_WIDTH:3 * GDN_WIDTH]
    ab = ab_ref[0]
    xs = ab + dtb_ref[...]
    sp = jnp.maximum(xs, 0.0) + jnp.log(1.0 + jnp.exp(-jnp.abs(xs)))
    g = -jnp.exp(alog_ref[...]) * sp
    lane = lax.broadcasted_iota(jnp.int32, ab.shape, 1)
    gate_ref[0] = jnp.where(lane < 2 * GDN_HEADS, g, _sigmoid(ab))


def _gdnprep(p_gdn, conv_w8, ab, alog_row, dtb_row, n_lat_blocks):
    B, S, W = p_gdn.shape
    tm = ROW_TILE
    nb = S // tm
    r8 = tm // 8
    row = lambda b, i: (b, i, 0)
    return pl.pallas_call(
        functools.partial(_gdnprep_kernel, nb_lat=n_lat_blocks, nb_all=nb),
        out_shape=(jax.ShapeDtypeStruct((B, S, GDN_WIDTH), F32),
                   jax.ShapeDtypeStruct((B, S, GDN_WIDTH), F32),
                   jax.ShapeDtypeStruct((B, S, GDN_WIDTH), F32),
                   jax.ShapeDtypeStruct((B, S, LANES), F32)),
        grid=(B, nb),
        in_specs=[pl.BlockSpec((1, tm, W), row),
                  pl.BlockSpec((1, 8, W), lambda b, i: (b, jnp.maximum(i * r8 - 1, 0), 0)),
                  pl.BlockSpec((1, 8, W), lambda b, i: (b, jnp.minimum((i + 1) * r8, nb * r8 - 1), 0)),
                  pl.BlockSpec((8, W), lambda b, i: (0, 0)),
                  pl.BlockSpec((1, tm, LANES), row),
                  pl.BlockSpec((1, LANES), lambda b, i: (0, 0)),
                  pl.BlockSpec((1, LANES), lambda b, i: (0, 0))],
        out_specs=(pl.BlockSpec((1, tm, GDN_WIDTH), row),
                   pl.BlockSpec((1, tm, GDN_WIDTH), row),
                   pl.BlockSpec((1, tm, GDN_WIDTH), row),
                   pl.BlockSpec((1, tm, LANES), row)),
        scratch_shapes=[pltpu.VMEM((tm + 16, W), F32)],
        compiler_params=_cparams(("parallel", "parallel")),
        name="gdnprep",
    )(p_gdn, p_gdn, p_gdn, conv_w8, ab, alog_row, dtb_row)


def _gdnchunk_kernel(q_ref, k_ref, v_ref, gate_ref, qg_ref, w_ref, kd_ref, u_ref, aq_ref, eg_ref):
    tm = q_ref.shape[1]
    nc = tm // GDN_CHUNK
    gates = gate_ref[0]
    ri = lax.broadcasted_iota(jnp.int32, (tm, tm), 0)
    ci = lax.broadcasted_iota(jnp.int32, (tm, tm), 1)
    same = (ri >> 6) == (ci >> 6)
    eye = ri == ci
    eye_f = jnp.where(eye, 1.0, 0.0).astype(F32)

    g1 = gates.astype(BF16)
    r1 = gates - g1.astype(F32)
    g2 = r1.astype(BF16)
    g3 = (r1 - g2.astype(F32)).astype(BF16)
    gparts = jnp.concatenate([g1, g2, g3], axis=1)

    def seg_sum(mask):
        r = _dot(jnp.where(mask, 1.0, 0.0).astype(BF16), gparts)
        return (r[:, :LANES] + r[:, LANES:2 * LANES]) + r[:, 2 * LANES:]

    tot = seg_sum(same)

    heads = []
    for hh in range(GDN_HEADS):
        sl = slice(hh * GDN_HEAD_DIM, (hh + 1) * GDN_HEAD_DIM)
        k = k_ref[0, :, sl]
        k16 = k.astype(BF16)
        q = q_ref[0, :, sl] * (GDN_HEAD_DIM ** -0.5)
        heads.append((sl, k, q, _dot_nt(k16, k16), _dot_nt(q.astype(BF16), k16)))

    chains = []
    for d in range(2):
        incl = jnp.logical_and(same, (ci <= ri) if d == 0 else (ci >= ri))
        strict = jnp.logical_and(incl, jnp.logical_not(eye))
        gc = seg_sum(incl)
        gc_t = gc.T
        for hh in range(GDN_HEADS):
            ln = d * GDN_HEADS + hh
            gcol = gc[:, ln:ln + 1]
            bcol = gates[:, 2 * GDN_HEADS + ln:2 * GDN_HEADS + ln + 1]
            tcol = tot[:, ln:ln + 1]
            diff = gcol - gc_t[ln:ln + 1, :]
            decay = jnp.where(incl, jnp.exp(jnp.where(incl, diff, 0.0)), 0.0)
            a = jnp.where(strict, heads[hh][3] * bcol * decay, 0.0)
            aqk = heads[hh][4] * decay
            chains.append(dict(d=d, hh=hh, x=-a, t=eye_f - a, aqk=aqk, gcol=gcol, bcol=bcol, tcol=tcol))

    for _ in range(5):
        for ch in chains:
            x16 = ch["x"].astype(BF16)
            ch["x"] = _dot(x16, x16)
        for ch in chains:
            ch["t"] = ch["t"] + _dot(ch["t"].astype(BF16), ch["x"].astype(BF16))

    for ch in chains:
        d, hh, gcol, bcol, tcol = ch["d"], ch["hh"], ch["gcol"], ch["bcol"], ch["tcol"]
        sl, k, q = heads[hh][0], heads[hh][1], heads[hh][2]
        t16 = ch["t"].astype(BF16)
        egc = jnp.exp(gcol)
        kb = k * bcol
        rhs = jnp.concatenate([(v_ref[0, :, sl] * bcol).astype(BF16), (kb * egc).astype(BF16)], axis=1)
        uw = _dot(t16, rhs)
        u_ref[0, d, :, sl] = uw[:, :GDN_HEAD_DIM]
        w_ref[0, d, :, sl] = uw[:, GDN_HEAD_DIM:].astype(BF16)
        kd_ref[0, d, :, sl] = (k * jnp.exp(tcol - gcol)).astype(BF16)
        qg_ref[0, d, :, sl] = (q * egc).astype(BF16)
        for cc in range(nc):
            rs = slice(cc * GDN_CHUNK, (cc + 1) * GDN_CHUNK)
            aq_ref[0, d, rs, hh * GDN_CHUNK:(hh + 1) * GDN_CHUNK] = ch["aqk"][rs, rs].astype(BF16)
            eg_ref[0, d, cc, hh:hh + 1, :] = jnp.broadcast_to(
                jnp.exp(tcol[cc * GDN_CHUNK:cc * GDN_CHUNK + 1, :]), (1, LANES))


def _gdnchunk(gq, gk, gv, gates):
    B, S, W = gq.shape
    tm = ROW_TILE
    nc = tm // GDN_CHUNK
    row = lambda b, i: (b, i, 0)
    drow = lambda b, i: (b, 0, i, 0)
    big = lambda dt: jax.ShapeDtypeStruct((B, 2, S, W), dt)
    return pl.pallas_call(
        _gdnchunk_kernel,
        out_shape=(big(BF16), big(BF16), big(BF16), big(F32),
                   jax.ShapeDtypeStruct((B, 2, S, GDN_HEADS * GDN_CHUNK), BF16),
                   jax.ShapeDtypeStruct((B, 2, S // GDN_CHUNK, GDN_HEADS, LANES), F32)),
        grid=(B, S // tm),
        in_specs=[pl.BlockSpec((1, tm, W), row), pl.BlockSpec((1, tm, W), row),
                  pl.BlockSpec((1, tm, W), row), pl.BlockSpec((1, tm, LANES), row)],
        out_specs=(pl.BlockSpec((1, 2, tm, W), drow), pl.BlockSpec((1, 2, tm, W), drow),
                   pl.BlockSpec((1, 2, tm, W), drow), pl.BlockSpec((1, 2, tm, W), drow),
                   pl.BlockSpec((1, 2, tm, GDN_HEADS * GDN_CHUNK), drow),
                   pl.BlockSpec((1, 2, nc, GDN_HEADS, LANES), lambda b, i: (b, 0, i, 0, 0))),
        compiler_params=_cparams(("parallel", "parallel")),
        name="gdnchunk",
    )(gq, gk, gv, gates)


def _gdnscan_kernel(qg0, w0, kd0, u0, aq0, eg0, qg1, w1, kd1, u1, aq1, eg1, of_ref, ob_ref, s_sc):
    @pl.when(pl.program_id(1) == 0)
    def _():
        s_sc[...] = jnp.zeros(s_sc.shape, F32)

    dirs = ((qg0, w0, kd0, u0, aq0, eg0, of_ref), (qg1, w1, kd1, u1, aq1, eg1, ob_ref))
    chains = [(d, hh) for d in range(2) for hh in range(GDN_HEADS)]
    st = {ch: s_sc[ch[0], ch[1]] for ch in chains}
    for step in range(SCAN_GROUP):
        st16, vn16, qs = {}, {}, {}
        for d, hh in chains:
            st16[d, hh] = st[d, hh].astype(BF16)
        for d, hh in chains:
            qg, w, kd, u, aq, eg, o_ref = dirs[d]
            cc = step if d == 0 else SCAN_GROUP - 1 - step
            rs = slice(cc * GDN_CHUNK, (cc + 1) * GDN_CHUNK)
            sl = slice(hh * GDN_HEAD_DIM, (hh + 1) * GDN_HEAD_DIM)
            vn16[d, hh] = (u[0, 0, rs, sl] - _dot(w[0, 0, rs, sl], st16[d, hh])).astype(BF16)
            qs[d, hh] = _dot(qg[0, 0, rs, sl], st16[d, hh])
        for d, hh in chains:
            qg, w, kd, u, aq, eg, o_ref = dirs[d]
            cc = step if d == 0 else SCAN_GROUP - 1 - step
            rs = slice(cc * GDN_CHUNK, (cc + 1) * GDN_CHUNK)
            sl = slice(hh * GDN_HEAD_DIM, (hh + 1) * GDN_HEAD_DIM)
            o_ref[0, rs, sl] = qs[d, hh] + _dot(aq[0, 0, rs, hh * GDN_CHUNK:(hh + 1) * GDN_CHUNK], vn16[d, hh])
            st[d, hh] = st[d, hh] * eg[0, 0, cc, hh:hh + 1, :] + _dot_tn(kd[0, 0, rs, sl], vn16[d, hh])
    for d, hh in chains:
        s_sc[d, hh] = st[d, hh]


def _gdnscan(qg, w, kd, u, aq, eg, n_lat_chunks, n_ctx_chunks):
    B, _, S, W = qg.shape
    G = SCAN_GROUP
    R = GDN_CHUNK * G
    n = S // R
    n_lat, n_ctx = n_lat_chunks // G, n_ctx_chunks // G
    assert n_lat * G == n_lat_chunks and n_ctx * G == n_ctx_chunks

    def fwd_blk(i):
        return jnp.where(i < n_ctx, n_lat + i, i - n_ctx)

    def bwd_blk(i):
        return jnp.where(i < n_ctx, n_lat + n_ctx - 1 - i, n - 1 - i)

    def specs(d, blk_of):
        big = pl.BlockSpec((1, 1, R, W), lambda b, i: (b, d, blk_of(i), 0))
        return [big, big, big, big,
                pl.BlockSpec((1, 1, R, GDN_HEADS * GDN_CHUNK), lambda b, i: (b, d, blk_of(i), 0)),
                pl.BlockSpec((1, 1, G, GDN_HEADS, LANES), lambda b, i: (b, d, blk_of(i), 0, 0))]

    return pl.pallas_call(
        _gdnscan_kernel,
        out_shape=(jax.ShapeDtypeStruct((B, S, W), F32), jax.ShapeDtypeStruct((B, S, W), F32)),
        grid=(B, n),
        in_specs=specs(0, fwd_blk) + specs(1, bwd_blk),
        out_specs=(pl.BlockSpec((1, R, W), lambda b, i: (b, fwd_blk(i), 0)),
                   pl.BlockSpec((1, R, W), lambda b, i: (b, bwd_blk(i), 0))),
        scratch_shapes=[pltpu.VMEM((2, GDN_HEADS, GDN_HEAD_DIM, GDN_HEAD_DIM), F32)],
        compiler_params=_cparams(("parallel", "arbitrary")),
        name="gdnscan",
    )(qg, w, kd, u, aq, eg, qg, w, kd, u, aq, eg)


def _mixout_kernel(oa_ref, of_ref, ob_ref, z_ref, h_ref, mod_ref, gg_ref, wo_ref, gf_ref, rw_ref, rb_ref,
                   hn_ref, v_ref, te_ref, tg_ref):
    og = of_ref[0] + ob_ref[0]
    z = z_ref[0]
    parts = [oa_ref[0]]
    for hh in range(GDN_HEADS):
        sl = slice(hh * GDN_HEAD_DIM, (hh + 1) * GDN_HEAD_DIM)
        t = og[:, sl]
        t = t * lax.rsqrt(jnp.mean(t * t, axis=-1, keepdims=True) + EPS) * gg_ref[...]
        zz = z[:, sl]
        parts.append((t * (zz * _sigmoid(zz))).astype(BF16))
    mix_in = jnp.concatenate(parts, axis=-1)
    mix = _dot(mix_in, wo_ref[...])
    hn = h_ref[0] + mod_ref[0, 0, 2:3, :] * mix
    hn_ref[0] = hn
    y = hn * lax.rsqrt(jnp.mean(hn * hn, axis=-1, keepdims=True) + EPS) * gf_ref[...]
    v = y * (1.0 + mod_ref[0, 0, 4:5, :]) + mod_ref[0, 0, 3:4, :]
    v_ref[0] = v.astype(BF16)
    v_hi = v.astype(BF16)
    v_lo = (v - v_hi.astype(F32)).astype(BF16)
    logits = (_dot(v_hi, rw_ref[0]) + _dot(v_lo, rw_ref[0]) + _dot(v_hi, rw_ref[1])) + rb_ref[...]
    lane = lax.broadcasted_iota(jnp.int32, logits.shape, 1)
    cur = logits
    vals, idxs = [], []
    for _ in range(TOP_K):
        m = jnp.max(cur, axis=-1, keepdims=True)
        idx = jnp.min(jnp.where(cur == m, lane, LANES), axis=-1, keepdims=True)
        vals.append(m)
        idxs.append(idx)
        cur = jnp.where(lane == idx, -jnp.inf, cur)
    es = [jnp.exp(vv - vals[0]) for vv in vals]
    inv = 1.0 / (es[0] + es[1] + es[2] + es[3])
    te = jnp.zeros(logits.shape, jnp.int32)
    tg = jnp.zeros(logits.shape, F32)
    for kk in range(TOP_K):
        te = jnp.where(lane == kk, idxs[kk], te)
        tg = jnp.where(lane == kk, es[kk] * inv, tg)
    te_ref[0] = te
    tg_ref[0] = tg


def _mixout(o_att, o_f, o_b, z, h, modtab, gg, w_out, gf, rw, rb, n_lat_blocks):
    B, S, _ = h.shape
    tm = ROW_TILE
    row = lambda b, i: (b, i, 0)
    const = lambda b, i: (0, 0)
    return pl.pallas_call(
        _mixout_kernel,
        out_shape=(jax.ShapeDtypeStruct((B, S, D_MODEL), F32),
                   jax.ShapeDtypeStruct((B, S, D_MODEL), BF16),
                   jax.ShapeDtypeStruct((B, S, LANES), jnp.int32),
                   jax.ShapeDtypeStruct((B, S, LANES), F32)),
        grid=(B, S // tm),
        in_specs=[pl.BlockSpec((1, tm, ATT_WIDTH), row),
                  pl.BlockSpec((1, tm, GDN_WIDTH), row),
                  pl.BlockSpec((1, tm, GDN_WIDTH), row),
                  pl.BlockSpec((1, tm, GDN_WIDTH), row),
                  pl.BlockSpec((1, tm, D_MODEL), row),
                  pl.BlockSpec((1, 1, 8, D_MODEL), lambda b, i: (b, (i >= n_lat_blocks).astype(jnp.int32), 0, 0)),
                  pl.BlockSpec((1, LANES), const),
                  pl.BlockSpec((D_MODEL, D_MODEL), const),
                  pl.BlockSpec((1, D_MODEL), const),
                  pl.BlockSpec((2, D_MODEL, LANES), lambda b, i: (0, 0, 0)),
                  pl.BlockSpec((1, LANES), const)],
        out_specs=(pl.BlockSpec((1, tm, D_MODEL), row),
                   pl.BlockSpec((1, tm, D_MODEL), row),
                   pl.BlockSpec((1, tm, LANES), row),
                   pl.BlockSpec((1, tm, LANES), row)),
        compiler_params=_cparams(("parallel", "parallel")),
        name="mixout",
    )(o_att, o_f, o_b, z, h, modtab, gg, w_out, gf, rw, rb)


def _expert_kernel(be_ref, nv_ref, x_ref, wgu_ref, bgu_ref, wd_ref, bd_ref, y_ref, wgu_sc, wd_sc):
    i = pl.program_id(0)
    new_expert = jnp.logical_or(i == 0, be_ref[i] != be_ref[jnp.maximum(i - 1, 0)])

    @pl.when(new_expert)
    def _():
        wgu_sc[...] = wgu_ref[0, 0].astype(BF16)
        wd_sc[...] = wd_ref[0, 0].astype(BF16)

    @pl.when(nv_ref[i] > 0)
    def _():
        gu = _dot(x_ref[...], wgu_sc[...]) + bgu_ref[0, 0]
        g_ = jnp.minimum(gu[:, :D_EXPERT], SWIGLU_LIMIT)
        up = jnp.clip(gu[:, D_EXPERT:], -SWIGLU_LIMIT, SWIGLU_LIMIT)
        glu = g_ * _sigmoid(SWIGLU_ALPHA * g_)
        act = ((up + 1.0) * glu).astype(BF16)
        y_ref[...] = (_dot(act, wd_sc[...]) + bd_ref[0, 0]).astype(y_ref.dtype)

    @pl.when(nv_ref[i] == 0)
    def _():
        y_ref[...] = jnp.zeros(y_ref.shape, y_ref.dtype)


def _experts(block_e, n_valid, x_sorted, wgu, bgu, wd, bd, layer):
    n_slots = x_sorted.shape[0]
    nb = n_slots // MOE_BLOCK
    grid_spec = pltpu.PrefetchScalarGridSpec(
        num_scalar_prefetch=2,
        grid=(nb,),
        in_specs=[pl.BlockSpec((MOE_BLOCK, D_MODEL), lambda i, be, nv: (i, 0)),
                  pl.BlockSpec((1, 1, D_MODEL, 2 * D_EXPERT), lambda i, be, nv: (layer, be[i], 0, 0)),
                  pl.BlockSpec((1, 1, 1, 2 * D_EXPERT), lambda i, be, nv: (layer, be[i], 0, 0)),
                  pl.BlockSpec((1, 1, D_EXPERT, D_MODEL), lambda i, be, nv: (layer, be[i], 0, 0)),
                  pl.BlockSpec((1, 1, 1, D_MODEL), lambda i, be, nv: (layer, be[i], 0, 0))],
        out_specs=pl.BlockSpec((MOE_BLOCK, D_MODEL), lambda i, be, nv: (i, 0)),
        scratch_shapes=[pltpu.VMEM((D_MODEL, 2 * D_EXPERT), BF16), pltpu.VMEM((D_EXPERT, D_MODEL), BF16)],
    )
    return pl.pallas_call(
        _expert_kernel,
        out_shape=jax.ShapeDtypeStruct((n_slots, D_MODEL), BF16),
        grid_spec=grid_spec,
        compiler_params=pltpu.CompilerParams(dimension_semantics=("arbitrary",),
                                             vmem_limit_bytes=EXPERT_VMEM_LIMIT),
        name="experts",
    )(block_e, n_valid, x_sorted, wgu, bgu, wd, bd)


def _moe_plan(top_e, top_g):
    T = top_e.shape[0]
    n_assign = T * TOP_K
    n_blocks = -(-n_assign // MOE_BLOCK) + N_EXPERTS
    n_slots = n_blocks * MOE_BLOCK
    i32 = jnp.int32
    flat_e = top_e.reshape(-1)
    gate_flat = top_g.reshape(-1)
    assert T < NO_TOKEN
    gate_bits = lax.bitcast_convert_type(gate_flat.astype(BF16), jnp.uint16).astype(jnp.uint32)
    tok_pack = (gate_bits << 16) | (jnp.arange(n_assign, dtype=jnp.uint32) // TOP_K)
    _, sorted_pack = lax.sort((flat_e, tok_pack), num_keys=1, is_stable=True)
    e_ids = jnp.arange(N_EXPERTS, dtype=i32)
    is_e = flat_e[:, None] == e_ids[None, :]
    counts = jnp.sum(is_e, axis=0, dtype=i32)
    start = jnp.cumsum(counts) - counts
    padded = (counts + MOE_BLOCK - 1) // MOE_BLOCK * MOE_BLOCK
    pad_end = jnp.cumsum(padded)
    pad_start = pad_end - padded
    blk0 = jnp.arange(n_blocks, dtype=i32) * MOE_BLOCK
    block_e = jnp.minimum(jnp.sum(pad_end[None, :] <= blk0[:, None], axis=1, dtype=i32), N_EXPERTS - 1)
    off = (blk0 - pad_start[block_e])[:, None] + jnp.arange(MOE_BLOCK, dtype=i32)[None, :]
    valid = off < counts[block_e][:, None]
    valid_flat = valid.reshape(-1)
    slot_pack = jnp.where(valid_flat,
                          sorted_pack[jnp.clip(off + start[block_e][:, None], 0, n_assign - 1).reshape(-1)],
                          jnp.uint32(NO_TOKEN))
    slot_tok = jnp.where(valid_flat, (slot_pack & NO_TOKEN).astype(i32), jnp.arange(n_slots, dtype=i32) % T)
    n_valid = jnp.sum(valid, axis=1, dtype=i32)

    n_tiles = T // COMBINE_TILE
    cnt = jnp.sum(top_e.reshape(n_tiles, COMBINE_TILE * TOP_K)[:, :, None] == e_ids[None, None, :], axis=1, dtype=i32)
    run_start = pad_start[None, :] + jnp.cumsum(cnt, axis=0) - cnt
    q_first = run_start // COMBINE_CHUNK
    n_ch = jnp.where(cnt > 0, (run_start + cnt - 1) // COMBINE_CHUNK - q_first + 1, 0)
    ch_end = jnp.cumsum(n_ch, axis=1)
    ch_off = ch_end - n_ch
    j = jnp.arange(COMBINE_MAX_CHUNKS, dtype=i32)
    e_j = jnp.minimum(jnp.sum(ch_end[:, None, :] <= j[None, :, None], axis=2, dtype=i32), N_EXPERTS - 1)
    used = j[None, :] < ch_end[:, -1:]
    base = jnp.sum(jnp.where(e_j[:, :, None] == e_ids[None, None, :], (q_first - ch_off)[:, None, :], 0),
                   axis=2, dtype=i32)
    chunk_id = jnp.where(used, base + j[None, :], 0)
    slot_pack = slot_pack.reshape(n_slots // COMBINE_CHUNK, COMBINE_CHUNK)
    row_pack = jnp.where(used[:, :, None], slot_pack[chunk_id], jnp.uint32(NO_TOKEN))
    row_pack = lax.bitcast_convert_type(row_pack, i32).reshape(n_tiles, 1, COMBINE_MAX_CHUNKS * COMBINE_CHUNK)
    return slot_tok, block_e, n_valid, chunk_id, row_pack


def _combine_kernel(cid_ref, tok_ref, h_ref, mod_ref, y_hbm, o_ref, ybuf, sem, *, lat_only, tiles_per_batch,
                    n_lat_tiles):
    i = pl.program_id(0)
    n = pl.num_programs(0)
    slot = lax.rem(i, 2)

    def chunk_copy(c, s, jj):
        return pltpu.make_async_copy(
            y_hbm.at[pl.ds(pl.multiple_of(c * COMBINE_CHUNK, COMBINE_CHUNK), COMBINE_CHUNK)],
            ybuf.at[s, pl.ds(jj * COMBINE_CHUNK, COMBINE_CHUNK)], sem.at[s])

    def start_tile(t, s):
        for jj in range(COMBINE_MAX_CHUNKS):
            chunk_copy(cid_ref[t, jj], s, jj).start()

    @pl.when(i == 0)
    def _():
        start_tile(0, 0)

    @pl.when(i + 1 < n)
    def _():
        start_tile(i + 1, 1 - slot)

    for jj in range(COMBINE_MAX_CHUNKS):
        chunk_copy(0, slot, jj).wait()

    packed = tok_ref[0]
    tok = packed & NO_TOKEN
    gate = lax.bitcast_convert_type(packed & ~NO_TOKEN, F32)
    t_ids = i * COMBINE_TILE + lax.broadcasted_iota(jnp.int32, (COMBINE_TILE, tok.shape[1]), 0)
    sel = jnp.where(tok == t_ids, gate, 0.0).astype(BF16)
    y = _dot(sel, ybuf[slot])
    out = h_ref[...] + mod_ref[0, 0, 5:6, :] * y
    if lat_only:
        @pl.when(lax.rem(i, tiles_per_batch) < n_lat_tiles)
        def _():
            o_ref[...] = out
    else:
        o_ref[...] = out


def _combine(chunk_id, row_tok, y_sorted, h, modtab, n_lat_blocks, lat_only):
    B, S, _ = h.shape
    T = B * S
    ct = COMBINE_TILE
    tiles_per_batch = S // ct
    n_lat_tiles = n_lat_blocks * ROW_TILE // ct
    n_rows = COMBINE_MAX_CHUNKS * COMBINE_CHUNK
    if lat_only:
        out_rows = B * n_lat_tiles * ct
        out_map = lambda i, cid: ((i // tiles_per_batch) * n_lat_tiles
                                  + jnp.minimum(lax.rem(i, tiles_per_batch), n_lat_tiles - 1), 0)
    else:
        out_rows = T
        out_map = lambda i, cid: (i, 0)
    grid_spec = pltpu.PrefetchScalarGridSpec(
        num_scalar_prefetch=1,
        grid=(T // ct,),
        in_specs=[pl.BlockSpec((1, 1, n_rows), lambda i, cid: (i, 0, 0)),
                  pl.BlockSpec((ct, D_MODEL), lambda i, cid: (i, 0)),
                  pl.BlockSpec((1, 1, 8, D_MODEL),
                               lambda i, cid: (i // tiles_per_batch,
                                               (lax.rem(i, tiles_per_batch) >= n_lat_tiles).astype(jnp.int32), 0, 0)),
                  pl.BlockSpec(memory_space=pl.ANY)],
        out_specs=pl.BlockSpec((ct, D_MODEL), out_map),
        scratch_shapes=[pltpu.VMEM((2, n_rows, D_MODEL), BF16), pltpu.SemaphoreType.DMA((2,))],
    )
    out = pl.pallas_call(
        functools.partial(_combine_kernel, lat_only=lat_only, tiles_per_batch=tiles_per_batch,
                          n_lat_tiles=n_lat_tiles),
        out_shape=jax.ShapeDtypeStruct((out_rows, D_MODEL), F32),
        grid_spec=grid_spec,
        compiler_params=_cparams(("arbitrary",)),
        name="combine",
    )(chunk_id, row_tok, h.reshape(T, D_MODEL), modtab, y_sorted)
    return out.reshape(B, out_rows // B, D_MODEL)


def _moe(v_ffn, top_e, top_g, h_new, modtab, wgu, bgu, wd, bd, layer, n_lat_blocks, lat_only):
    B, S, _ = v_ffn.shape
    T = B * S
    slot_tok, block_e, n_valid, chunk_id, row_tok = _moe_plan(
        top_e.reshape(T, LANES)[:, :TOP_K], top_g.reshape(T, LANES)[:, :TOP_K])
    x_sorted = v_ffn.reshape(T, D_MODEL)[slot_tok]
    y_sorted = _experts(block_e, n_valid, x_sorted, wgu, bgu, wd, bd, layer)
    return _combine(chunk_id, row_tok, y_sorted, h_new, modtab, n_lat_blocks, lat_only)


def _pick_tile(n, cands):
    for t in cands:
        if n % t == 0:
            return t
    raise ValueError(f"no tile for {n}")


def _rope_tables(L, Lc):
    rows = L // GRID_W
    row = jnp.repeat(jnp.arange(rows, dtype=F32), GRID_W)
    col = (jnp.arange(L, dtype=jnp.int32) % GRID_W).astype(F32)
    inv_freq = ROPE_BASE ** (-jnp.arange(ROPE_PAIRS, dtype=F32) / ROPE_PAIRS)
    ar = row[:, None] * inv_freq
    ac = col[:, None] * inv_freq
    cos64 = jnp.concatenate([jnp.cos(ar), jnp.cos(ar), jnp.cos(ac), jnp.cos(ac)], axis=-1)
    sin64 = jnp.concatenate([-jnp.sin(ar), jnp.sin(ar), -jnp.sin(ac), jnp.sin(ac)], axis=-1)
    cos_t = jnp.concatenate([jnp.tile(cos64, (1, 2)), jnp.ones((Lc, LANES), F32)], axis=0)
    sin_t = jnp.concatenate([jnp.tile(sin64, (1, 2)), jnp.zeros((Lc, LANES), F32)], axis=0)
    return cos_t, sin_t


def _pad_lanes(v):
    v = v.reshape(1, -1).astype(F32)
    return jnp.pad(v, ((0, 0), (0, LANES - v.shape[1])))


def kernel(x, c, ctx, c_ctx, w_mod, b_mod, norm_mix_g, w_in, q_norm_g, k_norm_g, lam_q1, lam_k1, lam_q2, lam_k2, subln_g, conv_w, a_log, dt_bias, gdn_norm_g, w_out, norm_ffn_g, router_w, router_b, w_gate_up, b_gate_up, w_down, b_down):
    B, L, D = x.shape
    Lc = ctx.shape[1]
    S = L + Lc
    depth = w_mod.shape[0]
    tm = ROW_TILE
    n_lat_blocks = L // tm
    cos_t, sin_t = _rope_tables(L, Lc)

    c_rows = jnp.zeros((8, D), F32).at[:B].set(c).at[B].set(c_ctx)
    h = jnp.concatenate([x, ctx], axis=1)

    for layer in range(depth):
        mod = _adaln(c_rows, w_mod, b_mod, layer)
        mod6 = mod.reshape(8, 6, D)
        lat_mod = mod6[:B]
        ctx_mod = jnp.broadcast_to(mod6[B][None], (B, 6, D))
        modtab = jnp.pad(jnp.stack([lat_mod, ctx_mod], axis=1), ((0, 0), (0, 0), (0, 2), (0, 0)))

        lam_init = 0.8 - 0.6 * math.exp(-0.3 * layer)
        lam_full = (jnp.exp(jnp.sum(lam_q1[layer] * lam_k1[layer]))
                    - jnp.exp(jnp.sum(lam_q2[layer] * lam_k2[layer])) + lam_init).reshape(1).astype(F32)

        w_l = w_in[layer]
        w_main = w_l[:, :IN_MAIN].astype(BF16)
        w_ab = jnp.pad(w_l[:, IN_MAIN:], ((0, 0), (0, LANES - (w_l.shape[1] - IN_MAIN)))).astype(BF16)
        gq = jnp.tile(q_norm_g[layer].reshape(1, ATT_HEAD_DIM), (1, 2))
        gk = jnp.tile(k_norm_g[layer].reshape(1, ATT_HEAD_DIM), (1, 2))
        qz, k_att, v_att, p_gdn, z, ab = _inproj(h, modtab, norm_mix_g[layer].reshape(1, D), w_main, w_ab,
                                                 cos_t, sin_t, gq, gk, n_lat_blocks)
        g_sub = (subln_g[layer] * (1.0 - lam_init)).reshape(1, LANES).astype(F32)
        o_att = _attention_lat(lam_full, qz, k_att, v_att, g_sub, L)
        o_att = _attention_ctx(lam_full, qz, k_att, v_att, g_sub, L, o_att)

        conv_w8 = jnp.pad(conv_w[layer], ((0, 8 - CONV_W), (0, 0)))
        gq_g, gk_g, gv_g, gates = _gdnprep(p_gdn, conv_w8, ab, _pad_lanes(a_log[layer]),
                                           _pad_lanes(dt_bias[layer]), n_lat_blocks)
        qg, w_g, kd, u_g, aq, eg = _gdnchunk(gq_g, gk_g, gv_g, gates)
        o_f, o_b = _gdnscan(qg, w_g, kd, u_g, aq, eg, L // GDN_CHUNK, Lc // GDN_CHUNK)

        rw = jnp.pad(router_w[layer].astype(F32), ((0, 0), (0, LANES - N_EXPERTS)))
        rw_hi = rw.astype(BF16)
        rw = jnp.stack([rw_hi, (rw - rw_hi.astype(F32)).astype(BF16)])
        rb = jnp.pad(router_b[layer].reshape(1, N_EXPERTS).astype(F32), ((0, 0), (0, LANES - N_EXPERTS)),
                     constant_values=-1e30)
        h_new, v_ffn, top_e, top_g = _mixout(
            o_att, o_f, o_b, z, h, modtab, gdn_norm_g[layer].reshape(1, LANES), w_out[layer].astype(BF16),
            norm_ffn_g[layer].reshape(1, D), rw, rb, n_lat_blocks)

        h = _moe(v_ffn, top_e, top_g, h_new, modtab,
                 w_gate_up, b_gate_up.reshape(depth, N_EXPERTS, 1, 2 * D_EXPERT),
                 w_down, b_down.reshape(depth, N_EXPERTS, 1, D), layer, n_lat_blocks,
                 lat_only=layer == depth - 1)
    return h
```

```python
import functools
import math

import jax
import jax.numpy as jnp
from jax import lax
from jax.experimental import pallas as pl
from jax.experimental.pallas import tpu as pltpu

F32 = jnp.float32
BF16 = jnp.bfloat16
HIGHEST = lax.Precision.HIGHEST

D_MODEL = 1024
GRID_W = 64
EPS = 1e-6
ATT_WIDTH = 512
ATT_HEAD_DIM = 64
ATT_HEADS = 4
ROPE_BASE = 10000.0
ROPE_PAIRS = ATT_HEAD_DIM // 4
GDN_WIDTH = 512
GDN_HEAD_DIM = 128
GDN_HEADS = 4
GDN_CHUNK = 64
CONV_W = 5
IN_MAIN = 3 * ATT_WIDTH + 4 * GDN_WIDTH
N_EXPERTS = 32
TOP_K = 4
D_EXPERT = 1024
SWIGLU_ALPHA = 1.702
SWIGLU_LIMIT = 7.0
MOE_BLOCK = 512
SCAN_GROUP = 4
NO_TOKEN = 0xFFFF
COMBINE_TILE = 256
COMBINE_CHUNK = 32
COMBINE_MAX_CHUNKS = COMBINE_TILE * TOP_K // COMBINE_CHUNK + 2 * N_EXPERTS

LANES = 128
ROW_TILE = 256
KV_CHUNK = 256
Q_SCALE = ATT_HEAD_DIM ** -0.5 * math.log2(math.e)
VMEM_LIMIT = 48 * 1024 * 1024
EXPERT_VMEM_LIMIT = 56 * 1024 * 1024


def _cparams(sem):
    return pltpu.CompilerParams(dimension_semantics=sem, vmem_limit_bytes=VMEM_LIMIT)


def _dot(a, b):
    return jnp.dot(a, b, preferred_element_type=F32)


def _dot_nt(a, b):
    return lax.dot_general(a, b, (((1,), (1,)), ((), ())), preferred_element_type=F32)


def _dot_tn(a, b):
    return lax.dot_general(a, b, (((0,), (0,)), ((), ())), preferred_element_type=F32)


def _dot_hi(a, b):
    return jnp.dot(a, b, preferred_element_type=F32, precision=HIGHEST)


def _sigmoid(x):
    return 1.0 / (1.0 + jnp.exp(-x))


def _adaln_kernel(c_ref, w_ref, b_ref, o_ref):
    c = c_ref[...]
    s = c * _sigmoid(c)
    o_ref[...] = _dot_hi(s, w_ref[0]) + b_ref[0]


def _adaln(c_rows, w, b, layer):
    depth, _, n = w.shape
    tn = 1024
    return pl.pallas_call(
        _adaln_kernel,
        out_shape=jax.ShapeDtypeStruct((8, n), F32),
        grid=(n // tn,),
        in_specs=[pl.BlockSpec((8, D_MODEL), lambda j: (0, 0)),
                  pl.BlockSpec((1, D_MODEL, tn), lambda j: (layer, 0, j)),
                  pl.BlockSpec((1, 1, tn), lambda j: (layer, 0, j))],
        out_specs=pl.BlockSpec((8, tn), lambda j: (0, j)),
        compiler_params=_cparams(("arbitrary",)),
        name="adaln",
    )(c_rows, w, b.reshape(depth, 1, n))


def _qk_norm_rope(p_qk, cos_ref, sin_ref, gq_ref, gk_ref, qz_ref, k_ref):
    tm = p_qk.shape[0]
    lane = lax.broadcasted_iota(jnp.int32, (tm, LANES), 1)
    hi16 = (lane & 16) != 0
    first = lax.broadcasted_iota(jnp.int32, (LANES, tm), 0) < ATT_HEAD_DIM
    r = lax.broadcasted_iota(jnp.int32, (LANES, LANES), 0) >> 6
    c = lax.broadcasted_iota(jnp.int32, (LANES, LANES), 1) >> 6
    gmat = jnp.where(r == c, 1.0 / ATT_HEAD_DIM, 0.0).astype(BF16)
    cosv = cos_ref[...]
    sinv = sin_ref[...]
    sq = jnp.concatenate([jnp.square(p_qk[:, j * LANES:(j + 1) * LANES]) for j in range(2 * ATT_HEADS)], axis=0)
    sq_hi = sq.astype(BF16)
    sq_lo = (sq - sq_hi.astype(F32)).astype(BF16)
    ms_all = _dot(sq_hi, gmat) + _dot(sq_lo, gmat)
    for j in range(2 * ATT_HEADS):
        x = p_qk[:, j * LANES:(j + 1) * LANES]
        ms = ms_all[j * tm:(j + 1) * tm, :]
        g = gq_ref[...] if j < ATT_HEADS else gk_ref[...]
        y = x * lax.rsqrt(ms + EPS) * g
        sw = jnp.where(hi16, pltpu.roll(y, 16, 1), pltpu.roll(y, LANES - 16, 1))
        y = y * cosv + sw * sinv
        if j < ATT_HEADS:
            yt = (y * Q_SCALE).T
            qz_ref[0, j, 0] = jnp.where(first, yt, 0.0).astype(BF16)
            qz_ref[0, j, 1] = jnp.where(first, 0.0, yt).astype(BF16)
        else:
            k_ref[0, j - ATT_HEADS] = y.astype(BF16)


def _inproj_kernel(h_ref, mod_ref, g_ref, w_ref, wab_ref, cos_ref, sin_ref, gq_ref, gk_ref,
                   qz_ref, k_ref, v_ref, gdn_ref, z_ref, ab_ref):
    x = h_ref[0]
    ms = jnp.mean(x * x, axis=-1, keepdims=True)
    y = x * lax.rsqrt(ms + EPS) * g_ref[...]
    shift = mod_ref[0, 0, 0:1, :]
    scale = mod_ref[0, 0, 1:2, :]
    u = (y * (1.0 + scale) + shift).astype(BF16)
    _qk_norm_rope(_dot(u, w_ref[:, 0:2 * ATT_WIDTH]), cos_ref, sin_ref, gq_ref, gk_ref, qz_ref, k_ref)
    vv = _dot(u, w_ref[:, 2 * ATT_WIDTH:3 * ATT_WIDTH])
    for hh in range(ATT_HEADS):
        v_ref[0, hh] = vv[:, hh * LANES:(hh + 1) * LANES].T.astype(BF16)
    off = 3 * ATT_WIDTH
    gdn_ref[0] = _dot(u, w_ref[:, off:off + 3 * GDN_WIDTH])
    z_ref[0] = _dot(u, w_ref[:, off + 3 * GDN_WIDTH:off + 4 * GDN_WIDTH])
    ab_ref[0] = _dot(u, wab_ref[...])


def _inproj(h, modtab, g, w_main, w_ab, cos_t, sin_t, gq, gk, n_lat_blocks):
    B, S, _ = h.shape
    tm = ROW_TILE
    row = lambda b, i: (b, i, 0)
    vec = lambda b, i: (0, 0)
    return pl.pallas_call(
        _inproj_kernel,
        out_shape=(jax.ShapeDtypeStruct((B, ATT_HEADS, 2, LANES, S), BF16),
                   jax.ShapeDtypeStruct((B, ATT_HEADS, S, LANES), BF16),
                   jax.ShapeDtypeStruct((B, ATT_HEADS, LANES, S), BF16),
                   jax.ShapeDtypeStruct((B, S, 3 * GDN_WIDTH), F32),
                   jax.ShapeDtypeStruct((B, S, GDN_WIDTH), F32),
                   jax.ShapeDtypeStruct((B, S, LANES), F32)),
        grid=(B, S // tm),
        in_specs=[pl.BlockSpec((1, tm, D_MODEL), row),
                  pl.BlockSpec((1, 1, 8, D_MODEL), lambda b, i: (b, (i >= n_lat_blocks).astype(jnp.int32), 0, 0)),
                  pl.BlockSpec((1, D_MODEL), lambda b, i: (0, 0)),
                  pl.BlockSpec((D_MODEL, IN_MAIN), vec),
                  pl.BlockSpec((D_MODEL, LANES), vec),
                  pl.BlockSpec((tm, LANES), lambda b, i: (i, 0)),
                  pl.BlockSpec((tm, LANES), lambda b, i: (i, 0)),
                  pl.BlockSpec((1, LANES), vec),
                  pl.BlockSpec((1, LANES), vec)],
        out_specs=(pl.BlockSpec((1, ATT_HEADS, 2, LANES, tm), lambda b, i: (b, 0, 0, 0, i)),
                   pl.BlockSpec((1, ATT_HEADS, tm, LANES), lambda b, i: (b, 0, i, 0)),
                   pl.BlockSpec((1, ATT_HEADS, LANES, tm), lambda b, i: (b, 0, 0, i)),
                   pl.BlockSpec((1, tm, 3 * GDN_WIDTH), row),
                   pl.BlockSpec((1, tm, GDN_WIDTH), row),
                   pl.BlockSpec((1, tm, LANES), row)),
        compiler_params=_cparams(("parallel", "parallel")),
        name="inproj",
    )(h, modtab, g, w_main, w_ab, cos_t, sin_t, gq, gk)


def _attn_kernel(lam_ref, qz_ref, k_ref, v_ref, g_ref, *rest, aliased, tk, n_kv):
    if aliased:
        rest = rest[1:]
    o_ref, s_sc, mt_sc, m_sc, l_sc, acc_sc = rest
    m_sc[...] = jnp.full(m_sc.shape, -jnp.inf, F32)
    l_sc[...] = jnp.zeros(l_sc.shape, F32)
    acc_sc[...] = jnp.zeros(acc_sc.shape, F32)

    def rows(j):
        if isinstance(j, int):
            return pl.ds(j * tk, tk)
        return pl.ds(pl.multiple_of(j * tk, tk), tk)

    def qk_tile(j, slot, ps=(0, 1)):
        kt = k_ref[0, 0, rows(j), :]
        for p in ps:
            s = _dot(kt, qz_ref[0, 0, p])
            s_sc[slot, p] = s
            mt_sc[slot, p] = jnp.max(s, axis=0, keepdims=True)

    def pv_tile(j, slot, ps=(0, 1)):
        for p in ps:
            m_prev = m_sc[p]
            m_new = jnp.maximum(m_prev, mt_sc[slot, p])
            alpha = jnp.exp2(m_prev - m_new)
            lsum = None
            acc = None
            for c in range(tk // KV_CHUNK):
                cs = slice(c * KV_CHUNK, (c + 1) * KV_CHUNK)
                pe = jnp.exp2(s_sc[slot, p, cs, :] - m_new)
                ps = jnp.sum(pe, axis=0, keepdims=True)
                pv = _dot(v_ref[0, 0, j, :, cs], pe.astype(BF16))
                lsum = ps if lsum is None else lsum + ps
                acc = pv if acc is None else acc + pv
            l_sc[p] = alpha * l_sc[p] + lsum
            acc_sc[p] = alpha * acc_sc[p] + acc
            m_sc[p] = m_new

    def fused_tile(jq, slot_q, jp, slot_p):
        m_new = [jnp.maximum(m_sc[p], mt_sc[slot_p, p]) for p in range(2)]
        alpha = [jnp.exp2(m_sc[p] - m_new[p]) for p in range(2)]
        mx, lsum, acc = [None, None], [None, None], [None, None]
        for c in range(tk // KV_CHUNK):
            cs = slice(c * KV_CHUNK, (c + 1) * KV_CHUNK)
            if isinstance(jq, int):
                kr = pl.ds(jq * tk + c * KV_CHUNK, KV_CHUNK)
            else:
                kr = pl.ds(pl.multiple_of(jq * tk + c * KV_CHUNK, KV_CHUNK), KV_CHUNK)
            kc = k_ref[0, 0, kr, :]
            vc = v_ref[0, 0, jp, :, cs]
            for p in range(2):
                s = _dot(kc, qz_ref[0, 0, p])
                s_sc[slot_q, p, cs, :] = s
                cm = jnp.max(s, axis=0, keepdims=True)
                mx[p] = cm if mx[p] is None else jnp.maximum(mx[p], cm)
                pe = jnp.exp2(s_sc[slot_p, p, cs, :] - m_new[p])
                ps = jnp.sum(pe, axis=0, keepdims=True)
                pv = _dot(vc, pe.astype(BF16))
                lsum[p] = ps if lsum[p] is None else lsum[p] + ps
                acc[p] = pv if acc[p] is None else acc[p] + pv
        for p in range(2):
            mt_sc[slot_q, p] = mx[p]
            l_sc[p] = alpha[p] * l_sc[p] + lsum[p]
            acc_sc[p] = alpha[p] * acc_sc[p] + acc[p]
            m_sc[p] = m_new[p]

    qk_tile(0, 0)
    n_pairs = (n_kv - 1) // 2

    def pair(jj, carry):
        j = 2 * jj
        fused_tile(j + 1, 1, j, 0)
        fused_tile(j + 2, 0, j + 1, 1)
        return carry

    if n_pairs > 0:
        lax.fori_loop(0, n_pairs, pair, 0)
    j = 2 * n_pairs
    if j == n_kv - 1:
        pv_tile(j, 0)
    else:
        qk_tile(j + 1, 1)
        pv_tile(j, 0)
        pv_tile(j + 1, 1)

    o = acc_sc[0] / l_sc[0] - lam_ref[0] * (acc_sc[1] / l_sc[1])
    ms = jnp.mean(o * o, axis=0, keepdims=True)
    o_ref[0] = ((o * lax.rsqrt(ms + EPS)).T * g_ref[...]).astype(BF16)


def _attention(lam, qz, k, vt, g, *, tq, tk, q_blk0, n_q, kv_blk0, n_kv, prev_out=None, name="attn_ctx"):
    B, H, _, _, S = qz.shape
    aliased = prev_out is not None
    skv = tk * n_kv
    in_specs = [pl.BlockSpec(memory_space=pltpu.SMEM),
                pl.BlockSpec((1, 1, 2, LANES, tq), lambda b, h, i: (b, h, 0, 0, q_blk0 + i)),
                pl.BlockSpec((1, 1, skv, LANES), lambda b, h, i: (b, h, kv_blk0, 0)),
                pl.BlockSpec((1, 1, n_kv, LANES, tk), lambda b, h, i: (b, h, 0, 0, 0)),
                pl.BlockSpec((1, LANES), lambda b, h, i: (0, 0))]
    args = [lam, qz, k, vt, g]
    aliases = {}
    if aliased:
        in_specs.append(pl.BlockSpec(memory_space=pl.ANY))
        args.append(prev_out)
        aliases = {5: 0}
    return pl.pallas_call(
        functools.partial(_attn_kernel, aliased=aliased, tk=tk, n_kv=n_kv),
        out_shape=jax.ShapeDtypeStruct((B, S, ATT_WIDTH), BF16),
        grid=(B, H, n_q),
        in_specs=in_specs,
        out_specs=pl.BlockSpec((1, tq, LANES), lambda b, h, i: (b, q_blk0 + i, h)),
        scratch_shapes=[pltpu.VMEM((2, 2, tk, tq), F32), pltpu.VMEM((2, 2, 1, tq), F32),
                        pltpu.VMEM((2, 1, tq), F32), pltpu.VMEM((2, 1, tq), F32),
                        pltpu.VMEM((2, LANES, tq), F32)],
        input_output_aliases=aliases,
        compiler_params=_cparams(("parallel", "parallel", "arbitrary")),
        name=name,
    )(*args)


def _attention_lat(lam, qz, k, vt, g, L):
    B, H, _, S = vt.shape
    tq = _pick_tile(L, (512, 256))
    tk = _pick_tile(S, (1280, 640, 256))
    n_kv = S // tk
    vt_tiles = vt.reshape(B, H, LANES, n_kv, tk).transpose(0, 1, 3, 2, 4)
    out0 = jnp.zeros((B, S, ATT_WIDTH), BF16)
    return _attention(lam, qz, k, vt_tiles, g, tq=tq, tk=tk, q_blk0=0, n_q=L // tq, kv_blk0=0, n_kv=n_kv,
                      prev_out=out0, name="attn_lat")


def _attention_ctx(lam, qz, k, vt, g, L, prev_out):
    B, H, _, S = vt.shape
    Lc = S - L
    vt_ctx = vt[:, :, :, L:].reshape(B, H, 1, LANES, Lc)
    return _attention(lam, qz, k, vt_ctx, g, tq=Lc, tk=Lc, q_blk0=L // Lc, n_q=1, kv_blk0=L // Lc, n_kv=1,
                      prev_out=prev_out)


def _gdnprep_kernel(x_ref, prev_ref, next_ref, cw_ref, ab_ref, alog_ref, dtb_ref,
                    q_ref, k_ref, v_ref, gate_ref, ext_sc, *, nb_lat, nb_all):
    i = pl.program_id(1)
    tm = x_ref.shape[1]
    first = jnp.logical_or(i == 0, i == nb_lat)
    last = jnp.logical_or(i == nb_lat - 1, i == nb_all - 1)
    keep_prev = jnp.where(first, 0.0, 1.0)
    keep_next = jnp.where(last, 0.0, 1.0)
    ext_sc[0:8, :] = prev_ref[0] * keep_prev
    ext_sc[8:8 + tm, :] = x_ref[0]
    ext_sc[8 + tm:16 + tm, :] = next_ref[0] * keep_next
    acc = None
    for j in range(CONV_W):
        term = ext_sc[pl.ds(8 + j - CONV_W // 2, tm), :] * cw_ref[j:j + 1, :]
        acc = term if acc is None else acc + term
    y = acc * _sigmoid(acc)
    for part, ref in ((0, q_ref), (1, k_ref)):
        for hh in range(GDN_HEADS):
            lo = part * GDN_WIDTH + hh * GDN_HEAD_DIM
            t = y[:, lo:lo + GDN_HEAD_DIM]
            ref[0, :, hh * GDN_HEAD_DIM:(hh + 1) * GDN_HEAD_DIM] = (
                t * lax.rsqrt(jnp.sum(t * t, axis=-1, keepdims=True) + EPS))
    v_ref[0] = y[:, 2 * GDN_WIDTH:3 * GDN_WIDTH]
    ab = ab_ref[0]
    xs = ab + dtb_ref[...]
    sp = jnp.maximum(xs, 0.0) + jnp.log(1.0 + jnp.exp(-jnp.abs(xs)))
    g = -jnp.exp(alog_ref[...]) * sp
    lane = lax.broadcasted_iota(jnp.int32, ab.shape, 1)
    gate_ref[0] = jnp.where(lane < 2 * GDN_HEADS, g, _sigmoid(ab))


def _gdnprep(p_gdn, conv_w8, ab, alog_row, dtb_row, n_lat_blocks):
    B, S, W = p_gdn.shape
    tm = ROW_TILE
    nb = S // tm
    r8 = tm // 8
    row = lambda b, i: (b, i, 0)
    return pl.pallas_call(
        functools.partial(_gdnprep_kernel, nb_lat=n_lat_blocks, nb_all=nb),
        out_shape=(jax.ShapeDtypeStruct((B, S, GDN_WIDTH), F32),
                   jax.ShapeDtypeStruct((B, S, GDN_WIDTH), F32),
                   jax.ShapeDtypeStruct((B, S, GDN_WIDTH), F32),
                   jax.ShapeDtypeStruct((B, S, LANES), F32)),
        grid=(B, nb),
        in_specs=[pl.BlockSpec((1, tm, W), row),
                  pl.BlockSpec((1, 8, W), lambda b, i: (b, jnp.maximum(i * r8 - 1, 0), 0)),
                  pl.BlockSpec((1, 8, W), lambda b, i: (b, jnp.minimum((i + 1) * r8, nb * r8 - 1), 0)),
                  pl.BlockSpec((8, W), lambda b, i: (0, 0)),
                  pl.BlockSpec((1, tm, LANES), row),
                  pl.BlockSpec((1, LANES), lambda b, i: (0, 0)),
                  pl.BlockSpec((1, LANES), lambda b, i: (0, 0))],
        out_specs=(pl.BlockSpec((1, tm, GDN_WIDTH), row),
                   pl.BlockSpec((1, tm, GDN_WIDTH), row),
                   pl.BlockSpec((1, tm, GDN_WIDTH), row),
                   pl.BlockSpec((1, tm, LANES), row)),
        scratch_shapes=[pltpu.VMEM((tm + 16, W), F32)],
        compiler_params=_cparams(("parallel", "parallel")),
        name="gdnprep",
    )(p_gdn, p_gdn, p_gdn, conv_w8, ab, alog_row, dtb_row)


def _gdnchunk_kernel(q_ref, k_ref, v_ref, gate_ref, qg_ref, w_ref, kd_ref, u_ref, aq_ref, eg_ref):
    tm = q_ref.shape[1]
    nc = tm // GDN_CHUNK
    gates = gate_ref[0]
    ri = lax.broadcasted_iota(jnp.int32, (tm, tm), 0)
    ci = lax.broadcasted_iota(jnp.int32, (tm, tm), 1)
    same = (ri >> 6) == (ci >> 6)
    eye = ri == ci
    eye_f = jnp.where(eye, 1.0, 0.0).astype(F32)

    g1 = gates.astype(BF16)
    r1 = gates - g1.astype(F32)
    g2 = r1.astype(BF16)
    g3 = (r1 - g2.astype(F32)).astype(BF16)
    gparts = jnp.concatenate([g1, g2, g3], axis=1)

    def seg_sum(mask):
        r = _dot(jnp.where(mask, 1.0, 0.0).astype(BF16), gparts)
        return (r[:, :LANES] + r[:, LANES:2 * LANES]) + r[:, 2 * LANES:]

    tot = seg_sum(same)

    heads = []
    for hh in range(GDN_HEADS):
        sl = slice(hh * GDN_HEAD_DIM, (hh + 1) * GDN_HEAD_DIM)
        k = k_ref[0, :, sl]
        k16 = k.astype(BF16)
        q = q_ref[0, :, sl] * (GDN_HEAD_DIM ** -0.5)
        heads.append((sl, k, q, _dot_nt(k16, k16), _dot_nt(q.astype(BF16), k16)))

    chains = []
    for d in range(2):
        incl = jnp.logical_and(same, (ci <= ri) if d == 0 else (ci >= ri))
        strict = jnp.logical_and(incl, jnp.logical_not(eye))
        gc = seg_sum(incl)
        gc_t = gc.T
        for hh in range(GDN_HEADS):
            ln = d * GDN_HEADS + hh
            gcol = gc[:, ln:ln + 1]
            bcol = gates[:, 2 * GDN_HEADS + ln:2 * GDN_HEADS + ln + 1]
            tcol = tot[:, ln:ln + 1]
            diff = gcol - gc_t[ln:ln + 1, :]
            decay = jnp.where(incl, jnp.exp(jnp.where(incl, diff, 0.0)), 0.0)
            a = jnp.where(strict, heads[hh][3] * bcol * decay, 0.0)
            aqk = heads[hh][4] * decay
            chains.append(dict(d=d, hh=hh, x=-a, t=eye_f - a, aqk=aqk, gcol=gcol, bcol=bcol, tcol=tcol))

    for _ in range(5):
        for ch in chains:
            x16 = ch["x"].astype(BF16)
            ch["x"] = _dot(x16, x16)
        for ch in chains:
            ch["t"] = ch["t"] + _dot(ch["t"].astype(BF16), ch["x"].astype(BF16))

    for ch in chains:
        d, hh, gcol, bcol, tcol = ch["d"], ch["hh"], ch["gcol"], ch["bcol"], ch["tcol"]
        sl, k, q = heads[hh][0], heads[hh][1], heads[hh][2]
        t16 = ch["t"].astype(BF16)
        egc = jnp.exp(gcol)
        kb = k * bcol
        rhs = jnp.concatenate([(v_ref[0, :, sl] * bcol).astype(BF16), (kb * egc).astype(BF16)], axis=1)
        uw = _dot(t16, rhs)
        u_ref[0, d, :, sl] = uw[:, :GDN_HEAD_DIM]
        w_ref[0, d, :, sl] = uw[:, GDN_HEAD_DIM:].astype(BF16)
        kd_ref[0, d, :, sl] = (k * jnp.exp(tcol - gcol)).astype(BF16)
        qg_ref[0, d, :, sl] = (q * egc).astype(BF16)
        for cc in range(nc):
            rs = slice(cc * GDN_CHUNK, (cc + 1) * GDN_CHUNK)
            aq_ref[0, d, rs, hh * GDN_CHUNK:(hh + 1) * GDN_CHUNK] = ch["aqk"][rs, rs].astype(BF16)
            eg_ref[0, d, cc, hh:hh + 1, :] = jnp.broadcast_to(
                jnp.exp(tcol[cc * GDN_CHUNK:cc * GDN_CHUNK + 1, :]), (1, LANES))


def _gdnchunk(gq, gk, gv, gates):
    B, S, W = gq.shape
    tm = ROW_TILE
    nc = tm // GDN_CHUNK
    row = lambda b, i: (b, i, 0)
    drow = lambda b, i: (b, 0, i, 0)
    big = lambda dt: jax.ShapeDtypeStruct((B, 2, S, W), dt)
    return pl.pallas_call(
        _gdnchunk_kernel,
        out_shape=(big(BF16), big(BF16), big(BF16), big(F32),
                   jax.ShapeDtypeStruct((B, 2, S, GDN_HEADS * GDN_CHUNK), BF16),
                   jax.ShapeDtypeStruct((B, 2, S // GDN_CHUNK, GDN_HEADS, LANES), F32)),
        grid=(B, S // tm),
        in_specs=[pl.BlockSpec((1, tm, W), row), pl.BlockSpec((1, tm, W), row),
                  pl.BlockSpec((1, tm, W), row), pl.BlockSpec((1, tm, LANES), row)],
        out_specs=(pl.BlockSpec((1, 2, tm, W), drow), pl.BlockSpec((1, 2, tm, W), drow),
                   pl.BlockSpec((1, 2, tm, W), drow), pl.BlockSpec((1, 2, tm, W), drow),
                   pl.BlockSpec((1, 2, tm, GDN_HEADS * GDN_CHUNK), drow),
                   pl.BlockSpec((1, 2, nc, GDN_HEADS, LANES), lambda b, i: (b, 0, i, 0, 0))),
        compiler_params=_cparams(("parallel", "parallel")),
        name="gdnchunk",
    )(gq, gk, gv, gates)


def _gdnscan_kernel(qg0, w0, kd0, u0, aq0, eg0, qg1, w1, kd1, u1, aq1, eg1, of_ref, ob_ref, s_sc):
    @pl.when(pl.program_id(1) == 0)
    def _():
        s_sc[...] = jnp.zeros(s_sc.shape, F32)

    dirs = ((qg0, w0, kd0, u0, aq0, eg0, of_ref), (qg1, w1, kd1, u1, aq1, eg1, ob_ref))
    chains = [(d, hh) for d in range(2) for hh in range(GDN_HEADS)]
    st = {ch: s_sc[ch[0], ch[1]] for ch in chains}
    for step in range(SCAN_GROUP):
        st16, vn16, qs = {}, {}, {}
        for d, hh in chains:
            st16[d, hh] = st[d, hh].astype(BF16)
        for d, hh in chains:
            qg, w, kd, u, aq, eg, o_ref = dirs[d]
            cc = step if d == 0 else SCAN_GROUP - 1 - step
            rs = slice(cc * GDN_CHUNK, (cc + 1) * GDN_CHUNK)
            sl = slice(hh * GDN_HEAD_DIM, (hh + 1) * GDN_HEAD_DIM)
            vn16[d, hh] = (u[0, 0, rs, sl] - _dot(w[0, 0, rs, sl], st16[d, hh])).astype(BF16)
            qs[d, hh] = _dot(qg[0, 0, rs, sl], st16[d, hh])
        for d, hh in chains:
            qg, w, kd, u, aq, eg, o_ref = dirs[d]
            cc = step if d == 0 else SCAN_GROUP - 1 - step
            rs = slice(cc * GDN_CHUNK, (cc + 1) * GDN_CHUNK)
            sl = slice(hh * GDN_HEAD_DIM, (hh + 1) * GDN_HEAD_DIM)
            o_ref[0, rs, sl] = qs[d, hh] + _dot(aq[0, 0, rs, hh * GDN_CHUNK:(hh + 1) * GDN_CHUNK], vn16[d, hh])
            st[d, hh] = st[d, hh] * eg[0, 0, cc, hh:hh + 1, :] + _dot_tn(kd[0, 0, rs, sl], vn16[d, hh])
    for d, hh in chains:
        s_sc[d, hh] = st[d, hh]


def _gdnscan(qg, w, kd, u, aq, eg, n_lat_chunks, n_ctx_chunks):
    B, _, S, W = qg.shape
    G = SCAN_GROUP
    R = GDN_CHUNK * G
    n = S // R
    n_lat, n_ctx = n_lat_chunks // G, n_ctx_chunks // G
    assert n_lat * G == n_lat_chunks and n_ctx * G == n_ctx_chunks

    def fwd_blk(i):
        return jnp.where(i < n_ctx, n_lat + i, i - n_ctx)

    def bwd_blk(i):
        return jnp.where(i < n_ctx, n_lat + n_ctx - 1 - i, n - 1 - i)

    def specs(d, blk_of):
        big = pl.BlockSpec((1, 1, R, W), lambda b, i: (b, d, blk_of(i), 0))
        return [big, big, big, big,
                pl.BlockSpec((1, 1, R, GDN_HEADS * GDN_CHUNK), lambda b, i: (b, d, blk_of(i), 0)),
                pl.BlockSpec((1, 1, G, GDN_HEADS, LANES), lambda b, i: (b, d, blk_of(i), 0, 0))]

    return pl.pallas_call(
        _gdnscan_kernel,
        out_shape=(jax.ShapeDtypeStruct((B, S, W), F32), jax.ShapeDtypeStruct((B, S, W), F32)),
        grid=(B, n),
        in_specs=specs(0, fwd_blk) + specs(1, bwd_blk),
        out_specs=(pl.BlockSpec((1, R, W), lambda b, i: (b, fwd_blk(i), 0)),
                   pl.BlockSpec((1, R, W), lambda b, i: (b, bwd_blk(i), 0))),
        scratch_shapes=[pltpu.VMEM((2, GDN_HEADS, GDN_HEAD_DIM, GDN_HEAD_DIM), F32)],
        compiler_params=_cparams(("parallel", "arbitrary")),
        name="gdnscan",
    )(qg, w, kd, u, aq, eg, qg, w, kd, u, aq, eg)


def _mixout_kernel(oa_ref, of_ref, ob_ref, z_ref, h_ref, mod_ref, gg_ref, wo_ref, gf_ref, rw_ref, rb_ref,
                   hn_ref, v_ref, te_ref, tg_ref):
    og = of_ref[0] + ob_ref[0]
    z = z_ref[0]
    parts = [oa_ref[0]]
    for hh in range(GDN_HEADS):
        sl = slice(hh * GDN_HEAD_DIM, (hh + 1) * GDN_HEAD_DIM)
        t = og[:, sl]
        t = t * lax.rsqrt(jnp.mean(t * t, axis=-1, keepdims=True) + EPS) * gg_ref[...]
        zz = z[:, sl]
        parts.append((t * (zz * _sigmoid(zz))).astype(BF16))
    mix_in = jnp.concatenate(parts, axis=-1)
    mix = _dot(mix_in, wo_ref[...])
    hn = h_ref[0] + mod_ref[0, 0, 2:3, :] * mix
    hn_ref[0] = hn
    y = hn * lax.rsqrt(jnp.mean(hn * hn, axis=-1, keepdims=True) + EPS) * gf_ref[...]
    v = y * (1.0 + mod_ref[0, 0, 4:5, :]) + mod_ref[0, 0, 3:4, :]
    v_ref[0] = v.astype(BF16)
    v_hi = v.astype(BF16)
    v_lo = (v - v_hi.astype(F32)).astype(BF16)
    logits = (_dot(v_hi, rw_ref[0]) + _dot(v_lo, rw_ref[0]) + _dot(v_hi, rw_ref[1])) + rb_ref[...]
    lane = lax.broadcasted_iota(jnp.int32, logits.shape, 1)
    cur = logits
    vals, idxs = [], []
    for _ in range(TOP_K):
        m = jnp.max(cur, axis=-1, keepdims=True)
        idx = jnp.min(jnp.where(cur == m, lane, LANES), axis=-1, keepdims=True)
        vals.append(m)
        idxs.append(idx)
        cur = jnp.where(lane == idx, -jnp.inf, cur)
    es = [jnp.exp(vv - vals[0]) for vv in vals]
    inv = 1.0 / (es[0] + es[1] + es[2] + es[3])
    te = jnp.zeros(logits.shape, jnp.int32)
    tg = jnp.zeros(logits.shape, F32)
    for kk in range(TOP_K):
        te = jnp.where(lane == kk, idxs[kk], te)
        tg = jnp.where(lane == kk, es[kk] * inv, tg)
    te_ref[0] = te
    tg_ref[0] = tg


def _mixout(o_att, o_f, o_b, z, h, modtab, gg, w_out, gf, rw, rb, n_lat_blocks):
    B, S, _ = h.shape
    tm = ROW_TILE
    row = lambda b, i: (b, i, 0)
    const = lambda b, i: (0, 0)
    return pl.pallas_call(
        _mixout_kernel,
        out_shape=(jax.ShapeDtypeStruct((B, S, D_MODEL), F32),
                   jax.ShapeDtypeStruct((B, S, D_MODEL), BF16),
                   jax.ShapeDtypeStruct((B, S, LANES), jnp.int32),
                   jax.ShapeDtypeStruct((B, S, LANES), F32)),
        grid=(B, S // tm),
        in_specs=[pl.BlockSpec((1, tm, ATT_WIDTH), row),
                  pl.BlockSpec((1, tm, GDN_WIDTH), row),
                  pl.BlockSpec((1, tm, GDN_WIDTH), row),
                  pl.BlockSpec((1, tm, GDN_WIDTH), row),
                  pl.BlockSpec((1, tm, D_MODEL), row),
                  pl.BlockSpec((1, 1, 8, D_MODEL), lambda b, i: (b, (i >= n_lat_blocks).astype(jnp.int32), 0, 0)),
                  pl.BlockSpec((1, LANES), const),
                  pl.BlockSpec((D_MODEL, D_MODEL), const),
                  pl.BlockSpec((1, D_MODEL), const),
                  pl.BlockSpec((2, D_MODEL, LANES), lambda b, i: (0, 0, 0)),
                  pl.BlockSpec((1, LANES), const)],
        out_specs=(pl.BlockSpec((1, tm, D_MODEL), row),
                   pl.BlockSpec((1, tm, D_MODEL), row),
                   pl.BlockSpec((1, tm, LANES), row),
                   pl.BlockSpec((1, tm, LANES), row)),
        compiler_params=_cparams(("parallel", "parallel")),
        name="mixout",
    )(o_att, o_f, o_b, z, h, modtab, gg, w_out, gf, rw, rb)


def _expert_kernel(be_ref, nv_ref, x_ref, wgu_ref, bgu_ref, wd_ref, bd_ref, y_ref, wgu_sc, wd_sc):
    i = pl.program_id(0)
    new_expert = jnp.logical_or(i == 0, be_ref[i] != be_ref[jnp.maximum(i - 1, 0)])

    @pl.when(new_expert)
    def _():
        wgu_sc[...] = wgu_ref[0, 0].astype(BF16)
        wd_sc[...] = wd_ref[0, 0].astype(BF16)

    @pl.when(nv_ref[i] > 0)
    def _():
        gu = _dot(x_ref[...], wgu_sc[...]) + bgu_ref[0, 0]
        g_ = jnp.minimum(gu[:, :D_EXPERT], SWIGLU_LIMIT)
        up = jnp.clip(gu[:, D_EXPERT:], -SWIGLU_LIMIT, SWIGLU_LIMIT)
        glu = g_ * _sigmoid(SWIGLU_ALPHA * g_)
        act = ((up + 1.0) * glu).astype(BF16)
        y_ref[...] = (_dot(act, wd_sc[...]) + bd_ref[0, 0]).astype(y_ref.dtype)

    @pl.when(nv_ref[i] == 0)
    def _():
        y_ref[...] = jnp.zeros(y_ref.shape, y_ref.dtype)


def _experts(block_e, n_valid, x_sorted, wgu, bgu, wd, bd, layer):
    n_slots = x_sorted.shape[0]
    nb = n_slots // MOE_BLOCK
    grid_spec = pltpu.PrefetchScalarGridSpec(
        num_scalar_prefetch=2,
        grid=(nb,),
        in_specs=[pl.BlockSpec((MOE_BLOCK, D_MODEL), lambda i, be, nv: (i, 0)),
                  pl.BlockSpec((1, 1, D_MODEL, 2 * D_EXPERT), lambda i, be, nv: (layer, be[i], 0, 0)),
                  pl.BlockSpec((1, 1, 1, 2 * D_EXPERT), lambda i, be, nv: (layer, be[i], 0, 0)),
                  pl.BlockSpec((1, 1, D_EXPERT, D_MODEL), lambda i, be, nv: (layer, be[i], 0, 0)),
                  pl.BlockSpec((1, 1, 1, D_MODEL), lambda i, be, nv: (layer, be[i], 0, 0))],
        out_specs=pl.BlockSpec((MOE_BLOCK, D_MODEL), lambda i, be, nv: (i, 0)),
        scratch_shapes=[pltpu.VMEM((D_MODEL, 2 * D_EXPERT), BF16), pltpu.VMEM((D_EXPERT, D_MODEL), BF16)],
    )
    return pl.pallas_call(
        _expert_kernel,
        out_shape=jax.ShapeDtypeStruct((n_slots, D_MODEL), BF16),
        grid_spec=grid_spec,
        compiler_params=pltpu.CompilerParams(dimension_semantics=("arbitrary",),
                                             vmem_limit_bytes=EXPERT_VMEM_LIMIT),
        name="experts",
    )(block_e, n_valid, x_sorted, wgu, bgu, wd, bd)


def _moe_plan(top_e, top_g):
    T = top_e.shape[0]
    n_assign = T * TOP_K
    n_blocks = -(-n_assign // MOE_BLOCK) + N_EXPERTS
    n_slots = n_blocks * MOE_BLOCK
    i32 = jnp.int32
    flat_e = top_e.reshape(-1)
    gate_flat = top_g.reshape(-1)
    assert T < NO_TOKEN
    gate_bits = lax.bitcast_convert_type(gate_flat.astype(BF16), jnp.uint16).astype(jnp.uint32)
    tok_pack = (gate_bits << 16) | (jnp.arange(n_assign, dtype=jnp.uint32) // TOP_K)
    _, sorted_pack = lax.sort((flat_e, tok_pack), num_keys=1, is_stable=True)
    e_ids = jnp.arange(N_EXPERTS, dtype=i32)
    is_e = flat_e[:, None] == e_ids[None, :]
    counts = jnp.sum(is_e, axis=0, dtype=i32)
    start = jnp.cumsum(counts) - counts
    padded = (counts + MOE_BLOCK - 1) // MOE_BLOCK * MOE_BLOCK
    pad_end = jnp.cumsum(padded)
    pad_start = pad_end - padded
    blk0 = jnp.arange(n_blocks, dtype=i32) * MOE_BLOCK
    block_e = jnp.minimum(jnp.sum(pad_end[None, :] <= blk0[:, None], axis=1, dtype=i32), N_EXPERTS - 1)
    off = (blk0 - pad_start[block_e])[:, None] + jnp.arange(MOE_BLOCK, dtype=i32)[None, :]
    valid = off < counts[block_e][:, None]
    valid_flat = valid.reshape(-1)
    slot_pack = jnp.where(valid_flat,
                          sorted_pack[jnp.clip(off + start[block_e][:, None], 0, n_assign - 1).reshape(-1)],
                          jnp.uint32(NO_TOKEN))
    slot_tok = jnp.where(valid_flat, (slot_pack & NO_TOKEN).astype(i32), jnp.arange(n_slots, dtype=i32) % T)
    n_valid = jnp.sum(valid, axis=1, dtype=i32)

    n_tiles = T // COMBINE_TILE
    cnt = jnp.sum(top_e.reshape(n_tiles, COMBINE_TILE * TOP_K)[:, :, None] == e_ids[None, None, :], axis=1, dtype=i32)
    run_start = pad_start[None, :] + jnp.cumsum(cnt, axis=0) - cnt
    q_first = run_start // COMBINE_CHUNK
    n_ch = jnp.where(cnt > 0, (run_start + cnt - 1) // COMBINE_CHUNK - q_first + 1, 0)
    ch_end = jnp.cumsum(n_ch, axis=1)
    ch_off = ch_end - n_ch
    j = jnp.arange(COMBINE_MAX_CHUNKS, dtype=i32)
    e_j = jnp.minimum(jnp.sum(ch_end[:, None, :] <= j[None, :, None], axis=2, dtype=i32), N_EXPERTS - 1)
    used = j[None, :] < ch_end[:, -1:]
    base = jnp.sum(jnp.where(e_j[:, :, None] == e_ids[None, None, :], (q_first - ch_off)[:, None, :], 0),
                   axis=2, dtype=i32)
    chunk_id = jnp.where(used, base + j[None, :], 0)
    slot_pack = slot_pack.reshape(n_slots // COMBINE_CHUNK, COMBINE_CHUNK)
    row_pack = jnp.where(used[:, :, None], slot_pack[chunk_id], jnp.uint32(NO_TOKEN))
    row_pack = lax.bitcast_convert_type(row_pack, i32).reshape(n_tiles, 1, COMBINE_MAX_CHUNKS * COMBINE_CHUNK)
    return slot_tok, block_e, n_valid, chunk_id, row_pack


def _combine_kernel(cid_ref, tok_ref, h_ref, mod_ref, y_hbm, o_ref, ybuf, sem, *, lat_only, tiles_per_batch,
                    n_lat_tiles):
    i = pl.program_id(0)
    n = pl.num_programs(0)
    slot = lax.rem(i, 2)

    def chunk_copy(c, s, jj):
        return pltpu.make_async_copy(
            y_hbm.at[pl.ds(pl.multiple_of(c * COMBINE_CHUNK, COMBINE_CHUNK), COMBINE_CHUNK)],
            ybuf.at[s, pl.ds(jj * COMBINE_CHUNK, COMBINE_CHUNK)], sem.at[s])

    def start_tile(t, s):
        for jj in range(COMBINE_MAX_CHUNKS):
            chunk_copy(cid_ref[t, jj], s, jj).start(priority=jj % 2)

    @pl.when(i == 0)
    def _():
        start_tile(0, 0)

    @pl.when(i + 1 < n)
    def _():
        start_tile(i + 1, 1 - slot)

    for jj in range(COMBINE_MAX_CHUNKS):
        chunk_copy(0, slot, jj).wait()

    packed = tok_ref[0]
    tok = packed & NO_TOKEN
    gate = lax.bitcast_convert_type(packed & ~NO_TOKEN, F32)
    t_ids = i * COMBINE_TILE + lax.broadcasted_iota(jnp.int32, (COMBINE_TILE, tok.shape[1]), 0)
    sel = jnp.where(tok == t_ids, gate, 0.0).astype(BF16)
    y = _dot(sel, ybuf[slot])
    out = h_ref[...] + mod_ref[0, 0, 5:6, :] * y
    if lat_only:
        @pl.when(lax.rem(i, tiles_per_batch) < n_lat_tiles)
        def _():
            o_ref[...] = out
    else:
        o_ref[...] = out


def _combine(chunk_id, row_tok, y_sorted, h, modtab, n_lat_blocks, lat_only):
    B, S, _ = h.shape
    T = B * S
    ct = COMBINE_TILE
    tiles_per_batch = S // ct
    n_lat_tiles = n_lat_blocks * ROW_TILE // ct
    n_rows = COMBINE_MAX_CHUNKS * COMBINE_CHUNK
    if lat_only:
        out_rows = B * n_lat_tiles * ct
        out_map = lambda i, cid: ((i // tiles_per_batch) * n_lat_tiles
                                  + jnp.minimum(lax.rem(i, tiles_per_batch), n_lat_tiles - 1), 0)
    else:
        out_rows = T
        out_map = lambda i, cid: (i, 0)
    grid_spec = pltpu.PrefetchScalarGridSpec(
        num_scalar_prefetch=1,
        grid=(T // ct,),
        in_specs=[pl.BlockSpec((1, 1, n_rows), lambda i, cid: (i, 0, 0)),
                  pl.BlockSpec((ct, D_MODEL), lambda i, cid: (i, 0)),
                  pl.BlockSpec((1, 1, 8, D_MODEL),
                               lambda i, cid: (i // tiles_per_batch,
                                               (lax.rem(i, tiles_per_batch) >= n_lat_tiles).astype(jnp.int32), 0, 0)),
                  pl.BlockSpec(memory_space=pl.ANY)],
        out_specs=pl.BlockSpec((ct, D_MODEL), out_map),
        scratch_shapes=[pltpu.VMEM((2, n_rows, D_MODEL), BF16), pltpu.SemaphoreType.DMA((2,))],
    )
    out = pl.pallas_call(
        functools.partial(_combine_kernel, lat_only=lat_only, tiles_per_batch=tiles_per_batch,
                          n_lat_tiles=n_lat_tiles),
        out_shape=jax.ShapeDtypeStruct((out_rows, D_MODEL), F32),
        grid_spec=grid_spec,
        compiler_params=_cparams(("arbitrary",)),
        name="combine",
    )(chunk_id, row_tok, h.reshape(T, D_MODEL), modtab, y_sorted)
    return out.reshape(B, out_rows // B, D_MODEL)


def _moe(v_ffn, top_e, top_g, h_new, modtab, wgu, bgu, wd, bd, layer, n_lat_blocks, lat_only):
    B, S, _ = v_ffn.shape
    T = B * S
    slot_tok, block_e, n_valid, chunk_id, row_tok = _moe_plan(
        top_e.reshape(T, LANES)[:, :TOP_K], top_g.reshape(T, LANES)[:, :TOP_K])
    x_sorted = v_ffn.reshape(T, D_MODEL)[slot_tok]
    y_sorted = _experts(block_e, n_valid, x_sorted, wgu, bgu, wd, bd, layer)
    return _combine(chunk_id, row_tok, y_sorted, h_new, modtab, n_lat_blocks, lat_only)


def _pick_tile(n, cands):
    for t in cands:
        if n % t == 0:
            return t
    raise ValueError(f"no tile for {n}")


def _rope_tables(L, Lc):
    rows = L // GRID_W
    row = jnp.repeat(jnp.arange(rows, dtype=F32), GRID_W)
    col = (jnp.arange(L, dtype=jnp.int32) % GRID_W).astype(F32)
    inv_freq = ROPE_BASE ** (-jnp.arange(ROPE_PAIRS, dtype=F32) / ROPE_PAIRS)
    ar = row[:, None] * inv_freq
    ac = col[:, None] * inv_freq
    cos64 = jnp.concatenate([jnp.cos(ar), jnp.cos(ar), jnp.cos(ac), jnp.cos(ac)], axis=-1)
    sin64 = jnp.concatenate([-jnp.sin(ar), jnp.sin(ar), -jnp.sin(ac), jnp.sin(ac)], axis=-1)
    cos_t = jnp.concatenate([jnp.tile(cos64, (1, 2)), jnp.ones((Lc, LANES), F32)], axis=0)
    sin_t = jnp.concatenate([jnp.tile(sin64, (1, 2)), jnp.zeros((Lc, LANES), F32)], axis=0)
    return cos_t, sin_t


def _pad_lanes(v):
    v = v.reshape(1, -1).astype(F32)
    return jnp.pad(v, ((0, 0), (0, LANES - v.shape[1])))


def kernel(x, c, ctx, c_ctx, w_mod, b_mod, norm_mix_g, w_in, q_norm_g, k_norm_g, lam_q1, lam_k1, lam_q2, lam_k2, subln_g, conv_w, a_log, dt_bias, gdn_norm_g, w_out, norm_ffn_g, router_w, router_b, w_gate_up, b_gate_up, w_down, b_down):
    B, L, D = x.shape
    Lc = ctx.shape[1]
    S = L + Lc
    depth = w_mod.shape[0]
    tm = ROW_TILE
    n_lat_blocks = L // tm
    cos_t, sin_t = _rope_tables(L, Lc)

    c_rows = jnp.zeros((8, D), F32).at[:B].set(c).at[B].set(c_ctx)
    h = jnp.concatenate([x, ctx], axis=1)

    for layer in range(depth):
        mod = _adaln(c_rows, w_mod, b_mod, layer)
        mod6 = mod.reshape(8, 6, D)
        lat_mod = mod6[:B]
        ctx_mod = jnp.broadcast_to(mod6[B][None], (B, 6, D))
        modtab = jnp.pad(jnp.stack([lat_mod, ctx_mod], axis=1), ((0, 0), (0, 0), (0, 2), (0, 0)))

        lam_init = 0.8 - 0.6 * math.exp(-0.3 * layer)
        lam_full = (jnp.exp(jnp.sum(lam_q1[layer] * lam_k1[layer]))
                    - jnp.exp(jnp.sum(lam_q2[layer] * lam_k2[layer])) + lam_init).reshape(1).astype(F32)

        w_l = w_in[layer]
        w_main = w_l[:, :IN_MAIN].astype(BF16)
        w_ab = jnp.pad(w_l[:, IN_MAIN:], ((0, 0), (0, LANES - (w_l.shape[1] - IN_MAIN)))).astype(BF16)
        gq = jnp.tile(q_norm_g[layer].reshape(1, ATT_HEAD_DIM), (1, 2))
        gk = jnp.tile(k_norm_g[layer].reshape(1, ATT_HEAD_DIM), (1, 2))
        qz, k_att, v_att, p_gdn, z, ab = _inproj(h, modtab, norm_mix_g[layer].reshape(1, D), w_main, w_ab,
                                                 cos_t, sin_t, gq, gk, n_lat_blocks)
        g_sub = (subln_g[layer] * (1.0 - lam_init)).reshape(1, LANES).astype(F32)
        o_att = _attention_lat(lam_full, qz, k_att, v_att, g_sub, L)
        o_att = _attention_ctx(lam_full, qz, k_att, v_att, g_sub, L, o_att)

        conv_w8 = jnp.pad(conv_w[layer], ((0, 8 - CONV_W), (0, 0)))
        gq_g, gk_g, gv_g, gates = _gdnprep(p_gdn, conv_w8, ab, _pad_lanes(a_log[layer]),
                                           _pad_lanes(dt_bias[layer]), n_lat_blocks)
        qg, w_g, kd, u_g, aq, eg = _gdnchunk(gq_g, gk_g, gv_g, gates)
        o_f, o_b = _gdnscan(qg, w_g, kd, u_g, aq, eg, L // GDN_CHUNK, Lc // GDN_CHUNK)

        rw = jnp.pad(router_w[layer].astype(F32), ((0, 0), (0, LANES - N_EXPERTS)))
        rw_hi = rw.astype(BF16)
        rw = jnp.stack([rw_hi, (rw - rw_hi.astype(F32)).astype(BF16)])
        rb = jnp.pad(router_b[layer].reshape(1, N_EXPERTS).astype(F32), ((0, 0), (0, LANES - N_EXPERTS)),
                     constant_values=-1e30)
        h_new, v_ffn, top_e, top_g = _mixout(
            o_att, o_f, o_b, z, h, modtab, gdn_norm_g[layer].reshape(1, LANES), w_out[layer].astype(BF16),
            norm_ffn_g[layer].reshape(1, D), rw, rb, n_lat_blocks)

        h = _moe(v_ffn, top_e, top_g, h_new, modtab,
                 w_gate_up, b_gate_up.reshape(depth, N_EXPERTS, 1, 2 * D_EXPERT),
                 w_down, b_down.reshape(depth, N_EXPERTS, 1, D), layer, n_lat_blocks,
                 lat_only=layer == depth - 1)
    return h
```
